```python
import math
import jax, jax.numpy as jnp
from jax import lax
import numpy as np


D_MODEL = 1024
BATCH = 8
SEQ = 2048
DEPTH = 1

MEM_LEN = 256
S5_WIDTH = 512
S5_GROUP = 16
S5_GROUPS = S5_WIDTH // S5_GROUP
S5_STATE = 64
DT_MIN = 1e-3
DT_MAX = 1e-1
DIFF_WIDTH = D_MODEL - S5_WIDTH
DIFF_HEAD_DIM = 64
DIFF_V_DIM = 2 * DIFF_HEAD_DIM
N_DIFF_HEADS = DIFF_WIDTH // DIFF_V_DIM
DIFF_QK_WIDTH = 2 * N_DIFF_HEADS * DIFF_HEAD_DIM
MIX_WIDTH = S5_WIDTH + DIFF_WIDTH
MIX_IN = S5_WIDTH + 2 * DIFF_QK_WIDTH + DIFF_WIDTH
Q_BLOCK = 128
NUM_BUCKETS = 32
MAX_DISTANCE = 128
CA_HEADS = 4
CA_HEAD_DIM = D_MODEL // CA_HEADS
FFN_HIDDEN = -(-8 * D_MODEL // (3 * 256)) * 256
DEEPNORM_ALPHA = (2.0 * DEPTH) ** 0.25
DEEPNORM_BETA = (8.0 * DEPTH) ** -0.25
LN_EPS = 1e-5

kernel_name = "hymba_s5_diffattn_deepnorm_layer"


def layer_norm(x, g, b):
    xf = x.astype(jnp.float32)
    mu = jnp.mean(xf, axis=-1, keepdims=True)
    var = jnp.mean(jnp.square(xf - mu), axis=-1, keepdims=True)
    y = (xf - mu) * lax.rsqrt(var + LN_EPS)
    return (y * g.astype(jnp.float32) + b.astype(jnp.float32)).astype(x.dtype)


def rms_norm(x, g):
    xf = x.astype(jnp.float32)
    y = xf * lax.rsqrt(jnp.mean(jnp.square(xf), axis=-1, keepdims=True) + LN_EPS)
    return (y * g.astype(jnp.float32)).astype(x.dtype)


def cmul(ar, ai, br, bi):
    return ar * br - ai * bi, ar * bi + ai * br


def s5_mixer(u, lam_re, lam_im, log_dt, b_re, b_im, c_re, c_im, d, glu_w, glu_b):
    f32 = jnp.float32
    bsz, seq, _ = u.shape
    uf = u.astype(f32).reshape(bsz, seq, S5_GROUPS, S5_GROUP)
    dt = jnp.exp(log_dt.astype(f32))[:, None]
    lr, li = lam_re.astype(f32), lam_im.astype(f32)
    mag = jnp.exp(lr * dt)
    ang = li * dt
    ab_re, ab_im = mag * jnp.cos(ang), mag * jnp.sin(ang)
    den = lr * lr + li * li
    nr, ni = ab_re - 1.0, ab_im
    f_re = (nr * lr + ni * li) / den
    f_im = (ni * lr - nr * li) / den
    bb_re, bb_im = cmul(f_re[..., None], f_im[..., None], b_re.astype(f32), b_im.astype(f32))
    bu_re = jnp.einsum('blgh,gph->blgp', uf, bb_re)
    bu_im = jnp.einsum('blgh,gph->blgp', uf, bb_im)
    a_re = jnp.broadcast_to(ab_re, bu_re.shape)
    a_im = jnp.broadcast_to(ab_im, bu_im.shape)

    def combine(e1, e2):
        a1r, a1i, b1r, b1i = e1
        a2r, a2i, b2r, b2i = e2
        ar, ai = cmul(a2r, a2i, a1r, a1i)
        br, bi = cmul(a2r, a2i, b1r, b1i)
        return ar, ai, br + b2r, bi + b2i

    _, _, xr, xi = lax.associative_scan(combine, (a_re, a_im, bu_re, bu_im), axis=1)
    y = (jnp.einsum('ghp,blgp->blgh', c_re.astype(f32), xr)
         - jnp.einsum('ghp,blgp->blgh', c_im.astype(f32), xi)
         + d.astype(f32) * uf)
    y = jax.nn.gelu(y.reshape(bsz, seq, S5_WIDTH))
    y = y * jax.nn.sigmoid(y @ glu_w.astype(f32) + glu_b.astype(f32))
    return y.astype(u.dtype)


def t5_bucket(dist):
    max_exact = NUM_BUCKETS // 2
    is_small = dist < max_exact
    df = jnp.maximum(dist, 1).astype(jnp.float32)
    large = max_exact + (jnp.log(df / max_exact) / math.log(MAX_DISTANCE / max_exact)
                         * (NUM_BUCKETS - max_exact)).astype(jnp.int32)
    large = jnp.minimum(large, NUM_BUCKETS - 1)
    return jnp.where(is_small, dist, large)


def diff_attention(q, k, v, rel_bias, lq1, lk1, lq2, lk2, subln_g, lambda_init):
    f32 = jnp.float32
    bsz, seq = q.shape[0], q.shape[1]
    lam = (jnp.exp(jnp.sum(lq1.astype(f32) * lk1.astype(f32)))
           - jnp.exp(jnp.sum(lq2.astype(f32) * lk2.astype(f32))) + lambda_init)
    scale = DIFF_HEAD_DIM ** -0.5
    outs = []
    for start in range(0, seq, Q_BLOCK):
        end = start + Q_BLOCK
        qb, kb, vb = q[:, start:end], k[:, :end], v[:, :end]
        s = jnp.einsum('bqhd,bkhd->bhqk', qb, kb).astype(f32) * scale
        s = s.reshape(bsz, N_DIFF_HEADS, 2, Q_BLOCK, end)
        dist = jnp.arange(start, end, dtype=jnp.int32)[:, None] - jnp.arange(end, dtype=jnp.int32)[None, :]
        bias = jnp.transpose(rel_bias.astype(f32)[t5_bucket(jnp.maximum(dist, 0))], (2, 0, 1))
        s = jnp.where(dist >= 0, s + bias[None, :, None], -jnp.inf)
        p = jax.nn.softmax(s, axis=-1)
        attn = p[:, :, 0] - lam * p[:, :, 1]
        outs.append(jnp.einsum('bhqk,bkhd->bqhd', attn.astype(v.dtype), vb))
    o = jnp.concatenate(outs, axis=1)
    o = rms_norm(o, subln_g) * (1.0 - lambda_init)
    return o.reshape(bsz, seq, DIFF_WIDTH)


def memory_cross_attention(h, mem, wq, wkv, wo):
    f32 = jnp.float32
    bsz, seq, _ = h.shape
    q = (h @ wq).reshape(bsz, seq, CA_HEADS, CA_HEAD_DIM)
    kv = mem @ wkv
    k = kv[..., :D_MODEL].reshape(bsz, mem.shape[1], CA_HEADS, CA_HEAD_DIM)
    v = kv[..., D_MODEL:].reshape(bsz, mem.shape[1], CA_HEADS, CA_HEAD_DIM)
    s = jnp.einsum('bqhd,bkhd->bhqk', q, k).astype(f32) * (CA_HEAD_DIM ** -0.5)
    p = jax.nn.softmax(s, axis=-1).astype(v.dtype)
    o = jnp.einsum('bhqk,bkhd->bqhd', p, v).reshape(bsz, seq, D_MODEL)
    return o @ wo


def swiglu_ffn(h, w_gate_up, w_down):
    gu = h @ w_gate_up
    return (jax.nn.silu(gu[..., :FFN_HIDDEN]) * gu[..., FFN_HIDDEN:]) @ w_down


def setup_inputs(seed: int = 0) -> dict:
    key = jax.random.key(seed)
    ks = jax.random.split(key, 40)
    f32 = jnp.float32

    def nrm(k, shape, s):
        return s * jax.random.normal(k, shape, f32)

    L = DEPTH
    return {
        "x": nrm(ks[0], (BATCH, SEQ, D_MODEL), 1.0),
        "mem": nrm(ks[1], (BATCH, MEM_LEN, D_MODEL), 1.0),
        "ln_in_g": 1.0 + nrm(ks[2], (D_MODEL,), 0.02),
        "ln_in_b": nrm(ks[3], (D_MODEL,), 0.02),
        "w_in": nrm(ks[4], (L, D_MODEL, MIX_IN), D_MODEL ** -0.5),
        "s5_lambda_re": -0.5 + nrm(ks[5], (L, S5_GROUPS, S5_STATE), 0.01),
        "s5_lambda_im": math.pi * jnp.arange(S5_STATE, dtype=f32) + nrm(ks[6], (L, S5_GROUPS, S5_STATE), 0.01),
        "s5_log_dt": jax.random.uniform(ks[7], (L, S5_GROUPS), f32, math.log(DT_MIN), math.log(DT_MAX)),
        "s5_b_re": nrm(ks[8], (L, S5_GROUPS, S5_STATE, S5_GROUP), (2.0 * S5_GROUP) ** -0.5),
        "s5_b_im": nrm(ks[9], (L, S5_GROUPS, S5_STATE, S5_GROUP), (2.0 * S5_GROUP) ** -0.5),
        "s5_c_re": nrm(ks[10], (L, S5_GROUPS, S5_GROUP, S5_STATE), S5_STATE ** -0.5),
        "s5_c_im": nrm(ks[11], (L, S5_GROUPS, S5_GROUP, S5_STATE), S5_STATE ** -0.5),
        "s5_d": nrm(ks[12], (L, S5_GROUPS, S5_GROUP), 1.0),
        "s5_glu_w": nrm(ks[13], (L, S5_WIDTH, S5_WIDTH), S5_WIDTH ** -0.5),
        "s5_glu_b": nrm(ks[14], (L, S5_WIDTH), 0.01),
        "diff_lq1": nrm(ks[15], (L, DIFF_HEAD_DIM), 0.1),
        "diff_lk1": nrm(ks[16], (L, DIFF_HEAD_DIM), 0.1),
        "diff_lq2": nrm(ks[17], (L, DIFF_HEAD_DIM), 0.1),
        "diff_lk2": nrm(ks[18], (L, DIFF_HEAD_DIM), 0.1),
        "diff_subln_g": 1.0 + nrm(ks[19], (L, DIFF_V_DIM), 0.02),
        "rel_bias": nrm(ks[20], (NUM_BUCKETS, N_DIFF_HEADS), 0.2),
        "w_out": nrm(ks[21], (L, MIX_WIDTH, D_MODEL), DEEPNORM_BETA * MIX_WIDTH ** -0.5),
        "ln1_g": 1.0 + nrm(ks[22], (L, D_MODEL), 0.02),
        "ln1_b": nrm(ks[23], (L, D_MODEL), 0.02),
        "ca_wq": nrm(ks[24], (L, D_MODEL, D_MODEL), D_MODEL ** -0.5),
        "ca_wkv": nrm(ks[25], (L, D_MODEL, 2 * D_MODEL), D_MODEL ** -0.5),
        "ca_wo": nrm(ks[26], (L, D_MODEL, D_MODEL), DEEPNORM_BETA * D_MODEL ** -0.5),
        "ln2_g": 1.0 + nrm(ks[27], (L, D_MODEL), 0.02),
        "ln2_b": nrm(ks[28], (L, D_MODEL), 0.02),
        "ffn_w_gate_up": nrm(ks[29], (L, D_MODEL, 2 * FFN_HIDDEN), D_MODEL ** -0.5),
        "ffn_w_down": nrm(ks[30], (L, FFN_HIDDEN, D_MODEL), DEEPNORM_BETA * FFN_HIDDEN ** -0.5),
        "ln3_g": 1.0 + nrm(ks[31], (L, D_MODEL), 0.02),
        "ln3_b": nrm(ks[32], (L, D_MODEL), 0.02),
    }


def reference(x, mem, ln_in_g, ln_in_b, w_in, s5_lambda_re, s5_lambda_im, s5_log_dt,
              s5_b_re, s5_b_im, s5_c_re, s5_c_im, s5_d, s5_glu_w, s5_glu_b,
              diff_lq1, diff_lk1, diff_lq2, diff_lk2, diff_subln_g, rel_bias, w_out,
              ln1_g, ln1_b, ca_wq, ca_wkv, ca_wo, ln2_g, ln2_b,
              ffn_w_gate_up, ffn_w_down, ln3_g, ln3_b):
    bsz, seq, _ = x.shape
    h = layer_norm(x, ln_in_g, ln_in_b)
    for l in range(DEPTH):
        lambda_init = 0.8 - 0.6 * math.exp(-0.3 * l)
        proj = h @ w_in[l]
        u = proj[..., :S5_WIDTH]
        q = proj[..., S5_WIDTH:S5_WIDTH + DIFF_QK_WIDTH].reshape(bsz, seq, 2 * N_DIFF_HEADS, DIFF_HEAD_DIM)
        k = proj[..., S5_WIDTH + DIFF_QK_WIDTH:S5_WIDTH + 2 * DIFF_QK_WIDTH].reshape(bsz, seq, 2 * N_DIFF_HEADS, DIFF_HEAD_DIM)
        v = proj[..., S5_WIDTH + 2 * DIFF_QK_WIDTH:].reshape(bsz, seq, N_DIFF_HEADS, DIFF_V_DIM)
        y_s5 = s5_mixer(u, s5_lambda_re[l], s5_lambda_im[l], s5_log_dt[l], s5_b_re[l], s5_b_im[l],
                        s5_c_re[l], s5_c_im[l], s5_d[l], s5_glu_w[l], s5_glu_b[l])
        y_diff = diff_attention(q, k, v, rel_bias, diff_lq1[l], diff_lk1[l], diff_lq2[l], diff_lk2[l],
                                diff_subln_g[l], lambda_init)
        mix = jnp.concatenate([y_s5, y_diff], axis=-1) @ w_out[l]
        h = layer_norm(DEEPNORM_ALPHA * h + mix, ln1_g[l], ln1_b[l])
        h = layer_norm(DEEPNORM_ALPHA * h + memory_cross_attention(h, mem, ca_wq[l], ca_wkv[l], ca_wo[l]),
                       ln2_g[l], ln2_b[l])
        h = layer_norm(DEEPNORM_ALPHA * h + swiglu_ffn(h, ffn_w_gate_up[l], ffn_w_down[l]),
                       ln3_g[l], ln3_b[l])
    return h
```

```python
import functools
import math

import jax
import jax.numpy as jnp
from jax import lax
from jax.experimental import pallas as pl
from jax.experimental.pallas import tpu as pltpu

F32 = jnp.float32
BF16 = jnp.bfloat16

D_MODEL = 1024
BATCH = 8
SEQ = 2048
N_TOK = BATCH * SEQ
MEM_LEN = 256
S5_WIDTH = 512
S5_GROUP = 16
S5_GROUPS = 32
S5_STATE = 64
S5_COLS = S5_GROUPS * S5_STATE
DIFF_WIDTH = 512
DIFF_HEAD_DIM = 64
DIFF_V_DIM = 128
N_DIFF_HEADS = 4
MIX_IN = 2048
NUM_BUCKETS = 32
MAX_DISTANCE = 128
CA_HEADS = 4
CA_HEAD_DIM = 256
FFN_HIDDEN = 2816
DEEPNORM_ALPHA = 2.0 ** 0.25
LN_EPS = 1e-5
LAMBDA_INIT = 0.8 - 0.6 * math.exp(0.0)

VMEM_LIMIT_BYTES = 56 * 1024 * 1024
MXU_DIM = 256

ROW_TILE = 512
S5_STEPS = 64
S5_SCAN_CHUNK = 512
ATT_TILE = 256
FFN_CHUNK = 256

_NT = (((1,), (1,)), ((), ()))


def _params(*sem):
    return pltpu.CompilerParams(dimension_semantics=sem, vmem_limit_bytes=VMEM_LIMIT_BYTES)


def _layer_norm(x, g, b):
    mu = jnp.mean(x, axis=-1, keepdims=True)
    xc = x - mu
    var = jnp.mean(xc * xc, axis=-1, keepdims=True)
    return xc * lax.rsqrt(var + LN_EPS) * g + b


def _dot(a, b):
    return jnp.dot(a, b, preferred_element_type=F32)


def _s5_prep_kernel(lr_ref, li_ref, ldt_ref, br_ref, bi_ref, ar_ref, ai_ref, bbr_ref, bbi_ref):
    lr = lr_ref[...]
    li = li_ref[...]
    dt = jnp.exp(ldt_ref[...])
    mag = jnp.exp(lr * dt)
    ang = li * dt
    ar = mag * jnp.cos(ang)
    ai = mag * jnp.sin(ang)
    den = lr * lr + li * li
    nr = ar - 1.0
    fr = (nr * lr + ai * li) / den
    fi = (ai * lr - nr * li) / den
    ar_ref[...] = ar
    ai_ref[...] = ai
    b_r = br_ref[...]
    b_i = bi_ref[...]
    bbr_ref[...] = fr * b_r - fi * b_i
    bbi_ref[...] = fr * b_i + fi * b_r


def _s5_prep(lam_re, lam_im, log_dt, b_re, b_im):
    g, p, h = S5_GROUPS, S5_STATE, S5_GROUP
    out = pl.pallas_call(
        _s5_prep_kernel,
        out_shape=(jax.ShapeDtypeStruct((g, 1, p), F32), jax.ShapeDtypeStruct((g, 1, p), F32),
                   jax.ShapeDtypeStruct((g, h, p), F32), jax.ShapeDtypeStruct((g, h, p), F32)),
        name="s5_prep",
    )(lam_re.reshape(g, 1, p), lam_im.reshape(g, 1, p), log_dt.reshape(g, 1, 1),
      jnp.transpose(b_re, (0, 2, 1)), jnp.transpose(b_im, (0, 2, 1)))
    return out


def _block_diag(blocks):
    nh, ng, r, c = blocks.shape
    eye = jnp.eye(ng, dtype=blocks.dtype)
    full = blocks[:, :, :, None, :] * eye[None, :, None, :, None]
    return full.reshape(nh, ng * r, ng * c)


def _bias_prep_kernel(rel_ref, bucket_ref, o_ref):
    h = pl.program_id(0)
    bucket = bucket_ref[0]
    far = rel_ref[NUM_BUCKETS - 1, h]
    acc = jnp.zeros(bucket.shape, F32)
    for b in range(NUM_BUCKETS):
        acc = jnp.where(bucket == b, rel_ref[b, h] - far, acc)
    o_ref[0, 0] = jnp.where(bucket < 0, -jnp.inf, acc)


def _t5_bucket(dist):
    max_exact = NUM_BUCKETS // 2
    is_small = dist < max_exact
    df = jnp.maximum(dist, 1).astype(F32)
    large = max_exact + (jnp.log(df / max_exact) / math.log(MAX_DISTANCE / max_exact)
                         * (NUM_BUCKETS - max_exact)).astype(jnp.int32)
    large = jnp.minimum(large, NUM_BUCKETS - 1)
    return jnp.where(is_small, dist, large)


def _bias_tiles(rel_bias):
    t = ATT_TILE
    qpos = jnp.arange(t, dtype=jnp.int32)[:, None]
    kpos = jnp.arange(t, dtype=jnp.int32)[None, :]
    d_diag = qpos - kpos
    d_prev = d_diag + t
    bucket = jnp.stack([jnp.where(d_diag >= 0, _t5_bucket(jnp.maximum(d_diag, 0)), -1),
                        _t5_bucket(d_prev)])
    return pl.pallas_call(
        _bias_prep_kernel,
        grid=(N_DIFF_HEADS, 2),
        in_specs=[pl.BlockSpec(memory_space=pltpu.SMEM),
                  pl.BlockSpec((1, t, t), lambda h, j: (j, 0, 0))],
        out_specs=pl.BlockSpec((1, 1, t, t), lambda h, j: (h, j, 0, 0)),
        out_shape=jax.ShapeDtypeStruct((N_DIFF_HEADS, 2, t, t), F32),
        compiler_params=_params("parallel", "parallel"),
        name="bias_prep",
    )(rel_bias, bucket)


def _proj_kernel(x_ref, g_ref, b_ref, w_ref, u_ref, q_ref, k_ref, v_ref):
    h = _layer_norm(x_ref[...], g_ref[...], b_ref[...]).astype(BF16)
    u_ref[...] = _dot(h, w_ref[:, 0:512])
    q_ref[...] = (_dot(h, w_ref[:, 512:1024]) * (DIFF_HEAD_DIM ** -0.5)).astype(BF16)
    k_ref[...] = _dot(h, w_ref[:, 1024:1536]).astype(BF16)
    v_ref[...] = _dot(h, w_ref[:, 1536:2048]).astype(BF16)


def _proj(x2, g, b, w):
    tm = ROW_TILE
    row = lambda i: (i, 0)
    const = lambda i: (0, 0)
    return pl.pallas_call(
        _proj_kernel,
        grid=(N_TOK // tm,),
        in_specs=[pl.BlockSpec((tm, D_MODEL), row), pl.BlockSpec((1, D_MODEL), const),
                  pl.BlockSpec((1, D_MODEL), const), pl.BlockSpec((D_MODEL, MIX_IN), const)],
        out_specs=[pl.BlockSpec((tm, 512), row)] * 4,
        out_shape=(jax.ShapeDtypeStruct((N_TOK, 512), F32),) + (jax.ShapeDtypeStruct((N_TOK, 512), BF16),) * 3,
        compiler_params=_params("parallel"),
        name="proj",
    )(x2, g, b, w)


def _s5_kernel(u_ref, ar_ref, ai_ref, bre_ref, bim_ref, cre_ref, cim_ref, d_ref, gw_ref, gb_ref,
               y_ref, sre, sim, xre, xim):
    @pl.when(pl.program_id(0) == 0)
    def _():
        xre[...] = jnp.zeros_like(xre)
        xim[...] = jnp.zeros_like(xim)

    u = u_ref[...]
    ub = u.astype(BF16)
    half = S5_COLS // 2
    for hf in range(2):
        uh = ub[:, hf * MXU_DIM:(hf + 1) * MXU_DIM]
        sre[:, hf * half:(hf + 1) * half] = _dot(uh, bre_ref[hf])
        sim[:, hf * half:(hf + 1) * half] = _dot(uh, bim_ref[hf])

    cw = S5_SCAN_CHUNK
    for c in range(S5_COLS // cw):
        cs = slice(c * cw, (c + 1) * cw)
        ar = ar_ref[:, cs]
        ai = ai_ref[:, cs]

        def body(t, carry, cs=cs, ar=ar, ai=ai):
            xr, xi = carry
            r = pl.multiple_of(t * BATCH, BATCH)
            nr = ar * xr - ai * xi + sre[pl.ds(r, BATCH), cs]
            ni = ar * xi + ai * xr + sim[pl.ds(r, BATCH), cs]
            sre[pl.ds(r, BATCH), cs] = nr
            sim[pl.ds(r, BATCH), cs] = ni
            return nr, ni

        xr, xi = lax.fori_loop(0, S5_STEPS, body, (xre[:, cs], xim[:, cs]), unroll=4)
        xre[:, cs] = xr
        xim[:, cs] = xi

    ys = []
    for hf in range(2):
        hs = slice(hf * half, (hf + 1) * half)
        ys.append(_dot(sre[:, hs].astype(BF16), cre_ref[hf]) - _dot(sim[:, hs].astype(BF16), cim_ref[hf]))
    y = jnp.concatenate(ys, axis=-1) + d_ref[...] * u
    y = jax.nn.gelu(y)
    z = _dot(y.astype(BF16), gw_ref[...]) + gb_ref[...]
    y_ref[...] = (y * jax.nn.sigmoid(z)).astype(BF16)


def _s5(u_tm, ar8, ai8, bre, bim, cre, cim, d, gw, gb):
    rows = S5_STEPS * BATCH
    row = lambda i: (i, 0)
    c2 = lambda i: (0, 0)
    c3 = lambda i: (0, 0, 0)
    half = S5_COLS // 2
    return pl.pallas_call(
        _s5_kernel,
        grid=(SEQ // S5_STEPS,),
        in_specs=[pl.BlockSpec((rows, S5_WIDTH), row),
                  pl.BlockSpec((BATCH, S5_COLS), c2), pl.BlockSpec((BATCH, S5_COLS), c2),
                  pl.BlockSpec((2, MXU_DIM, half), c3), pl.BlockSpec((2, MXU_DIM, half), c3),
                  pl.BlockSpec((2, half, MXU_DIM), c3), pl.BlockSpec((2, half, MXU_DIM), c3),
                  pl.BlockSpec((1, S5_WIDTH), c2), pl.BlockSpec((S5_WIDTH, S5_WIDTH), c2),
                  pl.BlockSpec((1, S5_WIDTH), c2)],
        out_specs=pl.BlockSpec((rows, S5_WIDTH), row),
        out_shape=jax.ShapeDtypeStruct((N_TOK, S5_WIDTH), BF16),
        scratch_shapes=[pltpu.VMEM((rows, S5_COLS), F32), pltpu.VMEM((rows, S5_COLS), F32),
                        pltpu.VMEM((BATCH, S5_COLS), F32), pltpu.VMEM((BATCH, S5_COLS), F32)],
        compiler_params=_params("arbitrary"),
        name="s5",
    )(u_tm, ar8, ai8, bre, bim, cre, cim, d, gw, gb)


def _diff_attn_kernel(q_ref, k_ref, v_ref, bias_ref, lq1_ref, lk1_ref, lq2_ref, lk2_ref, g_ref,
                      o_ref, m_sc, l_sc, acc_sc):
    t = ATT_TILE
    qi = pl.program_id(2)
    q = q_ref[...]
    lane = lax.broadcasted_iota(jnp.int32, q.shape, 1)
    zero = jnp.zeros_like(q)
    qm = (jnp.where(lane < DIFF_HEAD_DIM, q, zero), jnp.where(lane >= DIFF_HEAD_DIM, q, zero))

    m_sc[...] = jnp.full(m_sc.shape, -jnp.inf, F32)
    l_sc[...] = jnp.zeros(l_sc.shape, F32)
    acc_sc[...] = jnp.zeros(acc_sc.shape, F32)

    def step(j, bias):
        r = pl.multiple_of(j * t, t)
        kb = k_ref[pl.ds(r, t), :]
        vb = v_ref[pl.ds(r, t), :]
        for mi in range(2):
            s = lax.dot_general(qm[mi], kb, _NT, preferred_element_type=F32)
            if bias is not None:
                s = s + bias
            m_old = m_sc[mi]
            m_new = jnp.maximum(m_old, jnp.max(s, axis=-1, keepdims=True))
            alpha = jnp.exp(m_old - m_new)
            p = jnp.exp(s - m_new)
            l_sc[mi] = alpha * l_sc[mi] + jnp.sum(p, axis=-1, keepdims=True)
            acc_sc[mi] = alpha * acc_sc[mi] + _dot(p.astype(BF16), vb)
            m_sc[mi] = m_new

    def far(j, carry):
        step(j, None)
        return carry

    lax.fori_loop(0, jnp.maximum(qi - 1, 0), far, 0)

    @pl.when(qi >= 1)
    def _():
        step(qi - 1, bias_ref[0, 1])

    step(qi, bias_ref[0, 0])

    lam = (jnp.exp(jnp.sum(lq1_ref[...] * lk1_ref[...], axis=-1, keepdims=True))
           - jnp.exp(jnp.sum(lq2_ref[...] * lk2_ref[...], axis=-1, keepdims=True)) + LAMBDA_INIT)
    o = acc_sc[0] / l_sc[0] - lam * (acc_sc[1] / l_sc[1])
    ms = jnp.mean(o * o, axis=-1, keepdims=True)
    y = o * lax.rsqrt(ms + LN_EPS) * g_ref[...]
    o_ref[...] = (y * (1.0 - LAMBDA_INIT)).astype(BF16)


def _diff_attn(q, k, v, bias, lq1, lk1, lq2, lk2, g):
    t = ATT_TILE
    nq = SEQ // t
    vec = lambda b, h, i: (0, 0)
    return pl.pallas_call(
        _diff_attn_kernel,
        grid=(BATCH, N_DIFF_HEADS, nq),
        in_specs=[pl.BlockSpec((t, DIFF_V_DIM), lambda b, h, i: (b * nq + i, h)),
                  pl.BlockSpec((SEQ, DIFF_V_DIM), lambda b, h, i: (b, h)),
                  pl.BlockSpec((SEQ, DIFF_V_DIM), lambda b, h, i: (b, h)),
                  pl.BlockSpec((1, 2, t, t), lambda b, h, i: (h, 0, 0, 0)),
                  pl.BlockSpec((1, DIFF_HEAD_DIM), vec), pl.BlockSpec((1, DIFF_HEAD_DIM), vec),
                  pl.BlockSpec((1, DIFF_HEAD_DIM), vec), pl.BlockSpec((1, DIFF_HEAD_DIM), vec),
                  pl.BlockSpec((1, DIFF_V_DIM), vec)],
        out_specs=pl.BlockSpec((t, DIFF_V_DIM), lambda b, h, i: (b * nq + i, h)),
        out_shape=jax.ShapeDtypeStruct((N_TOK, DIFF_WIDTH), BF16),
        scratch_shapes=[pltpu.VMEM((2, t, 1), F32), pltpu.VMEM((2, t, 1), F32),
                        pltpu.VMEM((2, t, DIFF_V_DIM), F32)],
        compiler_params=_params("parallel", "parallel", "arbitrary"),
        name="diff_attn",
    )(q, k, v, bias, lq1, lk1, lq2, lk2, g)


def _kv_kernel(m_ref, w_ref, o_ref):
    o_ref[...] = _dot(m_ref[...].astype(BF16), w_ref[...]).astype(BF16)


def _kv(mem2, wkv):
    return pl.pallas_call(
        _kv_kernel,
        grid=(BATCH,),
        in_specs=[pl.BlockSpec((MEM_LEN, D_MODEL), lambda i: (i, 0)),
                  pl.BlockSpec((D_MODEL, 2 * D_MODEL), lambda i: (0, 0))],
        out_specs=pl.BlockSpec((MEM_LEN, 2 * D_MODEL), lambda i: (i, 0)),
        out_shape=jax.ShapeDtypeStruct((BATCH * MEM_LEN, 2 * D_MODEL), BF16),
        compiler_params=_params("parallel"),
        name="kv",
    )(mem2, wkv)


def _mix_ca_kernel(x_ref, ys_ref, yd_ref, g0_ref, b0_ref, wout_ref, g1_ref, b1_ref,
                   wq_ref, kv_ref, wo_ref, g2_ref, b2_ref, o_ref):
    h0 = _layer_norm(x_ref[...], g0_ref[...], b0_ref[...])
    mix = _dot(ys_ref[...], wout_ref[0:S5_WIDTH, :]) + _dot(yd_ref[...], wout_ref[S5_WIDTH:, :])
    h1 = _layer_norm(DEEPNORM_ALPHA * h0 + mix, g1_ref[...], b1_ref[...])
    h1b = h1.astype(BF16)
    heads = []
    for hd in range(CA_HEADS):
        cs = slice(hd * CA_HEAD_DIM, (hd + 1) * CA_HEAD_DIM)
        qh = (_dot(h1b, wq_ref[:, cs]) * (CA_HEAD_DIM ** -0.5)).astype(BF16)
        kh = kv_ref[:, cs]
        vh = kv_ref[:, D_MODEL + hd * CA_HEAD_DIM:D_MODEL + (hd + 1) * CA_HEAD_DIM]
        s = lax.dot_general(qh, kh, _NT, preferred_element_type=F32)
        e = jnp.exp(s - jnp.max(s, axis=-1, keepdims=True))
        oh = _dot(e.astype(BF16), vh) / jnp.sum(e, axis=-1, keepdims=True)
        heads.append(oh.astype(BF16))
    ca = _dot(jnp.concatenate(heads, axis=-1), wo_ref[...])
    o_ref[...] = _layer_norm(DEEPNORM_ALPHA * h1 + ca, g2_ref[...], b2_ref[...])


def _mix_ca(x2, ys, yd, g0, b0, wout, g1, b1, wq, kv, wo, g2, b2):
    tm = ROW_TILE
    per_batch = SEQ // tm
    row = lambda i: (i, 0)
    const = lambda i: (0, 0)
    vec = pl.BlockSpec((1, D_MODEL), const)
    sq = pl.BlockSpec((D_MODEL, D_MODEL), const)
    return pl.pallas_call(
        _mix_ca_kernel,
        grid=(N_TOK // tm,),
        in_specs=[pl.BlockSpec((tm, D_MODEL), row), pl.BlockSpec((tm, 512), row), pl.BlockSpec((tm, 512), row),
                  vec, vec, sq, vec, vec, sq,
                  pl.BlockSpec((MEM_LEN, 2 * D_MODEL), lambda i: (i // per_batch, 0)),
                  sq, vec, vec],
        out_specs=pl.BlockSpec((tm, D_MODEL), row),
        out_shape=jax.ShapeDtypeStruct((N_TOK, D_MODEL), F32),
        compiler_params=_params("parallel"),
        name="mix_ca",
    )(x2, ys, yd, g0, b0, wout, g1, b1, wq, kv, wo, g2, b2)


def _ffn_kernel(h_ref, wgu_ref, wd_ref, g_ref, b_ref, o_ref, acc):
    h = h_ref[...]
    hb = h.astype(BF16)
    ck = FFN_CHUNK
    for c in range(FFN_HIDDEN // ck):
        gate = _dot(hb, wgu_ref[:, c * ck:(c + 1) * ck])
        up = _dot(hb, wgu_ref[:, FFN_HIDDEN + c * ck:FFN_HIDDEN + (c + 1) * ck])
        act = (jax.nn.silu(gate) * up).astype(BF16)
        part = _dot(act, wd_ref[c * ck:(c + 1) * ck, :])
        if c == 0:
            acc[...] = part
        else:
            acc[...] += part
    o_ref[...] = _layer_norm(DEEPNORM_ALPHA * h + acc[...], g_ref[...], b_ref[...])


def _ffn(h2, wgu, wd, g, b):
    tm = ROW_TILE
    row = lambda i: (i, 0)
    const = lambda i: (0, 0)
    return pl.pallas_call(
        _ffn_kernel,
        grid=(N_TOK // tm,),
        in_specs=[pl.BlockSpec((tm, D_MODEL), row),
                  pl.BlockSpec((D_MODEL, 2 * FFN_HIDDEN), const),
                  pl.BlockSpec((FFN_HIDDEN, D_MODEL), const),
                  pl.BlockSpec((1, D_MODEL), const), pl.BlockSpec((1, D_MODEL), const)],
        out_specs=pl.BlockSpec((tm, D_MODEL), row),
        out_shape=jax.ShapeDtypeStruct((N_TOK, D_MODEL), F32),
        scratch_shapes=[pltpu.VMEM((tm, D_MODEL), F32)],
        compiler_params=_params("parallel"),
        name="ffn",
    )(h2, wgu, wd, g, b)


def kernel(x, mem, ln_in_g, ln_in_b, w_in, s5_lambda_re, s5_lambda_im, s5_log_dt, s5_b_re, s5_b_im,
           s5_c_re, s5_c_im, s5_d, s5_glu_w, s5_glu_b, diff_lq1, diff_lk1, diff_lq2, diff_lk2,
           diff_subln_g, rel_bias, w_out, ln1_g, ln1_b, ca_wq, ca_wkv, ca_wo, ln2_g, ln2_b,
           ffn_w_gate_up, ffn_w_down, ln3_g, ln3_b):
    vec = lambda a: a.reshape(1, -1).astype(F32)
    x2 = x.reshape(N_TOK, D_MODEL)

    u, q, k, v = _proj(x2, vec(ln_in_g), vec(ln_in_b), w_in[0].astype(BF16))

    ar, ai, bbr, bbi = _s5_prep(s5_lambda_re[0], s5_lambda_im[0], s5_log_dt[0], s5_b_re[0], s5_b_im[0])
    ar8 = jnp.broadcast_to(ar.reshape(1, S5_COLS), (BATCH, S5_COLS))
    ai8 = jnp.broadcast_to(ai.reshape(1, S5_COLS), (BATCH, S5_COLS))
    bre = _block_diag(bbr.reshape(2, 16, S5_GROUP, S5_STATE)).astype(BF16)
    bim = _block_diag(bbi.reshape(2, 16, S5_GROUP, S5_STATE)).astype(BF16)
    cre = _block_diag(jnp.transpose(s5_c_re[0], (0, 2, 1)).reshape(2, 16, S5_STATE, S5_GROUP)).astype(BF16)
    cim = _block_diag(jnp.transpose(s5_c_im[0], (0, 2, 1)).reshape(2, 16, S5_STATE, S5_GROUP)).astype(BF16)
    u_tm = jnp.transpose(u.reshape(BATCH, SEQ, S5_WIDTH), (1, 0, 2)).reshape(N_TOK, S5_WIDTH)
    y_tm = _s5(u_tm, ar8, ai8, bre, bim, cre, cim, vec(s5_d[0]), s5_glu_w[0].astype(BF16), vec(s5_glu_b[0]))
    y_s5 = jnp.transpose(y_tm.reshape(SEQ, BATCH, S5_WIDTH), (1, 0, 2)).reshape(N_TOK, S5_WIDTH)

    y_diff = _diff_attn(q, k, v, _bias_tiles(rel_bias), vec(diff_lq1[0]), vec(diff_lk1[0]),
                        vec(diff_lq2[0]), vec(diff_lk2[0]), vec(diff_subln_g[0]))

    kv = _kv(mem.reshape(BATCH * MEM_LEN, D_MODEL), ca_wkv[0].astype(BF16))
    h2 = _mix_ca(x2, y_s5, y_diff, vec(ln_in_g), vec(ln_in_b), w_out[0].astype(BF16), vec(ln1_g[0]),
                 vec(ln1_b[0]), ca_wq[0].astype(BF16), kv, ca_wo[0].astype(BF16), vec(ln2_g[0]), vec(ln2_b[0]))
    out = _ffn(h2, ffn_w_gate_up[0].astype(BF16), ffn_w_down[0].astype(BF16), vec(ln3_g[0]), vec(ln3_b[0]))
    return out.reshape(BATCH, SEQ, D_MODEL)
```

```python
import functools
import math

import jax
import jax.numpy as jnp
from jax import lax
from jax.experimental import pallas as pl
from jax.experimental.pallas import tpu as pltpu

F32 = jnp.float32
BF16 = jnp.bfloat16

D_MODEL = 1024
BATCH = 8
SEQ = 2048
N_TOK = BATCH * SEQ
MEM_LEN = 256
S5_WIDTH = 512
S5_GROUP = 16
S5_GROUPS = 32
S5_STATE = 64
S5_COLS = S5_GROUPS * S5_STATE
DIFF_WIDTH = 512
DIFF_HEAD_DIM = 64
DIFF_V_DIM = 128
N_DIFF_HEADS = 4
MIX_IN = 2048
NUM_BUCKETS = 32
MAX_DISTANCE = 128
CA_HEADS = 4
CA_HEAD_DIM = 256
FFN_HIDDEN = 2816
DEEPNORM_ALPHA = 2.0 ** 0.25
LN_EPS = 1e-5
LAMBDA_INIT = 0.8 - 0.6 * math.exp(0.0)

VMEM_LIMIT_BYTES = 56 * 1024 * 1024
MXU_DIM = 256
LANES = 128

ROW_TILE = 512
S5_STEPS = 64
S5_SCAN_CHUNK = 512
ATT_TILE = 512
FFN_CHUNK = 256

_NT = (((1,), (1,)), ((), ()))


def _params(*sem):
    return pltpu.CompilerParams(dimension_semantics=sem, vmem_limit_bytes=VMEM_LIMIT_BYTES)


def _layer_norm(x, g, b):
    mu = jnp.mean(x, axis=-1, keepdims=True)
    xc = x - mu
    var = jnp.mean(xc * xc, axis=-1, keepdims=True)
    return xc * lax.rsqrt(var + LN_EPS) * g + b


def _dot(a, b):
    return jnp.dot(a, b, preferred_element_type=F32)


def _s5_prep_kernel(lr_ref, li_ref, ldt_ref, br_ref, bi_ref, ar_ref, ai_ref, bbr_ref, bbi_ref):
    lr = lr_ref[...]
    li = li_ref[...]
    dt = jnp.exp(ldt_ref[...])
    mag = jnp.exp(lr * dt)
    ang = li * dt
    ar = mag * jnp.cos(ang)
    ai = mag * jnp.sin(ang)
    den = lr * lr + li * li
    nr = ar - 1.0
    fr = (nr * lr + ai * li) / den
    fi = (ai * lr - nr * li) / den
    ar_ref[...] = ar
    ai_ref[...] = ai
    b_r = br_ref[...]
    b_i = bi_ref[...]
    bbr_ref[...] = fr * b_r - fi * b_i
    bbi_ref[...] = fr * b_i + fi * b_r


def _s5_prep(lam_re, lam_im, log_dt, b_re, b_im):
    g, p, h = S5_GROUPS, S5_STATE, S5_GROUP
    out = pl.pallas_call(
        _s5_prep_kernel,
        out_shape=(jax.ShapeDtypeStruct((g, 1, p), F32), jax.ShapeDtypeStruct((g, 1, p), F32),
                   jax.ShapeDtypeStruct((g, h, p), F32), jax.ShapeDtypeStruct((g, h, p), F32)),
        name="s5_prep",
    )(lam_re.reshape(g, 1, p), lam_im.reshape(g, 1, p), log_dt.reshape(g, 1, 1),
      jnp.transpose(b_re, (0, 2, 1)), jnp.transpose(b_im, (0, 2, 1)))
    return out


def _block_diag(blocks):
    nh, ng, r, c = blocks.shape
    eye = jnp.eye(ng, dtype=blocks.dtype)
    full = blocks[:, :, :, None, :] * eye[None, :, None, :, None]
    return full.reshape(nh, ng * r, ng * c)


def _bias_prep_kernel(rel_ref, bucket_ref, o_ref):
    h = pl.program_id(0)
    bucket = bucket_ref[0]
    far = rel_ref[NUM_BUCKETS - 1, h]
    acc = jnp.zeros(bucket.shape, F32)
    for b in range(NUM_BUCKETS):
        acc = jnp.where(bucket == b, rel_ref[b, h] - far, acc)
    o_ref[0, 0] = jnp.where(bucket < 0, -jnp.inf, acc)


def _t5_bucket(dist):
    max_exact = NUM_BUCKETS // 2
    is_small = dist < max_exact
    df = jnp.maximum(dist, 1).astype(F32)
    large = max_exact + (jnp.log(df / max_exact) / math.log(MAX_DISTANCE / max_exact)
                         * (NUM_BUCKETS - max_exact)).astype(jnp.int32)
    large = jnp.minimum(large, NUM_BUCKETS - 1)
    return jnp.where(is_small, dist, large)


def _bias_tiles(rel_bias):
    t = ATT_TILE
    qpos = jnp.arange(t, dtype=jnp.int32)[:, None]
    kpos = jnp.arange(t, dtype=jnp.int32)[None, :]
    d_diag = qpos - kpos
    d_prev = d_diag + t
    bucket = jnp.stack([jnp.where(d_diag >= 0, _t5_bucket(jnp.maximum(d_diag, 0)), -1),
                        _t5_bucket(d_prev)])
    return pl.pallas_call(
        _bias_prep_kernel,
        grid=(N_DIFF_HEADS, 2),
        in_specs=[pl.BlockSpec(memory_space=pltpu.SMEM),
                  pl.BlockSpec((1, t, t), lambda h, j: (j, 0, 0))],
        out_specs=pl.BlockSpec((1, 1, t, t), lambda h, j: (h, j, 0, 0)),
        out_shape=jax.ShapeDtypeStruct((N_DIFF_HEADS, 2, t, t), F32),
        compiler_params=_params("parallel", "parallel"),
        name="bias_prep",
    )(rel_bias, bucket)


def _proj_kernel(x_ref, g_ref, b_ref, w_ref, u_ref, q_ref, k_ref, v_ref):
    h = _layer_norm(x_ref[...], g_ref[...], b_ref[...]).astype(BF16)
    u_ref[...] = _dot(h, w_ref[:, 0:512])
    q_ref[...] = (_dot(h, w_ref[:, 512:1024]) * (DIFF_HEAD_DIM ** -0.5)).astype(BF16)
    k_ref[...] = _dot(h, w_ref[:, 1024:1536]).astype(BF16)
    v_ref[...] = _dot(h, w_ref[:, 1536:2048]).astype(BF16)


def _proj(x2, g, b, w):
    tm = ROW_TILE
    row = lambda i: (i, 0)
    const = lambda i: (0, 0)
    return pl.pallas_call(
        _proj_kernel,
        grid=(N_TOK // tm,),
        in_specs=[pl.BlockSpec((tm, D_MODEL), row), pl.BlockSpec((1, D_MODEL), const),
                  pl.BlockSpec((1, D_MODEL), const), pl.BlockSpec((D_MODEL, MIX_IN), const)],
        out_specs=[pl.BlockSpec((tm, 512), row)] * 4,
        out_shape=(jax.ShapeDtypeStruct((N_TOK, 512), F32),) + (jax.ShapeDtypeStruct((N_TOK, 512), BF16),) * 3,
        compiler_params=_params("parallel"),
        name="proj",
    )(x2, g, b, w)


def _s5_kernel(u_ref, ar_ref, ai_ref, bre_ref, bim_ref, cre_ref, cim_ref, d_ref, gw_ref, gb_ref,
               y_ref, sre, sim, xre, xim):
    @pl.when(pl.program_id(0) == 0)
    def _():
        xre[...] = jnp.zeros_like(xre)
        xim[...] = jnp.zeros_like(xim)

    u = u_ref[...]
    ub = u.astype(BF16)
    half = S5_COLS // 2
    for hf in range(2):
        uh = ub[:, hf * MXU_DIM:(hf + 1) * MXU_DIM]
        sre[:, hf * half:(hf + 1) * half] = _dot(uh, bre_ref[hf])
        sim[:, hf * half:(hf + 1) * half] = _dot(uh, bim_ref[hf])

    cw = S5_SCAN_CHUNK
    for c in range(S5_COLS // cw):
        cs = slice(c * cw, (c + 1) * cw)
        ar = ar_ref[:, cs]
        ai = ai_ref[:, cs]

        def body(t, carry, cs=cs, ar=ar, ai=ai):
            xr, xi = carry
            r = pl.multiple_of(t * BATCH, BATCH)
            nr = ar * xr - ai * xi + sre[pl.ds(r, BATCH), cs]
            ni = ar * xi + ai * xr + sim[pl.ds(r, BATCH), cs]
            sre[pl.ds(r, BATCH), cs] = nr
            sim[pl.ds(r, BATCH), cs] = ni
            return nr, ni

        xr, xi = lax.fori_loop(0, S5_STEPS, body, (xre[:, cs], xim[:, cs]), unroll=4)
        xre[:, cs] = xr
        xim[:, cs] = xi

    ys = []
    for hf in range(2):
        hs = slice(hf * half, (hf + 1) * half)
        ys.append(_dot(sre[:, hs].astype(BF16), cre_ref[hf]) - _dot(sim[:, hs].astype(BF16), cim_ref[hf]))
    y = jnp.concatenate(ys, axis=-1) + d_ref[...] * u
    y = jax.nn.gelu(y)
    z = _dot(y.astype(BF16), gw_ref[...]) + gb_ref[...]
    y_ref[...] = (y * jax.nn.sigmoid(z)).astype(BF16)


def _s5(u_tm, ar8, ai8, bre, bim, cre, cim, d, gw, gb):
    rows = S5_STEPS * BATCH
    row = lambda i: (i, 0)
    c2 = lambda i: (0, 0)
    c3 = lambda i: (0, 0, 0)
    half = S5_COLS // 2
    return pl.pallas_call(
        _s5_kernel,
        grid=(SEQ // S5_STEPS,),
        in_specs=[pl.BlockSpec((rows, S5_WIDTH), row),
                  pl.BlockSpec((BATCH, S5_COLS), c2), pl.BlockSpec((BATCH, S5_COLS), c2),
                  pl.BlockSpec((2, MXU_DIM, half), c3), pl.BlockSpec((2, MXU_DIM, half), c3),
                  pl.BlockSpec((2, half, MXU_DIM), c3), pl.BlockSpec((2, half, MXU_DIM), c3),
                  pl.BlockSpec((1, S5_WIDTH), c2), pl.BlockSpec((S5_WIDTH, S5_WIDTH), c2),
                  pl.BlockSpec((1, S5_WIDTH), c2)],
        out_specs=pl.BlockSpec((rows, S5_WIDTH), row),
        out_shape=jax.ShapeDtypeStruct((N_TOK, S5_WIDTH), BF16),
        scratch_shapes=[pltpu.VMEM((rows, S5_COLS), F32), pltpu.VMEM((rows, S5_COLS), F32),
                        pltpu.VMEM((BATCH, S5_COLS), F32), pltpu.VMEM((BATCH, S5_COLS), F32)],
        compiler_params=_params("arbitrary"),
        name="s5",
    )(u_tm, ar8, ai8, bre, bim, cre, cim, d, gw, gb)


def _diff_attn_kernel(q_ref, k_ref, v_ref, bias_ref, lq1_ref, lk1_ref, lq2_ref, lk2_ref, g_ref,
                      o_ref, m_sc, l_sc, acc_sc):
    t = ATT_TILE
    qi = pl.program_id(2)
    q = q_ref[...]
    lane = lax.broadcasted_iota(jnp.int32, q.shape, 1)
    zero = jnp.zeros_like(q)
    qm = (jnp.where(lane < DIFF_HEAD_DIM, q, zero), jnp.where(lane >= DIFF_HEAD_DIM, q, zero))

    m_sc[...] = jnp.full(m_sc.shape, -jnp.inf, F32)
    l_sc[...] = jnp.zeros(l_sc.shape, F32)
    acc_sc[...] = jnp.zeros(acc_sc.shape, F32)

    def step(j, bias):
        r = pl.multiple_of(j * t, t)
        kb = k_ref[pl.ds(r, t), :]
        vb = v_ref[pl.ds(r, t), :]
        for mi in range(2):
            s = lax.dot_general(qm[mi], kb, _NT, preferred_element_type=F32)
            if bias is not None:
                s = s + bias
            sg = [s[:, g * LANES:(g + 1) * LANES] for g in range(t // LANES)]
            m_old = m_sc[mi]
            m_new = jnp.maximum(m_old, jnp.max(functools.reduce(jnp.maximum, sg), axis=-1, keepdims=True))
            alpha = jnp.exp(m_old - m_new)
            pg = [jnp.exp(x - m_new) for x in sg]
            l_sc[mi] = alpha * l_sc[mi] + functools.reduce(jnp.add, pg)
            p = jnp.concatenate(pg, axis=-1).astype(BF16)
            acc_sc[mi] = alpha * acc_sc[mi] + _dot(p, vb)
            m_sc[mi] = m_new

    def far(j, carry):
        step(j, None)
        return carry

    lax.fori_loop(0, jnp.maximum(qi - 1, 0), far, 0)

    @pl.when(qi >= 1)
    def _():
        step(qi - 1, bias_ref[0, 1])

    step(qi, bias_ref[0, 0])

    lam = (jnp.exp(jnp.sum(lq1_ref[...] * lk1_ref[...], axis=-1, keepdims=True))
           - jnp.exp(jnp.sum(lq2_ref[...] * lk2_ref[...], axis=-1, keepdims=True)) + LAMBDA_INIT)
    l0 = jnp.sum(l_sc[0], axis=-1, keepdims=True)
    l1 = jnp.sum(l_sc[1], axis=-1, keepdims=True)
    o = acc_sc[0] / l0 - lam * (acc_sc[1] / l1)
    ms = jnp.mean(o * o, axis=-1, keepdims=True)
    y = o * lax.rsqrt(ms + LN_EPS) * g_ref[...]
    o_ref[...] = (y * (1.0 - LAMBDA_INIT)).astype(BF16)


def _diff_attn(q, k, v, bias, lq1, lk1, lq2, lk2, g):
    t = ATT_TILE
    nq = SEQ // t
    vec = lambda b, h, i: (0, 0)
    return pl.pallas_call(
        _diff_attn_kernel,
        grid=(BATCH, N_DIFF_HEADS, nq),
        in_specs=[pl.BlockSpec((t, DIFF_V_DIM), lambda b, h, i: (b * nq + i, h)),
                  pl.BlockSpec((SEQ, DIFF_V_DIM), lambda b, h, i: (b, h)),
                  pl.BlockSpec((SEQ, DIFF_V_DIM), lambda b, h, i: (b, h)),
                  pl.BlockSpec((1, 2, t, t), lambda b, h, i: (h, 0, 0, 0)),
                  pl.BlockSpec((1, DIFF_HEAD_DIM), vec), pl.BlockSpec((1, DIFF_HEAD_DIM), vec),
                  pl.BlockSpec((1, DIFF_HEAD_DIM), vec), pl.BlockSpec((1, DIFF_HEAD_DIM), vec),
                  pl.BlockSpec((1, DIFF_V_DIM), vec)],
        out_specs=pl.BlockSpec((t, DIFF_V_DIM), lambda b, h, i: (b * nq + i, h)),
        out_shape=jax.ShapeDtypeStruct((N_TOK, DIFF_WIDTH), BF16),
        scratch_shapes=[pltpu.VMEM((2, t, LANES), F32), pltpu.VMEM((2, t, LANES), F32),
                        pltpu.VMEM((2, t, DIFF_V_DIM), F32)],
        compiler_params=_params("parallel", "parallel", "arbitrary"),
        name="diff_attn",
    )(q, k, v, bias, lq1, lk1, lq2, lk2, g)


def _kv_kernel(m_ref, w_ref, o_ref):
    o_ref[...] = _dot(m_ref[...].astype(BF16), w_ref[...]).astype(BF16)


def _kv(mem2, wkv):
    return pl.pallas_call(
        _kv_kernel,
        grid=(BATCH,),
        in_specs=[pl.BlockSpec((MEM_LEN, D_MODEL), lambda i: (i, 0)),
                  pl.BlockSpec((D_MODEL, 2 * D_MODEL), lambda i: (0, 0))],
        out_specs=pl.BlockSpec((MEM_LEN, 2 * D_MODEL), lambda i: (i, 0)),
        out_shape=jax.ShapeDtypeStruct((BATCH * MEM_LEN, 2 * D_MODEL), BF16),
        compiler_params=_params("parallel"),
        name="kv",
    )(mem2, wkv)


def _mix_ca_kernel(x_ref, ys_ref, yd_ref, g0_ref, b0_ref, wout_ref, g1_ref, b1_ref,
                   wq_ref, kv_ref, wo_ref, g2_ref, b2_ref, o_ref):
    h0 = _layer_norm(x_ref[...], g0_ref[...], b0_ref[...])
    mix = _dot(ys_ref[...], wout_ref[0:S5_WIDTH, :]) + _dot(yd_ref[...], wout_ref[S5_WIDTH:, :])
    h1 = _layer_norm(DEEPNORM_ALPHA * h0 + mix, g1_ref[...], b1_ref[...])
    h1b = h1.astype(BF16)
    heads = []
    for hd in range(CA_HEADS):
        cs = slice(hd * CA_HEAD_DIM, (hd + 1) * CA_HEAD_DIM)
        qh = (_dot(h1b, wq_ref[:, cs]) * (CA_HEAD_DIM ** -0.5)).astype(BF16)
        kh = kv_ref[:, cs]
        vh = kv_ref[:, D_MODEL + hd * CA_HEAD_DIM:D_MODEL + (hd + 1) * CA_HEAD_DIM]
        s = lax.dot_general(qh, kh, _NT, preferred_element_type=F32)
        e = jnp.exp(s - jnp.max(s, axis=-1, keepdims=True))
        oh = _dot(e.astype(BF16), vh) / jnp.sum(e, axis=-1, keepdims=True)
        heads.append(oh.astype(BF16))
    ca = _dot(jnp.concatenate(heads, axis=-1), wo_ref[...])
    o_ref[...] = _layer_norm(DEEPNORM_ALPHA * h1 + ca, g2_ref[...], b2_ref[...])


def _mix_ca(x2, ys, yd, g0, b0, wout, g1, b1, wq, kv, wo, g2, b2):
    tm = ROW_TILE
    per_batch = SEQ // tm
    row = lambda i: (i, 0)
    const = lambda i: (0, 0)
    vec = pl.BlockSpec((1, D_MODEL), const)
    sq = pl.BlockSpec((D_MODEL, D_MODEL), const)
    return pl.pallas_call(
        _mix_ca_kernel,
        grid=(N_TOK // tm,),
        in_specs=[pl.BlockSpec((tm, D_MODEL), row), pl.BlockSpec((tm, 512), row), pl.BlockSpec((tm, 512), row),
                  vec, vec, sq, vec, vec, sq,
                  pl.BlockSpec((MEM_LEN, 2 * D_MODEL), lambda i: (i // per_batch, 0)),
                  sq, vec, vec],
        out_specs=pl.BlockSpec((tm, D_MODEL), row),
        out_shape=jax.ShapeDtypeStruct((N_TOK, D_MODEL), F32),
        compiler_params=_params("parallel"),
        name="mix_ca",
    )(x2, ys, yd, g0, b0, wout, g1, b1, wq, kv, wo, g2, b2)


def _ffn_kernel(h_ref, wgu_ref, wd_ref, g_ref, b_ref, o_ref, acc):
    h = h_ref[...]
    hb = h.astype(BF16)
    ck = FFN_CHUNK
    for c in range(FFN_HIDDEN // ck):
        gate = _dot(hb, wgu_ref[:, c * ck:(c + 1) * ck])
        up = _dot(hb, wgu_ref[:, FFN_HIDDEN + c * ck:FFN_HIDDEN + (c + 1) * ck])
        act = (jax.nn.silu(gate) * up).astype(BF16)
        part = _dot(act, wd_ref[c * ck:(c + 1) * ck, :])
        if c == 0:
            acc[...] = part
        else:
            acc[...] += part
    o_ref[...] = _layer_norm(DEEPNORM_ALPHA * h + acc[...], g_ref[...], b_ref[...])


def _ffn(h2, wgu, wd, g, b):
    tm = ROW_TILE
    row = lambda i: (i, 0)
    const = lambda i: (0, 0)
    return pl.pallas_call(
        _ffn_kernel,
        grid=(N_TOK // tm,),
        in_specs=[pl.BlockSpec((tm, D_MODEL), row),
                  pl.BlockSpec((D_MODEL, 2 * FFN_HIDDEN), const),
                  pl.BlockSpec((FFN_HIDDEN, D_MODEL), const),
                  pl.BlockSpec((1, D_MODEL), const), pl.BlockSpec((1, D_MODEL), const)],
        out_specs=pl.BlockSpec((tm, D_MODEL), row),
        out_shape=jax.ShapeDtypeStruct((N_TOK, D_MODEL), F32),
        scratch_shapes=[pltpu.VMEM((tm, D_MODEL), F32)],
        compiler_params=_params("parallel"),
        name="ffn",
    )(h2, wgu, wd, g, b)


def kernel(x, mem, ln_in_g, ln_in_b, w_in, s5_lambda_re, s5_lambda_im, s5_log_dt, s5_b_re, s5_b_im,
           s5_c_re, s5_c_im, s5_d, s5_glu_w, s5_glu_b, diff_lq1, diff_lk1, diff_lq2, diff_lk2,
           diff_subln_g, rel_bias, w_out, ln1_g, ln1_b, ca_wq, ca_wkv, ca_wo, ln2_g, ln2_b,
           ffn_w_gate_up, ffn_w_down, ln3_g, ln3_b):
    vec = lambda a: a.reshape(1, -1).astype(F32)
    x2 = x.reshape(N_TOK, D_MODEL)

    u, q, k, v = _proj(x2, vec(ln_in_g), vec(ln_in_b), w_in[0].astype(BF16))

    ar, ai, bbr, bbi = _s5_prep(s5_lambda_re[0], s5_lambda_im[0], s5_log_dt[0], s5_b_re[0], s5_b_im[0])
    ar8 = jnp.broadcast_to(ar.reshape(1, S5_COLS), (BATCH, S5_COLS))
    ai8 = jnp.broadcast_to(ai.reshape(1, S5_COLS), (BATCH, S5_COLS))
    bre = _block_diag(bbr.reshape(2, 16, S5_GROUP, S5_STATE)).astype(BF16)
    bim = _block_diag(bbi.reshape(2, 16, S5_GROUP, S5_STATE)).astype(BF16)
    cre = _block_diag(jnp.transpose(s5_c_re[0], (0, 2, 1)).reshape(2, 16, S5_STATE, S5_GROUP)).astype(BF16)
    cim = _block_diag(jnp.transpose(s5_c_im[0], (0, 2, 1)).reshape(2, 16, S5_STATE, S5_GROUP)).astype(BF16)
    u_tm = jnp.transpose(u.reshape(BATCH, SEQ, S5_WIDTH), (1, 0, 2)).reshape(N_TOK, S5_WIDTH)
    y_tm = _s5(u_tm, ar8, ai8, bre, bim, cre, cim, vec(s5_d[0]), s5_glu_w[0].astype(BF16), vec(s5_glu_b[0]))
    y_s5 = jnp.transpose(y_tm.reshape(SEQ, BATCH, S5_WIDTH), (1, 0, 2)).reshape(N_TOK, S5_WIDTH)

    y_diff = _diff_attn(q, k, v, _bias_tiles(rel_bias), vec(diff_lq1[0]), vec(diff_lk1[0]),
                        vec(diff_lq2[0]), vec(diff_lk2[0]), vec(diff_subln_g[0]))

    kv = _kv(mem.reshape(BATCH * MEM_LEN, D_MODEL), ca_wkv[0].astype(BF16))
    h2 = _mix_ca(x2, y_s5, y_diff, vec(ln_in_g), vec(ln_in_b), w_out[0].astype(BF16), vec(ln1_g[0]),
                 vec(ln1_b[0]), ca_wq[0].astype(BF16), kv, ca_wo[0].astype(BF16), vec(ln2_g[0]), vec(ln2_b[0]))
    out = _ffn(h2, ffn_w_gate_up[0].astype(BF16), ffn_w_down[0].astype(BF16), vec(ln3_g[0]), vec(ln3_b[0]))
    return out.reshape(BATCH, SEQ, D_MODEL)
```

```python
import functools
import math

import jax
import jax.numpy as jnp
from jax import lax
from jax.experimental import pallas as pl
from jax.experimental.pallas import tpu as pltpu

F32 = jnp.float32
BF16 = jnp.bfloat16

D_MODEL = 1024
BATCH = 8
SEQ = 2048
N_TOK = BATCH * SEQ
MEM_LEN = 256
S5_WIDTH = 512
S5_GROUP = 16
S5_GROUPS = 32
S5_STATE = 64
S5_COLS = S5_GROUPS * S5_STATE
DIFF_WIDTH = 512
DIFF_HEAD_DIM = 64
DIFF_V_DIM = 128
N_DIFF_HEADS = 4
MIX_IN = 2048
NUM_BUCKETS = 32
MAX_DISTANCE = 128
CA_HEADS = 4
CA_HEAD_DIM = 256
FFN_HIDDEN = 2816
DEEPNORM_ALPHA = 2.0 ** 0.25
LN_EPS = 1e-5
LAMBDA_INIT = 0.8 - 0.6 * math.exp(0.0)
LOG2E = math.log2(math.e)

VMEM_LIMIT_BYTES = 56 * 1024 * 1024
MXU_DIM = 256
LANES = 128

ROW_TILE = 512
S5_STEPS = 64
S5_SCAN_CHUNK = 512
ATT_TILE = 512
FFN_CHUNK = 256
MIX_SUBTILES = 1

_NT = (((1,), (1,)), ((), ()))


def _params(*sem):
    return pltpu.CompilerParams(dimension_semantics=sem, vmem_limit_bytes=VMEM_LIMIT_BYTES)


def _layer_norm(x, g, b):
    mu = jnp.mean(x, axis=-1, keepdims=True)
    xc = x - mu
    var = jnp.mean(xc * xc, axis=-1, keepdims=True)
    return xc * lax.rsqrt(var + LN_EPS) * g + b


def _dot(a, b):
    return jnp.dot(a, b, preferred_element_type=F32)


def _s5_prep_kernel(lr_ref, li_ref, ldt_ref, br_ref, bi_ref, ar_ref, ai_ref, bbr_ref, bbi_ref):
    lr = lr_ref[...]
    li = li_ref[...]
    dt = jnp.exp(ldt_ref[...])
    mag = jnp.exp(lr * dt)
    ang = li * dt
    ar = mag * jnp.cos(ang)
    ai = mag * jnp.sin(ang)
    den = lr * lr + li * li
    nr = ar - 1.0
    fr = (nr * lr + ai * li) / den
    fi = (ai * lr - nr * li) / den
    ar_ref[...] = ar
    ai_ref[...] = ai
    b_r = br_ref[...]
    b_i = bi_ref[...]
    bbr_ref[...] = fr * b_r - fi * b_i
    bbi_ref[...] = fr * b_i + fi * b_r


def _s5_prep(lam_re, lam_im, log_dt, b_re, b_im):
    g, p, h = S5_GROUPS, S5_STATE, S5_GROUP
    out = pl.pallas_call(
        _s5_prep_kernel,
        out_shape=(jax.ShapeDtypeStruct((g, 1, p), F32), jax.ShapeDtypeStruct((g, 1, p), F32),
                   jax.ShapeDtypeStruct((g, h, p), F32), jax.ShapeDtypeStruct((g, h, p), F32)),
        name="s5_prep",
    )(lam_re.reshape(g, 1, p), lam_im.reshape(g, 1, p), log_dt.reshape(g, 1, 1),
      jnp.transpose(b_re, (0, 2, 1)), jnp.transpose(b_im, (0, 2, 1)))
    return out


def _block_diag(blocks):
    nh, ng, r, c = blocks.shape
    eye = jnp.eye(ng, dtype=blocks.dtype)
    full = blocks[:, :, :, None, :] * eye[None, :, None, :, None]
    return full.reshape(nh, ng * r, ng * c)


def _bias_prep_kernel(rel_ref, bucket_ref, o_ref):
    h = pl.program_id(0)
    bucket = bucket_ref[0]
    far = rel_ref[NUM_BUCKETS - 1, h]
    acc = jnp.zeros(bucket.shape, F32)
    for b in range(NUM_BUCKETS):
        acc = jnp.where(bucket == b, (rel_ref[b, h] - far) * LOG2E, acc)
    o_ref[0, 0] = jnp.where(bucket < 0, -jnp.inf, acc)


def _t5_bucket(dist):
    max_exact = NUM_BUCKETS // 2
    is_small = dist < max_exact
    df = jnp.maximum(dist, 1).astype(F32)
    large = max_exact + (jnp.log(df / max_exact) / math.log(MAX_DISTANCE / max_exact)
                         * (NUM_BUCKETS - max_exact)).astype(jnp.int32)
    large = jnp.minimum(large, NUM_BUCKETS - 1)
    return jnp.where(is_small, dist, large)


def _bias_tiles(rel_bias):
    t = LANES
    qpos = jnp.arange(t, dtype=jnp.int32)[:, None]
    kpos = jnp.arange(t, dtype=jnp.int32)[None, :]
    d_diag = qpos - kpos
    d_prev = d_diag + t
    bucket = jnp.stack([jnp.where(d_diag >= 0, _t5_bucket(jnp.maximum(d_diag, 0)), -1),
                        _t5_bucket(d_prev)])
    return pl.pallas_call(
        _bias_prep_kernel,
        grid=(N_DIFF_HEADS, 2),
        in_specs=[pl.BlockSpec(memory_space=pltpu.SMEM),
                  pl.BlockSpec((1, t, t), lambda h, j: (j, 0, 0))],
        out_specs=pl.BlockSpec((1, 1, t, t), lambda h, j: (h, j, 0, 0)),
        out_shape=jax.ShapeDtypeStruct((N_DIFF_HEADS, 2, t, t), F32),
        compiler_params=_params("parallel", "parallel"),
        name="bias_prep",
    )(rel_bias, bucket)


def _proj_kernel(x_ref, g_ref, b_ref, w_ref, u_ref, q_ref, k_ref, v_ref):
    h = _layer_norm(x_ref[...], g_ref[...], b_ref[...]).astype(BF16)
    u_ref[...] = _dot(h, w_ref[:, 0:512])
    q_ref[...] = (_dot(h, w_ref[:, 512:1024]) * (DIFF_HEAD_DIM ** -0.5 * LOG2E)).astype(BF16)
    k_ref[...] = _dot(h, w_ref[:, 1024:1536]).astype(BF16)
    v_ref[...] = _dot(h, w_ref[:, 1536:2048]).astype(BF16)


def _proj(x2, g, b, w):
    tm = ROW_TILE
    row = lambda i: (i, 0)
    const = lambda i: (0, 0)
    return pl.pallas_call(
        _proj_kernel,
        grid=(N_TOK // tm,),
        in_specs=[pl.BlockSpec((tm, D_MODEL), row), pl.BlockSpec((1, D_MODEL), const),
                  pl.BlockSpec((1, D_MODEL), const), pl.BlockSpec((D_MODEL, MIX_IN), const)],
        out_specs=[pl.BlockSpec((tm, 512), row)] * 4,
        out_shape=(jax.ShapeDtypeStruct((N_TOK, 512), F32),) + (jax.ShapeDtypeStruct((N_TOK, 512), BF16),) * 3,
        compiler_params=_params("parallel"),
        name="proj",
    )(x2, g, b, w)


def _s5_kernel(u_ref, ar_ref, ai_ref, bre_ref, bim_ref, cre_ref, cim_ref, d_ref, gw_ref, gb_ref,
               y_ref, sre, sim, xre, xim):
    @pl.when(pl.program_id(0) == 0)
    def _():
        xre[...] = jnp.zeros_like(xre)
        xim[...] = jnp.zeros_like(xim)

    u = u_ref[...]
    ub = u.astype(BF16)
    half = S5_COLS // 2
    for hf in range(2):
        uh = ub[:, hf * MXU_DIM:(hf + 1) * MXU_DIM]
        sre[:, hf * half:(hf + 1) * half] = _dot(uh, bre_ref[hf])
        sim[:, hf * half:(hf + 1) * half] = _dot(uh, bim_ref[hf])

    cw = S5_SCAN_CHUNK
    for c in range(S5_COLS // cw):
        cs = slice(c * cw, (c + 1) * cw)
        ar = ar_ref[:, cs]
        ai = ai_ref[:, cs]

        def body(t, carry, cs=cs, ar=ar, ai=ai):
            xr, xi = carry
            r = pl.multiple_of(t * BATCH, BATCH)
            nr = ar * xr - ai * xi + sre[pl.ds(r, BATCH), cs]
            ni = ar * xi + ai * xr + sim[pl.ds(r, BATCH), cs]
            sre[pl.ds(r, BATCH), cs] = nr
            sim[pl.ds(r, BATCH), cs] = ni
            return nr, ni

        xr, xi = lax.fori_loop(0, S5_STEPS, body, (xre[:, cs], xim[:, cs]), unroll=4)
        xre[:, cs] = xr
        xim[:, cs] = xi

    ys = []
    for hf in range(2):
        hs = slice(hf * half, (hf + 1) * half)
        ys.append(_dot(sre[:, hs].astype(BF16), cre_ref[hf]) - _dot(sim[:, hs].astype(BF16), cim_ref[hf]))
    y = jnp.concatenate(ys, axis=-1) + d_ref[...] * u
    y = jax.nn.gelu(y)
    z = _dot(y.astype(BF16), gw_ref[...]) + gb_ref[...]
    y_ref[...] = (y * jax.nn.sigmoid(z)).astype(BF16)


def _s5(u_tm, ar8, ai8, bre, bim, cre, cim, d, gw, gb):
    rows = S5_STEPS * BATCH
    row = lambda i: (i, 0)
    c2 = lambda i: (0, 0)
    c3 = lambda i: (0, 0, 0)
    half = S5_COLS // 2
    return pl.pallas_call(
        _s5_kernel,
        grid=(SEQ // S5_STEPS,),
        in_specs=[pl.BlockSpec((rows, S5_WIDTH), row),
                  pl.BlockSpec((BATCH, S5_COLS), c2), pl.BlockSpec((BATCH, S5_COLS), c2),
                  pl.BlockSpec((2, MXU_DIM, half), c3), pl.BlockSpec((2, MXU_DIM, half), c3),
                  pl.BlockSpec((2, half, MXU_DIM), c3), pl.BlockSpec((2, half, MXU_DIM), c3),
                  pl.BlockSpec((1, S5_WIDTH), c2), pl.BlockSpec((S5_WIDTH, S5_WIDTH), c2),
                  pl.BlockSpec((1, S5_WIDTH), c2)],
        out_specs=pl.BlockSpec((rows, S5_WIDTH), row),
        out_shape=jax.ShapeDtypeStruct((N_TOK, S5_WIDTH), BF16),
        scratch_shapes=[pltpu.VMEM((rows, S5_COLS), F32), pltpu.VMEM((rows, S5_COLS), F32),
                        pltpu.VMEM((BATCH, S5_COLS), F32), pltpu.VMEM((BATCH, S5_COLS), F32)],
        compiler_params=_params("arbitrary"),
        name="s5",
    )(u_tm, ar8, ai8, bre, bim, cre, cim, d, gw, gb)


def _diff_attn_kernel(q_ref, k_ref, v_ref, bias_ref, lq1_ref, lk1_ref, lq2_ref, lk2_ref, g_ref,
                      o_ref):
    t = ATT_TILE
    half = t // 2
    sb = LANES
    lam = (jnp.exp(jnp.sum(lq1_ref[...] * lk1_ref[...], axis=-1, keepdims=True))
           - jnp.exp(jnp.sum(lq2_ref[...] * lk2_ref[...], axis=-1, keepdims=True)) + LAMBDA_INIT)
    lane = lax.broadcasted_iota(jnp.int32, (t, DIFF_V_DIM), 1)
    zero = jnp.zeros((t, DIFF_V_DIM), BF16)

    def scores(qm, k0, width):
        kb = k_ref[k0:k0 + width, :]
        return [lax.dot_general(qm[mi], kb, _NT, preferred_element_type=F32) for mi in range(2)]

    def lane_groups(s, r0, k0):
        rows, width = s.shape
        groups = []
        for g in range(width // sb):
            kb = (k0 + g * sb) // sb
            col = s[:, g * sb:(g + 1) * sb]
            qbs = [(r0 + i * sb) // sb for i in range(rows // sb)]
            if all(kb < qb - 1 for qb in qbs):
                groups.append(col)
                continue
            pieces = []
            for i, qb in enumerate(qbs):
                piece = col[i * sb:(i + 1) * sb, :]
                if kb == qb:
                    piece = piece + bias_ref[0, 0]
                elif kb == qb - 1:
                    piece = piece + bias_ref[0, 1]
                elif kb > qb:
                    piece = jnp.full((sb, sb), -jnp.inf, F32)
                pieces.append(piece)
            groups.append(jnp.concatenate(pieces, axis=0))
        return groups

    def update(s, r0, k0, state):
        m_old, l_old, acc_old = state
        sg = lane_groups(s, r0, k0)
        vb = v_ref[k0:k0 + s.shape[1], :]
        row_max = jnp.max(functools.reduce(jnp.maximum, sg), axis=-1, keepdims=True)
        if m_old is None:
            m_new = jnp.broadcast_to(row_max, (s.shape[0], LANES))
            pg = [jnp.exp2(x - m_new) for x in sg]
            p = jnp.concatenate(pg, axis=-1).astype(BF16)
            return m_new, functools.reduce(jnp.add, pg), _dot(p, vb)
        m_new = jnp.maximum(m_old, row_max)
        alpha = jnp.exp2(m_old - m_new)
        pg = [jnp.exp2(x - m_new) for x in sg]
        p = jnp.concatenate(pg, axis=-1).astype(BF16)
        return m_new, alpha * l_old + functools.reduce(jnp.add, pg), alpha * acc_old + _dot(p, vb)

    def rows_of(state, lo, hi):
        return tuple(None if a is None else a[lo:hi] for a in state)

    def process(c):
        r0 = c * t
        q = q_ref[r0:r0 + t, :]
        qm = (jnp.where(lane < DIFF_HEAD_DIM, q, zero), jnp.where(lane >= DIFF_HEAD_DIM, q, zero))
        qtop = [x[:half] for x in qm]
        qbot = [x[half:] for x in qm]

        def diag_scores():
            return scores(qtop, r0, half), scores(qbot, r0, t)

        state = [(None, None, None), (None, None, None)]
        s_next = scores(qm, 0, t) if c > 0 else diag_scores()
        for j in range(c):
            s_cur = s_next
            s_next = scores(qm, (j + 1) * t, t) if j + 1 < c else diag_scores()
            state = [update(s_cur[mi], r0, j * t, state[mi]) for mi in range(2)]
        s_top, s_bot = s_next
        outs = []
        for mi in range(2):
            top = update(s_top[mi], r0, r0, rows_of(state[mi], 0, half))
            bot = update(s_bot[mi], r0 + half, r0, rows_of(state[mi], half, t))
            l = jnp.concatenate([top[1], bot[1]], axis=0)
            acc = jnp.concatenate([top[2], bot[2]], axis=0)
            outs.append(acc / jnp.sum(l, axis=-1, keepdims=True))
        o = outs[0] - lam * outs[1]
        ms = jnp.mean(o * o, axis=-1, keepdims=True)
        y = o * lax.rsqrt(ms + LN_EPS) * g_ref[...]
        o_ref[r0:r0 + t, :] = (y * (1.0 - LAMBDA_INIT)).astype(BF16)

    for c in range(SEQ // t):
        process(c)


def _diff_attn(q, k, v, bias, lq1, lk1, lq2, lk2, g):
    t = ATT_TILE
    vec = lambda b, h: (0, 0)
    seq = pl.BlockSpec((SEQ, DIFF_V_DIM), lambda b, h: (b, h))
    return pl.pallas_call(
        _diff_attn_kernel,
        grid=(BATCH, N_DIFF_HEADS),
        in_specs=[seq, seq, seq,
                  pl.BlockSpec((1, 2, LANES, LANES), lambda b, h: (h, 0, 0, 0)),
                  pl.BlockSpec((1, DIFF_HEAD_DIM), vec), pl.BlockSpec((1, DIFF_HEAD_DIM), vec),
                  pl.BlockSpec((1, DIFF_HEAD_DIM), vec), pl.BlockSpec((1, DIFF_HEAD_DIM), vec),
                  pl.BlockSpec((1, DIFF_V_DIM), vec)],
        out_specs=seq,
        out_shape=jax.ShapeDtypeStruct((N_TOK, DIFF_WIDTH), BF16),
        compiler_params=_params("parallel", "parallel"),
        name="diff_attn",
    )(q, k, v, bias, lq1, lk1, lq2, lk2, g)


def _kv_kernel(m_ref, w_ref, o_ref):
    o_ref[...] = _dot(m_ref[...].astype(BF16), w_ref[...]).astype(BF16)


def _kv(mem2, wkv):
    return pl.pallas_call(
        _kv_kernel,
        grid=(BATCH,),
        in_specs=[pl.BlockSpec((MEM_LEN, D_MODEL), lambda i: (i, 0)),
                  pl.BlockSpec((D_MODEL, 2 * D_MODEL), lambda i: (0, 0))],
        out_specs=pl.BlockSpec((MEM_LEN, 2 * D_MODEL), lambda i: (i, 0)),
        out_shape=jax.ShapeDtypeStruct((BATCH * MEM_LEN, 2 * D_MODEL), BF16),
        compiler_params=_params("parallel"),
        name="kv",
    )(mem2, wkv)


def _mix_ca_kernel(x_ref, ys_ref, yd_ref, g0_ref, b0_ref, wout_ref, g1_ref, b1_ref,
                   wq_ref, kv_ref, wo_ref, g2_ref, b2_ref, o_ref):
    sub = ROW_TILE // MIX_SUBTILES
    for st in range(MIX_SUBTILES):
        rs = slice(st * sub, (st + 1) * sub)
        h0 = _layer_norm(x_ref[rs, :], g0_ref[...], b0_ref[...])
        mix = _dot(ys_ref[rs, :], wout_ref[0:S5_WIDTH, :]) + _dot(yd_ref[rs, :], wout_ref[S5_WIDTH:, :])
        h1 = _layer_norm(DEEPNORM_ALPHA * h0 + mix, g1_ref[...], b1_ref[...])
        h1b = h1.astype(BF16)
        heads = []
        for hd in range(CA_HEADS):
            cs = slice(hd * CA_HEAD_DIM, (hd + 1) * CA_HEAD_DIM)
            qh = (_dot(h1b, wq_ref[:, cs]) * (CA_HEAD_DIM ** -0.5)).astype(BF16)
            kh = kv_ref[:, cs]
            vh = kv_ref[:, D_MODEL + hd * CA_HEAD_DIM:D_MODEL + (hd + 1) * CA_HEAD_DIM]
            s = lax.dot_general(qh, kh, _NT, preferred_element_type=F32)
            e = jnp.exp(s - jnp.max(s, axis=-1, keepdims=True))
            oh = _dot(e.astype(BF16), vh) / jnp.sum(e, axis=-1, keepdims=True)
            heads.append(oh.astype(BF16))
        ca = _dot(jnp.concatenate(heads, axis=-1), wo_ref[...])
        o_ref[rs, :] = _layer_norm(DEEPNORM_ALPHA * h1 + ca, g2_ref[...], b2_ref[...])


def _mix_ca(x2, ys, yd, g0, b0, wout, g1, b1, wq, kv, wo, g2, b2):
    tm = ROW_TILE
    per_batch = SEQ // tm
    row = lambda i: (i, 0)
    const = lambda i: (0, 0)
    vec = pl.BlockSpec((1, D_MODEL), const)
    sq = pl.BlockSpec((D_MODEL, D_MODEL), const)
    return pl.pallas_call(
        _mix_ca_kernel,
        grid=(N_TOK // tm,),
        in_specs=[pl.BlockSpec((tm, D_MODEL), row), pl.BlockSpec((tm, 512), row), pl.BlockSpec((tm, 512), row),
                  vec, vec, sq, vec, vec, sq,
                  pl.BlockSpec((MEM_LEN, 2 * D_MODEL), lambda i: (i // per_batch, 0)),
                  sq, vec, vec],
        out_specs=pl.BlockSpec((tm, D_MODEL), row),
        out_shape=jax.ShapeDtypeStruct((N_TOK, D_MODEL), F32),
        compiler_params=_params("parallel"),
        name="mix_ca",
    )(x2, ys, yd, g0, b0, wout, g1, b1, wq, kv, wo, g2, b2)


def _ffn_kernel(h_ref, wgu_ref, wd_ref, g_ref, b_ref, o_ref, acc):
    h = h_ref[...]
    hb = h.astype(BF16)
    ck = FFN_CHUNK
    for c in range(FFN_HIDDEN // ck):
        gate = _dot(hb, wgu_ref[:, c * ck:(c + 1) * ck])
        up = _dot(hb, wgu_ref[:, FFN_HIDDEN + c * ck:FFN_HIDDEN + (c + 1) * ck])
        act = (jax.nn.silu(gate) * up).astype(BF16)
        part = _dot(act, wd_ref[c * ck:(c + 1) * ck, :])
        if c == 0:
            acc[...] = part
        else:
            acc[...] += part
    o_ref[...] = _layer_norm(DEEPNORM_ALPHA * h + acc[...], g_ref[...], b_ref[...])


def _ffn(h2, wgu, wd, g, b):
    tm = ROW_TILE
    row = lambda i: (i, 0)
    const = lambda i: (0, 0)
    return pl.pallas_call(
        _ffn_kernel,
        grid=(N_TOK // tm,),
        in_specs=[pl.BlockSpec((tm, D_MODEL), row),
                  pl.BlockSpec((D_MODEL, 2 * FFN_HIDDEN), const),
                  pl.BlockSpec((FFN_HIDDEN, D_MODEL), const),
                  pl.BlockSpec((1, D_MODEL), const), pl.BlockSpec((1, D_MODEL), const)],
        out_specs=pl.BlockSpec((tm, D_MODEL), row),
        out_shape=jax.ShapeDtypeStruct((N_TOK, D_MODEL), F32),
        scratch_shapes=[pltpu.VMEM((tm, D_MODEL), F32)],
        compiler_params=_params("parallel"),
        name="ffn",
    )(h2, wgu, wd, g, b)


def kernel(x, mem, ln_in_g, ln_in_b, w_in, s5_lambda_re, s5_lambda_im, s5_log_dt, s5_b_re, s5_b_im,
           s5_c_re, s5_c_im, s5_d, s5_glu_w, s5_glu_b, diff_lq1, diff_lk1, diff_lq2, diff_lk2,
           diff_subln_g, rel_bias, w_out, ln1_g, ln1_b, ca_wq, ca_wkv, ca_wo, ln2_g, ln2_b,
           ffn_w_gate_up, ffn_w_down, ln3_g, ln3_b):
    vec = lambda a: a.reshape(1, -1).astype(F32)
    x2 = x.reshape(N_TOK, D_MODEL)

    u, q, k, v = _proj(x2, vec(ln_in_g), vec(ln_in_b), w_in[0].astype(BF16))

    ar, ai, bbr, bbi = _s5_prep(s5_lambda_re[0], s5_lambda_im[0], s5_log_dt[0], s5_b_re[0], s5_b_im[0])
    ar8 = jnp.broadcast_to(ar.reshape(1, S5_COLS), (BATCH, S5_COLS))
    ai8 = jnp.broadcast_to(ai.reshape(1, S5_COLS), (BATCH, S5_COLS))
    bre = _block_diag(bbr.reshape(2, 16, S5_GROUP, S5_STATE)).astype(BF16)
    bim = _block_diag(bbi.reshape(2, 16, S5_GROUP, S5_STATE)).astype(BF16)
    cre = _block_diag(jnp.transpose(s5_c_re[0], (0, 2, 1)).reshape(2, 16, S5_STATE, S5_GROUP)).astype(BF16)
    cim = _block_diag(jnp.transpose(s5_c_im[0], (0, 2, 1)).reshape(2, 16, S5_STATE, S5_GROUP)).astype(BF16)
    u_tm = jnp.transpose(u.reshape(BATCH, SEQ, S5_WIDTH), (1, 0, 2)).reshape(N_TOK, S5_WIDTH)
    y_tm = _s5(u_tm, ar8, ai8, bre, bim, cre, cim, vec(s5_d[0]), s5_glu_w[0].astype(BF16), vec(s5_glu_b[0]))
    y_s5 = jnp.transpose(y_tm.reshape(SEQ, BATCH, S5_WIDTH), (1, 0, 2)).reshape(N_TOK, S5_WIDTH)

    y_diff = _diff_attn(q, k, v, _bias_tiles(rel_bias), vec(diff_lq1[0]), vec(diff_lk1[0]),
                        vec(diff_lq2[0]), vec(diff_lk2[0]), vec(diff_subln_g[0]))

    kv = _kv(mem.reshape(BATCH * MEM_LEN, D_MODEL), ca_wkv[0].astype(BF16))
    h2 = _mix_ca(x2, y_s5, y_diff, vec(ln_in_g), vec(ln_in_b), w_out[0].astype(BF16), vec(ln1_g[0]),
                 vec(ln1_b[0]), ca_wq[0].astype(BF16), kv, ca_wo[0].astype(BF16), vec(ln2_g[0]), vec(ln2_b[0]))
    out = _ffn(h2, ffn_w_gate_up[0].astype(BF16), ffn_w_down[0].astype(BF16), vec(ln3_g[0]), vec(ln3_b[0]))
    return out.reshape(BATCH, SEQ, D_MODEL)
```

```python
import functools
import math

import jax
import jax.numpy as jnp
from jax import lax
from jax.experimental import pallas as pl
from jax.experimental.pallas import tpu as pltpu

F32 = jnp.float32
BF16 = jnp.bfloat16

D_MODEL = 1024
BATCH = 8
SEQ = 2048
N_TOK = BATCH * SEQ
MEM_LEN = 256
S5_WIDTH = 512
S5_GROUP = 16
S5_GROUPS = 32
S5_STATE = 64
S5_COLS = S5_GROUPS * S5_STATE
DIFF_WIDTH = 512
DIFF_HEAD_DIM = 64
DIFF_V_DIM = 128
N_DIFF_HEADS = 4
MIX_IN = 2048
NUM_BUCKETS = 32
MAX_DISTANCE = 128
CA_HEADS = 4
CA_HEAD_DIM = 256
FFN_HIDDEN = 2816
DEEPNORM_ALPHA = 2.0 ** 0.25
LN_EPS = 1e-5
LAMBDA_INIT = 0.8 - 0.6 * math.exp(0.0)
LOG2E = math.log2(math.e)

VMEM_LIMIT_BYTES = 56 * 1024 * 1024
MXU_DIM = 256
LANES = 128

ROW_TILE = 512
S5_STEPS = 64
S5_SCAN_CHUNK = 512
ATT_TILE = 512
FFN_CHUNK = 256
MIX_ROWS = 1024
MIX_SUBTILES = 2
MIX_SLICES = 4

_NT = (((1,), (1,)), ((), ()))


def _params(*sem):
    return pltpu.CompilerParams(dimension_semantics=sem, vmem_limit_bytes=VMEM_LIMIT_BYTES)


def _layer_norm(x, g, b):
    mu = jnp.mean(x, axis=-1, keepdims=True)
    xc = x - mu
    var = jnp.mean(xc * xc, axis=-1, keepdims=True)
    return xc * lax.rsqrt(var + LN_EPS) * g + b


def _dot(a, b):
    return jnp.dot(a, b, preferred_element_type=F32)


def _s5_prep_kernel(lr_ref, li_ref, ldt_ref, br_ref, bi_ref, ar_ref, ai_ref, bbr_ref, bbi_ref):
    lr = lr_ref[...]
    li = li_ref[...]
    dt = jnp.exp(ldt_ref[...])
    mag = jnp.exp(lr * dt)
    ang = li * dt
    ar = mag * jnp.cos(ang)
    ai = mag * jnp.sin(ang)
    den = lr * lr + li * li
    nr = ar - 1.0
    fr = (nr * lr + ai * li) / den
    fi = (ai * lr - nr * li) / den
    ar_ref[...] = ar
    ai_ref[...] = ai
    b_r = br_ref[...]
    b_i = bi_ref[...]
    bbr_ref[...] = fr * b_r - fi * b_i
    bbi_ref[...] = fr * b_i + fi * b_r


def _s5_prep(lam_re, lam_im, log_dt, b_re, b_im):
    g, p, h = S5_GROUPS, S5_STATE, S5_GROUP
    out = pl.pallas_call(
        _s5_prep_kernel,
        out_shape=(jax.ShapeDtypeStruct((g, 1, p), F32), jax.ShapeDtypeStruct((g, 1, p), F32),
                   jax.ShapeDtypeStruct((g, h, p), F32), jax.ShapeDtypeStruct((g, h, p), F32)),
        name="s5_prep",
    )(lam_re.reshape(g, 1, p), lam_im.reshape(g, 1, p), log_dt.reshape(g, 1, 1),
      jnp.transpose(b_re, (0, 2, 1)), jnp.transpose(b_im, (0, 2, 1)))
    return out


def _block_diag(blocks):
    nh, ng, r, c = blocks.shape
    eye = jnp.eye(ng, dtype=blocks.dtype)
    full = blocks[:, :, :, None, :] * eye[None, :, None, :, None]
    return full.reshape(nh, ng * r, ng * c)


def _bias_prep_kernel(rel_ref, bucket_ref, o_ref):
    h = pl.program_id(0)
    bucket = bucket_ref[0]
    far = rel_ref[NUM_BUCKETS - 1, h]
    acc = jnp.zeros(bucket.shape, F32)
    for b in range(NUM_BUCKETS):
        acc = jnp.where(bucket == b, (rel_ref[b, h] - far) * LOG2E, acc)
    o_ref[0, 0] = jnp.where(bucket < 0, -jnp.inf, acc)


def _t5_bucket(dist):
    max_exact = NUM_BUCKETS // 2
    is_small = dist < max_exact
    df = jnp.maximum(dist, 1).astype(F32)
    large = max_exact + (jnp.log(df / max_exact) / math.log(MAX_DISTANCE / max_exact)
                         * (NUM_BUCKETS - max_exact)).astype(jnp.int32)
    large = jnp.minimum(large, NUM_BUCKETS - 1)
    return jnp.where(is_small, dist, large)


def _bias_tiles(rel_bias):
    t = LANES
    qpos = jnp.arange(t, dtype=jnp.int32)[:, None]
    kpos = jnp.arange(t, dtype=jnp.int32)[None, :]
    d_diag = qpos - kpos
    d_prev = d_diag + t
    bucket = jnp.stack([jnp.where(d_diag >= 0, _t5_bucket(jnp.maximum(d_diag, 0)), -1),
                        _t5_bucket(d_prev)])
    return pl.pallas_call(
        _bias_prep_kernel,
        grid=(N_DIFF_HEADS, 2),
        in_specs=[pl.BlockSpec(memory_space=pltpu.SMEM),
                  pl.BlockSpec((1, t, t), lambda h, j: (j, 0, 0))],
        out_specs=pl.BlockSpec((1, 1, t, t), lambda h, j: (h, j, 0, 0)),
        out_shape=jax.ShapeDtypeStruct((N_DIFF_HEADS, 2, t, t), F32),
        compiler_params=_params("parallel", "parallel"),
        name="bias_prep",
    )(rel_bias, bucket)


def _proj_kernel(x_ref, g_ref, b_ref, w_ref, u_ref, q_ref, k_ref, v_ref):
    h = _layer_norm(x_ref[...], g_ref[...], b_ref[...]).astype(BF16)
    u_ref[...] = _dot(h, w_ref[:, 0:512])
    q_ref[...] = (_dot(h, w_ref[:, 512:1024]) * (DIFF_HEAD_DIM ** -0.5 * LOG2E)).astype(BF16)
    k_ref[...] = _dot(h, w_ref[:, 1024:1536]).astype(BF16)
    v_ref[...] = _dot(h, w_ref[:, 1536:2048]).astype(BF16)


def _proj(x2, g, b, w):
    tm = ROW_TILE
    row = lambda i: (i, 0)
    const = lambda i: (0, 0)
    return pl.pallas_call(
        _proj_kernel,
        grid=(N_TOK // tm,),
        in_specs=[pl.BlockSpec((tm, D_MODEL), row), pl.BlockSpec((1, D_MODEL), const),
                  pl.BlockSpec((1, D_MODEL), const), pl.BlockSpec((D_MODEL, MIX_IN), const)],
        out_specs=[pl.BlockSpec((tm, 512), row)] * 4,
        out_shape=(jax.ShapeDtypeStruct((N_TOK, 512), F32),) + (jax.ShapeDtypeStruct((N_TOK, 512), BF16),) * 3,
        compiler_params=_params("parallel"),
        name="proj",
    )(x2, g, b, w)


def _s5_kernel(u_ref, ar_ref, ai_ref, bre_ref, bim_ref, cre_ref, cim_ref, d_ref, gw_ref, gb_ref,
               y_ref, sre, sim, xre, xim):
    @pl.when(pl.program_id(0) == 0)
    def _():
        xre[...] = jnp.zeros_like(xre)
        xim[...] = jnp.zeros_like(xim)

    u = u_ref[...]
    ub = u.astype(BF16)
    half = S5_COLS // 2
    for hf in range(2):
        uh = ub[:, hf * MXU_DIM:(hf + 1) * MXU_DIM]
        sre[:, hf * half:(hf + 1) * half] = _dot(uh, bre_ref[hf])
        sim[:, hf * half:(hf + 1) * half] = _dot(uh, bim_ref[hf])

    cw = S5_SCAN_CHUNK
    for c in range(S5_COLS // cw):
        cs = slice(c * cw, (c + 1) * cw)
        ar = ar_ref[:, cs]
        ai = ai_ref[:, cs]

        def body(t, carry, cs=cs, ar=ar, ai=ai):
            xr, xi = carry
            r = pl.multiple_of(t * BATCH, BATCH)
            nr = ar * xr - ai * xi + sre[pl.ds(r, BATCH), cs]
            ni = ar * xi + ai * xr + sim[pl.ds(r, BATCH), cs]
            sre[pl.ds(r, BATCH), cs] = nr
            sim[pl.ds(r, BATCH), cs] = ni
            return nr, ni

        xr, xi = lax.fori_loop(0, S5_STEPS, body, (xre[:, cs], xim[:, cs]), unroll=4)
        xre[:, cs] = xr
        xim[:, cs] = xi

    ys = []
    for hf in range(2):
        hs = slice(hf * half, (hf + 1) * half)
        ys.append(_dot(sre[:, hs].astype(BF16), cre_ref[hf]) - _dot(sim[:, hs].astype(BF16), cim_ref[hf]))
    y = jnp.concatenate(ys, axis=-1) + d_ref[...] * u
    y = jax.nn.gelu(y)
    z = _dot(y.astype(BF16), gw_ref[...]) + gb_ref[...]
    y_ref[...] = (y * jax.nn.sigmoid(z)).astype(BF16)


def _s5(u_tm, ar8, ai8, bre, bim, cre, cim, d, gw, gb):
    rows = S5_STEPS * BATCH
    row = lambda i: (i, 0)
    c2 = lambda i: (0, 0)
    c3 = lambda i: (0, 0, 0)
    half = S5_COLS // 2
    return pl.pallas_call(
        _s5_kernel,
        grid=(SEQ // S5_STEPS,),
        in_specs=[pl.BlockSpec((rows, S5_WIDTH), row),
                  pl.BlockSpec((BATCH, S5_COLS), c2), pl.BlockSpec((BATCH, S5_COLS), c2),
                  pl.BlockSpec((2, MXU_DIM, half), c3), pl.BlockSpec((2, MXU_DIM, half), c3),
                  pl.BlockSpec((2, half, MXU_DIM), c3), pl.BlockSpec((2, half, MXU_DIM), c3),
                  pl.BlockSpec((1, S5_WIDTH), c2), pl.BlockSpec((S5_WIDTH, S5_WIDTH), c2),
                  pl.BlockSpec((1, S5_WIDTH), c2)],
        out_specs=pl.BlockSpec((rows, S5_WIDTH), row),
        out_shape=jax.ShapeDtypeStruct((N_TOK, S5_WIDTH), BF16),
        scratch_shapes=[pltpu.VMEM((rows, S5_COLS), F32), pltpu.VMEM((rows, S5_COLS), F32),
                        pltpu.VMEM((BATCH, S5_COLS), F32), pltpu.VMEM((BATCH, S5_COLS), F32)],
        compiler_params=_params("arbitrary"),
        name="s5",
    )(u_tm, ar8, ai8, bre, bim, cre, cim, d, gw, gb)


def _diff_attn_kernel(q_ref, k_ref, v_ref, bias_ref, lq1_ref, lk1_ref, lq2_ref, lk2_ref, g_ref,
                      o_ref):
    t = ATT_TILE
    half = t // 2
    sb = LANES
    lam = (jnp.exp(jnp.sum(lq1_ref[...] * lk1_ref[...], axis=-1, keepdims=True))
           - jnp.exp(jnp.sum(lq2_ref[...] * lk2_ref[...], axis=-1, keepdims=True)) + LAMBDA_INIT)
    lane = lax.broadcasted_iota(jnp.int32, (t, DIFF_V_DIM), 1)
    zero = jnp.zeros((t, DIFF_V_DIM), BF16)

    def scores(qm, k0, width):
        kb = k_ref[k0:k0 + width, :]
        return [lax.dot_general(qm[mi], kb, _NT, preferred_element_type=F32) for mi in range(2)]

    def lane_groups(s, r0, k0):
        rows, width = s.shape
        groups = []
        for g in range(width // sb):
            kb = (k0 + g * sb) // sb
            col = s[:, g * sb:(g + 1) * sb]
            qbs = [(r0 + i * sb) // sb for i in range(rows // sb)]
            if all(kb < qb - 1 for qb in qbs):
                groups.append(col)
                continue
            pieces = []
            for i, qb in enumerate(qbs):
                piece = col[i * sb:(i + 1) * sb, :]
                if kb == qb:
                    piece = piece + bias_ref[0, 0]
                elif kb == qb - 1:
                    piece = piece + bias_ref[0, 1]
                elif kb > qb:
                    piece = jnp.full((sb, sb), -jnp.inf, F32)
                pieces.append(piece)
            groups.append(jnp.concatenate(pieces, axis=0))
        return groups

    def update(s, r0, k0, state):
        m_old, l_old, acc_old = state
        sg = lane_groups(s, r0, k0)
        vb = v_ref[k0:k0 + s.shape[1], :]
        row_max = jnp.max(functools.reduce(jnp.maximum, sg), axis=-1, keepdims=True)
        if m_old is None:
            m_new = jnp.broadcast_to(row_max, (s.shape[0], LANES))
            pg = [jnp.exp2(x - m_new) for x in sg]
            p = jnp.concatenate(pg, axis=-1).astype(BF16)
            return m_new, functools.reduce(jnp.add, pg), _dot(p, vb)
        m_new = jnp.maximum(m_old, row_max)
        alpha = jnp.exp2(m_old - m_new)
        pg = [jnp.exp2(x - m_new) for x in sg]
        p = jnp.concatenate(pg, axis=-1).astype(BF16)
        return m_new, alpha * l_old + functools.reduce(jnp.add, pg), alpha * acc_old + _dot(p, vb)

    def rows_of(state, lo, hi):
        return tuple(None if a is None else a[lo:hi] for a in state)

    def process(c):
        r0 = c * t
        q = q_ref[r0:r0 + t, :]
        qm = (jnp.where(lane < DIFF_HEAD_DIM, q, zero), jnp.where(lane >= DIFF_HEAD_DIM, q, zero))
        qtop = [x[:half] for x in qm]
        qbot = [x[half:] for x in qm]

        def diag_scores():
            return scores(qtop, r0, half), scores(qbot, r0, t)

        state = [(None, None, None), (None, None, None)]
        s_next = scores(qm, 0, t) if c > 0 else diag_scores()
        for j in range(c):
            s_cur = s_next
            s_next = scores(qm, (j + 1) * t, t) if j + 1 < c else diag_scores()
            state = [update(s_cur[mi], r0, j * t, state[mi]) for mi in range(2)]
        s_top, s_bot = s_next
        outs = []
        for mi in range(2):
            top = update(s_top[mi], r0, r0, rows_of(state[mi], 0, half))
            bot = update(s_bot[mi], r0 + half, r0, rows_of(state[mi], half, t))
            l = jnp.concatenate([top[1], bot[1]], axis=0)
            acc = jnp.concatenate([top[2], bot[2]], axis=0)
            outs.append(acc / jnp.sum(l, axis=-1, keepdims=True))
        o = outs[0] - lam * outs[1]
        ms = jnp.mean(o * o, axis=-1, keepdims=True)
        y = o * lax.rsqrt(ms + LN_EPS) * g_ref[...]
        o_ref[r0:r0 + t, :] = (y * (1.0 - LAMBDA_INIT)).astype(BF16)

    for c in range(SEQ // t):
        process(c)


def _diff_attn(q, k, v, bias, lq1, lk1, lq2, lk2, g):
    t = ATT_TILE
    vec = lambda b, h: (0, 0)
    seq = pl.BlockSpec((SEQ, DIFF_V_DIM), lambda b, h: (b, h))
    return pl.pallas_call(
        _diff_attn_kernel,
        grid=(BATCH, N_DIFF_HEADS),
        in_specs=[seq, seq, seq,
                  pl.BlockSpec((1, 2, LANES, LANES), lambda b, h: (h, 0, 0, 0)),
                  pl.BlockSpec((1, DIFF_HEAD_DIM), vec), pl.BlockSpec((1, DIFF_HEAD_DIM), vec),
                  pl.BlockSpec((1, DIFF_HEAD_DIM), vec), pl.BlockSpec((1, DIFF_HEAD_DIM), vec),
                  pl.BlockSpec((1, DIFF_V_DIM), vec)],
        out_specs=seq,
        out_shape=jax.ShapeDtypeStruct((N_TOK, DIFF_WIDTH), BF16),
        compiler_params=_params("parallel", "parallel"),
        name="diff_attn",
    )(q, k, v, bias, lq1, lk1, lq2, lk2, g)


def _kv_kernel(m_ref, w_ref, o_ref):
    o_ref[...] = _dot(m_ref[...].astype(BF16), w_ref[...]).astype(BF16)


def _kv(mem2, wkv):
    return pl.pallas_call(
        _kv_kernel,
        grid=(BATCH,),
        in_specs=[pl.BlockSpec((MEM_LEN, D_MODEL), lambda i: (i, 0)),
                  pl.BlockSpec((D_MODEL, 2 * D_MODEL), lambda i: (0, 0))],
        out_specs=pl.BlockSpec((MEM_LEN, 2 * D_MODEL), lambda i: (i, 0)),
        out_shape=jax.ShapeDtypeStruct((BATCH * MEM_LEN, 2 * D_MODEL), BF16),
        compiler_params=_params("parallel"),
        name="kv",
    )(mem2, wkv)


def _interleave(gens, lag):
    gens = list(gens)
    live = [True] * len(gens)
    tick = 0
    while any(live):
        for i, g in enumerate(gens):
            if live[i] and tick >= lag * i:
                try:
                    next(g)
                except StopIteration:
                    live[i] = False
        tick += 1


def _mix_ca_kernel(x_ref, ys_ref, yd_ref, g0_ref, b0_ref, wout_ref, g1_ref, b1_ref,
                   wq_ref, kv_ref, wo_ref, g2_ref, b2_ref, o_ref):
    sub = MIX_ROWS // MIX_SUBTILES
    nsl = MIX_SLICES
    cw = D_MODEL // nsl
    rw = sub // nsl

    def work(st):
        r0 = st * sub
        ys = ys_ref[r0:r0 + sub, :]
        yd = yd_ref[r0:r0 + sub, :]
        mix = []
        for j in range(nsl):
            cs = slice(j * cw, (j + 1) * cw)
            mix.append(_dot(ys, wout_ref[0:S5_WIDTH, cs]) + _dot(yd, wout_ref[S5_WIDTH:, cs]))
            yield
        mix = jnp.concatenate(mix, axis=-1)
        h1 = []
        for j in range(nsl):
            rs = slice(j * rw, (j + 1) * rw)
            h0 = _layer_norm(x_ref[r0 + j * rw:r0 + (j + 1) * rw, :], g0_ref[...], b0_ref[...])
            h1.append(_layer_norm(DEEPNORM_ALPHA * h0 + mix[rs], g1_ref[...], b1_ref[...]))
            yield
        h1 = jnp.concatenate(h1, axis=0)
        h1b = h1.astype(BF16)
        qs = []
        for hd in range(CA_HEADS):
            cs = slice(hd * CA_HEAD_DIM, (hd + 1) * CA_HEAD_DIM)
            qs.append((_dot(h1b, wq_ref[:, cs]) * (CA_HEAD_DIM ** -0.5)).astype(BF16))
            yield
        heads = []
        for hd in range(CA_HEADS):
            kh = kv_ref[:, hd * CA_HEAD_DIM:(hd + 1) * CA_HEAD_DIM]
            vh = kv_ref[:, D_MODEL + hd * CA_HEAD_DIM:D_MODEL + (hd + 1) * CA_HEAD_DIM]
            s = lax.dot_general(qs[hd], kh, _NT, preferred_element_type=F32)
            e = jnp.exp(s - jnp.max(s, axis=-1, keepdims=True))
            oh = _dot(e.astype(BF16), vh) / jnp.sum(e, axis=-1, keepdims=True)
            heads.append(oh.astype(BF16))
            yield
        o = jnp.concatenate(heads, axis=-1)
        ca = []
        for j in range(nsl):
            ca.append(_dot(o, wo_ref[:, j * cw:(j + 1) * cw]))
            yield
        ca = jnp.concatenate(ca, axis=-1)
        for j in range(nsl):
            rs = slice(j * rw, (j + 1) * rw)
            o_ref[r0 + j * rw:r0 + (j + 1) * rw, :] = _layer_norm(
                DEEPNORM_ALPHA * h1[rs] + ca[rs], g2_ref[...], b2_ref[...])
            yield

    _interleave([work(st) for st in range(MIX_SUBTILES)], lag=nsl)


def _mix_ca(x2, ys, yd, g0, b0, wout, g1, b1, wq, kv, wo, g2, b2):
    tm = MIX_ROWS
    per_batch = SEQ // tm
    row = lambda i: (i, 0)
    const = lambda i: (0, 0)
    vec = pl.BlockSpec((1, D_MODEL), const)
    sq = pl.BlockSpec((D_MODEL, D_MODEL), const)
    return pl.pallas_call(
        _mix_ca_kernel,
        grid=(N_TOK // tm,),
        in_specs=[pl.BlockSpec((tm, D_MODEL), row), pl.BlockSpec((tm, 512), row), pl.BlockSpec((tm, 512), row),
                  vec, vec, sq, vec, vec, sq,
                  pl.BlockSpec((MEM_LEN, 2 * D_MODEL), lambda i: (i // per_batch, 0)),
                  sq, vec, vec],
        out_specs=pl.BlockSpec((tm, D_MODEL), row),
        out_shape=jax.ShapeDtypeStruct((N_TOK, D_MODEL), F32),
        compiler_params=_params("parallel"),
        name="mix_ca",
    )(x2, ys, yd, g0, b0, wout, g1, b1, wq, kv, wo, g2, b2)


def _ffn_kernel(h_ref, wgu_ref, wd_ref, g_ref, b_ref, o_ref, acc):
    h = h_ref[...]
    hb = h.astype(BF16)
    ck = FFN_CHUNK
    for c in range(FFN_HIDDEN // ck):
        gate = _dot(hb, wgu_ref[:, c * ck:(c + 1) * ck])
        up = _dot(hb, wgu_ref[:, FFN_HIDDEN + c * ck:FFN_HIDDEN + (c + 1) * ck])
        act = (jax.nn.silu(gate) * up).astype(BF16)
        part = _dot(act, wd_ref[c * ck:(c + 1) * ck, :])
        if c == 0:
            acc[...] = part
        else:
            acc[...] += part
    o_ref[...] = _layer_norm(DEEPNORM_ALPHA * h + acc[...], g_ref[...], b_ref[...])


def _ffn(h2, wgu, wd, g, b):
    tm = ROW_TILE
    row = lambda i: (i, 0)
    const = lambda i: (0, 0)
    return pl.pallas_call(
        _ffn_kernel,
        grid=(N_TOK // tm,),
        in_specs=[pl.BlockSpec((tm, D_MODEL), row),
                  pl.BlockSpec((D_MODEL, 2 * FFN_HIDDEN), const),
                  pl.BlockSpec((FFN_HIDDEN, D_MODEL), const),
                  pl.BlockSpec((1, D_MODEL), const), pl.BlockSpec((1, D_MODEL), const)],
        out_specs=pl.BlockSpec((tm, D_MODEL), row),
        out_shape=jax.ShapeDtypeStruct((N_TOK, D_MODEL), F32),
        scratch_shapes=[pltpu.VMEM((tm, D_MODEL), F32)],
        compiler_params=_params("parallel"),
        name="ffn",
    )(h2, wgu, wd, g, b)


def kernel(x, mem, ln_in_g, ln_in_b, w_in, s5_lambda_re, s5_lambda_im, s5_log_dt, s5_b_re, s5_b_im,
           s5_c_re, s5_c_im, s5_d, s5_glu_w, s5_glu_b, diff_lq1, diff_lk1, diff_lq2, diff_lk2,
           diff_subln_g, rel_bias, w_out, ln1_g, ln1_b, ca_wq, ca_wkv, ca_wo, ln2_g, ln2_b,
           ffn_w_gate_up, ffn_w_down, ln3_g, ln3_b):
    vec = lambda a: a.reshape(1, -1).astype(F32)
    x2 = x.reshape(N_TOK, D_MODEL)

    u, q, k, v = _proj(x2, vec(ln_in_g), vec(ln_in_b), w_in[0].astype(BF16))

    ar, ai, bbr, bbi = _s5_prep(s5_lambda_re[0], s5_lambda_im[0], s5_log_dt[0], s5_b_re[0], s5_b_im[0])
    ar8 = jnp.broadcast_to(ar.reshape(1, S5_COLS), (BATCH, S5_COLS))
    ai8 = jnp.broadcast_to(ai.reshape(1, S5_COLS), (BATCH, S5_COLS))
    bre = _block_diag(bbr.reshape(2, 16, S5_GROUP, S5_STATE)).astype(BF16)
    bim = _block_diag(bbi.reshape(2, 16, S5_GROUP, S5_STATE)).astype(BF16)
    cre = _block_diag(jnp.transpose(s5_c_re[0], (0, 2, 1)).reshape(2, 16, S5_STATE, S5_GROUP)).astype(BF16)
    cim = _block_diag(jnp.transpose(s5_c_im[0], (0, 2, 1)).reshape(2, 16, S5_STATE, S5_GROUP)).astype(BF16)
    u_tm = jnp.transpose(u.reshape(BATCH, SEQ, S5_WIDTH), (1, 0, 2)).reshape(N_TOK, S5_WIDTH)
    y_tm = _s5(u_tm, ar8, ai8, bre, bim, cre, cim, vec(s5_d[0]), s5_glu_w[0].astype(BF16), vec(s5_glu_b[0]))
    y_s5 = jnp.transpose(y_tm.reshape(SEQ, BATCH, S5_WIDTH), (1, 0, 2)).reshape(N_TOK, S5_WIDTH)

    y_diff = _diff_attn(q, k, v, _bias_tiles(rel_bias), vec(diff_lq1[0]), vec(diff_lk1[0]),
                        vec(diff_lq2[0]), vec(diff_lk2[0]), vec(diff_subln_g[0]))

    kv = _kv(mem.reshape(BATCH * MEM_LEN, D_MODEL), ca_wkv[0].astype(BF16))
    h2 = _mix_ca(x2, y_s5, y_diff, vec(ln_in_g), vec(ln_in_b), w_out[0].astype(BF16), vec(ln1_g[0]),
                 vec(ln1_b[0]), ca_wq[0].astype(BF16), kv, ca_wo[0].astype(BF16), vec(ln2_g[0]), vec(ln2_b[0]))
    out = _ffn(h2, ffn_w_gate_up[0].astype(BF16), ffn_w_down[0].astype(BF16), vec(ln3_g[0]), vec(ln3_b[0]))
    return out.reshape(BATCH, SEQ, D_MODEL)
```

```python
import functools
import math

import numpy as np

import jax
import jax.numpy as jnp
from jax import lax
from jax.experimental import pallas as pl
from jax.experimental.pallas import tpu as pltpu

F32 = jnp.float32
BF16 = jnp.bfloat16

D_MODEL = 1024
BATCH = 8
SEQ = 2048
N_TOK = BATCH * SEQ
MEM_LEN = 256
S5_WIDTH = 512
S5_GROUP = 16
S5_GROUPS = 32
S5_STATE = 64
S5_COLS = S5_GROUPS * S5_STATE
DIFF_WIDTH = 512
DIFF_HEAD_DIM = 64
DIFF_V_DIM = 128
N_DIFF_HEADS = 4
MIX_IN = 2048
NUM_BUCKETS = 32
MAX_DISTANCE = 128
CA_HEADS = 4
CA_HEAD_DIM = 256
FFN_HIDDEN = 2816
DEEPNORM_ALPHA = 2.0 ** 0.25
LN_EPS = 1e-5
LAMBDA_INIT = 0.8 - 0.6 * math.exp(0.0)
LOG2E = math.log2(math.e)

VMEM_LIMIT_BYTES = 56 * 1024 * 1024
MXU_DIM = 256
LANES = 128

ROW_TILE = 512
PROJ_ROWS = 1024
PROJ_SUBTILES = 2
S5_SUB_STEPS = 64
S5_SUBTILES = 2
S5_SCAN_CHUNK = 512
ATT_TILE = 512
FFN_CHUNK = 256
MIX_ROWS = 1024
MIX_SUBTILES = 2
MIX_SLICES = 4

_NT = (((1,), (1,)), ((), ()))


def _params(*sem):
    return pltpu.CompilerParams(dimension_semantics=sem, vmem_limit_bytes=VMEM_LIMIT_BYTES)


def _layer_norm(x, g, b):
    mu = jnp.mean(x, axis=-1, keepdims=True)
    xc = x - mu
    var = jnp.mean(xc * xc, axis=-1, keepdims=True)
    return xc * lax.rsqrt(var + LN_EPS) * g + b


def _dot(a, b):
    return jnp.dot(a, b, preferred_element_type=F32)


def _interleave(gens, lag):
    gens = list(gens)
    live = [True] * len(gens)
    tick = 0
    while any(live):
        for i, g in enumerate(gens):
            if live[i] and tick >= lag * i:
                try:
                    next(g)
                except StopIteration:
                    live[i] = False
        tick += 1


def _s5_prep_kernel(lr_ref, li_ref, ldt_ref, br_ref, bi_ref, ar_ref, ai_ref, bbr_ref, bbi_ref):
    lr = lr_ref[...]
    li = li_ref[...]
    dt = jnp.exp(ldt_ref[...])
    mag = jnp.exp(lr * dt)
    ang = li * dt
    ar = mag * jnp.cos(ang)
    ai = mag * jnp.sin(ang)
    den = lr * lr + li * li
    nr = ar - 1.0
    fr = (nr * lr + ai * li) / den
    fi = (ai * lr - nr * li) / den
    ar_ref[...] = ar
    ai_ref[...] = ai
    b_r = br_ref[...]
    b_i = bi_ref[...]
    bbr_ref[...] = fr * b_r - fi * b_i
    bbi_ref[...] = fr * b_i + fi * b_r


def _s5_prep(lam_re, lam_im, log_dt, b_re, b_im):
    g, p, h = S5_GROUPS, S5_STATE, S5_GROUP
    out = pl.pallas_call(
        _s5_prep_kernel,
        out_shape=(jax.ShapeDtypeStruct((g, 1, p), F32), jax.ShapeDtypeStruct((g, 1, p), F32),
                   jax.ShapeDtypeStruct((g, h, p), F32), jax.ShapeDtypeStruct((g, h, p), F32)),
        name="s5_prep",
    )(lam_re.reshape(g, 1, p), lam_im.reshape(g, 1, p), log_dt.reshape(g, 1, 1),
      jnp.transpose(b_re, (0, 2, 1)), jnp.transpose(b_im, (0, 2, 1)))
    return out


def _block_diag(blocks):
    nh, ng, r, c = blocks.shape
    eye = jnp.eye(ng, dtype=blocks.dtype)
    full = blocks[:, :, :, None, :] * eye[None, :, None, :, None]
    return full.reshape(nh, ng * r, ng * c)


def _bias_prep_kernel(rel_ref, bucket_ref, o_ref):
    h = pl.program_id(0)
    bucket = bucket_ref[0]
    far = rel_ref[NUM_BUCKETS - 1, h]
    acc = jnp.zeros(bucket.shape, F32)
    for b in range(NUM_BUCKETS):
        acc = jnp.where(bucket == b, (rel_ref[b, h] - far) * LOG2E, acc)
    o_ref[0, 0] = jnp.where(bucket < 0, -jnp.inf, acc)


def _t5_bucket(dist):
    max_exact = NUM_BUCKETS // 2
    df = np.maximum(dist, 1).astype(np.float32)
    large = max_exact + (np.log(df / np.float32(max_exact)) / np.float32(math.log(MAX_DISTANCE / max_exact))
                         * np.float32(NUM_BUCKETS - max_exact)).astype(np.int32)
    large = np.minimum(large, NUM_BUCKETS - 1)
    return np.where(dist < max_exact, dist, large).astype(np.int32)


def _bias_tiles(rel_bias):
    t = LANES
    qpos = np.arange(t, dtype=np.int32)[:, None]
    kpos = np.arange(t, dtype=np.int32)[None, :]
    d_diag = qpos - kpos
    d_prev = d_diag + t
    bucket = jnp.asarray(np.stack([np.where(d_diag >= 0, _t5_bucket(np.maximum(d_diag, 0)), -1),
                                   _t5_bucket(d_prev)]).astype(np.int32))
    return pl.pallas_call(
        _bias_prep_kernel,
        grid=(N_DIFF_HEADS, 2),
        in_specs=[pl.BlockSpec(memory_space=pltpu.SMEM),
                  pl.BlockSpec((1, t, t), lambda h, j: (j, 0, 0))],
        out_specs=pl.BlockSpec((1, 1, t, t), lambda h, j: (h, j, 0, 0)),
        out_shape=jax.ShapeDtypeStruct((N_DIFF_HEADS, 2, t, t), F32),
        compiler_params=_params("parallel", "parallel"),
        name="bias_prep",
    )(rel_bias, bucket)


def _proj_kernel(x_ref, g_ref, b_ref, w_ref, u_ref, q_ref, k_ref, v_ref):
    sub = PROJ_ROWS // PROJ_SUBTILES
    nsl = 4
    rw = sub // nsl

    def work(st):
        r0 = st * sub
        h = []
        for j in range(nsl):
            h.append(_layer_norm(x_ref[r0 + j * rw:r0 + (j + 1) * rw, :], g_ref[...], b_ref[...]).astype(BF16))
            yield
        h = jnp.concatenate(h, axis=0)
        rs = slice(r0, r0 + sub)
        u_ref[rs, :] = _dot(h, w_ref[:, 0:512])
        yield
        q_ref[rs, :] = (_dot(h, w_ref[:, 512:1024]) * (DIFF_HEAD_DIM ** -0.5 * LOG2E)).astype(BF16)
        yield
        k_ref[rs, :] = _dot(h, w_ref[:, 1024:1536]).astype(BF16)
        yield
        v_ref[rs, :] = _dot(h, w_ref[:, 1536:2048]).astype(BF16)
        yield

    _interleave([work(st) for st in range(PROJ_SUBTILES)], lag=nsl)


def _proj(x2, g, b, w):
    tm = PROJ_ROWS
    row = lambda i: (i, 0)
    const = lambda i: (0, 0)
    return pl.pallas_call(
        _proj_kernel,
        grid=(N_TOK // tm,),
        in_specs=[pl.BlockSpec((tm, D_MODEL), row), pl.BlockSpec((1, D_MODEL), const),
                  pl.BlockSpec((1, D_MODEL), const), pl.BlockSpec((D_MODEL, MIX_IN), const)],
        out_specs=[pl.BlockSpec((tm, 512), row)] * 4,
        out_shape=(jax.ShapeDtypeStruct((N_TOK, 512), F32),) + (jax.ShapeDtypeStruct((N_TOK, 512), BF16),) * 3,
        compiler_params=_params("parallel"),
        name="proj",
    )(x2, g, b, w)


def _s5_kernel(u_ref, ar_ref, ai_ref, bre_ref, bim_ref, cre_ref, cim_ref, d_ref, gw_ref, gb_ref,
               y_ref, sre, sim, xre, xim, utm, ytm):
    @pl.when(pl.program_id(0) == 0)
    def _():
        xre[...] = jnp.zeros_like(xre)
        xim[...] = jnp.zeros_like(xim)

    half = S5_COLS // 2
    cw = S5_SCAN_CHUNK
    nchunk = S5_COLS // cw
    rows = S5_SUB_STEPS * BATCH
    steps = S5_SUBTILES * S5_SUB_STEPS
    nslab = S5_WIDTH // LANES
    carry = [None] * nchunk

    for b in range(BATCH):
        for sl in range(nslab):
            utm[sl, pl.ds(b, steps, stride=BATCH), :] = u_ref[b, :, sl * LANES:(sl + 1) * LANES]

    def work(st):
        r0 = st * rows
        u = jnp.concatenate([utm[sl, r0:r0 + rows, :] for sl in range(nslab)], axis=-1)
        ub = u.astype(BF16)
        for hf in range(2):
            uh = ub[:, hf * MXU_DIM:(hf + 1) * MXU_DIM]
            sre[r0:r0 + rows, hf * half:(hf + 1) * half] = _dot(uh, bre_ref[hf])
            yield
            sim[r0:r0 + rows, hf * half:(hf + 1) * half] = _dot(uh, bim_ref[hf])
            yield
        for c in range(nchunk):
            cs = slice(c * cw, (c + 1) * cw)
            ar = ar_ref[:, cs]
            ai = ai_ref[:, cs]
            xr, xi = (xre[:, cs], xim[:, cs]) if st == 0 else carry[c]
            for t in range(S5_SUB_STEPS):
                r = r0 + t * BATCH
                nr = ar * xr - ai * xi + sre[r:r + BATCH, cs]
                ni = ar * xi + ai * xr + sim[r:r + BATCH, cs]
                sre[r:r + BATCH, cs] = nr
                sim[r:r + BATCH, cs] = ni
                xr, xi = nr, ni
            carry[c] = (xr, xi)
            if st == S5_SUBTILES - 1:
                xre[:, cs] = xr
                xim[:, cs] = xi
            yield
        ys = []
        for hf in range(2):
            hs = slice(hf * half, (hf + 1) * half)
            ys.append(_dot(sre[r0:r0 + rows, hs].astype(BF16), cre_ref[hf])
                      - _dot(sim[r0:r0 + rows, hs].astype(BF16), cim_ref[hf]))
            yield
        y = jax.nn.gelu(jnp.concatenate(ys, axis=-1) + d_ref[...] * u)
        yield
        z = _dot(y.astype(BF16), gw_ref[...]) + gb_ref[...]
        y = y * jax.nn.sigmoid(z)
        for sl in range(nslab):
            ytm[sl, r0:r0 + rows, :] = y[:, sl * LANES:(sl + 1) * LANES]
        t0 = st * S5_SUB_STEPS
        for b in range(BATCH):
            y_ref[b, t0:t0 + S5_SUB_STEPS, :] = jnp.concatenate(
                [ytm[sl, pl.ds(r0 + b, S5_SUB_STEPS, stride=BATCH), :] for sl in range(nslab)],
                axis=-1).astype(BF16)
        yield

    _interleave([work(st) for st in range(S5_SUBTILES)], lag=4)


def _s5(u, ar8, ai8, bre, bim, cre, cim, d, gw, gb):
    steps = S5_SUBTILES * S5_SUB_STEPS
    rows = steps * BATCH
    nslab = S5_WIDTH // LANES
    blk = pl.BlockSpec((BATCH, steps, S5_WIDTH), lambda i: (0, i, 0))
    c2 = lambda i: (0, 0)
    c3 = lambda i: (0, 0, 0)
    half = S5_COLS // 2
    return pl.pallas_call(
        _s5_kernel,
        grid=(SEQ // steps,),
        in_specs=[blk,
                  pl.BlockSpec((BATCH, S5_COLS), c2), pl.BlockSpec((BATCH, S5_COLS), c2),
                  pl.BlockSpec((2, MXU_DIM, half), c3), pl.BlockSpec((2, MXU_DIM, half), c3),
                  pl.BlockSpec((2, half, MXU_DIM), c3), pl.BlockSpec((2, half, MXU_DIM), c3),
                  pl.BlockSpec((1, S5_WIDTH), c2), pl.BlockSpec((S5_WIDTH, S5_WIDTH), c2),
                  pl.BlockSpec((1, S5_WIDTH), c2)],
        out_specs=blk,
        out_shape=jax.ShapeDtypeStruct((BATCH, SEQ, S5_WIDTH), BF16),
        scratch_shapes=[pltpu.VMEM((rows, S5_COLS), F32), pltpu.VMEM((rows, S5_COLS), F32),
                        pltpu.VMEM((BATCH, S5_COLS), F32), pltpu.VMEM((BATCH, S5_COLS), F32),
                        pltpu.VMEM((nslab, rows, LANES), F32), pltpu.VMEM((nslab, rows, LANES), F32)],
        compiler_params=_params("arbitrary"),
        name="s5",
    )(u, ar8, ai8, bre, bim, cre, cim, d, gw, gb)


def _diff_attn_kernel(q_ref, k_ref, v_ref, bias_ref, lq1_ref, lk1_ref, lq2_ref, lk2_ref, g_ref,
                      o_ref):
    t = ATT_TILE
    half = t // 2
    sb = LANES
    lam = (jnp.exp(jnp.sum(lq1_ref[...] * lk1_ref[...], axis=-1, keepdims=True))
           - jnp.exp(jnp.sum(lq2_ref[...] * lk2_ref[...], axis=-1, keepdims=True)) + LAMBDA_INIT)
    lane = lax.broadcasted_iota(jnp.int32, (t, DIFF_V_DIM), 1)
    zero = jnp.zeros((t, DIFF_V_DIM), BF16)

    def scores(qm, k0, width):
        kb = k_ref[k0:k0 + width, :]
        return [lax.dot_general(qm[mi], kb, _NT, preferred_element_type=F32) for mi in range(2)]

    def lane_groups(s, r0, k0):
        rows, width = s.shape
        groups = []
        for g in range(width // sb):
            kb = (k0 + g * sb) // sb
            col = s[:, g * sb:(g + 1) * sb]
            qbs = [(r0 + i * sb) // sb for i in range(rows // sb)]
            if all(kb < qb - 1 for qb in qbs):
                groups.append(col)
                continue
            pieces = []
            for i, qb in enumerate(qbs):
                piece = col[i * sb:(i + 1) * sb, :]
                if kb == qb:
                    piece = piece + bias_ref[0, 0]
                elif kb == qb - 1:
                    piece = piece + bias_ref[0, 1]
                elif kb > qb:
                    piece = jnp.full((sb, sb), -jnp.inf, F32)
                pieces.append(piece)
            groups.append(jnp.concatenate(pieces, axis=0))
        return groups

    def update(s, r0, k0, state):
        m_old, l_old, acc_old = state
        sg = lane_groups(s, r0, k0)
        vb = v_ref[k0:k0 + s.shape[1], :]
        row_max = jnp.max(functools.reduce(jnp.maximum, sg), axis=-1, keepdims=True)
        if m_old is None:
            m_new = jnp.broadcast_to(row_max, (s.shape[0], LANES))
            pg = [jnp.exp2(x - m_new) for x in sg]
            p = jnp.concatenate(pg, axis=-1).astype(BF16)
            return m_new, functools.reduce(jnp.add, pg), _dot(p, vb)
        m_new = jnp.maximum(m_old, row_max)
        alpha = jnp.exp2(m_old - m_new)
        pg = [jnp.exp2(x - m_new) for x in sg]
        p = jnp.concatenate(pg, axis=-1).astype(BF16)
        return m_new, alpha * l_old + functools.reduce(jnp.add, pg), alpha * acc_old + _dot(p, vb)

    def rows_of(state, lo, hi):
        return tuple(None if a is None else a[lo:hi] for a in state)

    def process(c):
        r0 = c * t
        q = q_ref[r0:r0 + t, :]
        qm = (jnp.where(lane < DIFF_HEAD_DIM, q, zero), jnp.where(lane >= DIFF_HEAD_DIM, q, zero))
        qtop = [x[:half] for x in qm]
        qbot = [x[half:] for x in qm]

        def diag_scores():
            return scores(qtop, r0, half), scores(qbot, r0, t)

        state = [(None, None, None), (None, None, None)]
        s_next = scores(qm, 0, t) if c > 0 else diag_scores()
        for j in range(c):
            s_cur = s_next
            s_next = scores(qm, (j + 1) * t, t) if j + 1 < c else diag_scores()
            state = [update(s_cur[mi], r0, j * t, state[mi]) for mi in range(2)]
        s_top, s_bot = s_next
        outs = []
        for mi in range(2):
            top = update(s_top[mi], r0, r0, rows_of(state[mi], 0, half))
            bot = update(s_bot[mi], r0 + half, r0, rows_of(state[mi], half, t))
            l = jnp.concatenate([top[1], bot[1]], axis=0)
            acc = jnp.concatenate([top[2], bot[2]], axis=0)
            outs.append(acc / jnp.sum(l, axis=-1, keepdims=True))
        o = outs[0] - lam * outs[1]
        ms = jnp.mean(o * o, axis=-1, keepdims=True)
        y = o * lax.rsqrt(ms + LN_EPS) * g_ref[...]
        o_ref[r0:r0 + t, :] = (y * (1.0 - LAMBDA_INIT)).astype(BF16)

    for c in range(SEQ // t):
        process(c)


def _diff_attn(q, k, v, bias, lq1, lk1, lq2, lk2, g):
    t = ATT_TILE
    vec = lambda b, h: (0, 0)
    seq = pl.BlockSpec((SEQ, DIFF_V_DIM), lambda b, h: (b, h))
    return pl.pallas_call(
        _diff_attn_kernel,
        grid=(BATCH, N_DIFF_HEADS),
        in_specs=[seq, seq, seq,
                  pl.BlockSpec((1, 2, LANES, LANES), lambda b, h: (h, 0, 0, 0)),
                  pl.BlockSpec((1, DIFF_HEAD_DIM), vec), pl.BlockSpec((1, DIFF_HEAD_DIM), vec),
                  pl.BlockSpec((1, DIFF_HEAD_DIM), vec), pl.BlockSpec((1, DIFF_HEAD_DIM), vec),
                  pl.BlockSpec((1, DIFF_V_DIM), vec)],
        out_specs=seq,
        out_shape=jax.ShapeDtypeStruct((N_TOK, DIFF_WIDTH), BF16),
        compiler_params=_params("parallel", "parallel"),
        name="diff_attn",
    )(q, k, v, bias, lq1, lk1, lq2, lk2, g)


def _kv_kernel(m_ref, w_ref, o_ref):
    o_ref[...] = _dot(m_ref[...].astype(BF16), w_ref[...]).astype(BF16)


def _kv(mem2, wkv):
    return pl.pallas_call(
        _kv_kernel,
        grid=(BATCH,),
        in_specs=[pl.BlockSpec((MEM_LEN, D_MODEL), lambda i: (i, 0)),
                  pl.BlockSpec((D_MODEL, 2 * D_MODEL), lambda i: (0, 0))],
        out_specs=pl.BlockSpec((MEM_LEN, 2 * D_MODEL), lambda i: (i, 0)),
        out_shape=jax.ShapeDtypeStruct((BATCH * MEM_LEN, 2 * D_MODEL), BF16),
        compiler_params=_params("parallel"),
        name="kv",
    )(mem2, wkv)


def _mix_ca_kernel(x_ref, ys_ref, yd_ref, g0_ref, b0_ref, wout_ref, g1_ref, b1_ref,
                   wq_ref, kv_ref, wo_ref, g2_ref, b2_ref, o_ref):
    sub = MIX_ROWS // MIX_SUBTILES
    nsl = MIX_SLICES
    cw = D_MODEL // nsl
    rw = sub // nsl

    def work(st):
        r0 = st * sub
        ys = ys_ref[r0:r0 + sub, :]
        yd = yd_ref[r0:r0 + sub, :]
        mix = []
        for j in range(nsl):
            cs = slice(j * cw, (j + 1) * cw)
            mix.append(_dot(ys, wout_ref[0:S5_WIDTH, cs]) + _dot(yd, wout_ref[S5_WIDTH:, cs]))
            yield
        mix = jnp.concatenate(mix, axis=-1)
        h1 = []
        for j in range(nsl):
            rs = slice(j * rw, (j + 1) * rw)
            h0 = _layer_norm(x_ref[r0 + j * rw:r0 + (j + 1) * rw, :], g0_ref[...], b0_ref[...])
            h1.append(_layer_norm(DEEPNORM_ALPHA * h0 + mix[rs], g1_ref[...], b1_ref[...]))
            yield
        h1 = jnp.concatenate(h1, axis=0)
        h1b = h1.astype(BF16)
        qs = []
        for hd in range(CA_HEADS):
            cs = slice(hd * CA_HEAD_DIM, (hd + 1) * CA_HEAD_DIM)
            qs.append((_dot(h1b, wq_ref[:, cs]) * (CA_HEAD_DIM ** -0.5)).astype(BF16))
            yield
        heads = []
        for hd in range(CA_HEADS):
            kh = kv_ref[:, hd * CA_HEAD_DIM:(hd + 1) * CA_HEAD_DIM]
            vh = kv_ref[:, D_MODEL + hd * CA_HEAD_DIM:D_MODEL + (hd + 1) * CA_HEAD_DIM]
            s = lax.dot_general(qs[hd], kh, _NT, preferred_element_type=F32)
            e = jnp.exp(s - jnp.max(s, axis=-1, keepdims=True))
            oh = _dot(e.astype(BF16), vh) / jnp.sum(e, axis=-1, keepdims=True)
            heads.append(oh.astype(BF16))
            yield
        o = jnp.concatenate(heads, axis=-1)
        ca = []
        for j in range(nsl):
            ca.append(_dot(o, wo_ref[:, j * cw:(j + 1) * cw]))
            yield
        ca = jnp.concatenate(ca, axis=-1)
        for j in range(nsl):
            rs = slice(j * rw, (j + 1) * rw)
            o_ref[r0 + j * rw:r0 + (j + 1) * rw, :] = _layer_norm(
                DEEPNORM_ALPHA * h1[rs] + ca[rs], g2_ref[...], b2_ref[...])
            yield

    _interleave([work(st) for st in range(MIX_SUBTILES)], lag=nsl)


def _mix_ca(x2, ys, yd, g0, b0, wout, g1, b1, wq, kv, wo, g2, b2):
    tm = MIX_ROWS
    per_batch = SEQ // tm
    row = lambda i: (i, 0)
    const = lambda i: (0, 0)
    vec = pl.BlockSpec((1, D_MODEL), const)
    sq = pl.BlockSpec((D_MODEL, D_MODEL), const)
    return pl.pallas_call(
        _mix_ca_kernel,
        grid=(N_TOK // tm,),
        in_specs=[pl.BlockSpec((tm, D_MODEL), row), pl.BlockSpec((tm, 512), row), pl.BlockSpec((tm, 512), row),
                  vec, vec, sq, vec, vec, sq,
                  pl.BlockSpec((MEM_LEN, 2 * D_MODEL), lambda i: (i // per_batch, 0)),
                  sq, vec, vec],
        out_specs=pl.BlockSpec((tm, D_MODEL), row),
        out_shape=jax.ShapeDtypeStruct((N_TOK, D_MODEL), F32),
        compiler_params=_params("parallel"),
        name="mix_ca",
    )(x2, ys, yd, g0, b0, wout, g1, b1, wq, kv, wo, g2, b2)


def _ffn_kernel(h_ref, wgu_ref, wd_ref, g_ref, b_ref, o_ref, acc):
    h = h_ref[...]
    hb = h.astype(BF16)
    ck = FFN_CHUNK
    for c in range(FFN_HIDDEN // ck):
        gate = _dot(hb, wgu_ref[:, c * ck:(c + 1) * ck])
        up = _dot(hb, wgu_ref[:, FFN_HIDDEN + c * ck:FFN_HIDDEN + (c + 1) * ck])
        act = (jax.nn.silu(gate) * up).astype(BF16)
        part = _dot(act, wd_ref[c * ck:(c + 1) * ck, :])
        if c == 0:
            acc[...] = part
        else:
            acc[...] += part
    o_ref[...] = _layer_norm(DEEPNORM_ALPHA * h + acc[...], g_ref[...], b_ref[...])


def _ffn(h2, wgu, wd, g, b):
    tm = ROW_TILE
    row = lambda i: (i, 0)
    const = lambda i: (0, 0)
    return pl.pallas_call(
        _ffn_kernel,
        grid=(N_TOK // tm,),
        in_specs=[pl.BlockSpec((tm, D_MODEL), row),
                  pl.BlockSpec((D_MODEL, 2 * FFN_HIDDEN), const),
                  pl.BlockSpec((FFN_HIDDEN, D_MODEL), const),
                  pl.BlockSpec((1, D_MODEL), const), pl.BlockSpec((1, D_MODEL), const)],
        out_specs=pl.BlockSpec((tm, D_MODEL), row),
        out_shape=jax.ShapeDtypeStruct((N_TOK, D_MODEL), F32),
        scratch_shapes=[pltpu.VMEM((tm, D_MODEL), F32)],
        compiler_params=_params("parallel"),
        name="ffn",
    )(h2, wgu, wd, g, b)


def kernel(x, mem, ln_in_g, ln_in_b, w_in, s5_lambda_re, s5_lambda_im, s5_log_dt, s5_b_re, s5_b_im,
           s5_c_re, s5_c_im, s5_d, s5_glu_w, s5_glu_b, diff_lq1, diff_lk1, diff_lq2, diff_lk2,
           diff_subln_g, rel_bias, w_out, ln1_g, ln1_b, ca_wq, ca_wkv, ca_wo, ln2_g, ln2_b,
           ffn_w_gate_up, ffn_w_down, ln3_g, ln3_b):
    vec = lambda a: a.reshape(1, -1).astype(F32)
    x2 = x.reshape(N_TOK, D_MODEL)

    u, q, k, v = _proj(x2, vec(ln_in_g), vec(ln_in_b), w_in[0].astype(BF16))

    ar, ai, bbr, bbi = _s5_prep(s5_lambda_re[0], s5_lambda_im[0], s5_log_dt[0], s5_b_re[0], s5_b_im[0])
    ar8 = jnp.broadcast_to(ar.reshape(1, S5_COLS), (BATCH, S5_COLS))
    ai8 = jnp.broadcast_to(ai.reshape(1, S5_COLS), (BATCH, S5_COLS))
    bre = _block_diag(bbr.reshape(2, 16, S5_GROUP, S5_STATE)).astype(BF16)
    bim = _block_diag(bbi.reshape(2, 16, S5_GROUP, S5_STATE)).astype(BF16)
    cre = _block_diag(jnp.transpose(s5_c_re[0], (0, 2, 1)).reshape(2, 16, S5_STATE, S5_GROUP)).astype(BF16)
    cim = _block_diag(jnp.transpose(s5_c_im[0], (0, 2, 1)).reshape(2, 16, S5_STATE, S5_GROUP)).astype(BF16)
    y_s5 = _s5(u.reshape(BATCH, SEQ, S5_WIDTH), ar8, ai8, bre, bim, cre, cim, vec(s5_d[0]),
               s5_glu_w[0].astype(BF16), vec(s5_glu_b[0])).reshape(N_TOK, S5_WIDTH)

    y_diff = _diff_attn(q, k, v, _bias_tiles(rel_bias), vec(diff_lq1[0]), vec(diff_lk1[0]),
                        vec(diff_lq2[0]), vec(diff_lk2[0]), vec(diff_subln_g[0]))

    kv = _kv(mem.reshape(BATCH * MEM_LEN, D_MODEL), ca_wkv[0].astype(BF16))
    h2 = _mix_ca(x2, y_s5, y_diff, vec(ln_in_g), vec(ln_in_b), w_out[0].astype(BF16), vec(ln1_g[0]),
                 vec(ln1_b[0]), ca_wq[0].astype(BF16), kv, ca_wo[0].astype(BF16), vec(ln2_g[0]), vec(ln2_b[0]))
    out = _ffn(h2, ffn_w_gate_up[0].astype(BF16), ffn_w_down[0].astype(BF16), vec(ln3_g[0]), vec(ln3_b[0]))
    return out.reshape(BATCH, SEQ, D_MODEL)
```

```python
import functools
import math

import numpy as np

import jax
import jax.numpy as jnp
from jax import lax
from jax.experimental import pallas as pl
from jax.experimental.pallas import tpu as pltpu

F32 = jnp.float32
BF16 = jnp.bfloat16

D_MODEL = 1024
BATCH = 8
SEQ = 2048
N_TOK = BATCH * SEQ
MEM_LEN = 256
S5_WIDTH = 512
S5_GROUP = 16
S5_GROUPS = 32
S5_STATE = 64
S5_COLS = S5_GROUPS * S5_STATE
DIFF_WIDTH = 512
DIFF_HEAD_DIM = 64
DIFF_V_DIM = 128
N_DIFF_HEADS = 4
MIX_IN = 2048
NUM_BUCKETS = 32
MAX_DISTANCE = 128
CA_HEADS = 4
CA_HEAD_DIM = 256
FFN_HIDDEN = 2816
DEEPNORM_ALPHA = 2.0 ** 0.25
LN_EPS = 1e-5
LAMBDA_INIT = 0.8 - 0.6 * math.exp(0.0)
LOG2E = math.log2(math.e)

VMEM_LIMIT_BYTES = 56 * 1024 * 1024
MXU_DIM = 256
LANES = 128

PROJ_ROWS = 1024
PROJ_SUBTILES = 2
S5_SUB_STEPS = 64
S5_SUBTILES = 2
S5_SCAN_CHUNK = 512
ATT_TILE = 512
ATT_HEADS = 1
FFN_CHUNK = 256
FFN_ROWS = 1024
FFN_SUBTILES = 2
MIX_ROWS = 1024
MIX_SUBTILES = 2
MIX_SLICES = 4

_NT = (((1,), (1,)), ((), ()))


def _params(*sem):
    return pltpu.CompilerParams(dimension_semantics=sem, vmem_limit_bytes=VMEM_LIMIT_BYTES)


def _layer_norm(x, g, b):
    mu = jnp.mean(x, axis=-1, keepdims=True)
    xc = x - mu
    var = jnp.mean(xc * xc, axis=-1, keepdims=True)
    return xc * lax.rsqrt(var + LN_EPS) * g + b


def _dot(a, b):
    return jnp.dot(a, b, preferred_element_type=F32)


def _interleave(gens, lag):
    gens = list(gens)
    live = [True] * len(gens)
    tick = 0
    while any(live):
        for i, g in enumerate(gens):
            if live[i] and tick >= lag * i:
                try:
                    next(g)
                except StopIteration:
                    live[i] = False
        tick += 1


def _s5_prep_kernel(lr_ref, li_ref, ldt_ref, br_ref, bi_ref, ar_ref, ai_ref, bbr_ref, bbi_ref):
    lr = lr_ref[...]
    li = li_ref[...]
    dt = jnp.exp(ldt_ref[...])
    mag = jnp.exp(lr * dt)
    ang = li * dt
    ar = mag * jnp.cos(ang)
    ai = mag * jnp.sin(ang)
    den = lr * lr + li * li
    nr = ar - 1.0
    fr = (nr * lr + ai * li) / den
    fi = (ai * lr - nr * li) / den
    ar_ref[...] = ar
    ai_ref[...] = ai
    b_r = br_ref[...]
    b_i = bi_ref[...]
    bbr_ref[...] = fr * b_r - fi * b_i
    bbi_ref[...] = fr * b_i + fi * b_r


def _s5_prep(lam_re, lam_im, log_dt, b_re, b_im):
    g, p, h = S5_GROUPS, S5_STATE, S5_GROUP
    out = pl.pallas_call(
        _s5_prep_kernel,
        out_shape=(jax.ShapeDtypeStruct((g, 1, p), F32), jax.ShapeDtypeStruct((g, 1, p), F32),
                   jax.ShapeDtypeStruct((g, h, p), F32), jax.ShapeDtypeStruct((g, h, p), F32)),
        name="s5_prep",
    )(lam_re.reshape(g, 1, p), lam_im.reshape(g, 1, p), log_dt.reshape(g, 1, 1),
      jnp.transpose(b_re, (0, 2, 1)), jnp.transpose(b_im, (0, 2, 1)))
    return out


def _block_diag(blocks):
    nh, ng, r, c = blocks.shape
    eye = jnp.eye(ng, dtype=blocks.dtype)
    full = blocks[:, :, :, None, :] * eye[None, :, None, :, None]
    return full.reshape(nh, ng * r, ng * c)


def _bias_prep_kernel(rel_ref, bucket_ref, o_ref):
    h = pl.program_id(0)
    bucket = bucket_ref[0]
    far = rel_ref[NUM_BUCKETS - 1, h]
    acc = jnp.zeros(bucket.shape, F32)
    for b in range(NUM_BUCKETS):
        acc = jnp.where(bucket == b, (rel_ref[b, h] - far) * LOG2E, acc)
    o_ref[0, 0] = jnp.where(bucket < 0, -jnp.inf, acc)


def _t5_bucket(dist):
    max_exact = NUM_BUCKETS // 2
    df = np.maximum(dist, 1).astype(np.float32)
    large = max_exact + (np.log(df / np.float32(max_exact)) / np.float32(math.log(MAX_DISTANCE / max_exact))
                         * np.float32(NUM_BUCKETS - max_exact)).astype(np.int32)
    large = np.minimum(large, NUM_BUCKETS - 1)
    return np.where(dist < max_exact, dist, large).astype(np.int32)


def _bias_tiles(rel_bias):
    t = LANES
    qpos = np.arange(t, dtype=np.int32)[:, None]
    kpos = np.arange(t, dtype=np.int32)[None, :]
    d_diag = qpos - kpos
    d_prev = d_diag + t
    bucket = jnp.asarray(np.stack([np.where(d_diag >= 0, _t5_bucket(np.maximum(d_diag, 0)), -1),
                                   _t5_bucket(d_prev)]).astype(np.int32))
    return pl.pallas_call(
        _bias_prep_kernel,
        grid=(N_DIFF_HEADS, 2),
        in_specs=[pl.BlockSpec(memory_space=pltpu.SMEM),
                  pl.BlockSpec((1, t, t), lambda h, j: (j, 0, 0))],
        out_specs=pl.BlockSpec((1, 1, t, t), lambda h, j: (h, j, 0, 0)),
        out_shape=jax.ShapeDtypeStruct((N_DIFF_HEADS, 2, t, t), F32),
        compiler_params=_params("parallel", "parallel"),
        name="bias_prep",
    )(rel_bias, bucket)


def _proj_kernel(x_ref, g_ref, b_ref, w_ref, u_ref, q_ref, k_ref, v_ref):
    sub = PROJ_ROWS // PROJ_SUBTILES
    nsl = 4
    rw = sub // nsl

    def work(st):
        r0 = st * sub
        h = []
        for j in range(nsl):
            h.append(_layer_norm(x_ref[r0 + j * rw:r0 + (j + 1) * rw, :], g_ref[...], b_ref[...]).astype(BF16))
            yield
        h = jnp.concatenate(h, axis=0)
        rs = slice(r0, r0 + sub)
        u_ref[rs, :] = _dot(h, w_ref[:, 0:512])
        yield
        q_ref[rs, :] = (_dot(h, w_ref[:, 512:1024]) * (DIFF_HEAD_DIM ** -0.5 * LOG2E)).astype(BF16)
        yield
        k_ref[rs, :] = _dot(h, w_ref[:, 1024:1536]).astype(BF16)
        yield
        v_ref[rs, :] = _dot(h, w_ref[:, 1536:2048]).astype(BF16)
        yield

    _interleave([work(st) for st in range(PROJ_SUBTILES)], lag=nsl)


def _proj(x2, g, b, w):
    tm = PROJ_ROWS
    row = lambda i: (i, 0)
    const = lambda i: (0, 0)
    return pl.pallas_call(
        _proj_kernel,
        grid=(N_TOK // tm,),
        in_specs=[pl.BlockSpec((tm, D_MODEL), row), pl.BlockSpec((1, D_MODEL), const),
                  pl.BlockSpec((1, D_MODEL), const), pl.BlockSpec((D_MODEL, MIX_IN), const)],
        out_specs=[pl.BlockSpec((tm, 512), row)] * 4,
        out_shape=(jax.ShapeDtypeStruct((N_TOK, 512), F32),) + (jax.ShapeDtypeStruct((N_TOK, 512), BF16),) * 3,
        compiler_params=_params("parallel"),
        name="proj",
    )(x2, g, b, w)


def _s5_kernel(u_ref, ar_ref, ai_ref, bre_ref, bim_ref, cre_ref, cim_ref, d_ref, gw_ref, gb_ref,
               y_ref, sre, sim, xre, xim, utm, ytm):
    @pl.when(pl.program_id(0) == 0)
    def _():
        xre[...] = jnp.zeros_like(xre)
        xim[...] = jnp.zeros_like(xim)

    half = S5_COLS // 2
    cw = S5_SCAN_CHUNK
    nchunk = S5_COLS // cw
    rows = S5_SUB_STEPS * BATCH
    steps = S5_SUBTILES * S5_SUB_STEPS
    nslab = S5_WIDTH // LANES
    carry = [None] * nchunk

    for b in range(BATCH):
        for sl in range(nslab):
            utm[sl, pl.ds(b, steps, stride=BATCH), :] = u_ref[b, :, sl * LANES:(sl + 1) * LANES]

    def work(st):
        r0 = st * rows
        u = jnp.concatenate([utm[sl, r0:r0 + rows, :] for sl in range(nslab)], axis=-1)
        ub = u.astype(BF16)
        for hf in range(2):
            uh = ub[:, hf * MXU_DIM:(hf + 1) * MXU_DIM]
            sre[r0:r0 + rows, hf * half:(hf + 1) * half] = _dot(uh, bre_ref[hf])
            yield
            sim[r0:r0 + rows, hf * half:(hf + 1) * half] = _dot(uh, bim_ref[hf])
            yield
        for c in range(nchunk):
            cs = slice(c * cw, (c + 1) * cw)
            ar = ar_ref[:, cs]
            ai = ai_ref[:, cs]
            xr, xi = (xre[:, cs], xim[:, cs]) if st == 0 else carry[c]
            for t in range(S5_SUB_STEPS):
                r = r0 + t * BATCH
                nr = ar * xr - ai * xi + sre[r:r + BATCH, cs]
                ni = ar * xi + ai * xr + sim[r:r + BATCH, cs]
                sre[r:r + BATCH, cs] = nr
                sim[r:r + BATCH, cs] = ni
                xr, xi = nr, ni
            carry[c] = (xr, xi)
            if st == S5_SUBTILES - 1:
                xre[:, cs] = xr
                xim[:, cs] = xi
            yield
        ys = []
        for hf in range(2):
            hs = slice(hf * half, (hf + 1) * half)
            ys.append(_dot(sre[r0:r0 + rows, hs].astype(BF16), cre_ref[hf])
                      - _dot(sim[r0:r0 + rows, hs].astype(BF16), cim_ref[hf]))
            yield
        y = jax.nn.gelu(jnp.concatenate(ys, axis=-1) + d_ref[...] * u)
        yield
        z = _dot(y.astype(BF16), gw_ref[...]) + gb_ref[...]
        y = y * jax.nn.sigmoid(z)
        for sl in range(nslab):
            ytm[sl, r0:r0 + rows, :] = y[:, sl * LANES:(sl + 1) * LANES]
        t0 = st * S5_SUB_STEPS
        for b in range(BATCH):
            y_ref[b, t0:t0 + S5_SUB_STEPS, :] = jnp.concatenate(
                [ytm[sl, pl.ds(r0 + b, S5_SUB_STEPS, stride=BATCH), :] for sl in range(nslab)],
                axis=-1).astype(BF16)
        yield

    _interleave([work(st) for st in range(S5_SUBTILES)], lag=4)


def _s5(u, ar8, ai8, bre, bim, cre, cim, d, gw, gb):
    steps = S5_SUBTILES * S5_SUB_STEPS
    rows = steps * BATCH
    nslab = S5_WIDTH // LANES
    blk = pl.BlockSpec((BATCH, steps, S5_WIDTH), lambda i: (0, i, 0))
    c2 = lambda i: (0, 0)
    c3 = lambda i: (0, 0, 0)
    half = S5_COLS // 2
    return pl.pallas_call(
        _s5_kernel,
        grid=(SEQ // steps,),
        in_specs=[blk,
                  pl.BlockSpec((BATCH, S5_COLS), c2), pl.BlockSpec((BATCH, S5_COLS), c2),
                  pl.BlockSpec((2, MXU_DIM, half), c3), pl.BlockSpec((2, MXU_DIM, half), c3),
                  pl.BlockSpec((2, half, MXU_DIM), c3), pl.BlockSpec((2, half, MXU_DIM), c3),
                  pl.BlockSpec((1, S5_WIDTH), c2), pl.BlockSpec((S5_WIDTH, S5_WIDTH), c2),
                  pl.BlockSpec((1, S5_WIDTH), c2)],
        out_specs=blk,
        out_shape=jax.ShapeDtypeStruct((BATCH, SEQ, S5_WIDTH), BF16),
        scratch_shapes=[pltpu.VMEM((rows, S5_COLS), F32), pltpu.VMEM((rows, S5_COLS), F32),
                        pltpu.VMEM((BATCH, S5_COLS), F32), pltpu.VMEM((BATCH, S5_COLS), F32),
                        pltpu.VMEM((nslab, rows, LANES), F32), pltpu.VMEM((nslab, rows, LANES), F32)],
        compiler_params=_params("arbitrary"),
        name="s5",
    )(u, ar8, ai8, bre, bim, cre, cim, d, gw, gb)


def _diff_attn_kernel(q_ref, k_ref, v_ref, bias_ref, lq1_ref, lk1_ref, lq2_ref, lk2_ref, g_ref,
                      o_ref):
    t = ATT_TILE
    half = t // 2
    sb = LANES
    lam = (jnp.exp(jnp.sum(lq1_ref[...] * lk1_ref[...], axis=-1, keepdims=True))
           - jnp.exp(jnp.sum(lq2_ref[...] * lk2_ref[...], axis=-1, keepdims=True)) + LAMBDA_INIT)
    lane = lax.broadcasted_iota(jnp.int32, (t, DIFF_V_DIM), 1)
    zero = jnp.zeros((t, DIFF_V_DIM), BF16)

    def head_work(hh):
        hs = slice(hh * DIFF_V_DIM, (hh + 1) * DIFF_V_DIM)

        def scores(qm, k0, width):
            kb = k_ref[k0:k0 + width, hs]
            return [lax.dot_general(qm[mi], kb, _NT, preferred_element_type=F32) for mi in range(2)]

        def lane_groups(s, r0, k0):
            rows, width = s.shape
            groups = []
            for g in range(width // sb):
                kb = (k0 + g * sb) // sb
                col = s[:, g * sb:(g + 1) * sb]
                qbs = [(r0 + i * sb) // sb for i in range(rows // sb)]
                if all(kb < qb - 1 for qb in qbs):
                    groups.append(col)
                    continue
                pieces = []
                for i, qb in enumerate(qbs):
                    piece = col[i * sb:(i + 1) * sb, :]
                    if kb == qb:
                        piece = piece + bias_ref[hh, 0]
                    elif kb == qb - 1:
                        piece = piece + bias_ref[hh, 1]
                    elif kb > qb:
                        piece = jnp.full((sb, sb), -jnp.inf, F32)
                    pieces.append(piece)
                groups.append(jnp.concatenate(pieces, axis=0))
            return groups

        def update(s, r0, k0, state):
            m_old, l_old, acc_old = state
            sg = lane_groups(s, r0, k0)
            vb = v_ref[k0:k0 + s.shape[1], hs]
            row_max = jnp.max(functools.reduce(jnp.maximum, sg), axis=-1, keepdims=True)
            if m_old is None:
                m_new = jnp.broadcast_to(row_max, (s.shape[0], LANES))
                pg = [jnp.exp2(x - m_new) for x in sg]
                p = jnp.concatenate(pg, axis=-1).astype(BF16)
                return m_new, functools.reduce(jnp.add, pg), _dot(p, vb)
            m_new = jnp.maximum(m_old, row_max)
            alpha = jnp.exp2(m_old - m_new)
            pg = [jnp.exp2(x - m_new) for x in sg]
            p = jnp.concatenate(pg, axis=-1).astype(BF16)
            return m_new, alpha * l_old + functools.reduce(jnp.add, pg), alpha * acc_old + _dot(p, vb)

        def rows_of(state, lo, hi):
            return tuple(None if a is None else a[lo:hi] for a in state)

        for c in range(SEQ // t):
            r0 = c * t
            q = q_ref[r0:r0 + t, hs]
            qm = (jnp.where(lane < DIFF_HEAD_DIM, q, zero), jnp.where(lane >= DIFF_HEAD_DIM, q, zero))
            qtop = [x[:half] for x in qm]
            qbot = [x[half:] for x in qm]

            def diag_scores(qtop=qtop, qbot=qbot, r0=r0):
                return scores(qtop, r0, half), scores(qbot, r0, t)

            state = [(None, None, None), (None, None, None)]
            s_next = scores(qm, 0, t) if c > 0 else diag_scores()
            for j in range(c):
                s_cur = s_next
                s_next = scores(qm, (j + 1) * t, t) if j + 1 < c else diag_scores()
                state = [update(s_cur[mi], r0, j * t, state[mi]) for mi in range(2)]
                yield
            s_top, s_bot = s_next
            outs = []
            for mi in range(2):
                top = update(s_top[mi], r0, r0, rows_of(state[mi], 0, half))
                bot = update(s_bot[mi], r0 + half, r0, rows_of(state[mi], half, t))
                l = jnp.concatenate([top[1], bot[1]], axis=0)
                acc = jnp.concatenate([top[2], bot[2]], axis=0)
                outs.append(acc / jnp.sum(l, axis=-1, keepdims=True))
            o = outs[0] - lam * outs[1]
            ms = jnp.mean(o * o, axis=-1, keepdims=True)
            y = o * lax.rsqrt(ms + LN_EPS) * g_ref[...]
            o_ref[r0:r0 + t, hs] = (y * (1.0 - LAMBDA_INIT)).astype(BF16)
            yield

    _interleave([head_work(hh) for hh in range(ATT_HEADS)], lag=1)


def _diff_attn(q, k, v, bias, lq1, lk1, lq2, lk2, g):
    vec = lambda b, h: (0, 0)
    seq = pl.BlockSpec((SEQ, ATT_HEADS * DIFF_V_DIM), lambda b, h: (b, h))
    return pl.pallas_call(
        _diff_attn_kernel,
        grid=(BATCH, N_DIFF_HEADS // ATT_HEADS),
        in_specs=[seq, seq, seq,
                  pl.BlockSpec((ATT_HEADS, 2, LANES, LANES), lambda b, h: (h, 0, 0, 0)),
                  pl.BlockSpec((1, DIFF_HEAD_DIM), vec), pl.BlockSpec((1, DIFF_HEAD_DIM), vec),
                  pl.BlockSpec((1, DIFF_HEAD_DIM), vec), pl.BlockSpec((1, DIFF_HEAD_DIM), vec),
                  pl.BlockSpec((1, DIFF_V_DIM), vec)],
        out_specs=seq,
        out_shape=jax.ShapeDtypeStruct((N_TOK, DIFF_WIDTH), BF16),
        compiler_params=_params("parallel", "parallel"),
        name="diff_attn",
    )(q, k, v, bias, lq1, lk1, lq2, lk2, g)


def _kv_kernel(m_ref, w_ref, o_ref):
    o_ref[...] = _dot(m_ref[...].astype(BF16), w_ref[...]).astype(BF16)


def _kv(mem2, wkv):
    return pl.pallas_call(
        _kv_kernel,
        grid=(BATCH,),
        in_specs=[pl.BlockSpec((MEM_LEN, D_MODEL), lambda i: (i, 0)),
                  pl.BlockSpec((D_MODEL, 2 * D_MODEL), lambda i: (0, 0))],
        out_specs=pl.BlockSpec((MEM_LEN, 2 * D_MODEL), lambda i: (i, 0)),
        out_shape=jax.ShapeDtypeStruct((BATCH * MEM_LEN, 2 * D_MODEL), BF16),
        compiler_params=_params("parallel"),
        name="kv",
    )(mem2, wkv)


def _mix_ca_kernel(x_ref, ys_ref, yd_ref, g0_ref, b0_ref, wout_ref, g1_ref, b1_ref,
                   wq_ref, kv_ref, wo_ref, g2_ref, b2_ref, o_ref):
    sub = MIX_ROWS // MIX_SUBTILES
    nsl = MIX_SLICES
    cw = D_MODEL // nsl
    rw = sub // nsl

    def work(st):
        r0 = st * sub
        ys = ys_ref[r0:r0 + sub, :]
        yd = yd_ref[r0:r0 + sub, :]
        mix = []
        for j in range(nsl):
            cs = slice(j * cw, (j + 1) * cw)
            mix.append(_dot(ys, wout_ref[0:S5_WIDTH, cs]) + _dot(yd, wout_ref[S5_WIDTH:, cs]))
            yield
        mix = jnp.concatenate(mix, axis=-1)
        h1 = []
        for j in range(nsl):
            rs = slice(j * rw, (j + 1) * rw)
            h0 = _layer_norm(x_ref[r0 + j * rw:r0 + (j + 1) * rw, :], g0_ref[...], b0_ref[...])
            h1.append(_layer_norm(DEEPNORM_ALPHA * h0 + mix[rs], g1_ref[...], b1_ref[...]))
            yield
        h1 = jnp.concatenate(h1, axis=0)
        h1b = h1.astype(BF16)
        qs = []
        for hd in range(CA_HEADS):
            cs = slice(hd * CA_HEAD_DIM, (hd + 1) * CA_HEAD_DIM)
            qs.append((_dot(h1b, wq_ref[:, cs]) * (CA_HEAD_DIM ** -0.5)).astype(BF16))
            yield
        heads = []
        for hd in range(CA_HEADS):
            kh = kv_ref[:, hd * CA_HEAD_DIM:(hd + 1) * CA_HEAD_DIM]
            vh = kv_ref[:, D_MODEL + hd * CA_HEAD_DIM:D_MODEL + (hd + 1) * CA_HEAD_DIM]
            s = lax.dot_general(qs[hd], kh, _NT, preferred_element_type=F32)
            e = jnp.exp(s - jnp.max(s, axis=-1, keepdims=True))
            oh = _dot(e.astype(BF16), vh) / jnp.sum(e, axis=-1, keepdims=True)
            heads.append(oh.astype(BF16))
            yield
        o = jnp.concatenate(heads, axis=-1)
        ca = []
        for j in range(nsl):
            ca.append(_dot(o, wo_ref[:, j * cw:(j + 1) * cw]))
            yield
        ca = jnp.concatenate(ca, axis=-1)
        for j in range(nsl):
            rs = slice(j * rw, (j + 1) * rw)
            o_ref[r0 + j * rw:r0 + (j + 1) * rw, :] = _layer_norm(
                DEEPNORM_ALPHA * h1[rs] + ca[rs], g2_ref[...], b2_ref[...])
            yield

    _interleave([work(st) for st in range(MIX_SUBTILES)], lag=nsl)


def _mix_ca(x2, ys, yd, g0, b0, wout, g1, b1, wq, kv, wo, g2, b2):
    tm = MIX_ROWS
    per_batch = SEQ // tm
    row = lambda i: (i, 0)
    const = lambda i: (0, 0)
    vec = pl.BlockSpec((1, D_MODEL), const)
    sq = pl.BlockSpec((D_MODEL, D_MODEL), const)
    return pl.pallas_call(
        _mix_ca_kernel,
        grid=(N_TOK // tm,),
        in_specs=[pl.BlockSpec((tm, D_MODEL), row), pl.BlockSpec((tm, 512), row), pl.BlockSpec((tm, 512), row),
                  vec, vec, sq, vec, vec, sq,
                  pl.BlockSpec((MEM_LEN, 2 * D_MODEL), lambda i: (i // per_batch, 0)),
                  sq, vec, vec],
        out_specs=pl.BlockSpec((tm, D_MODEL), row),
        out_shape=jax.ShapeDtypeStruct((N_TOK, D_MODEL), F32),
        compiler_params=_params("parallel"),
        name="mix_ca",
    )(x2, ys, yd, g0, b0, wout, g1, b1, wq, kv, wo, g2, b2)


def _ffn_kernel(h_ref, wgu_ref, wd_ref, g_ref, b_ref, o_ref):
    sub = FFN_ROWS // FFN_SUBTILES
    ck = FFN_CHUNK
    nsl = 4
    rw = sub // nsl

    def work(st):
        r0 = st * sub
        hb = h_ref[r0:r0 + sub, :].astype(BF16)
        acc = None
        for c in range(FFN_HIDDEN // ck):
            gate = _dot(hb, wgu_ref[:, c * ck:(c + 1) * ck])
            up = _dot(hb, wgu_ref[:, FFN_HIDDEN + c * ck:FFN_HIDDEN + (c + 1) * ck])
            act = (jax.nn.silu(gate) * up).astype(BF16)
            part = _dot(act, wd_ref[c * ck:(c + 1) * ck, :])
            acc = part if acc is None else acc + part
            yield
        for j in range(nsl):
            rs = slice(r0 + j * rw, r0 + (j + 1) * rw)
            o_ref[rs, :] = _layer_norm(DEEPNORM_ALPHA * h_ref[rs, :] + acc[j * rw:(j + 1) * rw],
                                       g_ref[...], b_ref[...])
            yield

    _interleave([work(st) for st in range(FFN_SUBTILES)], lag=FFN_HIDDEN // ck // 2 + 1)


def _ffn(h2, wgu, wd, g, b):
    tm = FFN_ROWS
    row = lambda i: (i, 0)
    const = lambda i: (0, 0)
    resident = dict(pipeline_mode=pl.Buffered(1))
    return pl.pallas_call(
        _ffn_kernel,
        grid=(N_TOK // tm,),
        in_specs=[pl.BlockSpec((tm, D_MODEL), row),
                  pl.BlockSpec((D_MODEL, 2 * FFN_HIDDEN), const, **resident),
                  pl.BlockSpec((FFN_HIDDEN, D_MODEL), const, **resident),
                  pl.BlockSpec((1, D_MODEL), const), pl.BlockSpec((1, D_MODEL), const)],
        out_specs=pl.BlockSpec((tm, D_MODEL), row),
        out_shape=jax.ShapeDtypeStruct((N_TOK, D_MODEL), F32),
        compiler_params=_params("parallel"),
        name="ffn",
    )(h2, wgu, wd, g, b)


def kernel(x, mem, ln_in_g, ln_in_b, w_in, s5_lambda_re, s5_lambda_im, s5_log_dt, s5_b_re, s5_b_im,
           s5_c_re, s5_c_im, s5_d, s5_glu_w, s5_glu_b, diff_lq1, diff_lk1, diff_lq2, diff_lk2,
           diff_subln_g, rel_bias, w_out, ln1_g, ln1_b, ca_wq, ca_wkv, ca_wo, ln2_g, ln2_b,
           ffn_w_gate_up, ffn_w_down, ln3_g, ln3_b):
    vec = lambda a: a.reshape(1, -1).astype(F32)
    x2 = x.reshape(N_TOK, D_MODEL)

    u, q, k, v = _proj(x2, vec(ln_in_g), vec(ln_in_b), w_in[0].astype(BF16))

    ar, ai, bbr, bbi = _s5_prep(s5_lambda_re[0], s5_lambda_im[0], s5_log_dt[0], s5_b_re[0], s5_b_im[0])
    ar8 = jnp.broadcast_to(ar.reshape(1, S5_COLS), (BATCH, S5_COLS))
    ai8 = jnp.broadcast_to(ai.reshape(1, S5_COLS), (BATCH, S5_COLS))
    bre = _block_diag(bbr.reshape(2, 16, S5_GROUP, S5_STATE)).astype(BF16)
    bim = _block_diag(bbi.reshape(2, 16, S5_GROUP, S5_STATE)).astype(BF16)
    cre = _block_diag(jnp.transpose(s5_c_re[0], (0, 2, 1)).reshape(2, 16, S5_STATE, S5_GROUP)).astype(BF16)
    cim = _block_diag(jnp.transpose(s5_c_im[0], (0, 2, 1)).reshape(2, 16, S5_STATE, S5_GROUP)).astype(BF16)
    y_s5 = _s5(u.reshape(BATCH, SEQ, S5_WIDTH), ar8, ai8, bre, bim, cre, cim, vec(s5_d[0]),
               s5_glu_w[0].astype(BF16), vec(s5_glu_b[0])).reshape(N_TOK, S5_WIDTH)

    y_diff = _diff_attn(q, k, v, _bias_tiles(rel_bias), vec(diff_lq1[0]), vec(diff_lk1[0]),
                        vec(diff_lq2[0]), vec(diff_lk2[0]), vec(diff_subln_g[0]))

    kv = _kv(mem.reshape(BATCH * MEM_LEN, D_MODEL), ca_wkv[0].astype(BF16))
    h2 = _mix_ca(x2, y_s5, y_diff, vec(ln_in_g), vec(ln_in_b), w_out[0].astype(BF16), vec(ln1_g[0]),
                 vec(ln1_b[0]), ca_wq[0].astype(BF16), kv, ca_wo[0].astype(BF16), vec(ln2_g[0]), vec(ln2_b[0]))
    out = _ffn(h2, ffn_w_gate_up[0].astype(BF16), ffn_w_down[0].astype(BF16), vec(ln3_g[0]), vec(ln3_b[0]))
    return out.reshape(BATCH, SEQ, D_MODEL)
```

```python
import functools
import math

import numpy as np

import jax
import jax.numpy as jnp
from jax import lax
from jax.experimental import pallas as pl
from jax.experimental.pallas import tpu as pltpu

F32 = jnp.float32
BF16 = jnp.bfloat16

D_MODEL = 1024
BATCH = 8
SEQ = 2048
N_TOK = BATCH * SEQ
MEM_LEN = 256
S5_WIDTH = 512
S5_GROUP = 16
S5_GROUPS = 32
S5_STATE = 64
S5_COLS = S5_GROUPS * S5_STATE
DIFF_WIDTH = 512
DIFF_HEAD_DIM = 64
DIFF_V_DIM = 128
N_DIFF_HEADS = 4
MIX_IN = 2048
NUM_BUCKETS = 32
MAX_DISTANCE = 128
CA_HEADS = 4
CA_HEAD_DIM = 256
FFN_HIDDEN = 2816
DEEPNORM_ALPHA = 2.0 ** 0.25
LN_EPS = 1e-5
LAMBDA_INIT = 0.8 - 0.6 * math.exp(0.0)
LOG2E = math.log2(math.e)

VMEM_LIMIT_BYTES = 56 * 1024 * 1024
MXU_DIM = 256
LANES = 128

PROJ_ROWS = 1024
PROJ_SUBTILES = 2
S5_SUB_STEPS = 64
S5_SUBTILES = 2
S5_SCAN_CHUNK = 512
ATT_TILE = 512
ATT_HEADS = 1
FFN_CHUNK = 256
FFN_ROWS = 1024
FFN_SUBTILES = 2
MIX_ROWS = 1024
MIX_SUBTILES = 2
MIX_SLICES = 4
N_LN_ROWS = 8

_NT = (((1,), (1,)), ((), ()))


def _params(*sem):
    return pltpu.CompilerParams(dimension_semantics=sem, vmem_limit_bytes=VMEM_LIMIT_BYTES)


def _layer_norm(x, g, b):
    mu = jnp.mean(x, axis=-1, keepdims=True)
    xc = x - mu
    var = jnp.mean(xc * xc, axis=-1, keepdims=True)
    return xc * lax.rsqrt(var + LN_EPS) * g + b


def _dot(a, b):
    return jnp.dot(a, b, preferred_element_type=F32)


def _interleave(gens, lag):
    gens = list(gens)
    live = [True] * len(gens)
    tick = 0
    while any(live):
        for i, g in enumerate(gens):
            if live[i] and tick >= lag * i:
                try:
                    next(g)
                except StopIteration:
                    live[i] = False
        tick += 1


def _s5_prep_kernel(lam_ref, bc_ref, a_ref, w_ref):
    lr = lam_ref[0:1, :]
    li = lam_ref[1:2, :]
    dt = jnp.exp(lam_ref[2:3, :])
    mag = jnp.exp(lr * dt)
    ang = li * dt
    ar = mag * jnp.cos(ang)
    ai = mag * jnp.sin(ang)
    den = lr * lr + li * li
    nr = ar - 1.0
    fr = (nr * lr + ai * li) / den
    fi = (ai * lr - nr * li) / den
    a_ref[0] = jnp.broadcast_to(ar, (BATCH, S5_COLS))
    a_ref[1] = jnp.broadcast_to(ai, (BATCH, S5_COLS))
    b_r = bc_ref[0]
    b_i = bc_ref[1]
    mats = (fr * b_r - fi * b_i, fr * b_i + fi * b_r, bc_ref[2], bc_ref[3])
    half = S5_COLS // 2
    lane_group = lax.broadcasted_iota(jnp.int32, (S5_GROUP, half), 1) // S5_STATE
    for k, mat in enumerate(mats):
        for hf in range(2):
            src = mat[:, hf * half:(hf + 1) * half]
            for gl in range(MXU_DIM // S5_GROUP):
                w_ref[k, hf, gl * S5_GROUP:(gl + 1) * S5_GROUP, :] = jnp.where(
                    lane_group == gl, src, 0.0).astype(BF16)


def _s5_prep(lam_re, lam_im, log_dt, b_re, b_im, c_re, c_im):
    lam = jnp.stack([lam_re.reshape(S5_COLS), lam_im.reshape(S5_COLS), jnp.repeat(log_dt, S5_STATE)])
    bc = jnp.stack([jnp.transpose(b_re, (2, 0, 1)), jnp.transpose(b_im, (2, 0, 1)),
                    jnp.transpose(c_re, (1, 0, 2)), jnp.transpose(c_im, (1, 0, 2))]).reshape(4, S5_GROUP, S5_COLS)
    return pl.pallas_call(
        _s5_prep_kernel,
        out_shape=(jax.ShapeDtypeStruct((2, BATCH, S5_COLS), F32),
                   jax.ShapeDtypeStruct((4, 2, MXU_DIM, S5_COLS // 2), BF16)),
        name="s5_prep",
    )(lam, bc)


def _bias_prep_kernel(rel_ref, bucket_ref, o_ref):
    h = pl.program_id(0)
    bucket = bucket_ref[0]
    far = rel_ref[NUM_BUCKETS - 1, h]
    acc = jnp.zeros(bucket.shape, F32)
    for b in range(NUM_BUCKETS):
        acc = jnp.where(bucket == b, (rel_ref[b, h] - far) * LOG2E, acc)
    o_ref[0, 0] = jnp.where(bucket < 0, -jnp.inf, acc)


def _t5_bucket(dist):
    max_exact = NUM_BUCKETS // 2
    df = np.maximum(dist, 1).astype(np.float32)
    large = max_exact + (np.log(df / np.float32(max_exact)) / np.float32(math.log(MAX_DISTANCE / max_exact))
                         * np.float32(NUM_BUCKETS - max_exact)).astype(np.int32)
    large = np.minimum(large, NUM_BUCKETS - 1)
    return np.where(dist < max_exact, dist, large).astype(np.int32)


def _bias_tiles(rel_bias):
    t = LANES
    qpos = np.arange(t, dtype=np.int32)[:, None]
    kpos = np.arange(t, dtype=np.int32)[None, :]
    d_diag = qpos - kpos
    d_prev = d_diag + t
    bucket = jnp.asarray(np.stack([np.where(d_diag >= 0, _t5_bucket(np.maximum(d_diag, 0)), -1),
                                   _t5_bucket(d_prev)]).astype(np.int32))
    return pl.pallas_call(
        _bias_prep_kernel,
        grid=(N_DIFF_HEADS, 2),
        in_specs=[pl.BlockSpec(memory_space=pltpu.SMEM),
                  pl.BlockSpec((1, t, t), lambda h, j: (j, 0, 0))],
        out_specs=pl.BlockSpec((1, 1, t, t), lambda h, j: (h, j, 0, 0)),
        out_shape=jax.ShapeDtypeStruct((N_DIFF_HEADS, 2, t, t), F32),
        compiler_params=_params("parallel", "parallel"),
        name="bias_prep",
    )(rel_bias, bucket)


def _proj_kernel(x_ref, ln_ref, w_ref, u_ref, q_ref, k_ref, v_ref):
    sub = PROJ_ROWS // PROJ_SUBTILES
    nsl = 4
    rw = sub // nsl

    def work(st):
        r0 = st * sub
        h = []
        for j in range(nsl):
            h.append(_layer_norm(x_ref[r0 + j * rw:r0 + (j + 1) * rw, :], ln_ref[0:1, :], ln_ref[1:2, :])
                     .astype(BF16))
            yield
        h = jnp.concatenate(h, axis=0)
        rs = slice(r0, r0 + sub)
        u_ref[rs, :] = _dot(h, w_ref[:, 0:512])
        yield
        q_ref[rs, :] = (_dot(h, w_ref[:, 512:1024]) * (DIFF_HEAD_DIM ** -0.5 * LOG2E)).astype(BF16)
        yield
        k_ref[rs, :] = _dot(h, w_ref[:, 1024:1536]).astype(BF16)
        yield
        v_ref[rs, :] = _dot(h, w_ref[:, 1536:2048]).astype(BF16)
        yield

    _interleave([work(st) for st in range(PROJ_SUBTILES)], lag=nsl)


def _proj(x2, ln, w):
    tm = PROJ_ROWS
    row = lambda i: (i, 0)
    const = lambda i: (0, 0)
    return pl.pallas_call(
        _proj_kernel,
        grid=(N_TOK // tm,),
        in_specs=[pl.BlockSpec((tm, D_MODEL), row), pl.BlockSpec((N_LN_ROWS, D_MODEL), const),
                  pl.BlockSpec((D_MODEL, MIX_IN), const)],
        out_specs=[pl.BlockSpec((tm, 512), row)] * 4,
        out_shape=(jax.ShapeDtypeStruct((N_TOK, 512), F32),) + (jax.ShapeDtypeStruct((N_TOK, 512), BF16),) * 3,
        compiler_params=_params("parallel"),
        name="proj",
    )(x2, ln, w)


def _s5_kernel(u_ref, a_ref, w_ref, dg_ref, gw_ref, y_ref, sre, sim, xre, xim, utm, ytm):
    @pl.when(pl.program_id(0) == 0)
    def _():
        xre[...] = jnp.zeros_like(xre)
        xim[...] = jnp.zeros_like(xim)

    half = S5_COLS // 2
    cw = S5_SCAN_CHUNK
    nchunk = S5_COLS // cw
    rows = S5_SUB_STEPS * BATCH
    steps = S5_SUBTILES * S5_SUB_STEPS
    nslab = S5_WIDTH // LANES
    carry = [None] * nchunk

    for b in range(BATCH):
        for sl in range(nslab):
            utm[sl, pl.ds(b, steps, stride=BATCH), :] = u_ref[b, :, sl * LANES:(sl + 1) * LANES]

    def work(st):
        r0 = st * rows
        u = jnp.concatenate([utm[sl, r0:r0 + rows, :] for sl in range(nslab)], axis=-1)
        ub = u.astype(BF16)
        for hf in range(2):
            uh = ub[:, hf * MXU_DIM:(hf + 1) * MXU_DIM]
            sre[r0:r0 + rows, hf * half:(hf + 1) * half] = _dot(uh, w_ref[0, hf])
            yield
            sim[r0:r0 + rows, hf * half:(hf + 1) * half] = _dot(uh, w_ref[1, hf])
            yield
        for c in range(nchunk):
            cs = slice(c * cw, (c + 1) * cw)
            ar = a_ref[0, :, cs]
            ai = a_ref[1, :, cs]
            xr, xi = (xre[:, cs], xim[:, cs]) if st == 0 else carry[c]
            for t in range(S5_SUB_STEPS):
                r = r0 + t * BATCH
                nr = ar * xr - ai * xi + sre[r:r + BATCH, cs]
                ni = ar * xi + ai * xr + sim[r:r + BATCH, cs]
                sre[r:r + BATCH, cs] = nr
                sim[r:r + BATCH, cs] = ni
                xr, xi = nr, ni
            carry[c] = (xr, xi)
            if st == S5_SUBTILES - 1:
                xre[:, cs] = xr
                xim[:, cs] = xi
            yield
        ys = []
        for hf in range(2):
            hs = slice(hf * half, (hf + 1) * half)
            ys.append(lax.dot_general(sre[r0:r0 + rows, hs].astype(BF16), w_ref[2, hf], _NT,
                                      preferred_element_type=F32)
                      - lax.dot_general(sim[r0:r0 + rows, hs].astype(BF16), w_ref[3, hf], _NT,
                                        preferred_element_type=F32))
            yield
        y = jax.nn.gelu(jnp.concatenate(ys, axis=-1) + dg_ref[0:1, :] * u)
        yield
        z = _dot(y.astype(BF16), gw_ref[...]) + dg_ref[1:2, :]
        y = y * jax.nn.sigmoid(z)
        for sl in range(nslab):
            ytm[sl, r0:r0 + rows, :] = y[:, sl * LANES:(sl + 1) * LANES]
        t0 = st * S5_SUB_STEPS
        for b in range(BATCH):
            y_ref[b, t0:t0 + S5_SUB_STEPS, :] = jnp.concatenate(
                [ytm[sl, pl.ds(r0 + b, S5_SUB_STEPS, stride=BATCH), :] for sl in range(nslab)],
                axis=-1).astype(BF16)
        yield

    _interleave([work(st) for st in range(S5_SUBTILES)], lag=4)


def _s5(u, a8, w, dg, gw):
    steps = S5_SUBTILES * S5_SUB_STEPS
    rows = steps * BATCH
    nslab = S5_WIDTH // LANES
    blk = pl.BlockSpec((BATCH, steps, S5_WIDTH), lambda i: (0, i, 0))
    c2 = lambda i: (0, 0)
    return pl.pallas_call(
        _s5_kernel,
        grid=(SEQ // steps,),
        in_specs=[blk,
                  pl.BlockSpec((2, BATCH, S5_COLS), lambda i: (0, 0, 0)),
                  pl.BlockSpec((4, 2, MXU_DIM, S5_COLS // 2), lambda i: (0, 0, 0, 0)),
                  pl.BlockSpec((2, S5_WIDTH), c2), pl.BlockSpec((S5_WIDTH, S5_WIDTH), c2)],
        out_specs=blk,
        out_shape=jax.ShapeDtypeStruct((BATCH, SEQ, S5_WIDTH), BF16),
        scratch_shapes=[pltpu.VMEM((rows, S5_COLS), F32), pltpu.VMEM((rows, S5_COLS), F32),
                        pltpu.VMEM((BATCH, S5_COLS), F32), pltpu.VMEM((BATCH, S5_COLS), F32),
                        pltpu.VMEM((nslab, rows, LANES), F32), pltpu.VMEM((nslab, rows, LANES), F32)],
        compiler_params=_params("arbitrary"),
        name="s5",
    )(u, a8, w, dg, gw)


def _diff_attn_kernel(q_ref, k_ref, v_ref, bias_ref, lv_ref, g_ref, o_ref):
    t = ATT_TILE
    half = t // 2
    sb = LANES
    lam = (jnp.exp(jnp.sum(lv_ref[0:1, :] * lv_ref[1:2, :], axis=-1, keepdims=True))
           - jnp.exp(jnp.sum(lv_ref[2:3, :] * lv_ref[3:4, :], axis=-1, keepdims=True)) + LAMBDA_INIT)
    lane = lax.broadcasted_iota(jnp.int32, (t, DIFF_V_DIM), 1)
    zero = jnp.zeros((t, DIFF_V_DIM), BF16)

    def head_work(hh):
        hs = slice(hh * DIFF_V_DIM, (hh + 1) * DIFF_V_DIM)

        def scores(qm, k0, width):
            kb = k_ref[k0:k0 + width, hs]
            return [lax.dot_general(qm[mi], kb, _NT, preferred_element_type=F32) for mi in range(2)]

        def lane_groups(s, r0, k0):
            rows, width = s.shape
            groups = []
            for g in range(width // sb):
                kb = (k0 + g * sb) // sb
                col = s[:, g * sb:(g + 1) * sb]
                qbs = [(r0 + i * sb) // sb for i in range(rows // sb)]
                if all(kb < qb - 1 for qb in qbs):
                    groups.append(col)
                    continue
                pieces = []
                for i, qb in enumerate(qbs):
                    piece = col[i * sb:(i + 1) * sb, :]
                    if kb == qb:
                        piece = piece + bias_ref[hh, 0]
                    elif kb == qb - 1:
                        piece = piece + bias_ref[hh, 1]
                    elif kb > qb:
                        piece = jnp.full((sb, sb), -jnp.inf, F32)
                    pieces.append(piece)
                groups.append(jnp.concatenate(pieces, axis=0))
            return groups

        def update(s, r0, k0, state):
            m_old, l_old, acc_old = state
            sg = lane_groups(s, r0, k0)
            vb = v_ref[k0:k0 + s.shape[1], hs]
            row_max = jnp.max(functools.reduce(jnp.maximum, sg), axis=-1, keepdims=True)
            if m_old is None:
                m_new = jnp.broadcast_to(row_max, (s.shape[0], LANES))
                pg = [jnp.exp2(x - m_new) for x in sg]
                p = jnp.concatenate(pg, axis=-1).astype(BF16)
                return m_new, functools.reduce(jnp.add, pg), _dot(p, vb)
            m_new = jnp.maximum(m_old, row_max)
            alpha = jnp.exp2(m_old - m_new)
            pg = [jnp.exp2(x - m_new) for x in sg]
            p = jnp.concatenate(pg, axis=-1).astype(BF16)
            return m_new, alpha * l_old + functools.reduce(jnp.add, pg), alpha * acc_old + _dot(p, vb)

        def rows_of(state, lo, hi):
            return tuple(None if a is None else a[lo:hi] for a in state)

        for c in range(SEQ // t):
            r0 = c * t
            q = q_ref[r0:r0 + t, hs]
            qm = (jnp.where(lane < DIFF_HEAD_DIM, q, zero), jnp.where(lane >= DIFF_HEAD_DIM, q, zero))
            qtop = [x[:half] for x in qm]
            qbot = [x[half:] for x in qm]

            def diag_scores(qtop=qtop, qbot=qbot, r0=r0):
                return scores(qtop, r0, half), scores(qbot, r0, t)

            state = [(None, None, None), (None, None, None)]
            s_next = scores(qm, 0, t) if c > 0 else diag_scores()
            for j in range(c):
                s_cur = s_next
                s_next = scores(qm, (j + 1) * t, t) if j + 1 < c else diag_scores()
                state = [update(s_cur[mi], r0, j * t, state[mi]) for mi in range(2)]
                yield
            s_top, s_bot = s_next
            outs = []
            for mi in range(2):
                top = update(s_top[mi], r0, r0, rows_of(state[mi], 0, half))
                bot = update(s_bot[mi], r0 + half, r0, rows_of(state[mi], half, t))
                l = jnp.concatenate([top[1], bot[1]], axis=0)
                acc = jnp.concatenate([top[2], bot[2]], axis=0)
                outs.append(acc / jnp.sum(l, axis=-1, keepdims=True))
            o = outs[0] - lam * outs[1]
            ms = jnp.mean(o * o, axis=-1, keepdims=True)
            y = o * lax.rsqrt(ms + LN_EPS) * g_ref[...]
            o_ref[r0:r0 + t, hs] = (y * (1.0 - LAMBDA_INIT)).astype(BF16)
            yield

    _interleave([head_work(hh) for hh in range(ATT_HEADS)], lag=1)


def _diff_attn(q, k, v, bias, lv, g):
    vec = lambda b, h: (0, 0)
    seq = pl.BlockSpec((SEQ, ATT_HEADS * DIFF_V_DIM), lambda b, h: (b, h))
    return pl.pallas_call(
        _diff_attn_kernel,
        grid=(BATCH, N_DIFF_HEADS // ATT_HEADS),
        in_specs=[seq, seq, seq,
                  pl.BlockSpec((ATT_HEADS, 2, LANES, LANES), lambda b, h: (h, 0, 0, 0)),
                  pl.BlockSpec((4, DIFF_HEAD_DIM), vec), pl.BlockSpec((1, DIFF_V_DIM), vec)],
        out_specs=seq,
        out_shape=jax.ShapeDtypeStruct((N_TOK, DIFF_WIDTH), BF16),
        compiler_params=_params("parallel", "parallel"),
        name="diff_attn",
    )(q, k, v, bias, lv, g)


def _kv_kernel(m_ref, w_ref, o_ref):
    o_ref[...] = _dot(m_ref[...].astype(BF16), w_ref[...]).astype(BF16)


def _kv(mem2, wkv):
    return pl.pallas_call(
        _kv_kernel,
        grid=(BATCH,),
        in_specs=[pl.BlockSpec((MEM_LEN, D_MODEL), lambda i: (i, 0)),
                  pl.BlockSpec((D_MODEL, 2 * D_MODEL), lambda i: (0, 0))],
        out_specs=pl.BlockSpec((MEM_LEN, 2 * D_MODEL), lambda i: (i, 0)),
        out_shape=jax.ShapeDtypeStruct((BATCH * MEM_LEN, 2 * D_MODEL), BF16),
        compiler_params=_params("parallel"),
        name="kv",
    )(mem2, wkv)


def _mix_ca_kernel(x_ref, ys_ref, yd_ref, ln_ref, wout_ref, wq_ref, kv_ref, wo_ref, o_ref):
    sub = MIX_ROWS // MIX_SUBTILES
    nsl = MIX_SLICES
    cw = D_MODEL // nsl
    rw = sub // nsl

    def work(st):
        r0 = st * sub
        ys = ys_ref[r0:r0 + sub, :]
        yd = yd_ref[r0:r0 + sub, :]
        mix = []
        for j in range(nsl):
            cs = slice(j * cw, (j + 1) * cw)
            mix.append(_dot(ys, wout_ref[0:S5_WIDTH, cs]) + _dot(yd, wout_ref[S5_WIDTH:, cs]))
            yield
        mix = jnp.concatenate(mix, axis=-1)
        h1 = []
        for j in range(nsl):
            rs = slice(j * rw, (j + 1) * rw)
            h0 = _layer_norm(x_ref[r0 + j * rw:r0 + (j + 1) * rw, :], ln_ref[0:1, :], ln_ref[1:2, :])
            h1.append(_layer_norm(DEEPNORM_ALPHA * h0 + mix[rs], ln_ref[2:3, :], ln_ref[3:4, :]))
            yield
        h1 = jnp.concatenate(h1, axis=0)
        h1b = h1.astype(BF16)
        qs = []
        for hd in range(CA_HEADS):
            cs = slice(hd * CA_HEAD_DIM, (hd + 1) * CA_HEAD_DIM)
            qs.append((_dot(h1b, wq_ref[:, cs]) * (CA_HEAD_DIM ** -0.5)).astype(BF16))
            yield
        heads = []
        for hd in range(CA_HEADS):
            kh = kv_ref[:, hd * CA_HEAD_DIM:(hd + 1) * CA_HEAD_DIM]
            vh = kv_ref[:, D_MODEL + hd * CA_HEAD_DIM:D_MODEL + (hd + 1) * CA_HEAD_DIM]
            s = lax.dot_general(qs[hd], kh, _NT, preferred_element_type=F32)
            e = jnp.exp(s - jnp.max(s, axis=-1, keepdims=True))
            oh = _dot(e.astype(BF16), vh) / jnp.sum(e, axis=-1, keepdims=True)
            heads.append(oh.astype(BF16))
            yield
        o = jnp.concatenate(heads, axis=-1)
        ca = []
        for j in range(nsl):
            ca.append(_dot(o, wo_ref[:, j * cw:(j + 1) * cw]))
            yield
        ca = jnp.concatenate(ca, axis=-1)
        for j in range(nsl):
            rs = slice(j * rw, (j + 1) * rw)
            o_ref[r0 + j * rw:r0 + (j + 1) * rw, :] = _layer_norm(
                DEEPNORM_ALPHA * h1[rs] + ca[rs], ln_ref[4:5, :], ln_ref[5:6, :])
            yield

    _interleave([work(st) for st in range(MIX_SUBTILES)], lag=nsl)


def _mix_ca(x2, ys, yd, ln, wout, wq, kv, wo):
    tm = MIX_ROWS
    per_batch = SEQ // tm
    row = lambda i: (i, 0)
    const = lambda i: (0, 0)
    sq = pl.BlockSpec((D_MODEL, D_MODEL), const)
    return pl.pallas_call(
        _mix_ca_kernel,
        grid=(N_TOK // tm,),
        in_specs=[pl.BlockSpec((tm, D_MODEL), row), pl.BlockSpec((tm, 512), row), pl.BlockSpec((tm, 512), row),
                  pl.BlockSpec((N_LN_ROWS, D_MODEL), const), sq, sq,
                  pl.BlockSpec((MEM_LEN, 2 * D_MODEL), lambda i: (i // per_batch, 0)),
                  sq],
        out_specs=pl.BlockSpec((tm, D_MODEL), row),
        out_shape=jax.ShapeDtypeStruct((N_TOK, D_MODEL), F32),
        compiler_params=_params("parallel"),
        name="mix_ca",
    )(x2, ys, yd, ln, wout, wq, kv, wo)


def _ffn_kernel(h_ref, wgu_ref, wd_ref, ln_ref, o_ref):
    sub = FFN_ROWS // FFN_SUBTILES
    ck = FFN_CHUNK
    nsl = 4
    rw = sub // nsl

    def work(st):
        r0 = st * sub
        hb = h_ref[r0:r0 + sub, :].astype(BF16)
        acc = None
        for c in range(FFN_HIDDEN // ck):
            gate = _dot(hb, wgu_ref[:, c * ck:(c + 1) * ck])
            up = _dot(hb, wgu_ref[:, FFN_HIDDEN + c * ck:FFN_HIDDEN + (c + 1) * ck])
            act = (jax.nn.silu(gate) * up).astype(BF16)
            part = _dot(act, wd_ref[c * ck:(c + 1) * ck, :])
            acc = part if acc is None else acc + part
            yield
        for j in range(nsl):
            rs = slice(r0 + j * rw, r0 + (j + 1) * rw)
            o_ref[rs, :] = _layer_norm(DEEPNORM_ALPHA * h_ref[rs, :] + acc[j * rw:(j + 1) * rw],
                                       ln_ref[6:7, :], ln_ref[7:8, :])
            yield

    _interleave([work(st) for st in range(FFN_SUBTILES)], lag=FFN_HIDDEN // ck // 2 + 1)


def _ffn(h2, wgu, wd, ln):
    tm = FFN_ROWS
    row = lambda i: (i, 0)
    const = lambda i: (0, 0)
    resident = dict(pipeline_mode=pl.Buffered(1))
    return pl.pallas_call(
        _ffn_kernel,
        grid=(N_TOK // tm,),
        in_specs=[pl.BlockSpec((tm, D_MODEL), row),
                  pl.BlockSpec((D_MODEL, 2 * FFN_HIDDEN), const, **resident),
                  pl.BlockSpec((FFN_HIDDEN, D_MODEL), const, **resident),
                  pl.BlockSpec((N_LN_ROWS, D_MODEL), const)],
        out_specs=pl.BlockSpec((tm, D_MODEL), row),
        out_shape=jax.ShapeDtypeStruct((N_TOK, D_MODEL), F32),
        compiler_params=_params("parallel"),
        name="ffn",
    )(h2, wgu, wd, ln)


def kernel(x, mem, ln_in_g, ln_in_b, w_in, s5_lambda_re, s5_lambda_im, s5_log_dt, s5_b_re, s5_b_im,
           s5_c_re, s5_c_im, s5_d, s5_glu_w, s5_glu_b, diff_lq1, diff_lk1, diff_lq2, diff_lk2,
           diff_subln_g, rel_bias, w_out, ln1_g, ln1_b, ca_wq, ca_wkv, ca_wo, ln2_g, ln2_b,
           ffn_w_gate_up, ffn_w_down, ln3_g, ln3_b):
    x2 = x.reshape(N_TOK, D_MODEL)
    ln = jnp.stack([ln_in_g, ln_in_b, ln1_g[0], ln1_b[0], ln2_g[0], ln2_b[0], ln3_g[0], ln3_b[0]])
    lv = jnp.stack([diff_lq1[0], diff_lk1[0], diff_lq2[0], diff_lk2[0]])
    dg = jnp.stack([s5_d[0].reshape(S5_WIDTH), s5_glu_b[0]])

    u, q, k, v = _proj(x2, ln, w_in[0].astype(BF16))

    a8, w_s5 = _s5_prep(s5_lambda_re[0], s5_lambda_im[0], s5_log_dt[0], s5_b_re[0], s5_b_im[0],
                        s5_c_re[0], s5_c_im[0])
    y_s5 = _s5(u.reshape(BATCH, SEQ, S5_WIDTH), a8, w_s5, dg,
               s5_glu_w[0].astype(BF16)).reshape(N_TOK, S5_WIDTH)

    y_diff = _diff_attn(q, k, v, _bias_tiles(rel_bias), lv, diff_subln_g)

    kv = _kv(mem.reshape(BATCH * MEM_LEN, D_MODEL), ca_wkv[0].astype(BF16))
    h2 = _mix_ca(x2, y_s5, y_diff, ln, w_out[0].astype(BF16), ca_wq[0].astype(BF16), kv, ca_wo[0].astype(BF16))
    out = _ffn(h2, ffn_w_gate_up[0].astype(BF16), ffn_w_down[0].astype(BF16), ln)
    return out.reshape(BATCH, SEQ, D_MODEL)
```

```python
import functools
import math

import numpy as np

import jax
import jax.numpy as jnp
from jax import lax
from jax.experimental import pallas as pl
from jax.experimental.pallas import tpu as pltpu

F32 = jnp.float32
BF16 = jnp.bfloat16

D_MODEL = 1024
BATCH = 8
SEQ = 2048
N_TOK = BATCH * SEQ
MEM_LEN = 256
S5_WIDTH = 512
S5_GROUP = 16
S5_GROUPS = 32
S5_STATE = 64
S5_COLS = S5_GROUPS * S5_STATE
DIFF_WIDTH = 512
DIFF_HEAD_DIM = 64
DIFF_V_DIM = 128
N_DIFF_HEADS = 4
MIX_IN = 2048
NUM_BUCKETS = 32
MAX_DISTANCE = 128
CA_HEADS = 4
CA_HEAD_DIM = 256
FFN_HIDDEN = 2816
DEEPNORM_ALPHA = 2.0 ** 0.25
LN_EPS = 1e-5
LAMBDA_INIT = 0.8 - 0.6 * math.exp(0.0)
LOG2E = math.log2(math.e)

VMEM_LIMIT_BYTES = 56 * 1024 * 1024
MXU_DIM = 256
LANES = 128

PROJ_ROWS = 1024
PROJ_SUBTILES = 2
S5_SUB_STEPS = 64
S5_SUBTILES = 2
S5_SCAN_CHUNK = 512
ATT_TILE = 512
ATT_HEADS = 1
FFN_CHUNK = 256
FFN_ROWS = 1024
FFN_SUBTILES = 2
MIX_ROWS = 1024
MIX_SUBTILES = 2
MIX_SLICES = 4
N_LN_ROWS = 8
KV_ROWS = 1024

_NT = (((1,), (1,)), ((), ()))


def _params(*sem):
    return pltpu.CompilerParams(dimension_semantics=sem, vmem_limit_bytes=VMEM_LIMIT_BYTES)


def _layer_norm(x, g, b):
    mu = jnp.mean(x, axis=-1, keepdims=True)
    xc = x - mu
    var = jnp.mean(xc * xc, axis=-1, keepdims=True)
    return xc * lax.rsqrt(var + LN_EPS) * g + b


def _dot(a, b):
    return jnp.dot(a, b, preferred_element_type=F32)


def _cast_once(w_ref, wbf_ref):
    @pl.when(pl.program_id(0) == 0)
    def _():
        wbf_ref[...] = w_ref[...].astype(BF16)


def _resident(shape):
    return pl.BlockSpec(shape, lambda *_: (0,) * len(shape), pipeline_mode=pl.Buffered(1))


def _interleave(gens, lag):
    gens = list(gens)
    live = [True] * len(gens)
    tick = 0
    while any(live):
        for i, g in enumerate(gens):
            if live[i] and tick >= lag * i:
                try:
                    next(g)
                except StopIteration:
                    live[i] = False
        tick += 1


def _s5_prep_kernel(lam_ref, bc_ref, a_ref, w_ref):
    lr = lam_ref[0:1, :]
    li = lam_ref[1:2, :]
    dt = jnp.exp(lam_ref[2:3, :])
    mag = jnp.exp(lr * dt)
    ang = li * dt
    ar = mag * jnp.cos(ang)
    ai = mag * jnp.sin(ang)
    den = lr * lr + li * li
    nr = ar - 1.0
    fr = (nr * lr + ai * li) / den
    fi = (ai * lr - nr * li) / den
    a_ref[0] = jnp.broadcast_to(ar, (BATCH, S5_COLS))
    a_ref[1] = jnp.broadcast_to(ai, (BATCH, S5_COLS))
    b_r = bc_ref[0]
    b_i = bc_ref[1]
    mats = (fr * b_r - fi * b_i, fr * b_i + fi * b_r, bc_ref[2], bc_ref[3])
    half = S5_COLS // 2
    lane_group = lax.broadcasted_iota(jnp.int32, (S5_GROUP, half), 1) // S5_STATE
    for k, mat in enumerate(mats):
        for hf in range(2):
            src = mat[:, hf * half:(hf + 1) * half]
            for gl in range(MXU_DIM // S5_GROUP):
                w_ref[k, hf, gl * S5_GROUP:(gl + 1) * S5_GROUP, :] = jnp.where(
                    lane_group == gl, src, 0.0).astype(BF16)


def _s5_prep(lam_re, lam_im, log_dt, b_re, b_im, c_re, c_im):
    lam = jnp.stack([lam_re.reshape(S5_COLS), lam_im.reshape(S5_COLS), jnp.repeat(log_dt, S5_STATE)])
    bc = jnp.stack([jnp.transpose(b_re, (2, 0, 1)), jnp.transpose(b_im, (2, 0, 1)),
                    jnp.transpose(c_re, (1, 0, 2)), jnp.transpose(c_im, (1, 0, 2))]).reshape(4, S5_GROUP, S5_COLS)
    return pl.pallas_call(
        _s5_prep_kernel,
        out_shape=(jax.ShapeDtypeStruct((2, BATCH, S5_COLS), F32),
                   jax.ShapeDtypeStruct((4, 2, MXU_DIM, S5_COLS // 2), BF16)),
        name="s5_prep",
    )(lam, bc)


def _bias_prep_kernel(rel_ref, bucket_ref, o_ref):
    for j in range(2):
        bucket = bucket_ref[j]
        for h in range(N_DIFF_HEADS):
            far = rel_ref[NUM_BUCKETS - 1, h]
            acc = jnp.zeros(bucket.shape, F32)
            for b in range(NUM_BUCKETS):
                acc = jnp.where(bucket == b, (rel_ref[b, h] - far) * LOG2E, acc)
            o_ref[h, j] = jnp.where(bucket < 0, -jnp.inf, acc)


def _t5_bucket(dist):
    max_exact = NUM_BUCKETS // 2
    df = np.maximum(dist, 1).astype(np.float32)
    large = max_exact + (np.log(df / np.float32(max_exact)) / np.float32(math.log(MAX_DISTANCE / max_exact))
                         * np.float32(NUM_BUCKETS - max_exact)).astype(np.int32)
    large = np.minimum(large, NUM_BUCKETS - 1)
    return np.where(dist < max_exact, dist, large).astype(np.int32)


def _bias_tiles(rel_bias):
    t = LANES
    qpos = np.arange(t, dtype=np.int32)[:, None]
    kpos = np.arange(t, dtype=np.int32)[None, :]
    d_diag = qpos - kpos
    d_prev = d_diag + t
    bucket = jnp.asarray(np.stack([np.where(d_diag >= 0, _t5_bucket(np.maximum(d_diag, 0)), -1),
                                   _t5_bucket(d_prev)]).astype(np.int32))
    return pl.pallas_call(
        _bias_prep_kernel,
        in_specs=[pl.BlockSpec(memory_space=pltpu.SMEM), pl.BlockSpec(memory_space=pltpu.VMEM)],
        out_specs=pl.BlockSpec(memory_space=pltpu.VMEM),
        out_shape=jax.ShapeDtypeStruct((N_DIFF_HEADS, 2, t, t), F32),
        name="bias_prep",
    )(rel_bias, bucket)


def _proj_kernel(x_ref, ln_ref, w32_ref, u_ref, q_ref, k_ref, v_ref, w_ref):
    _cast_once(w32_ref, w_ref)
    sub = PROJ_ROWS // PROJ_SUBTILES
    nsl = 4
    rw = sub // nsl

    def work(st):
        r0 = st * sub
        h = []
        for j in range(nsl):
            h.append(_layer_norm(x_ref[r0 + j * rw:r0 + (j + 1) * rw, :], ln_ref[0:1, :], ln_ref[1:2, :])
                     .astype(BF16))
            yield
        h = jnp.concatenate(h, axis=0)
        rs = slice(r0, r0 + sub)
        u_ref[rs, :] = _dot(h, w_ref[:, 0:512])
        yield
        q_ref[rs, :] = (_dot(h, w_ref[:, 512:1024]) * (DIFF_HEAD_DIM ** -0.5 * LOG2E)).astype(BF16)
        yield
        k_ref[rs, :] = _dot(h, w_ref[:, 1024:1536]).astype(BF16)
        yield
        v_ref[rs, :] = _dot(h, w_ref[:, 1536:2048]).astype(BF16)
        yield

    _interleave([work(st) for st in range(PROJ_SUBTILES)], lag=nsl)


def _proj(x2, ln, w):
    tm = PROJ_ROWS
    row = lambda i: (i, 0)
    const = lambda i: (0, 0)
    return pl.pallas_call(
        _proj_kernel,
        grid=(N_TOK // tm,),
        in_specs=[pl.BlockSpec((tm, D_MODEL), row), pl.BlockSpec((N_LN_ROWS, D_MODEL), const),
                  _resident((D_MODEL, MIX_IN))],
        out_specs=[pl.BlockSpec((tm, 512), row)] * 4,
        out_shape=(jax.ShapeDtypeStruct((N_TOK, 512), F32),) + (jax.ShapeDtypeStruct((N_TOK, 512), BF16),) * 3,
        scratch_shapes=[pltpu.VMEM((D_MODEL, MIX_IN), BF16)],
        compiler_params=_params("arbitrary"),
        name="proj",
    )(x2, ln, w)


def _s5_kernel(u_ref, a_ref, w_ref, dg_ref, gw32_ref, y_ref, sre, sim, xre, xim, utm, ytm, gw_ref):
    @pl.when(pl.program_id(0) == 0)
    def _():
        xre[...] = jnp.zeros_like(xre)
        xim[...] = jnp.zeros_like(xim)

    _cast_once(gw32_ref, gw_ref)
    half = S5_COLS // 2
    cw = S5_SCAN_CHUNK
    nchunk = S5_COLS // cw
    rows = S5_SUB_STEPS * BATCH
    steps = S5_SUBTILES * S5_SUB_STEPS
    nslab = S5_WIDTH // LANES
    carry = [None] * nchunk

    for b in range(BATCH):
        for sl in range(nslab):
            utm[sl, pl.ds(b, steps, stride=BATCH), :] = u_ref[b, :, sl * LANES:(sl + 1) * LANES]

    def work(st):
        r0 = st * rows
        u = jnp.concatenate([utm[sl, r0:r0 + rows, :] for sl in range(nslab)], axis=-1)
        ub = u.astype(BF16)
        for hf in range(2):
            uh = ub[:, hf * MXU_DIM:(hf + 1) * MXU_DIM]
            sre[r0:r0 + rows, hf * half:(hf + 1) * half] = _dot(uh, w_ref[0, hf])
            yield
            sim[r0:r0 + rows, hf * half:(hf + 1) * half] = _dot(uh, w_ref[1, hf])
            yield
        for c in range(nchunk):
            cs = slice(c * cw, (c + 1) * cw)
            ar = a_ref[0, :, cs]
            ai = a_ref[1, :, cs]
            xr, xi = (xre[:, cs], xim[:, cs]) if st == 0 else carry[c]
            for t in range(S5_SUB_STEPS):
                r = r0 + t * BATCH
                nr = ar * xr - ai * xi + sre[r:r + BATCH, cs]
                ni = ar * xi + ai * xr + sim[r:r + BATCH, cs]
                sre[r:r + BATCH, cs] = nr
                sim[r:r + BATCH, cs] = ni
                xr, xi = nr, ni
            carry[c] = (xr, xi)
            if st == S5_SUBTILES - 1:
                xre[:, cs] = xr
                xim[:, cs] = xi
            yield
        ys = []
        for hf in range(2):
            hs = slice(hf * half, (hf + 1) * half)
            ys.append(lax.dot_general(sre[r0:r0 + rows, hs].astype(BF16), w_ref[2, hf], _NT,
                                      preferred_element_type=F32)
                      - lax.dot_general(sim[r0:r0 + rows, hs].astype(BF16), w_ref[3, hf], _NT,
                                        preferred_element_type=F32))
            yield
        y = jax.nn.gelu(jnp.concatenate(ys, axis=-1) + dg_ref[0:1, :] * u)
        yield
        z = _dot(y.astype(BF16), gw_ref[...]) + dg_ref[1:2, :]
        y = y * jax.nn.sigmoid(z)
        for sl in range(nslab):
            ytm[sl, r0:r0 + rows, :] = y[:, sl * LANES:(sl + 1) * LANES]
        t0 = st * S5_SUB_STEPS
        for b in range(BATCH):
            y_ref[b, t0:t0 + S5_SUB_STEPS, :] = jnp.concatenate(
                [ytm[sl, pl.ds(r0 + b, S5_SUB_STEPS, stride=BATCH), :] for sl in range(nslab)],
                axis=-1).astype(BF16)
        yield

    _interleave([work(st) for st in range(S5_SUBTILES)], lag=4)


def _s5(u, a8, w, dg, gw):
    steps = S5_SUBTILES * S5_SUB_STEPS
    rows = steps * BATCH
    nslab = S5_WIDTH // LANES
    blk = pl.BlockSpec((BATCH, steps, S5_WIDTH), lambda i: (0, i, 0))
    c2 = lambda i: (0, 0)
    return pl.pallas_call(
        _s5_kernel,
        grid=(SEQ // steps,),
        in_specs=[blk,
                  pl.BlockSpec((2, BATCH, S5_COLS), lambda i: (0, 0, 0)),
                  pl.BlockSpec((4, 2, MXU_DIM, S5_COLS // 2), lambda i: (0, 0, 0, 0)),
                  pl.BlockSpec((2, S5_WIDTH), c2), pl.BlockSpec((S5_WIDTH, S5_WIDTH), c2)],
        out_specs=blk,
        out_shape=jax.ShapeDtypeStruct((BATCH, SEQ, S5_WIDTH), BF16),
        scratch_shapes=[pltpu.VMEM((rows, S5_COLS), F32), pltpu.VMEM((rows, S5_COLS), F32),
                        pltpu.VMEM((BATCH, S5_COLS), F32), pltpu.VMEM((BATCH, S5_COLS), F32),
                        pltpu.VMEM((nslab, rows, LANES), F32), pltpu.VMEM((nslab, rows, LANES), F32),
                        pltpu.VMEM((S5_WIDTH, S5_WIDTH), BF16)],
        compiler_params=_params("arbitrary"),
        name="s5",
    )(u, a8, w, dg, gw)


def _diff_attn_kernel(q_ref, k_ref, v_ref, bias_ref, lv_ref, g_ref, o_ref):
    t = ATT_TILE
    half = t // 2
    sb = LANES
    lam = (jnp.exp(jnp.sum(lv_ref[0:1, :] * lv_ref[1:2, :], axis=-1, keepdims=True))
           - jnp.exp(jnp.sum(lv_ref[2:3, :] * lv_ref[3:4, :], axis=-1, keepdims=True)) + LAMBDA_INIT)
    lane = lax.broadcasted_iota(jnp.int32, (t, DIFF_V_DIM), 1)
    zero = jnp.zeros((t, DIFF_V_DIM), BF16)

    def head_work(hh):
        hs = slice(hh * DIFF_V_DIM, (hh + 1) * DIFF_V_DIM)

        def scores(qm, k0, width):
            kb = k_ref[k0:k0 + width, hs]
            return [lax.dot_general(qm[mi], kb, _NT, preferred_element_type=F32) for mi in range(2)]

        def lane_groups(s, r0, k0):
            rows, width = s.shape
            groups = []
            for g in range(width // sb):
                kb = (k0 + g * sb) // sb
                col = s[:, g * sb:(g + 1) * sb]
                qbs = [(r0 + i * sb) // sb for i in range(rows // sb)]
                if all(kb < qb - 1 for qb in qbs):
                    groups.append(col)
                    continue
                pieces = []
                for i, qb in enumerate(qbs):
                    piece = col[i * sb:(i + 1) * sb, :]
                    if kb == qb:
                        piece = piece + bias_ref[hh, 0]
                    elif kb == qb - 1:
                        piece = piece + bias_ref[hh, 1]
                    elif kb > qb:
                        piece = jnp.full((sb, sb), -jnp.inf, F32)
                    pieces.append(piece)
                groups.append(jnp.concatenate(pieces, axis=0))
            return groups

        def update(s, r0, k0, state):
            m_old, l_old, acc_old = state
            sg = lane_groups(s, r0, k0)
            vb = v_ref[k0:k0 + s.shape[1], hs]
            row_max = jnp.max(functools.reduce(jnp.maximum, sg), axis=-1, keepdims=True)
            if m_old is None:
                m_new = jnp.broadcast_to(row_max, (s.shape[0], LANES))
                pg = [jnp.exp2(x - m_new) for x in sg]
                p = jnp.concatenate(pg, axis=-1).astype(BF16)
                return m_new, functools.reduce(jnp.add, pg), _dot(p, vb)
            m_new = jnp.maximum(m_old, row_max)
            alpha = jnp.exp2(m_old - m_new)
            pg = [jnp.exp2(x - m_new) for x in sg]
            p = jnp.concatenate(pg, axis=-1).astype(BF16)
            return m_new, alpha * l_old + functools.reduce(jnp.add, pg), alpha * acc_old + _dot(p, vb)

        def rows_of(state, lo, hi):
            return tuple(None if a is None else a[lo:hi] for a in state)

        for c in range(SEQ // t):
            r0 = c * t
            q = q_ref[r0:r0 + t, hs]
            qm = (jnp.where(lane < DIFF_HEAD_DIM, q, zero), jnp.where(lane >= DIFF_HEAD_DIM, q, zero))
            qtop = [x[:half] for x in qm]
            qbot = [x[half:] for x in qm]

            def diag_scores(qtop=qtop, qbot=qbot, r0=r0):
                return scores(qtop, r0, half), scores(qbot, r0, t)

            state = [(None, None, None), (None, None, None)]
            s_next = scores(qm, 0, t) if c > 0 else diag_scores()
            for j in range(c):
                s_cur = s_next
                s_next = scores(qm, (j + 1) * t, t) if j + 1 < c else diag_scores()
                state = [update(s_cur[mi], r0, j * t, state[mi]) for mi in range(2)]
                yield
            s_top, s_bot = s_next
            outs = []
            for mi in range(2):
                top = update(s_top[mi], r0, r0, rows_of(state[mi], 0, half))
                bot = update(s_bot[mi], r0 + half, r0, rows_of(state[mi], half, t))
                l = jnp.concatenate([top[1], bot[1]], axis=0)
                acc = jnp.concatenate([top[2], bot[2]], axis=0)
                outs.append(acc / jnp.sum(l, axis=-1, keepdims=True))
            o = outs[0] - lam * outs[1]
            ms = jnp.mean(o * o, axis=-1, keepdims=True)
            y = o * lax.rsqrt(ms + LN_EPS) * g_ref[...]
            o_ref[r0:r0 + t, hs] = (y * (1.0 - LAMBDA_INIT)).astype(BF16)
            yield

    _interleave([head_work(hh) for hh in range(ATT_HEADS)], lag=1)


def _diff_attn(q, k, v, bias, lv, g):
    vec = lambda b, h: (0, 0)
    seq = pl.BlockSpec((SEQ, ATT_HEADS * DIFF_V_DIM), lambda b, h: (b, h))
    return pl.pallas_call(
        _diff_attn_kernel,
        grid=(BATCH, N_DIFF_HEADS // ATT_HEADS),
        in_specs=[seq, seq, seq,
                  pl.BlockSpec((ATT_HEADS, 2, LANES, LANES), lambda b, h: (h, 0, 0, 0)),
                  pl.BlockSpec((4, DIFF_HEAD_DIM), vec), pl.BlockSpec((1, DIFF_V_DIM), vec)],
        out_specs=seq,
        out_shape=jax.ShapeDtypeStruct((N_TOK, DIFF_WIDTH), BF16),
        compiler_params=_params("parallel", "parallel"),
        name="diff_attn",
    )(q, k, v, bias, lv, g)


def _kv_kernel(m_ref, w32_ref, o_ref, w_ref):
    _cast_once(w32_ref, w_ref)
    o_ref[...] = _dot(m_ref[...].astype(BF16), w_ref[...]).astype(BF16)


def _kv(mem2, wkv):
    tm = KV_ROWS
    return pl.pallas_call(
        _kv_kernel,
        grid=(BATCH * MEM_LEN // tm,),
        in_specs=[pl.BlockSpec((tm, D_MODEL), lambda i: (i, 0)), _resident((D_MODEL, 2 * D_MODEL))],
        out_specs=pl.BlockSpec((tm, 2 * D_MODEL), lambda i: (i, 0)),
        out_shape=jax.ShapeDtypeStruct((BATCH * MEM_LEN, 2 * D_MODEL), BF16),
        scratch_shapes=[pltpu.VMEM((D_MODEL, 2 * D_MODEL), BF16)],
        compiler_params=_params("arbitrary"),
        name="kv",
    )(mem2, wkv)


def _mix_ca_kernel(x_ref, ys_ref, yd_ref, ln_ref, wout32_ref, wq32_ref, kv_ref, wo32_ref, o_ref,
                   wout_ref, wq_ref, wo_ref):
    _cast_once(wout32_ref, wout_ref)
    _cast_once(wq32_ref, wq_ref)
    _cast_once(wo32_ref, wo_ref)
    sub = MIX_ROWS // MIX_SUBTILES
    nsl = MIX_SLICES
    cw = D_MODEL // nsl
    rw = sub // nsl

    def work(st):
        r0 = st * sub
        ys = ys_ref[r0:r0 + sub, :]
        yd = yd_ref[r0:r0 + sub, :]
        mix = []
        for j in range(nsl):
            cs = slice(j * cw, (j + 1) * cw)
            mix.append(_dot(ys, wout_ref[0:S5_WIDTH, cs]) + _dot(yd, wout_ref[S5_WIDTH:, cs]))
            yield
        mix = jnp.concatenate(mix, axis=-1)
        h1 = []
        for j in range(nsl):
            rs = slice(j * rw, (j + 1) * rw)
            h0 = _layer_norm(x_ref[r0 + j * rw:r0 + (j + 1) * rw, :], ln_ref[0:1, :], ln_ref[1:2, :])
            h1.append(_layer_norm(DEEPNORM_ALPHA * h0 + mix[rs], ln_ref[2:3, :], ln_ref[3:4, :]))
            yield
        h1 = jnp.concatenate(h1, axis=0)
        h1b = h1.astype(BF16)
        qs = []
        for hd in range(CA_HEADS):
            cs = slice(hd * CA_HEAD_DIM, (hd + 1) * CA_HEAD_DIM)
            qs.append((_dot(h1b, wq_ref[:, cs]) * (CA_HEAD_DIM ** -0.5)).astype(BF16))
            yield
        heads = []
        for hd in range(CA_HEADS):
            kh = kv_ref[:, hd * CA_HEAD_DIM:(hd + 1) * CA_HEAD_DIM]
            vh = kv_ref[:, D_MODEL + hd * CA_HEAD_DIM:D_MODEL + (hd + 1) * CA_HEAD_DIM]
            s = lax.dot_general(qs[hd], kh, _NT, preferred_element_type=F32)
            e = jnp.exp(s - jnp.max(s, axis=-1, keepdims=True))
            oh = _dot(e.astype(BF16), vh) / jnp.sum(e, axis=-1, keepdims=True)
            heads.append(oh.astype(BF16))
            yield
        o = jnp.concatenate(heads, axis=-1)
        ca = []
        for j in range(nsl):
            ca.append(_dot(o, wo_ref[:, j * cw:(j + 1) * cw]))
            yield
        ca = jnp.concatenate(ca, axis=-1)
        for j in range(nsl):
            rs = slice(j * rw, (j + 1) * rw)
            o_ref[r0 + j * rw:r0 + (j + 1) * rw, :] = _layer_norm(
                DEEPNORM_ALPHA * h1[rs] + ca[rs], ln_ref[4:5, :], ln_ref[5:6, :])
            yield

    _interleave([work(st) for st in range(MIX_SUBTILES)], lag=nsl)


def _mix_ca(x2, ys, yd, ln, wout, wq, kv, wo):
    tm = MIX_ROWS
    per_batch = SEQ // tm
    row = lambda i: (i, 0)
    const = lambda i: (0, 0)
    sq = _resident((D_MODEL, D_MODEL))
    return pl.pallas_call(
        _mix_ca_kernel,
        grid=(N_TOK // tm,),
        in_specs=[pl.BlockSpec((tm, D_MODEL), row), pl.BlockSpec((tm, 512), row), pl.BlockSpec((tm, 512), row),
                  pl.BlockSpec((N_LN_ROWS, D_MODEL), const), sq, sq,
                  pl.BlockSpec((MEM_LEN, 2 * D_MODEL), lambda i: (i // per_batch, 0)),
                  sq],
        out_specs=pl.BlockSpec((tm, D_MODEL), row),
        out_shape=jax.ShapeDtypeStruct((N_TOK, D_MODEL), F32),
        scratch_shapes=[pltpu.VMEM((D_MODEL, D_MODEL), BF16)] * 3,
        compiler_params=_params("arbitrary"),
        name="mix_ca",
    )(x2, ys, yd, ln, wout, wq, kv, wo)


def _ffn_kernel(h_ref, wgu_ref, wd_ref, ln_ref, o_ref):
    sub = FFN_ROWS // FFN_SUBTILES
    ck = FFN_CHUNK
    nsl = 4
    rw = sub // nsl

    def work(st):
        r0 = st * sub
        hb = h_ref[r0:r0 + sub, :].astype(BF16)
        acc = None
        for c in range(FFN_HIDDEN // ck):
            gate = _dot(hb, wgu_ref[:, c * ck:(c + 1) * ck])
            up = _dot(hb, wgu_ref[:, FFN_HIDDEN + c * ck:FFN_HIDDEN + (c + 1) * ck])
            act = (jax.nn.silu(gate) * up).astype(BF16)
            part = _dot(act, wd_ref[c * ck:(c + 1) * ck, :])
            acc = part if acc is None else acc + part
            yield
        for j in range(nsl):
            rs = slice(r0 + j * rw, r0 + (j + 1) * rw)
            o_ref[rs, :] = _layer_norm(DEEPNORM_ALPHA * h_ref[rs, :] + acc[j * rw:(j + 1) * rw],
                                       ln_ref[6:7, :], ln_ref[7:8, :])
            yield

    _interleave([work(st) for st in range(FFN_SUBTILES)], lag=FFN_HIDDEN // ck // 2 + 1)


def _ffn(h2, wgu, wd, ln):
    tm = FFN_ROWS
    row = lambda i: (i, 0)
    const = lambda i: (0, 0)
    resident = dict(pipeline_mode=pl.Buffered(1))
    return pl.pallas_call(
        _ffn_kernel,
        grid=(N_TOK // tm,),
        in_specs=[pl.BlockSpec((tm, D_MODEL), row),
                  pl.BlockSpec((D_MODEL, 2 * FFN_HIDDEN), const, **resident),
                  pl.BlockSpec((FFN_HIDDEN, D_MODEL), const, **resident),
                  pl.BlockSpec((N_LN_ROWS, D_MODEL), const)],
        out_specs=pl.BlockSpec((tm, D_MODEL), row),
        out_shape=jax.ShapeDtypeStruct((N_TOK, D_MODEL), F32),
        compiler_params=_params("parallel"),
        name="ffn",
    )(h2, wgu, wd, ln)


def kernel(x, mem, ln_in_g, ln_in_b, w_in, s5_lambda_re, s5_lambda_im, s5_log_dt, s5_b_re, s5_b_im,
           s5_c_re, s5_c_im, s5_d, s5_glu_w, s5_glu_b, diff_lq1, diff_lk1, diff_lq2, diff_lk2,
           diff_subln_g, rel_bias, w_out, ln1_g, ln1_b, ca_wq, ca_wkv, ca_wo, ln2_g, ln2_b,
           ffn_w_gate_up, ffn_w_down, ln3_g, ln3_b):
    x2 = x.reshape(N_TOK, D_MODEL)
    ln = jnp.stack([ln_in_g, ln_in_b, ln1_g[0], ln1_b[0], ln2_g[0], ln2_b[0], ln3_g[0], ln3_b[0]])
    lv = jnp.stack([diff_lq1[0], diff_lk1[0], diff_lq2[0], diff_lk2[0]])
    dg = jnp.stack([s5_d[0].reshape(S5_WIDTH), s5_glu_b[0]])

    u, q, k, v = _proj(x2, ln, w_in[0])

    a8, w_s5 = _s5_prep(s5_lambda_re[0], s5_lambda_im[0], s5_log_dt[0], s5_b_re[0], s5_b_im[0],
                        s5_c_re[0], s5_c_im[0])
    y_s5 = _s5(u.reshape(BATCH, SEQ, S5_WIDTH), a8, w_s5, dg, s5_glu_w[0]).reshape(N_TOK, S5_WIDTH)

    y_diff = _diff_attn(q, k, v, _bias_tiles(rel_bias), lv, diff_subln_g)

    kv = _kv(mem.reshape(BATCH * MEM_LEN, D_MODEL), ca_wkv[0])
    h2 = _mix_ca(x2, y_s5, y_diff, ln, w_out[0], ca_wq[0], kv, ca_wo[0])
    out = _ffn(h2, ffn_w_gate_up[0].astype(BF16), ffn_w_down[0].astype(BF16), ln)
    return out.reshape(BATCH, SEQ, D_MODEL)
```

```python
import functools
import math

import numpy as np

import jax
import jax.numpy as jnp
from jax import lax
from jax.experimental import pallas as pl
from jax.experimental.pallas import tpu as pltpu

F32 = jnp.float32
BF16 = jnp.bfloat16

D_MODEL = 1024
BATCH = 8
SEQ = 2048
N_TOK = BATCH * SEQ
MEM_LEN = 256
S5_WIDTH = 512
S5_GROUP = 16
S5_GROUPS = 32
S5_STATE = 64
S5_COLS = S5_GROUPS * S5_STATE
DIFF_WIDTH = 512
DIFF_HEAD_DIM = 64
DIFF_V_DIM = 128
N_DIFF_HEADS = 4
MIX_IN = 2048
NUM_BUCKETS = 32
MAX_DISTANCE = 128
CA_HEADS = 4
CA_HEAD_DIM = 256
FFN_HIDDEN = 2816
DEEPNORM_ALPHA = 2.0 ** 0.25
LN_EPS = 1e-5
LAMBDA_INIT = 0.8 - 0.6 * math.exp(0.0)
LOG2E = math.log2(math.e)

VMEM_LIMIT_BYTES = 56 * 1024 * 1024
MXU_DIM = 256
LANES = 128

PROJ_ROWS = 1024
PROJ_SUBTILES = 2
S5_SUB_STEPS = 64
S5_SUBTILES = 2
S5_SCAN_CHUNK = 512
ATT_TILE = 512
ATT_HEADS = 1
FFN_CHUNK = 256
FFN_ROWS = 1024
FFN_SUBTILES = 2
MIX_ROWS = 1024
MIX_SUBTILES = 2
MIX_SLICES = 4
N_LN_ROWS = 8
KV_ROWS = 1024

_NT = (((1,), (1,)), ((), ()))


def _params(*sem):
    return pltpu.CompilerParams(dimension_semantics=sem, vmem_limit_bytes=VMEM_LIMIT_BYTES)


def _layer_norm(x, g, b):
    mu = jnp.mean(x, axis=-1, keepdims=True)
    xc = x - mu
    var = jnp.mean(xc * xc, axis=-1, keepdims=True)
    return xc * lax.rsqrt(var + LN_EPS) * g + b


def _dot(a, b):
    return jnp.dot(a, b, preferred_element_type=F32)


def _cast_once(w_ref, wbf_ref):
    @pl.when(pl.program_id(0) == 0)
    def _():
        wbf_ref[...] = w_ref[...].astype(BF16)


def _resident(shape):
    return pl.BlockSpec(shape, lambda *_: (0,) * len(shape), pipeline_mode=pl.Buffered(1))


def _interleave(gens, lag):
    gens = list(gens)
    live = [True] * len(gens)
    tick = 0
    while any(live):
        for i, g in enumerate(gens):
            if live[i] and tick >= lag * i:
                try:
                    next(g)
                except StopIteration:
                    live[i] = False
        tick += 1


def _s5_prep_kernel(lam_ref, bc_ref, a_ref, w_ref):
    lr = lam_ref[0:1, :]
    li = lam_ref[1:2, :]
    dt = jnp.exp(lam_ref[2:3, :])
    mag = jnp.exp(lr * dt)
    ang = li * dt
    ar = mag * jnp.cos(ang)
    ai = mag * jnp.sin(ang)
    den = lr * lr + li * li
    nr = ar - 1.0
    fr = (nr * lr + ai * li) / den
    fi = (ai * lr - nr * li) / den
    a_ref[0] = jnp.broadcast_to(ar, (BATCH, S5_COLS))
    a_ref[1] = jnp.broadcast_to(ai, (BATCH, S5_COLS))
    b_r = bc_ref[0]
    b_i = bc_ref[1]
    mats = (fr * b_r - fi * b_i, fr * b_i + fi * b_r, bc_ref[2], bc_ref[3])
    half = S5_COLS // 2
    lane_group = lax.broadcasted_iota(jnp.int32, (S5_GROUP, half), 1) // S5_STATE
    for k, mat in enumerate(mats):
        for hf in range(2):
            src = mat[:, hf * half:(hf + 1) * half]
            for gl in range(MXU_DIM // S5_GROUP):
                w_ref[k, hf, gl * S5_GROUP:(gl + 1) * S5_GROUP, :] = jnp.where(
                    lane_group == gl, src, 0.0).astype(BF16)


def _s5_prep(lam_re, lam_im, log_dt, b_re, b_im, c_re, c_im):
    lam = jnp.stack([lam_re.reshape(S5_COLS), lam_im.reshape(S5_COLS), jnp.repeat(log_dt, S5_STATE)])
    bc = jnp.stack([jnp.transpose(b_re, (2, 0, 1)), jnp.transpose(b_im, (2, 0, 1)),
                    jnp.transpose(c_re, (1, 0, 2)), jnp.transpose(c_im, (1, 0, 2))]).reshape(4, S5_GROUP, S5_COLS)
    return pl.pallas_call(
        _s5_prep_kernel,
        out_shape=(jax.ShapeDtypeStruct((2, BATCH, S5_COLS), F32),
                   jax.ShapeDtypeStruct((4, 2, MXU_DIM, S5_COLS // 2), BF16)),
        name="s5_prep",
    )(lam, bc)


def _bias_prep_kernel(rel_ref, bucket_ref, o_ref):
    for j in range(2):
        bucket = bucket_ref[j]
        for h in range(N_DIFF_HEADS):
            far = rel_ref[NUM_BUCKETS - 1, h]
            acc = jnp.zeros(bucket.shape, F32)
            for b in range(NUM_BUCKETS):
                acc = jnp.where(bucket == b, (rel_ref[b, h] - far) * LOG2E, acc)
            o_ref[h, j] = jnp.where(bucket < 0, -jnp.inf, acc)


def _t5_bucket(dist):
    max_exact = NUM_BUCKETS // 2
    df = np.maximum(dist, 1).astype(np.float32)
    large = max_exact + (np.log(df / np.float32(max_exact)) / np.float32(math.log(MAX_DISTANCE / max_exact))
                         * np.float32(NUM_BUCKETS - max_exact)).astype(np.int32)
    large = np.minimum(large, NUM_BUCKETS - 1)
    return np.where(dist < max_exact, dist, large).astype(np.int32)


def _bias_tiles(rel_bias):
    t = LANES
    qpos = np.arange(t, dtype=np.int32)[:, None]
    kpos = np.arange(t, dtype=np.int32)[None, :]
    d_diag = qpos - kpos
    d_prev = d_diag + t
    bucket = jnp.asarray(np.stack([np.where(d_diag >= 0, _t5_bucket(np.maximum(d_diag, 0)), -1),
                                   _t5_bucket(d_prev)]).astype(np.int32))
    return pl.pallas_call(
        _bias_prep_kernel,
        in_specs=[pl.BlockSpec(memory_space=pltpu.SMEM), pl.BlockSpec(memory_space=pltpu.VMEM)],
        out_specs=pl.BlockSpec(memory_space=pltpu.VMEM),
        out_shape=jax.ShapeDtypeStruct((N_DIFF_HEADS, 2, t, t), F32),
        name="bias_prep",
    )(rel_bias, bucket)


def _proj_kernel(x_ref, ln_ref, w32_ref, h_ref, u_ref, q_ref, k_ref, v_ref, w_ref):
    _cast_once(w32_ref, w_ref)
    sub = PROJ_ROWS // PROJ_SUBTILES
    nsl = 4
    rw = sub // nsl

    def work(st):
        r0 = st * sub
        h = []
        for j in range(nsl):
            rows = slice(r0 + j * rw, r0 + (j + 1) * rw)
            hj = _layer_norm(x_ref[rows, :], ln_ref[0:1, :], ln_ref[1:2, :])
            h_ref[rows, :] = hj
            h.append(hj.astype(BF16))
            yield
        h = jnp.concatenate(h, axis=0)
        rs = slice(r0, r0 + sub)
        u_ref[rs, :] = _dot(h, w_ref[:, 0:512])
        yield
        q_ref[rs, :] = (_dot(h, w_ref[:, 512:1024]) * (DIFF_HEAD_DIM ** -0.5 * LOG2E)).astype(BF16)
        yield
        k_ref[rs, :] = _dot(h, w_ref[:, 1024:1536]).astype(BF16)
        yield
        v_ref[rs, :] = _dot(h, w_ref[:, 1536:2048]).astype(BF16)
        yield

    _interleave([work(st) for st in range(PROJ_SUBTILES)], lag=nsl)


def _proj(x2, ln, w):
    tm = PROJ_ROWS
    row = lambda i: (i, 0)
    const = lambda i: (0, 0)
    return pl.pallas_call(
        _proj_kernel,
        grid=(N_TOK // tm,),
        in_specs=[pl.BlockSpec((tm, D_MODEL), row), pl.BlockSpec((N_LN_ROWS, D_MODEL), const),
                  _resident((D_MODEL, MIX_IN))],
        out_specs=[pl.BlockSpec((tm, D_MODEL), row)] + [pl.BlockSpec((tm, 512), row)] * 4,
        out_shape=((jax.ShapeDtypeStruct((N_TOK, D_MODEL), F32), jax.ShapeDtypeStruct((N_TOK, 512), F32))
                   + (jax.ShapeDtypeStruct((N_TOK, 512), BF16),) * 3),
        scratch_shapes=[pltpu.VMEM((D_MODEL, MIX_IN), BF16)],
        compiler_params=_params("arbitrary"),
        name="proj",
    )(x2, ln, w)


def _s5_kernel(u_ref, a_ref, w_ref, dg_ref, gw32_ref, y_ref, sre, sim, xre, xim, utm, ytm, gw_ref):
    @pl.when(pl.program_id(0) == 0)
    def _():
        xre[...] = jnp.zeros_like(xre)
        xim[...] = jnp.zeros_like(xim)

    _cast_once(gw32_ref, gw_ref)
    half = S5_COLS // 2
    cw = S5_SCAN_CHUNK
    nchunk = S5_COLS // cw
    rows = S5_SUB_STEPS * BATCH
    steps = S5_SUBTILES * S5_SUB_STEPS
    nslab = S5_WIDTH // LANES
    carry = [None] * nchunk

    for b in range(BATCH):
        for sl in range(nslab):
            utm[sl, pl.ds(b, steps, stride=BATCH), :] = u_ref[b, :, sl * LANES:(sl + 1) * LANES]

    def work(st):
        r0 = st * rows
        u = jnp.concatenate([utm[sl, r0:r0 + rows, :] for sl in range(nslab)], axis=-1)
        ub = u.astype(BF16)
        for hf in range(2):
            uh = ub[:, hf * MXU_DIM:(hf + 1) * MXU_DIM]
            sre[r0:r0 + rows, hf * half:(hf + 1) * half] = _dot(uh, w_ref[0, hf])
            yield
            sim[r0:r0 + rows, hf * half:(hf + 1) * half] = _dot(uh, w_ref[1, hf])
            yield
        for c in range(nchunk):
            cs = slice(c * cw, (c + 1) * cw)
            ar = a_ref[0, :, cs]
            ai = a_ref[1, :, cs]
            xr, xi = (xre[:, cs], xim[:, cs]) if st == 0 else carry[c]
            for t in range(S5_SUB_STEPS):
                r = r0 + t * BATCH
                nr = ar * xr - ai * xi + sre[r:r + BATCH, cs]
                ni = ar * xi + ai * xr + sim[r:r + BATCH, cs]
                sre[r:r + BATCH, cs] = nr
                sim[r:r + BATCH, cs] = ni
                xr, xi = nr, ni
            carry[c] = (xr, xi)
            if st == S5_SUBTILES - 1:
                xre[:, cs] = xr
                xim[:, cs] = xi
            yield
        ys = []
        for hf in range(2):
            hs = slice(hf * half, (hf + 1) * half)
            ys.append(lax.dot_general(sre[r0:r0 + rows, hs].astype(BF16), w_ref[2, hf], _NT,
                                      preferred_element_type=F32)
                      - lax.dot_general(sim[r0:r0 + rows, hs].astype(BF16), w_ref[3, hf], _NT,
                                        preferred_element_type=F32))
            yield
        y = jax.nn.gelu(jnp.concatenate(ys, axis=-1) + dg_ref[0:1, :] * u)
        yield
        z = _dot(y.astype(BF16), gw_ref[...]) + dg_ref[1:2, :]
        y = y * jax.nn.sigmoid(z)
        for sl in range(nslab):
            ytm[sl, r0:r0 + rows, :] = y[:, sl * LANES:(sl + 1) * LANES]
        t0 = st * S5_SUB_STEPS
        for b in range(BATCH):
            y_ref[b, t0:t0 + S5_SUB_STEPS, :] = jnp.concatenate(
                [ytm[sl, pl.ds(r0 + b, S5_SUB_STEPS, stride=BATCH), :] for sl in range(nslab)],
                axis=-1).astype(BF16)
        yield

    _interleave([work(st) for st in range(S5_SUBTILES)], lag=4)


def _s5(u, a8, w, dg, gw):
    steps = S5_SUBTILES * S5_SUB_STEPS
    rows = steps * BATCH
    nslab = S5_WIDTH // LANES
    blk = pl.BlockSpec((BATCH, steps, S5_WIDTH), lambda i: (0, i, 0))
    c2 = lambda i: (0, 0)
    return pl.pallas_call(
        _s5_kernel,
        grid=(SEQ // steps,),
        in_specs=[blk,
                  pl.BlockSpec((2, BATCH, S5_COLS), lambda i: (0, 0, 0)),
                  pl.BlockSpec((4, 2, MXU_DIM, S5_COLS // 2), lambda i: (0, 0, 0, 0)),
                  pl.BlockSpec((2, S5_WIDTH), c2), pl.BlockSpec((S5_WIDTH, S5_WIDTH), c2)],
        out_specs=blk,
        out_shape=jax.ShapeDtypeStruct((BATCH, SEQ, S5_WIDTH), BF16),
        scratch_shapes=[pltpu.VMEM((rows, S5_COLS), F32), pltpu.VMEM((rows, S5_COLS), F32),
                        pltpu.VMEM((BATCH, S5_COLS), F32), pltpu.VMEM((BATCH, S5_COLS), F32),
                        pltpu.VMEM((nslab, rows, LANES), F32), pltpu.VMEM((nslab, rows, LANES), F32),
                        pltpu.VMEM((S5_WIDTH, S5_WIDTH), BF16)],
        compiler_params=_params("arbitrary"),
        name="s5",
    )(u, a8, w, dg, gw)


def _diff_attn_kernel(q_ref, k_ref, v_ref, bias_ref, lv_ref, g_ref, o_ref):
    t = ATT_TILE
    half = t // 2
    sb = LANES
    lam = (jnp.exp(jnp.sum(lv_ref[0:1, :] * lv_ref[1:2, :], axis=-1, keepdims=True))
           - jnp.exp(jnp.sum(lv_ref[2:3, :] * lv_ref[3:4, :], axis=-1, keepdims=True)) + LAMBDA_INIT)
    lane = lax.broadcasted_iota(jnp.int32, (t, DIFF_V_DIM), 1)
    zero = jnp.zeros((t, DIFF_V_DIM), BF16)

    def head_work(hh):
        hs = slice(hh * DIFF_V_DIM, (hh + 1) * DIFF_V_DIM)

        def scores(qm, k0, width):
            kb = k_ref[k0:k0 + width, hs]
            return [lax.dot_general(qm[mi], kb, _NT, preferred_element_type=F32) for mi in range(2)]

        def lane_groups(s, r0, k0):
            rows, width = s.shape
            groups = []
            for g in range(width // sb):
                kb = (k0 + g * sb) // sb
                col = s[:, g * sb:(g + 1) * sb]
                qbs = [(r0 + i * sb) // sb for i in range(rows // sb)]
                if all(kb < qb - 1 for qb in qbs):
                    groups.append(col)
                    continue
                pieces = []
                for i, qb in enumerate(qbs):
                    piece = col[i * sb:(i + 1) * sb, :]
                    if kb == qb:
                        piece = piece + bias_ref[hh, 0]
                    elif kb == qb - 1:
                        piece = piece + bias_ref[hh, 1]
                    elif kb > qb:
                        piece = jnp.full((sb, sb), -jnp.inf, F32)
                    pieces.append(piece)
                groups.append(jnp.concatenate(pieces, axis=0))
            return groups

        def update(s, r0, k0, state):
            m_old, l_old, acc_old = state
            sg = lane_groups(s, r0, k0)
            vb = v_ref[k0:k0 + s.shape[1], hs]
            row_max = jnp.max(functools.reduce(jnp.maximum, sg), axis=-1, keepdims=True)
            if m_old is None:
                m_new = jnp.broadcast_to(row_max, (s.shape[0], LANES))
                pg = [jnp.exp2(x - m_new) for x in sg]
                p = jnp.concatenate(pg, axis=-1).astype(BF16)
                return m_new, functools.reduce(jnp.add, pg), _dot(p, vb)
            m_new = jnp.maximum(m_old, row_max)
            alpha = jnp.exp2(m_old - m_new)
            pg = [jnp.exp2(x - m_new) for x in sg]
            p = jnp.concatenate(pg, axis=-1).astype(BF16)
            return m_new, alpha * l_old + functools.reduce(jnp.add, pg), alpha * acc_old + _dot(p, vb)

        def rows_of(state, lo, hi):
            return tuple(None if a is None else a[lo:hi] for a in state)

        for c in range(SEQ // t):
            r0 = c * t
            q = q_ref[r0:r0 + t, hs]
            qm = (jnp.where(lane < DIFF_HEAD_DIM, q, zero), jnp.where(lane >= DIFF_HEAD_DIM, q, zero))
            qtop = [x[:half] for x in qm]
            qbot = [x[half:] for x in qm]

            def diag_scores(qtop=qtop, qbot=qbot, r0=r0):
                return scores(qtop, r0, half), scores(qbot, r0, t)

            state = [(None, None, None), (None, None, None)]
            s_next = scores(qm, 0, t) if c > 0 else diag_scores()
            for j in range(c):
                s_cur = s_next
                s_next = scores(qm, (j + 1) * t, t) if j + 1 < c else diag_scores()
                state = [update(s_cur[mi], r0, j * t, state[mi]) for mi in range(2)]
                yield
            s_top, s_bot = s_next
            outs = []
            for mi in range(2):
                top = update(s_top[mi], r0, r0, rows_of(state[mi], 0, half))
                bot = update(s_bot[mi], r0 + half, r0, rows_of(state[mi], half, t))
                l = jnp.concatenate([top[1], bot[1]], axis=0)
                acc = jnp.concatenate([top[2], bot[2]], axis=0)
                outs.append(acc / jnp.sum(l, axis=-1, keepdims=True))
            o = outs[0] - lam * outs[1]
            ms = jnp.mean(o * o, axis=-1, keepdims=True)
            y = o * lax.rsqrt(ms + LN_EPS) * g_ref[...]
            o_ref[r0:r0 + t, hs] = (y * (1.0 - LAMBDA_INIT)).astype(BF16)
            yield

    _interleave([head_work(hh) for hh in range(ATT_HEADS)], lag=1)


def _diff_attn(q, k, v, bias, lv, g):
    vec = lambda b, h: (0, 0)
    seq = pl.BlockSpec((SEQ, ATT_HEADS * DIFF_V_DIM), lambda b, h: (b, h))
    return pl.pallas_call(
        _diff_attn_kernel,
        grid=(BATCH, N_DIFF_HEADS // ATT_HEADS),
        in_specs=[seq, seq, seq,
                  pl.BlockSpec((ATT_HEADS, 2, LANES, LANES), lambda b, h: (h, 0, 0, 0)),
                  pl.BlockSpec((4, DIFF_HEAD_DIM), vec), pl.BlockSpec((1, DIFF_V_DIM), vec)],
        out_specs=seq,
        out_shape=jax.ShapeDtypeStruct((N_TOK, DIFF_WIDTH), BF16),
        compiler_params=_params("parallel", "parallel"),
        name="diff_attn",
    )(q, k, v, bias, lv, g)


def _kv_kernel(m_ref, w32_ref, o_ref, w_ref):
    _cast_once(w32_ref, w_ref)
    o_ref[...] = _dot(m_ref[...].astype(BF16), w_ref[...]).astype(BF16)


def _kv(mem2, wkv):
    tm = KV_ROWS
    return pl.pallas_call(
        _kv_kernel,
        grid=(BATCH * MEM_LEN // tm,),
        in_specs=[pl.BlockSpec((tm, D_MODEL), lambda i: (i, 0)), _resident((D_MODEL, 2 * D_MODEL))],
        out_specs=pl.BlockSpec((tm, 2 * D_MODEL), lambda i: (i, 0)),
        out_shape=jax.ShapeDtypeStruct((BATCH * MEM_LEN, 2 * D_MODEL), BF16),
        scratch_shapes=[pltpu.VMEM((D_MODEL, 2 * D_MODEL), BF16)],
        compiler_params=_params("arbitrary"),
        name="kv",
    )(mem2, wkv)


def _mix_ca_kernel(h0_ref, ys_ref, yd_ref, ln_ref, wout32_ref, wq32_ref, kv_ref, wo32_ref, o_ref,
                   wout_ref, wq_ref, wo_ref):
    _cast_once(wout32_ref, wout_ref)
    _cast_once(wq32_ref, wq_ref)
    _cast_once(wo32_ref, wo_ref)
    sub = MIX_ROWS // MIX_SUBTILES
    nsl = MIX_SLICES
    cw = D_MODEL // nsl
    rw = sub // nsl

    def work(st):
        r0 = st * sub
        ys = ys_ref[r0:r0 + sub, :]
        yd = yd_ref[r0:r0 + sub, :]
        mix = []
        for j in range(nsl):
            cs = slice(j * cw, (j + 1) * cw)
            mix.append(_dot(ys, wout_ref[0:S5_WIDTH, cs]) + _dot(yd, wout_ref[S5_WIDTH:, cs]))
            yield
        mix = jnp.concatenate(mix, axis=-1)
        h1 = []
        for j in range(nsl):
            rs = slice(j * rw, (j + 1) * rw)
            h0 = h0_ref[r0 + j * rw:r0 + (j + 1) * rw, :]
            h1.append(_layer_norm(DEEPNORM_ALPHA * h0 + mix[rs], ln_ref[2:3, :], ln_ref[3:4, :]))
            yield
        h1 = jnp.concatenate(h1, axis=0)
        h1b = h1.astype(BF16)
        qs = []
        for hd in range(CA_HEADS):
            cs = slice(hd * CA_HEAD_DIM, (hd + 1) * CA_HEAD_DIM)
            qs.append((_dot(h1b, wq_ref[:, cs]) * (CA_HEAD_DIM ** -0.5)).astype(BF16))
            yield
        heads = []
        for hd in range(CA_HEADS):
            kh = kv_ref[:, hd * CA_HEAD_DIM:(hd + 1) * CA_HEAD_DIM]
            vh = kv_ref[:, D_MODEL + hd * CA_HEAD_DIM:D_MODEL + (hd + 1) * CA_HEAD_DIM]
            s = lax.dot_general(qs[hd], kh, _NT, preferred_element_type=F32)
            e = jnp.exp(s - jnp.max(s, axis=-1, keepdims=True))
            oh = _dot(e.astype(BF16), vh) / jnp.sum(e, axis=-1, keepdims=True)
            heads.append(oh.astype(BF16))
            yield
        o = jnp.concatenate(heads, axis=-1)
        ca = []
        for j in range(nsl):
            ca.append(_dot(o, wo_ref[:, j * cw:(j + 1) * cw]))
            yield
        ca = jnp.concatenate(ca, axis=-1)
        for j in range(nsl):
            rs = slice(j * rw, (j + 1) * rw)
            o_ref[r0 + j * rw:r0 + (j + 1) * rw, :] = _layer_norm(
                DEEPNORM_ALPHA * h1[rs] + ca[rs], ln_ref[4:5, :], ln_ref[5:6, :])
            yield

    _interleave([work(st) for st in range(MIX_SUBTILES)], lag=nsl)


def _mix_ca(h0, ys, yd, ln, wout, wq, kv, wo):
    tm = MIX_ROWS
    per_batch = SEQ // tm
    row = lambda i: (i, 0)
    const = lambda i: (0, 0)
    sq = _resident((D_MODEL, D_MODEL))
    return pl.pallas_call(
        _mix_ca_kernel,
        grid=(N_TOK // tm,),
        in_specs=[pl.BlockSpec((tm, D_MODEL), row), pl.BlockSpec((tm, 512), row), pl.BlockSpec((tm, 512), row),
                  pl.BlockSpec((N_LN_ROWS, D_MODEL), const), sq, sq,
                  pl.BlockSpec((MEM_LEN, 2 * D_MODEL), lambda i: (i // per_batch, 0)),
                  sq],
        out_specs=pl.BlockSpec((tm, D_MODEL), row),
        out_shape=jax.ShapeDtypeStruct((N_TOK, D_MODEL), F32),
        scratch_shapes=[pltpu.VMEM((D_MODEL, D_MODEL), BF16)] * 3,
        compiler_params=_params("arbitrary"),
        name="mix_ca",
    )(h0, ys, yd, ln, wout, wq, kv, wo)


def _ffn_kernel(h_ref, wgu_ref, wd_ref, ln_ref, o_ref):
    sub = FFN_ROWS // FFN_SUBTILES
    ck = FFN_CHUNK
    nsl = 4
    rw = sub // nsl

    def work(st):
        r0 = st * sub
        hb = h_ref[r0:r0 + sub, :].astype(BF16)
        acc = None
        for c in range(FFN_HIDDEN // ck):
            gate = _dot(hb, wgu_ref[:, c * ck:(c + 1) * ck])
            up = _dot(hb, wgu_ref[:, FFN_HIDDEN + c * ck:FFN_HIDDEN + (c + 1) * ck])
            act = (jax.nn.silu(gate) * up).astype(BF16)
            part = _dot(act, wd_ref[c * ck:(c + 1) * ck, :])
            acc = part if acc is None else acc + part
            yield
        for j in range(nsl):
            rs = slice(r0 + j * rw, r0 + (j + 1) * rw)
            o_ref[rs, :] = _layer_norm(DEEPNORM_ALPHA * h_ref[rs, :] + acc[j * rw:(j + 1) * rw],
                                       ln_ref[6:7, :], ln_ref[7:8, :])
            yield

    _interleave([work(st) for st in range(FFN_SUBTILES)], lag=FFN_HIDDEN // ck // 2 + 1)


def _ffn(h2, wgu, wd, ln):
    tm = FFN_ROWS
    row = lambda i: (i, 0)
    const = lambda i: (0, 0)
    resident = dict(pipeline_mode=pl.Buffered(1))
    return pl.pallas_call(
        _ffn_kernel,
        grid=(N_TOK // tm,),
        in_specs=[pl.BlockSpec((tm, D_MODEL), row),
                  pl.BlockSpec((D_MODEL, 2 * FFN_HIDDEN), const, **resident),
                  pl.BlockSpec((FFN_HIDDEN, D_MODEL), const, **resident),
                  pl.BlockSpec((N_LN_ROWS, D_MODEL), const)],
        out_specs=pl.BlockSpec((tm, D_MODEL), row),
        out_shape=jax.ShapeDtypeStruct((N_TOK, D_MODEL), F32),
        compiler_params=_params("parallel"),
        name="ffn",
    )(h2, wgu, wd, ln)


def kernel(x, mem, ln_in_g, ln_in_b, w_in, s5_lambda_re, s5_lambda_im, s5_log_dt, s5_b_re, s5_b_im,
           s5_c_re, s5_c_im, s5_d, s5_glu_w, s5_glu_b, diff_lq1, diff_lk1, diff_lq2, diff_lk2,
           diff_subln_g, rel_bias, w_out, ln1_g, ln1_b, ca_wq, ca_wkv, ca_wo, ln2_g, ln2_b,
           ffn_w_gate_up, ffn_w_down, ln3_g, ln3_b):
    x2 = x.reshape(N_TOK, D_MODEL)
    ln = jnp.stack([ln_in_g, ln_in_b, ln1_g[0], ln1_b[0], ln2_g[0], ln2_b[0], ln3_g[0], ln3_b[0]])
    lv = jnp.stack([diff_lq1[0], diff_lk1[0], diff_lq2[0], diff_lk2[0]])
    dg = jnp.stack([s5_d[0].reshape(S5_WIDTH), s5_glu_b[0]])

    h0, u, q, k, v = _proj(x2, ln, w_in[0])

    a8, w_s5 = _s5_prep(s5_lambda_re[0], s5_lambda_im[0], s5_log_dt[0], s5_b_re[0], s5_b_im[0],
                        s5_c_re[0], s5_c_im[0])
    y_s5 = _s5(u.reshape(BATCH, SEQ, S5_WIDTH), a8, w_s5, dg, s5_glu_w[0]).reshape(N_TOK, S5_WIDTH)

    y_diff = _diff_attn(q, k, v, _bias_tiles(rel_bias), lv, diff_subln_g)

    kv = _kv(mem.reshape(BATCH * MEM_LEN, D_MODEL), ca_wkv[0])
    h2 = _mix_ca(h0, y_s5, y_diff, ln, w_out[0], ca_wq[0], kv, ca_wo[0])
    out = _ffn(h2, ffn_w_gate_up[0].astype(BF16), ffn_w_down[0].astype(BF16), ln)
    return out.reshape(BATCH, SEQ, D_MODEL)
```

```python
import functools
import math

import numpy as np

import jax
import jax.numpy as jnp
from jax import lax
from jax.experimental import pallas as pl
from jax.experimental.pallas import tpu as pltpu

F32 = jnp.float32
BF16 = jnp.bfloat16

D_MODEL = 1024
BATCH = 8
SEQ = 2048
N_TOK = BATCH * SEQ
MEM_LEN = 256
S5_WIDTH = 512
S5_GROUP = 16
S5_GROUPS = 32
S5_STATE = 64
S5_COLS = S5_GROUPS * S5_STATE
DIFF_WIDTH = 512
DIFF_HEAD_DIM = 64
DIFF_V_DIM = 128
N_DIFF_HEADS = 4
MIX_IN = 2048
NUM_BUCKETS = 32
MAX_DISTANCE = 128
CA_HEADS = 4
CA_HEAD_DIM = 256
FFN_HIDDEN = 2816
DEEPNORM_ALPHA = 2.0 ** 0.25
LN_EPS = 1e-5
LAMBDA_INIT = 0.8 - 0.6 * math.exp(0.0)
LOG2E = math.log2(math.e)

VMEM_LIMIT_BYTES = 56 * 1024 * 1024
MXU_DIM = 256
LANES = 128

PROJ_ROWS = 1024
PROJ_SUBTILES = 2
S5_SUB_STEPS = 64
S5_SUBTILES = 2
S5_SCAN_CHUNK = 512
ATT_TILE = 512
FFN_CHUNK = 256
FFN_ROWS = 1024
FFN_SUBTILES = 2
MIX_ROWS = 1024
MIX_SUBTILES = 2
MIX_SLICES = 4
N_LN_ROWS = 8
KV_ROWS = 1024

_NT = (((1,), (1,)), ((), ()))


def _params(*sem):
    return pltpu.CompilerParams(dimension_semantics=sem, vmem_limit_bytes=VMEM_LIMIT_BYTES)


def _layer_norm(x, g, b):
    mu = jnp.mean(x, axis=-1, keepdims=True)
    xc = x - mu
    var = jnp.mean(xc * xc, axis=-1, keepdims=True)
    return xc * lax.rsqrt(var + LN_EPS) * g + b


def _dot(a, b):
    return jnp.dot(a, b, preferred_element_type=F32)


def _cast_once(w_ref, wbf_ref):
    @pl.when(pl.program_id(0) == 0)
    def _():
        wbf_ref[...] = w_ref[...].astype(BF16)


def _resident(shape):
    return pl.BlockSpec(shape, lambda *_: (0,) * len(shape), pipeline_mode=pl.Buffered(1))


def _interleave(gens, lag):
    gens = list(gens)
    live = [True] * len(gens)
    tick = 0
    while any(live):
        for i, g in enumerate(gens):
            if live[i] and tick >= lag * i:
                try:
                    next(g)
                except StopIteration:
                    live[i] = False
        tick += 1


def _s5_prep_kernel(lam_ref, bc_ref, a_ref, w_ref):
    lr = lam_ref[0:1, :]
    li = lam_ref[1:2, :]
    dt = jnp.exp(lam_ref[2:3, :])
    mag = jnp.exp(lr * dt)
    ang = li * dt
    ar = mag * jnp.cos(ang)
    ai = mag * jnp.sin(ang)
    den = lr * lr + li * li
    nr = ar - 1.0
    fr = (nr * lr + ai * li) / den
    fi = (ai * lr - nr * li) / den
    a_ref[0] = jnp.broadcast_to(ar, (BATCH, S5_COLS))
    a_ref[1] = jnp.broadcast_to(ai, (BATCH, S5_COLS))
    b_r = bc_ref[0]
    b_i = bc_ref[1]
    mats = (fr * b_r - fi * b_i, fr * b_i + fi * b_r, bc_ref[2], bc_ref[3])
    half = S5_COLS // 2
    lane_group = lax.broadcasted_iota(jnp.int32, (S5_GROUP, half), 1) // S5_STATE
    for k, mat in enumerate(mats):
        for hf in range(2):
            src = mat[:, hf * half:(hf + 1) * half]
            for gl in range(MXU_DIM // S5_GROUP):
                w_ref[k, hf, gl * S5_GROUP:(gl + 1) * S5_GROUP, :] = jnp.where(
                    lane_group == gl, src, 0.0).astype(BF16)


def _s5_prep(lam_re, lam_im, log_dt, b_re, b_im, c_re, c_im):
    lam = jnp.stack([lam_re.reshape(S5_COLS), lam_im.reshape(S5_COLS), jnp.repeat(log_dt, S5_STATE)])
    bc = jnp.stack([jnp.transpose(b_re, (2, 0, 1)), jnp.transpose(b_im, (2, 0, 1)),
                    jnp.transpose(c_re, (1, 0, 2)), jnp.transpose(c_im, (1, 0, 2))]).reshape(4, S5_GROUP, S5_COLS)
    return pl.pallas_call(
        _s5_prep_kernel,
        out_shape=(jax.ShapeDtypeStruct((2, BATCH, S5_COLS), F32),
                   jax.ShapeDtypeStruct((4, 2, MXU_DIM, S5_COLS // 2), BF16)),
        name="s5_prep",
    )(lam, bc)


def _bias_prep_kernel(rel_ref, bucket_ref, o_ref):
    for j in range(2):
        bucket = bucket_ref[j]
        for h in range(N_DIFF_HEADS):
            far = rel_ref[NUM_BUCKETS - 1, h]
            acc = jnp.zeros(bucket.shape, F32)
            for b in range(NUM_BUCKETS):
                acc = jnp.where(bucket == b, (rel_ref[b, h] - far) * LOG2E, acc)
            o_ref[h, j] = jnp.where(bucket < 0, -jnp.inf, acc)


def _t5_bucket(dist):
    max_exact = NUM_BUCKETS // 2
    df = np.maximum(dist, 1).astype(np.float32)
    large = max_exact + (np.log(df / np.float32(max_exact)) / np.float32(math.log(MAX_DISTANCE / max_exact))
                         * np.float32(NUM_BUCKETS - max_exact)).astype(np.int32)
    large = np.minimum(large, NUM_BUCKETS - 1)
    return np.where(dist < max_exact, dist, large).astype(np.int32)


def _bias_tiles(rel_bias):
    t = LANES
    qpos = np.arange(t, dtype=np.int32)[None, :]
    kpos = np.arange(t, dtype=np.int32)[:, None]
    d_diag = qpos - kpos
    d_prev = d_diag + t
    bucket = jnp.asarray(np.stack([np.where(d_diag >= 0, _t5_bucket(np.maximum(d_diag, 0)), -1),
                                   _t5_bucket(d_prev)]).astype(np.int32))
    return pl.pallas_call(
        _bias_prep_kernel,
        in_specs=[pl.BlockSpec(memory_space=pltpu.SMEM), pl.BlockSpec(memory_space=pltpu.VMEM)],
        out_specs=pl.BlockSpec(memory_space=pltpu.VMEM),
        out_shape=jax.ShapeDtypeStruct((N_DIFF_HEADS, 2, t, t), F32),
        name="bias_prep",
    )(rel_bias, bucket)


def _proj_kernel(x_ref, ln_ref, w32_ref, h_ref, u_ref, q_ref, k_ref, v_ref, w_ref):
    _cast_once(w32_ref, w_ref)
    sub = PROJ_ROWS // PROJ_SUBTILES
    nsl = 4
    rw = sub // nsl

    def work(st):
        r0 = st * sub
        h = []
        for j in range(nsl):
            rows = slice(r0 + j * rw, r0 + (j + 1) * rw)
            hj = _layer_norm(x_ref[rows, :], ln_ref[0:1, :], ln_ref[1:2, :])
            h_ref[rows, :] = hj
            h.append(hj.astype(BF16))
            yield
        h = jnp.concatenate(h, axis=0)
        rs = slice(r0, r0 + sub)
        u_ref[rs, :] = _dot(h, w_ref[:, 0:512])
        yield
        q_ref[rs, :] = (_dot(h, w_ref[:, 512:1024]) * (DIFF_HEAD_DIM ** -0.5 * LOG2E)).astype(BF16)
        yield
        k_ref[rs, :] = _dot(h, w_ref[:, 1024:1536]).astype(BF16)
        yield
        v_ref[rs, :] = _dot(h, w_ref[:, 1536:2048]).astype(BF16)
        yield

    _interleave([work(st) for st in range(PROJ_SUBTILES)], lag=nsl)


def _proj(x2, ln, w):
    tm = PROJ_ROWS
    row = lambda i: (i, 0)
    const = lambda i: (0, 0)
    return pl.pallas_call(
        _proj_kernel,
        grid=(N_TOK // tm,),
        in_specs=[pl.BlockSpec((tm, D_MODEL), row), pl.BlockSpec((N_LN_ROWS, D_MODEL), const),
                  _resident((D_MODEL, MIX_IN))],
        out_specs=[pl.BlockSpec((tm, D_MODEL), row)] + [pl.BlockSpec((tm, 512), row)] * 4,
        out_shape=((jax.ShapeDtypeStruct((N_TOK, D_MODEL), F32), jax.ShapeDtypeStruct((N_TOK, 512), F32))
                   + (jax.ShapeDtypeStruct((N_TOK, 512), BF16),) * 3),
        scratch_shapes=[pltpu.VMEM((D_MODEL, MIX_IN), BF16)],
        compiler_params=_params("arbitrary"),
        name="proj",
    )(x2, ln, w)


def _s5_kernel(u_ref, a_ref, w_ref, dg_ref, gw32_ref, y_ref, sre, sim, xre, xim, utm, ytm, gw_ref):
    @pl.when(pl.program_id(0) == 0)
    def _():
        xre[...] = jnp.zeros_like(xre)
        xim[...] = jnp.zeros_like(xim)

    _cast_once(gw32_ref, gw_ref)
    half = S5_COLS // 2
    cw = S5_SCAN_CHUNK
    nchunk = S5_COLS // cw
    rows = S5_SUB_STEPS * BATCH
    steps = S5_SUBTILES * S5_SUB_STEPS
    nslab = S5_WIDTH // LANES
    carry = [None] * nchunk

    for b in range(BATCH):
        for sl in range(nslab):
            utm[sl, pl.ds(b, steps, stride=BATCH), :] = u_ref[b, :, sl * LANES:(sl + 1) * LANES]

    def work(st):
        r0 = st * rows
        u = jnp.concatenate([utm[sl, r0:r0 + rows, :] for sl in range(nslab)], axis=-1)
        ub = u.astype(BF16)
        for hf in range(2):
            uh = ub[:, hf * MXU_DIM:(hf + 1) * MXU_DIM]
            sre[r0:r0 + rows, hf * half:(hf + 1) * half] = _dot(uh, w_ref[0, hf])
            yield
            sim[r0:r0 + rows, hf * half:(hf + 1) * half] = _dot(uh, w_ref[1, hf])
            yield
        for c in range(nchunk):
            cs = slice(c * cw, (c + 1) * cw)
            ar = a_ref[0, :, cs]
            ai = a_ref[1, :, cs]
            xr, xi = (xre[:, cs], xim[:, cs]) if st == 0 else carry[c]
            for t in range(S5_SUB_STEPS):
                r = r0 + t * BATCH
                nr = ar * xr - ai * xi + sre[r:r + BATCH, cs]
                ni = ar * xi + ai * xr + sim[r:r + BATCH, cs]
                sre[r:r + BATCH, cs] = nr
                sim[r:r + BATCH, cs] = ni
                xr, xi = nr, ni
            carry[c] = (xr, xi)
            if st == S5_SUBTILES - 1:
                xre[:, cs] = xr
                xim[:, cs] = xi
            yield
        ys = []
        for hf in range(2):
            hs = slice(hf * half, (hf + 1) * half)
            ys.append(lax.dot_general(sre[r0:r0 + rows, hs].astype(BF16), w_ref[2, hf], _NT,
                                      preferred_element_type=F32)
                      - lax.dot_general(sim[r0:r0 + rows, hs].astype(BF16), w_ref[3, hf], _NT,
                                        preferred_element_type=F32))
            yield
        y = jax.nn.gelu(jnp.concatenate(ys, axis=-1) + dg_ref[0:1, :] * u)
        yield
        z = _dot(y.astype(BF16), gw_ref[...]) + dg_ref[1:2, :]
        y = y * jax.nn.sigmoid(z)
        for sl in range(nslab):
            ytm[sl, r0:r0 + rows, :] = y[:, sl * LANES:(sl + 1) * LANES]
        t0 = st * S5_SUB_STEPS
        for b in range(BATCH):
            y_ref[b, t0:t0 + S5_SUB_STEPS, :] = jnp.concatenate(
                [ytm[sl, pl.ds(r0 + b, S5_SUB_STEPS, stride=BATCH), :] for sl in range(nslab)],
                axis=-1).astype(BF16)
        yield

    _interleave([work(st) for st in range(S5_SUBTILES)], lag=4)


def _s5(u, a8, w, dg, gw):
    steps = S5_SUBTILES * S5_SUB_STEPS
    rows = steps * BATCH
    nslab = S5_WIDTH // LANES
    blk = pl.BlockSpec((BATCH, steps, S5_WIDTH), lambda i: (0, i, 0))
    c2 = lambda i: (0, 0)
    return pl.pallas_call(
        _s5_kernel,
        grid=(SEQ // steps,),
        in_specs=[blk,
                  pl.BlockSpec((2, BATCH, S5_COLS), lambda i: (0, 0, 0)),
                  pl.BlockSpec((4, 2, MXU_DIM, S5_COLS // 2), lambda i: (0, 0, 0, 0)),
                  pl.BlockSpec((2, S5_WIDTH), c2), pl.BlockSpec((S5_WIDTH, S5_WIDTH), c2)],
        out_specs=blk,
        out_shape=jax.ShapeDtypeStruct((BATCH, SEQ, S5_WIDTH), BF16),
        scratch_shapes=[pltpu.VMEM((rows, S5_COLS), F32), pltpu.VMEM((rows, S5_COLS), F32),
                        pltpu.VMEM((BATCH, S5_COLS), F32), pltpu.VMEM((BATCH, S5_COLS), F32),
                        pltpu.VMEM((nslab, rows, LANES), F32), pltpu.VMEM((nslab, rows, LANES), F32),
                        pltpu.VMEM((S5_WIDTH, S5_WIDTH), BF16)],
        compiler_params=_params("arbitrary"),
        name="s5",
    )(u, a8, w, dg, gw)


def _diff_attn_kernel(q_ref, k_ref, v_ref, bias_ref, lv_ref, g_ref, o_ref):
    t = ATT_TILE
    half = t // 2
    sb = LANES
    lam = (jnp.exp(jnp.sum(lv_ref[0:1, :] * lv_ref[1:2, :], axis=-1, keepdims=True))
           - jnp.exp(jnp.sum(lv_ref[2:3, :] * lv_ref[3:4, :], axis=-1, keepdims=True)) + LAMBDA_INIT)
    v_t = v_ref[...].astype(F32).T.astype(BF16)
    dim = lax.broadcasted_iota(jnp.int32, (DIFF_V_DIM, t), 0)
    zero = jnp.zeros((DIFF_V_DIM, t), BF16)

    def scores(qm, k0, width, lo, hi):
        kb = k_ref[k0:k0 + width, :]
        return [_dot(kb, qm[mi][:, lo:hi]) for mi in range(2)]

    def with_bias(s, r0, k0):
        width, rows = s.shape
        slabs = []
        for i in range(width // sb):
            kb = (k0 + i * sb) // sb
            slab = s[i * sb:(i + 1) * sb, :]
            qbs = [(r0 + j * sb) // sb for j in range(rows // sb)]
            if all(kb < qb - 1 for qb in qbs):
                slabs.append(slab)
                continue
            pieces = []
            for j, qb in enumerate(qbs):
                piece = slab[:, j * sb:(j + 1) * sb]
                if kb == qb:
                    piece = piece + bias_ref[0, 0]
                elif kb == qb - 1:
                    piece = piece + bias_ref[0, 1]
                elif kb > qb:
                    piece = jnp.full((sb, sb), -jnp.inf, F32)
                pieces.append(piece)
            slabs.append(jnp.concatenate(pieces, axis=-1))
        return jnp.concatenate(slabs, axis=0)

    def update(s, r0, k0, state):
        m_old, l_old, acc_old = state
        s = with_bias(s, r0, k0)
        vb = v_t[:, k0:k0 + s.shape[0]]
        blk_max = jnp.max(s, axis=0, keepdims=True)
        if m_old is None:
            p = jnp.exp2(s - blk_max)
            return blk_max, jnp.sum(p, axis=0, keepdims=True), _dot(vb, p.astype(BF16))
        m_new = jnp.maximum(m_old, blk_max)
        alpha = jnp.exp2(m_old - m_new)
        p = jnp.exp2(s - m_new)
        return (m_new, alpha * l_old + jnp.sum(p, axis=0, keepdims=True),
                alpha * acc_old + _dot(vb, p.astype(BF16)))

    def lanes_of(state, lo, hi):
        return tuple(None if a is None else a[:, lo:hi] for a in state)

    for c in range(SEQ // t):
        r0 = c * t
        q_t = q_ref[r0:r0 + t, :].astype(F32).T.astype(BF16)
        qm = (jnp.where(dim < DIFF_HEAD_DIM, q_t, zero), jnp.where(dim >= DIFF_HEAD_DIM, q_t, zero))

        def diag_scores(qm=qm, r0=r0):
            return scores(qm, r0, half, 0, half), scores(qm, r0, t, half, t)

        state = [(None, None, None), (None, None, None)]
        s_next = scores(qm, 0, t, 0, t) if c > 0 else diag_scores()
        for j in range(c):
            s_cur = s_next
            s_next = scores(qm, (j + 1) * t, t, 0, t) if j + 1 < c else diag_scores()
            state = [update(s_cur[mi], r0, j * t, state[mi]) for mi in range(2)]
        s_lo, s_hi = s_next
        outs = []
        for mi in range(2):
            lo = update(s_lo[mi], r0, r0, lanes_of(state[mi], 0, half))
            hi = update(s_hi[mi], r0 + half, r0, lanes_of(state[mi], half, t))
            l = jnp.concatenate([lo[1], hi[1]], axis=-1)
            acc = jnp.concatenate([lo[2], hi[2]], axis=-1)
            outs.append(acc / l)
        o = outs[0] - lam * outs[1]
        ms = jnp.mean(o * o, axis=0, keepdims=True)
        y = o * lax.rsqrt(ms + LN_EPS) * g_ref[...] * (1.0 - LAMBDA_INIT)
        o_ref[r0:r0 + t, :] = y.T.astype(BF16)


def _diff_attn(q, k, v, bias, lv, g):
    vec = lambda b, h: (0, 0)
    seq = pl.BlockSpec((SEQ, DIFF_V_DIM), lambda b, h: (b, h))
    return pl.pallas_call(
        _diff_attn_kernel,
        grid=(BATCH, N_DIFF_HEADS),
        in_specs=[seq, seq, seq,
                  pl.BlockSpec((1, 2, LANES, LANES), lambda b, h: (h, 0, 0, 0)),
                  pl.BlockSpec((4, DIFF_HEAD_DIM), vec), pl.BlockSpec((DIFF_V_DIM, 1), vec)],
        out_specs=seq,
        out_shape=jax.ShapeDtypeStruct((N_TOK, DIFF_WIDTH), BF16),
        compiler_params=_params("parallel", "parallel"),
        name="diff_attn",
    )(q, k, v, bias, lv, g)


def _kv_kernel(m_ref, w32_ref, o_ref, w_ref):
    _cast_once(w32_ref, w_ref)
    o_ref[...] = _dot(m_ref[...].astype(BF16), w_ref[...]).astype(BF16)


def _kv(mem2, wkv):
    tm = KV_ROWS
    return pl.pallas_call(
        _kv_kernel,
        grid=(BATCH * MEM_LEN // tm,),
        in_specs=[pl.BlockSpec((tm, D_MODEL), lambda i: (i, 0)), _resident((D_MODEL, 2 * D_MODEL))],
        out_specs=pl.BlockSpec((tm, 2 * D_MODEL), lambda i: (i, 0)),
        out_shape=jax.ShapeDtypeStruct((BATCH * MEM_LEN, 2 * D_MODEL), BF16),
        scratch_shapes=[pltpu.VMEM((D_MODEL, 2 * D_MODEL), BF16)],
        compiler_params=_params("arbitrary"),
        name="kv",
    )(mem2, wkv)


def _mix_ca_kernel(h0_ref, ys_ref, yd_ref, ln_ref, wout32_ref, wq32_ref, kv_ref, wo32_ref, o_ref,
                   wout_ref, wq_ref, wo_ref):
    _cast_once(wout32_ref, wout_ref)
    _cast_once(wq32_ref, wq_ref)
    _cast_once(wo32_ref, wo_ref)
    sub = MIX_ROWS // MIX_SUBTILES
    nsl = MIX_SLICES
    cw = D_MODEL // nsl
    rw = sub // nsl

    def work(st):
        r0 = st * sub
        ys = ys_ref[r0:r0 + sub, :]
        yd = yd_ref[r0:r0 + sub, :]
        mix = []
        for j in range(nsl):
            cs = slice(j * cw, (j + 1) * cw)
            mix.append(_dot(ys, wout_ref[0:S5_WIDTH, cs]) + _dot(yd, wout_ref[S5_WIDTH:, cs]))
            yield
        mix = jnp.concatenate(mix, axis=-1)
        h1 = []
        for j in range(nsl):
            rs = slice(j * rw, (j + 1) * rw)
            h0 = h0_ref[r0 + j * rw:r0 + (j + 1) * rw, :]
            h1.append(_layer_norm(DEEPNORM_ALPHA * h0 + mix[rs], ln_ref[2:3, :], ln_ref[3:4, :]))
            yield
        h1 = jnp.concatenate(h1, axis=0)
        h1b = h1.astype(BF16)
        qs = []
        for hd in range(CA_HEADS):
            cs = slice(hd * CA_HEAD_DIM, (hd + 1) * CA_HEAD_DIM)
            qs.append((_dot(h1b, wq_ref[:, cs]) * (CA_HEAD_DIM ** -0.5)).astype(BF16))
            yield
        heads = []
        for hd in range(CA_HEADS):
            kh = kv_ref[:, hd * CA_HEAD_DIM:(hd + 1) * CA_HEAD_DIM]
            vh = kv_ref[:, D_MODEL + hd * CA_HEAD_DIM:D_MODEL + (hd + 1) * CA_HEAD_DIM]
            s = lax.dot_general(qs[hd], kh, _NT, preferred_element_type=F32)
            e = jnp.exp(s - jnp.max(s, axis=-1, keepdims=True))
            oh = _dot(e.astype(BF16), vh) / jnp.sum(e, axis=-1, keepdims=True)
            heads.append(oh.astype(BF16))
            yield
        o = jnp.concatenate(heads, axis=-1)
        ca = []
        for j in range(nsl):
            ca.append(_dot(o, wo_ref[:, j * cw:(j + 1) * cw]))
            yield
        ca = jnp.concatenate(ca, axis=-1)
        for j in range(nsl):
            rs = slice(j * rw, (j + 1) * rw)
            o_ref[r0 + j * rw:r0 + (j + 1) * rw, :] = _layer_norm(
                DEEPNORM_ALPHA * h1[rs] + ca[rs], ln_ref[4:5, :], ln_ref[5:6, :])
            yield

    _interleave([work(st) for st in range(MIX_SUBTILES)], lag=nsl)


def _mix_ca(h0, ys, yd, ln, wout, wq, kv, wo):
    tm = MIX_ROWS
    per_batch = SEQ // tm
    row = lambda i: (i, 0)
    const = lambda i: (0, 0)
    sq = _resident((D_MODEL, D_MODEL))
    return pl.pallas_call(
        _mix_ca_kernel,
        grid=(N_TOK // tm,),
        in_specs=[pl.BlockSpec((tm, D_MODEL), row), pl.BlockSpec((tm, 512), row), pl.BlockSpec((tm, 512), row),
                  pl.BlockSpec((N_LN_ROWS, D_MODEL), const), sq, sq,
                  pl.BlockSpec((MEM_LEN, 2 * D_MODEL), lambda i: (i // per_batch, 0)),
                  sq],
        out_specs=pl.BlockSpec((tm, D_MODEL), row),
        out_shape=jax.ShapeDtypeStruct((N_TOK, D_MODEL), F32),
        scratch_shapes=[pltpu.VMEM((D_MODEL, D_MODEL), BF16)] * 3,
        compiler_params=_params("arbitrary"),
        name="mix_ca",
    )(h0, ys, yd, ln, wout, wq, kv, wo)


def _ffn_kernel(h_ref, wgu_ref, wd_ref, ln_ref, o_ref):
    sub = FFN_ROWS // FFN_SUBTILES
    ck = FFN_CHUNK
    nsl = 4
    rw = sub // nsl

    def work(st):
        r0 = st * sub
        hb = h_ref[r0:r0 + sub, :].astype(BF16)
        acc = None
        for c in range(FFN_HIDDEN // ck):
            gate = _dot(hb, wgu_ref[:, c * ck:(c + 1) * ck])
            up = _dot(hb, wgu_ref[:, FFN_HIDDEN + c * ck:FFN_HIDDEN + (c + 1) * ck])
            act = (jax.nn.silu(gate) * up).astype(BF16)
            part = _dot(act, wd_ref[c * ck:(c + 1) * ck, :])
            acc = part if acc is None else acc + part
            yield
        for j in range(nsl):
            rs = slice(r0 + j * rw, r0 + (j + 1) * rw)
            o_ref[rs, :] = _layer_norm(DEEPNORM_ALPHA * h_ref[rs, :] + acc[j * rw:(j + 1) * rw],
                                       ln_ref[6:7, :], ln_ref[7:8, :])
            yield

    _interleave([work(st) for st in range(FFN_SUBTILES)], lag=FFN_HIDDEN // ck // 2 + 1)


def _ffn(h2, wgu, wd, ln):
    tm = FFN_ROWS
    row = lambda i: (i, 0)
    const = lambda i: (0, 0)
    resident = dict(pipeline_mode=pl.Buffered(1))
    return pl.pallas_call(
        _ffn_kernel,
        grid=(N_TOK // tm,),
        in_specs=[pl.BlockSpec((tm, D_MODEL), row),
                  pl.BlockSpec((D_MODEL, 2 * FFN_HIDDEN), const, **resident),
                  pl.BlockSpec((FFN_HIDDEN, D_MODEL), const, **resident),
                  pl.BlockSpec((N_LN_ROWS, D_MODEL), const)],
        out_specs=pl.BlockSpec((tm, D_MODEL), row),
        out_shape=jax.ShapeDtypeStruct((N_TOK, D_MODEL), F32),
        compiler_params=_params("parallel"),
        name="ffn",
    )(h2, wgu, wd, ln)


def kernel(x, mem, ln_in_g, ln_in_b, w_in, s5_lambda_re, s5_lambda_im, s5_log_dt, s5_b_re, s5_b_im,
           s5_c_re, s5_c_im, s5_d, s5_glu_w, s5_glu_b, diff_lq1, diff_lk1, diff_lq2, diff_lk2,
           diff_subln_g, rel_bias, w_out, ln1_g, ln1_b, ca_wq, ca_wkv, ca_wo, ln2_g, ln2_b,
           ffn_w_gate_up, ffn_w_down, ln3_g, ln3_b):
    x2 = x.reshape(N_TOK, D_MODEL)
    ln = jnp.stack([ln_in_g, ln_in_b, ln1_g[0], ln1_b[0], ln2_g[0], ln2_b[0], ln3_g[0], ln3_b[0]])
    lv = jnp.stack([diff_lq1[0], diff_lk1[0], diff_lq2[0], diff_lk2[0]])
    dg = jnp.stack([s5_d[0].reshape(S5_WIDTH), s5_glu_b[0]])

    h0, u, q, k, v = _proj(x2, ln, w_in[0])

    a8, w_s5 = _s5_prep(s5_lambda_re[0], s5_lambda_im[0], s5_log_dt[0], s5_b_re[0], s5_b_im[0],
                        s5_c_re[0], s5_c_im[0])
    y_s5 = _s5(u.reshape(BATCH, SEQ, S5_WIDTH), a8, w_s5, dg, s5_glu_w[0]).reshape(N_TOK, S5_WIDTH)

    y_diff = _diff_attn(q, k, v, _bias_tiles(rel_bias), lv, diff_subln_g.reshape(DIFF_V_DIM, 1))

    kv = _kv(mem.reshape(BATCH * MEM_LEN, D_MODEL), ca_wkv[0])
    h2 = _mix_ca(h0, y_s5, y_diff, ln, w_out[0], ca_wq[0], kv, ca_wo[0])
    out = _ffn(h2, ffn_w_gate_up[0].astype(BF16), ffn_w_down[0].astype(BF16), ln)
    return out.reshape(BATCH, SEQ, D_MODEL)
```

```python
import functools
import math

import numpy as np

import jax
import jax.numpy as jnp
from jax import lax
from jax.experimental import pallas as pl
from jax.experimental.pallas import tpu as pltpu

F32 = jnp.float32
BF16 = jnp.bfloat16

D_MODEL = 1024
BATCH = 8
SEQ = 2048
N_TOK = BATCH * SEQ
MEM_LEN = 256
S5_WIDTH = 512
S5_GROUP = 16
S5_GROUPS = 32
S5_STATE = 64
S5_COLS = S5_GROUPS * S5_STATE
DIFF_WIDTH = 512
DIFF_HEAD_DIM = 64
DIFF_V_DIM = 128
N_DIFF_HEADS = 4
MIX_IN = 2048
NUM_BUCKETS = 32
MAX_DISTANCE = 128
CA_HEADS = 4
CA_HEAD_DIM = 256
FFN_HIDDEN = 2816
DEEPNORM_ALPHA = 2.0 ** 0.25
LN_EPS = 1e-5
LAMBDA_INIT = 0.8 - 0.6 * math.exp(0.0)
LOG2E = math.log2(math.e)

VMEM_LIMIT_BYTES = 56 * 1024 * 1024
MXU_DIM = 256
LANES = 128

PROJ_ROWS = 1024
PROJ_SUBTILES = 2
S5_SUB_STEPS = 64
S5_SUBTILES = 2
S5_SCAN_CHUNK = 512
ATT_TILE = 512
ATT_ONES_ROWS = 16
FFN_CHUNK = 256
FFN_ROWS = 1024
FFN_SUBTILES = 2
MIX_ROWS = 1024
MIX_SUBTILES = 2
MIX_SLICES = 4
N_LN_ROWS = 8
KV_ROWS = 1024

_NT = (((1,), (1,)), ((), ()))


def _params(*sem):
    return pltpu.CompilerParams(dimension_semantics=sem, vmem_limit_bytes=VMEM_LIMIT_BYTES)


def _layer_norm(x, g, b):
    mu = jnp.mean(x, axis=-1, keepdims=True)
    xc = x - mu
    var = jnp.mean(xc * xc, axis=-1, keepdims=True)
    return xc * lax.rsqrt(var + LN_EPS) * g + b


def _dot(a, b):
    return jnp.dot(a, b, preferred_element_type=F32)


def _cast_once(w_ref, wbf_ref):
    @pl.when(pl.program_id(0) == 0)
    def _():
        wbf_ref[...] = w_ref[...].astype(BF16)


def _resident(shape):
    return pl.BlockSpec(shape, lambda *_: (0,) * len(shape), pipeline_mode=pl.Buffered(1))


def _interleave(gens, lag):
    gens = list(gens)
    live = [True] * len(gens)
    tick = 0
    while any(live):
        for i, g in enumerate(gens):
            if live[i] and tick >= lag * i:
                try:
                    next(g)
                except StopIteration:
                    live[i] = False
        tick += 1


def _s5_prep_kernel(lam_ref, bc_ref, a_ref, w_ref):
    lr = lam_ref[0:1, :]
    li = lam_ref[1:2, :]
    dt = jnp.exp(lam_ref[2:3, :])
    mag = jnp.exp(lr * dt)
    ang = li * dt
    ar = mag * jnp.cos(ang)
    ai = mag * jnp.sin(ang)
    den = lr * lr + li * li
    nr = ar - 1.0
    fr = (nr * lr + ai * li) / den
    fi = (ai * lr - nr * li) / den
    a_ref[0] = jnp.broadcast_to(ar, (BATCH, S5_COLS))
    a_ref[1] = jnp.broadcast_to(ai, (BATCH, S5_COLS))
    b_r = bc_ref[0]
    b_i = bc_ref[1]
    mats = (fr * b_r - fi * b_i, fr * b_i + fi * b_r, bc_ref[2], bc_ref[3])
    half = S5_COLS // 2
    lane_group = lax.broadcasted_iota(jnp.int32, (S5_GROUP, half), 1) // S5_STATE
    for k, mat in enumerate(mats):
        for hf in range(2):
            src = mat[:, hf * half:(hf + 1) * half]
            for gl in range(MXU_DIM // S5_GROUP):
                w_ref[k, hf, gl * S5_GROUP:(gl + 1) * S5_GROUP, :] = jnp.where(
                    lane_group == gl, src, 0.0).astype(BF16)


def _s5_prep(lam_re, lam_im, log_dt, b_re, b_im, c_re, c_im):
    lam = jnp.stack([lam_re.reshape(S5_COLS), lam_im.reshape(S5_COLS), jnp.repeat(log_dt, S5_STATE)])
    bc = jnp.stack([jnp.transpose(b_re, (2, 0, 1)), jnp.transpose(b_im, (2, 0, 1)),
                    jnp.transpose(c_re, (1, 0, 2)), jnp.transpose(c_im, (1, 0, 2))]).reshape(4, S5_GROUP, S5_COLS)
    return pl.pallas_call(
        _s5_prep_kernel,
        out_shape=(jax.ShapeDtypeStruct((2, BATCH, S5_COLS), F32),
                   jax.ShapeDtypeStruct((4, 2, MXU_DIM, S5_COLS // 2), BF16)),
        name="s5_prep",
    )(lam, bc)


def _bias_prep_kernel(rel_ref, bucket_ref, o_ref):
    for j in range(2):
        bucket = bucket_ref[j]
        for h in range(N_DIFF_HEADS):
            far = rel_ref[NUM_BUCKETS - 1, h]
            acc = jnp.zeros(bucket.shape, F32)
            for b in range(NUM_BUCKETS):
                acc = jnp.where(bucket == b, (rel_ref[b, h] - far) * LOG2E, acc)
            o_ref[h, j] = jnp.where(bucket < 0, -jnp.inf, acc)


def _t5_bucket(dist):
    max_exact = NUM_BUCKETS // 2
    df = np.maximum(dist, 1).astype(np.float32)
    large = max_exact + (np.log(df / np.float32(max_exact)) / np.float32(math.log(MAX_DISTANCE / max_exact))
                         * np.float32(NUM_BUCKETS - max_exact)).astype(np.int32)
    large = np.minimum(large, NUM_BUCKETS - 1)
    return np.where(dist < max_exact, dist, large).astype(np.int32)


def _bias_tiles(rel_bias):
    t = LANES
    qpos = np.arange(t, dtype=np.int32)[None, :]
    kpos = np.arange(t, dtype=np.int32)[:, None]
    d_diag = qpos - kpos
    d_prev = d_diag + t
    bucket = jnp.asarray(np.stack([np.where(d_diag >= 0, _t5_bucket(np.maximum(d_diag, 0)), -1),
                                   _t5_bucket(d_prev)]).astype(np.int32))
    return pl.pallas_call(
        _bias_prep_kernel,
        in_specs=[pl.BlockSpec(memory_space=pltpu.SMEM), pl.BlockSpec(memory_space=pltpu.VMEM)],
        out_specs=pl.BlockSpec(memory_space=pltpu.VMEM),
        out_shape=jax.ShapeDtypeStruct((N_DIFF_HEADS, 2, t, t), F32),
        name="bias_prep",
    )(rel_bias, bucket)


def _proj_kernel(x_ref, ln_ref, w32_ref, h_ref, u_ref, q_ref, k_ref, v_ref, w_ref):
    _cast_once(w32_ref, w_ref)
    sub = PROJ_ROWS // PROJ_SUBTILES
    nsl = 4
    rw = sub // nsl

    def work(st):
        r0 = st * sub
        h = []
        for j in range(nsl):
            rows = slice(r0 + j * rw, r0 + (j + 1) * rw)
            hj = _layer_norm(x_ref[rows, :], ln_ref[0:1, :], ln_ref[1:2, :])
            h_ref[rows, :] = hj
            h.append(hj.astype(BF16))
            yield
        h = jnp.concatenate(h, axis=0)
        rs = slice(r0, r0 + sub)
        u_ref[rs, :] = _dot(h, w_ref[:, 0:512])
        yield
        q_ref[rs, :] = (_dot(h, w_ref[:, 512:1024]) * (DIFF_HEAD_DIM ** -0.5 * LOG2E)).astype(BF16)
        yield
        k_ref[rs, :] = _dot(h, w_ref[:, 1024:1536]).astype(BF16)
        yield
        v_ref[rs, :] = _dot(h, w_ref[:, 1536:2048]).astype(BF16)
        yield

    _interleave([work(st) for st in range(PROJ_SUBTILES)], lag=nsl)


def _proj(x2, ln, w):
    tm = PROJ_ROWS
    row = lambda i: (i, 0)
    const = lambda i: (0, 0)
    return pl.pallas_call(
        _proj_kernel,
        grid=(N_TOK // tm,),
        in_specs=[pl.BlockSpec((tm, D_MODEL), row), pl.BlockSpec((N_LN_ROWS, D_MODEL), const),
                  _resident((D_MODEL, MIX_IN))],
        out_specs=[pl.BlockSpec((tm, D_MODEL), row)] + [pl.BlockSpec((tm, 512), row)] * 4,
        out_shape=((jax.ShapeDtypeStruct((N_TOK, D_MODEL), F32), jax.ShapeDtypeStruct((N_TOK, 512), F32))
                   + (jax.ShapeDtypeStruct((N_TOK, 512), BF16),) * 3),
        scratch_shapes=[pltpu.VMEM((D_MODEL, MIX_IN), BF16)],
        compiler_params=_params("arbitrary"),
        name="proj",
    )(x2, ln, w)


def _s5_kernel(u_ref, a_ref, w_ref, dg_ref, gw32_ref, y_ref, sre, sim, xre, xim, utm, ytm, gw_ref):
    @pl.when(pl.program_id(0) == 0)
    def _():
        xre[...] = jnp.zeros_like(xre)
        xim[...] = jnp.zeros_like(xim)

    _cast_once(gw32_ref, gw_ref)
    half = S5_COLS // 2
    cw = S5_SCAN_CHUNK
    nchunk = S5_COLS // cw
    rows = S5_SUB_STEPS * BATCH
    steps = S5_SUBTILES * S5_SUB_STEPS
    nslab = S5_WIDTH // LANES
    carry = [None] * nchunk

    for b in range(BATCH):
        for sl in range(nslab):
            utm[sl, pl.ds(b, steps, stride=BATCH), :] = u_ref[b, :, sl * LANES:(sl + 1) * LANES]

    def work(st):
        r0 = st * rows
        u = jnp.concatenate([utm[sl, r0:r0 + rows, :] for sl in range(nslab)], axis=-1)
        ub = u.astype(BF16)
        for hf in range(2):
            uh = ub[:, hf * MXU_DIM:(hf + 1) * MXU_DIM]
            sre[r0:r0 + rows, hf * half:(hf + 1) * half] = _dot(uh, w_ref[0, hf])
            yield
            sim[r0:r0 + rows, hf * half:(hf + 1) * half] = _dot(uh, w_ref[1, hf])
            yield
        for c in range(nchunk):
            cs = slice(c * cw, (c + 1) * cw)
            ar = a_ref[0, :, cs]
            ai = a_ref[1, :, cs]
            xr, xi = (xre[:, cs], xim[:, cs]) if st == 0 else carry[c]
            for t in range(S5_SUB_STEPS):
                r = r0 + t * BATCH
                nr = ar * xr - ai * xi + sre[r:r + BATCH, cs]
                ni = ar * xi + ai * xr + sim[r:r + BATCH, cs]
                sre[r:r + BATCH, cs] = nr
                sim[r:r + BATCH, cs] = ni
                xr, xi = nr, ni
            carry[c] = (xr, xi)
            if st == S5_SUBTILES - 1:
                xre[:, cs] = xr
                xim[:, cs] = xi
            yield
        ys = []
        for hf in range(2):
            hs = slice(hf * half, (hf + 1) * half)
            ys.append(lax.dot_general(sre[r0:r0 + rows, hs].astype(BF16), w_ref[2, hf], _NT,
                                      preferred_element_type=F32)
                      - lax.dot_general(sim[r0:r0 + rows, hs].astype(BF16), w_ref[3, hf], _NT,
                                        preferred_element_type=F32))
            yield
        y = jax.nn.gelu(jnp.concatenate(ys, axis=-1) + dg_ref[0:1, :] * u)
        yield
        z = _dot(y.astype(BF16), gw_ref[...]) + dg_ref[1:2, :]
        y = y * jax.nn.sigmoid(z)
        for sl in range(nslab):
            ytm[sl, r0:r0 + rows, :] = y[:, sl * LANES:(sl + 1) * LANES]
        t0 = st * S5_SUB_STEPS
        for b in range(BATCH):
            y_ref[b, t0:t0 + S5_SUB_STEPS, :] = jnp.concatenate(
                [ytm[sl, pl.ds(r0 + b, S5_SUB_STEPS, stride=BATCH), :] for sl in range(nslab)],
                axis=-1).astype(BF16)
        yield

    _interleave([work(st) for st in range(S5_SUBTILES)], lag=4)


def _s5(u, a8, w, dg, gw):
    steps = S5_SUBTILES * S5_SUB_STEPS
    rows = steps * BATCH
    nslab = S5_WIDTH // LANES
    blk = pl.BlockSpec((BATCH, steps, S5_WIDTH), lambda i: (0, i, 0))
    c2 = lambda i: (0, 0)
    return pl.pallas_call(
        _s5_kernel,
        grid=(SEQ // steps,),
        in_specs=[blk,
                  pl.BlockSpec((2, BATCH, S5_COLS), lambda i: (0, 0, 0)),
                  pl.BlockSpec((4, 2, MXU_DIM, S5_COLS // 2), lambda i: (0, 0, 0, 0)),
                  pl.BlockSpec((2, S5_WIDTH), c2), pl.BlockSpec((S5_WIDTH, S5_WIDTH), c2)],
        out_specs=blk,
        out_shape=jax.ShapeDtypeStruct((BATCH, SEQ, S5_WIDTH), BF16),
        scratch_shapes=[pltpu.VMEM((rows, S5_COLS), F32), pltpu.VMEM((rows, S5_COLS), F32),
                        pltpu.VMEM((BATCH, S5_COLS), F32), pltpu.VMEM((BATCH, S5_COLS), F32),
                        pltpu.VMEM((nslab, rows, LANES), F32), pltpu.VMEM((nslab, rows, LANES), F32),
                        pltpu.VMEM((S5_WIDTH, S5_WIDTH), BF16)],
        compiler_params=_params("arbitrary"),
        name="s5",
    )(u, a8, w, dg, gw)


def _diff_attn_kernel(q_ref, k_ref, v_ref, bias_ref, lv_ref, g_ref, o_ref):
    t = ATT_TILE
    half = t // 2
    sb = LANES
    lam = (jnp.exp(jnp.sum(lv_ref[0:1, :] * lv_ref[1:2, :], axis=-1, keepdims=True))
           - jnp.exp(jnp.sum(lv_ref[2:3, :] * lv_ref[3:4, :], axis=-1, keepdims=True)) + LAMBDA_INIT)
    v_t = jnp.concatenate([v_ref[...].astype(F32).T.astype(BF16), jnp.ones((ATT_ONES_ROWS, SEQ), BF16)], axis=0)
    dim = lax.broadcasted_iota(jnp.int32, (DIFF_V_DIM, t), 0)
    zero = jnp.zeros((DIFF_V_DIM, t), BF16)

    def scores(qm, k0, width, lo, hi):
        kb = k_ref[k0:k0 + width, :]
        return [_dot(kb, qm[mi][:, lo:hi]) for mi in range(2)]

    def with_bias(s, r0, k0):
        width, rows = s.shape
        slabs = []
        for i in range(width // sb):
            kb = (k0 + i * sb) // sb
            slab = s[i * sb:(i + 1) * sb, :]
            qbs = [(r0 + j * sb) // sb for j in range(rows // sb)]
            if all(kb < qb - 1 for qb in qbs):
                slabs.append(slab)
                continue
            pieces = []
            for j, qb in enumerate(qbs):
                piece = slab[:, j * sb:(j + 1) * sb]
                if kb == qb:
                    piece = piece + bias_ref[0, 0]
                elif kb == qb - 1:
                    piece = piece + bias_ref[0, 1]
                elif kb > qb:
                    piece = jnp.full((sb, sb), -jnp.inf, F32)
                pieces.append(piece)
            slabs.append(jnp.concatenate(pieces, axis=-1))
        return jnp.concatenate(slabs, axis=0)

    def update(s, r0, k0, state):
        m_old, acc_old = state
        s = with_bias(s, r0, k0)
        vb = v_t[:, k0:k0 + s.shape[0]]
        blk_max = jnp.max(s, axis=0, keepdims=True)
        if m_old is None:
            return blk_max, _dot(vb, jnp.exp2(s - blk_max).astype(BF16))
        m_new = jnp.maximum(m_old, blk_max)
        alpha = jnp.exp2(m_old - m_new)
        return m_new, alpha * acc_old + _dot(vb, jnp.exp2(s - m_new).astype(BF16))

    def lanes_of(state, lo, hi):
        return tuple(None if a is None else a[:, lo:hi] for a in state)

    for c in range(SEQ // t):
        r0 = c * t
        q_t = q_ref[r0:r0 + t, :].astype(F32).T.astype(BF16)
        qm = (jnp.where(dim < DIFF_HEAD_DIM, q_t, zero), jnp.where(dim >= DIFF_HEAD_DIM, q_t, zero))

        def diag_scores(qm=qm, r0=r0):
            return scores(qm, r0, half, 0, half), scores(qm, r0, t, half, t)

        state = [(None, None), (None, None)]
        s_next = scores(qm, 0, t, 0, t) if c > 0 else diag_scores()
        for j in range(c):
            s_cur = s_next
            s_next = scores(qm, (j + 1) * t, t, 0, t) if j + 1 < c else diag_scores()
            state = [update(s_cur[mi], r0, j * t, state[mi]) for mi in range(2)]
        s_lo, s_hi = s_next
        outs = []
        for mi in range(2):
            lo = update(s_lo[mi], r0, r0, lanes_of(state[mi], 0, half))
            hi = update(s_hi[mi], r0 + half, r0, lanes_of(state[mi], half, t))
            acc = jnp.concatenate([lo[1], hi[1]], axis=-1)
            outs.append(acc[:DIFF_V_DIM] / acc[DIFF_V_DIM:DIFF_V_DIM + 1])
        o = outs[0] - lam * outs[1]
        ms = jnp.mean(o * o, axis=0, keepdims=True)
        y = o * lax.rsqrt(ms + LN_EPS) * g_ref[...] * (1.0 - LAMBDA_INIT)
        o_ref[r0:r0 + t, :] = y.T.astype(BF16)


def _diff_attn(q, k, v, bias, lv, g):
    vec = lambda b, h: (0, 0)
    seq = pl.BlockSpec((SEQ, DIFF_V_DIM), lambda b, h: (b, h))
    return pl.pallas_call(
        _diff_attn_kernel,
        grid=(BATCH, N_DIFF_HEADS),
        in_specs=[seq, seq, seq,
                  pl.BlockSpec((1, 2, LANES, LANES), lambda b, h: (h, 0, 0, 0)),
                  pl.BlockSpec((4, DIFF_HEAD_DIM), vec), pl.BlockSpec((DIFF_V_DIM, 1), vec)],
        out_specs=seq,
        out_shape=jax.ShapeDtypeStruct((N_TOK, DIFF_WIDTH), BF16),
        compiler_params=_params("parallel", "parallel"),
        name="diff_attn",
    )(q, k, v, bias, lv, g)


def _kv_kernel(m_ref, w32_ref, o_ref, w_ref):
    _cast_once(w32_ref, w_ref)
    o_ref[...] = _dot(m_ref[...].astype(BF16), w_ref[...]).astype(BF16)


def _kv(mem2, wkv):
    tm = KV_ROWS
    return pl.pallas_call(
        _kv_kernel,
        grid=(BATCH * MEM_LEN // tm,),
        in_specs=[pl.BlockSpec((tm, D_MODEL), lambda i: (i, 0)), _resident((D_MODEL, 2 * D_MODEL))],
        out_specs=pl.BlockSpec((tm, 2 * D_MODEL), lambda i: (i, 0)),
        out_shape=jax.ShapeDtypeStruct((BATCH * MEM_LEN, 2 * D_MODEL), BF16),
        scratch_shapes=[pltpu.VMEM((D_MODEL, 2 * D_MODEL), BF16)],
        compiler_params=_params("arbitrary"),
        name="kv",
    )(mem2, wkv)


def _mix_ca_kernel(h0_ref, ys_ref, yd_ref, ln_ref, wout32_ref, wq32_ref, kv_ref, wo32_ref, o_ref,
                   wout_ref, wq_ref, wo_ref):
    _cast_once(wout32_ref, wout_ref)
    _cast_once(wq32_ref, wq_ref)
    _cast_once(wo32_ref, wo_ref)
    sub = MIX_ROWS // MIX_SUBTILES
    nsl = MIX_SLICES
    cw = D_MODEL // nsl
    rw = sub // nsl

    def work(st):
        r0 = st * sub
        ys = ys_ref[r0:r0 + sub, :]
        yd = yd_ref[r0:r0 + sub, :]
        mix = []
        for j in range(nsl):
            cs = slice(j * cw, (j + 1) * cw)
            mix.append(_dot(ys, wout_ref[0:S5_WIDTH, cs]) + _dot(yd, wout_ref[S5_WIDTH:, cs]))
            yield
        mix = jnp.concatenate(mix, axis=-1)
        h1 = []
        for j in range(nsl):
            rs = slice(j * rw, (j + 1) * rw)
            h0 = h0_ref[r0 + j * rw:r0 + (j + 1) * rw, :]
            h1.append(_layer_norm(DEEPNORM_ALPHA * h0 + mix[rs], ln_ref[2:3, :], ln_ref[3:4, :]))
            yield
        h1 = jnp.concatenate(h1, axis=0)
        h1b = h1.astype(BF16)
        qs = []
        for hd in range(CA_HEADS):
            cs = slice(hd * CA_HEAD_DIM, (hd + 1) * CA_HEAD_DIM)
            qs.append((_dot(h1b, wq_ref[:, cs]) * (CA_HEAD_DIM ** -0.5)).astype(BF16))
            yield
        heads = []
        for hd in range(CA_HEADS):
            kh = kv_ref[:, hd * CA_HEAD_DIM:(hd + 1) * CA_HEAD_DIM]
            vh = kv_ref[:, D_MODEL + hd * CA_HEAD_DIM:D_MODEL + (hd + 1) * CA_HEAD_DIM]
            s = lax.dot_general(qs[hd], kh, _NT, preferred_element_type=F32)
            e = jnp.exp(s - jnp.max(s, axis=-1, keepdims=True))
            oh = _dot(e.astype(BF16), vh) / jnp.sum(e, axis=-1, keepdims=True)
            heads.append(oh.astype(BF16))
            yield
        o = jnp.concatenate(heads, axis=-1)
        ca = []
        for j in range(nsl):
            ca.append(_dot(o, wo_ref[:, j * cw:(j + 1) * cw]))
            yield
        ca = jnp.concatenate(ca, axis=-1)
        for j in range(nsl):
            rs = slice(j * rw, (j + 1) * rw)
            o_ref[r0 + j * rw:r0 + (j + 1) * rw, :] = _layer_norm(
                DEEPNORM_ALPHA * h1[rs] + ca[rs], ln_ref[4:5, :], ln_ref[5:6, :])
            yield

    _interleave([work(st) for st in range(MIX_SUBTILES)], lag=nsl)


def _mix_ca(h0, ys, yd, ln, wout, wq, kv, wo):
    tm = MIX_ROWS
    per_batch = SEQ // tm
    row = lambda i: (i, 0)
    const = lambda i: (0, 0)
    sq = _resident((D_MODEL, D_MODEL))
    return pl.pallas_call(
        _mix_ca_kernel,
        grid=(N_TOK // tm,),
        in_specs=[pl.BlockSpec((tm, D_MODEL), row), pl.BlockSpec((tm, 512), row), pl.BlockSpec((tm, 512), row),
                  pl.BlockSpec((N_LN_ROWS, D_MODEL), const), sq, sq,
                  pl.BlockSpec((MEM_LEN, 2 * D_MODEL), lambda i: (i // per_batch, 0)),
                  sq],
        out_specs=pl.BlockSpec((tm, D_MODEL), row),
        out_shape=jax.ShapeDtypeStruct((N_TOK, D_MODEL), F32),
        scratch_shapes=[pltpu.VMEM((D_MODEL, D_MODEL), BF16)] * 3,
        compiler_params=_params("arbitrary"),
        name="mix_ca",
    )(h0, ys, yd, ln, wout, wq, kv, wo)


def _ffn_kernel(h_ref, wgu_ref, wd_ref, ln_ref, o_ref):
    sub = FFN_ROWS // FFN_SUBTILES
    ck = FFN_CHUNK
    nsl = 4
    rw = sub // nsl

    def work(st):
        r0 = st * sub
        hb = h_ref[r0:r0 + sub, :].astype(BF16)
        acc = None
        for c in range(FFN_HIDDEN // ck):
            gate = _dot(hb, wgu_ref[:, c * ck:(c + 1) * ck])
            up = _dot(hb, wgu_ref[:, FFN_HIDDEN + c * ck:FFN_HIDDEN + (c + 1) * ck])
            act = (jax.nn.silu(gate) * up).astype(BF16)
            part = _dot(act, wd_ref[c * ck:(c + 1) * ck, :])
            acc = part if acc is None else acc + part
            yield
        for j in range(nsl):
            rs = slice(r0 + j * rw, r0 + (j + 1) * rw)
            o_ref[rs, :] = _layer_norm(DEEPNORM_ALPHA * h_ref[rs, :] + acc[j * rw:(j + 1) * rw],
                                       ln_ref[6:7, :], ln_ref[7:8, :])
            yield

    _interleave([work(st) for st in range(FFN_SUBTILES)], lag=FFN_HIDDEN // ck // 2 + 1)


def _ffn(h2, wgu, wd, ln):
    tm = FFN_ROWS
    row = lambda i: (i, 0)
    const = lambda i: (0, 0)
    resident = dict(pipeline_mode=pl.Buffered(1))
    return pl.pallas_call(
        _ffn_kernel,
        grid=(N_TOK // tm,),
        in_specs=[pl.BlockSpec((tm, D_MODEL), row),
                  pl.BlockSpec((D_MODEL, 2 * FFN_HIDDEN), const, **resident),
                  pl.BlockSpec((FFN_HIDDEN, D_MODEL), const, **resident),
                  pl.BlockSpec((N_LN_ROWS, D_MODEL), const)],
        out_specs=pl.BlockSpec((tm, D_MODEL), row),
        out_shape=jax.ShapeDtypeStruct((N_TOK, D_MODEL), F32),
        compiler_params=_params("parallel"),
        name="ffn",
    )(h2, wgu, wd, ln)


def kernel(x, mem, ln_in_g, ln_in_b, w_in, s5_lambda_re, s5_lambda_im, s5_log_dt, s5_b_re, s5_b_im,
           s5_c_re, s5_c_im, s5_d, s5_glu_w, s5_glu_b, diff_lq1, diff_lk1, diff_lq2, diff_lk2,
           diff_subln_g, rel_bias, w_out, ln1_g, ln1_b, ca_wq, ca_wkv, ca_wo, ln2_g, ln2_b,
           ffn_w_gate_up, ffn_w_down, ln3_g, ln3_b):
    x2 = x.reshape(N_TOK, D_MODEL)
    ln = jnp.stack([ln_in_g, ln_in_b, ln1_g[0], ln1_b[0], ln2_g[0], ln2_b[0], ln3_g[0], ln3_b[0]])
    lv = jnp.stack([diff_lq1[0], diff_lk1[0], diff_lq2[0], diff_lk2[0]])
    dg = jnp.stack([s5_d[0].reshape(S5_WIDTH), s5_glu_b[0]])

    h0, u, q, k, v = _proj(x2, ln, w_in[0])

    a8, w_s5 = _s5_prep(s5_lambda_re[0], s5_lambda_im[0], s5_log_dt[0], s5_b_re[0], s5_b_im[0],
                        s5_c_re[0], s5_c_im[0])
    y_s5 = _s5(u.reshape(BATCH, SEQ, S5_WIDTH), a8, w_s5, dg, s5_glu_w[0]).reshape(N_TOK, S5_WIDTH)

    y_diff = _diff_attn(q, k, v, _bias_tiles(rel_bias), lv, diff_subln_g.reshape(DIFF_V_DIM, 1))

    kv = _kv(mem.reshape(BATCH * MEM_LEN, D_MODEL), ca_wkv[0])
    h2 = _mix_ca(h0, y_s5, y_diff, ln, w_out[0], ca_wq[0], kv, ca_wo[0])
    out = _ffn(h2, ffn_w_gate_up[0].astype(BF16), ffn_w_down[0].astype(BF16), ln)
    return out.reshape(BATCH, SEQ, D_MODEL)
```

```python
import functools
import math

import numpy as np

import jax
import jax.numpy as jnp
from jax import lax
from jax.experimental import pallas as pl
from jax.experimental.pallas import tpu as pltpu

F32 = jnp.float32
BF16 = jnp.bfloat16

D_MODEL = 1024
BATCH = 8
SEQ = 2048
N_TOK = BATCH * SEQ
MEM_LEN = 256
S5_WIDTH = 512
S5_GROUP = 16
S5_GROUPS = 32
S5_STATE = 64
S5_COLS = S5_GROUPS * S5_STATE
DIFF_WIDTH = 512
DIFF_HEAD_DIM = 64
DIFF_V_DIM = 128
N_DIFF_HEADS = 4
MIX_IN = 2048
NUM_BUCKETS = 32
MAX_DISTANCE = 128
CA_HEADS = 4
CA_HEAD_DIM = 256
FFN_HIDDEN = 2816
DEEPNORM_ALPHA = 2.0 ** 0.25
LN_EPS = 1e-5
LAMBDA_INIT = 0.8 - 0.6 * math.exp(0.0)
LOG2E = math.log2(math.e)

VMEM_LIMIT_BYTES = 56 * 1024 * 1024
MXU_DIM = 256
LANES = 128

PROJ_ROWS = 1024
PROJ_SUBTILES = 2
S5_SUB_STEPS = 64
S5_SUBTILES = 2
S5_SCAN_CHUNK = 512
ATT_TILE = 512
ATT_ONES_ROWS = 16
FFN_CHUNK = 256
FFN_ROWS = 1024
FFN_SUBTILES = 2
FFN_STAGE_COLS = 512
FFN_STAGE_ROWS = 256
MIX_ROWS = 1024
MIX_SUBTILES = 2
MIX_SLICES = 4
N_LN_ROWS = 8
KV_ROWS = 1024

_NT = (((1,), (1,)), ((), ()))


def _params(*sem):
    return pltpu.CompilerParams(dimension_semantics=sem, vmem_limit_bytes=VMEM_LIMIT_BYTES)


def _layer_norm(x, g, b):
    mu = jnp.mean(x, axis=-1, keepdims=True)
    xc = x - mu
    var = jnp.mean(xc * xc, axis=-1, keepdims=True)
    return xc * lax.rsqrt(var + LN_EPS) * g + b


def _dot(a, b):
    return jnp.dot(a, b, preferred_element_type=F32)


def _cast_once(w_ref, wbf_ref):
    @pl.when(pl.program_id(0) == 0)
    def _():
        wbf_ref[...] = w_ref[...].astype(BF16)


def _resident(shape):
    return pl.BlockSpec(shape, lambda *_: (0,) * len(shape), pipeline_mode=pl.Buffered(1))


def _interleave(gens, lag):
    gens = list(gens)
    live = [True] * len(gens)
    tick = 0
    while any(live):
        for i, g in enumerate(gens):
            if live[i] and tick >= lag * i:
                try:
                    next(g)
                except StopIteration:
                    live[i] = False
        tick += 1


def _s5_prep_kernel(lam_ref, bc_ref, a_ref, w_ref):
    lr = lam_ref[0:1, :]
    li = lam_ref[1:2, :]
    dt = jnp.exp(lam_ref[2:3, :])
    mag = jnp.exp(lr * dt)
    ang = li * dt
    ar = mag * jnp.cos(ang)
    ai = mag * jnp.sin(ang)
    den = lr * lr + li * li
    nr = ar - 1.0
    fr = (nr * lr + ai * li) / den
    fi = (ai * lr - nr * li) / den
    a_ref[0] = jnp.broadcast_to(ar, (BATCH, S5_COLS))
    a_ref[1] = jnp.broadcast_to(ai, (BATCH, S5_COLS))
    b_r = bc_ref[0]
    b_i = bc_ref[1]
    mats = (fr * b_r - fi * b_i, fr * b_i + fi * b_r, bc_ref[2], bc_ref[3])
    half = S5_COLS // 2
    lane_group = lax.broadcasted_iota(jnp.int32, (S5_GROUP, half), 1) // S5_STATE
    for k, mat in enumerate(mats):
        for hf in range(2):
            src = mat[:, hf * half:(hf + 1) * half]
            for gl in range(MXU_DIM // S5_GROUP):
                w_ref[k, hf, gl * S5_GROUP:(gl + 1) * S5_GROUP, :] = jnp.where(
                    lane_group == gl, src, 0.0).astype(BF16)


def _s5_prep(lam_re, lam_im, log_dt, b_re, b_im, c_re, c_im):
    lam = jnp.stack([lam_re.reshape(S5_COLS), lam_im.reshape(S5_COLS), jnp.repeat(log_dt, S5_STATE)])
    bc = jnp.stack([jnp.transpose(b_re, (2, 0, 1)), jnp.transpose(b_im, (2, 0, 1)),
                    jnp.transpose(c_re, (1, 0, 2)), jnp.transpose(c_im, (1, 0, 2))]).reshape(4, S5_GROUP, S5_COLS)
    return pl.pallas_call(
        _s5_prep_kernel,
        out_shape=(jax.ShapeDtypeStruct((2, BATCH, S5_COLS), F32),
                   jax.ShapeDtypeStruct((4, 2, MXU_DIM, S5_COLS // 2), BF16)),
        name="s5_prep",
    )(lam, bc)


def _bias_prep_kernel(rel_ref, bucket_ref, o_ref):
    for j in range(2):
        bucket = bucket_ref[j]
        for h in range(N_DIFF_HEADS):
            far = rel_ref[NUM_BUCKETS - 1, h]
            acc = jnp.zeros(bucket.shape, F32)
            for b in range(NUM_BUCKETS):
                acc = jnp.where(bucket == b, (rel_ref[b, h] - far) * LOG2E, acc)
            o_ref[h, j] = jnp.where(bucket < 0, -jnp.inf, acc)


def _t5_bucket(dist):
    max_exact = NUM_BUCKETS // 2
    df = np.maximum(dist, 1).astype(np.float32)
    large = max_exact + (np.log(df / np.float32(max_exact)) / np.float32(math.log(MAX_DISTANCE / max_exact))
                         * np.float32(NUM_BUCKETS - max_exact)).astype(np.int32)
    large = np.minimum(large, NUM_BUCKETS - 1)
    return np.where(dist < max_exact, dist, large).astype(np.int32)


def _bias_tiles(rel_bias):
    t = LANES
    qpos = np.arange(t, dtype=np.int32)[None, :]
    kpos = np.arange(t, dtype=np.int32)[:, None]
    d_diag = qpos - kpos
    d_prev = d_diag + t
    bucket = jnp.asarray(np.stack([np.where(d_diag >= 0, _t5_bucket(np.maximum(d_diag, 0)), -1),
                                   _t5_bucket(d_prev)]).astype(np.int32))
    return pl.pallas_call(
        _bias_prep_kernel,
        in_specs=[pl.BlockSpec(memory_space=pltpu.SMEM), pl.BlockSpec(memory_space=pltpu.VMEM)],
        out_specs=pl.BlockSpec(memory_space=pltpu.VMEM),
        out_shape=jax.ShapeDtypeStruct((N_DIFF_HEADS, 2, t, t), F32),
        name="bias_prep",
    )(rel_bias, bucket)


def _proj_kernel(x_ref, ln_ref, w32_ref, h_ref, u_ref, q_ref, k_ref, v_ref, w_ref):
    _cast_once(w32_ref, w_ref)
    sub = PROJ_ROWS // PROJ_SUBTILES
    nsl = 4
    rw = sub // nsl

    def work(st):
        r0 = st * sub
        h = []
        for j in range(nsl):
            rows = slice(r0 + j * rw, r0 + (j + 1) * rw)
            hj = _layer_norm(x_ref[rows, :], ln_ref[0:1, :], ln_ref[1:2, :])
            h_ref[rows, :] = hj
            h.append(hj.astype(BF16))
            yield
        h = jnp.concatenate(h, axis=0)
        rs = slice(r0, r0 + sub)
        u_ref[rs, :] = _dot(h, w_ref[:, 0:512])
        yield
        q_ref[rs, :] = (_dot(h, w_ref[:, 512:1024]) * (DIFF_HEAD_DIM ** -0.5 * LOG2E)).astype(BF16)
        yield
        k_ref[rs, :] = _dot(h, w_ref[:, 1024:1536]).astype(BF16)
        yield
        v_ref[rs, :] = _dot(h, w_ref[:, 1536:2048]).astype(BF16)
        yield

    _interleave([work(st) for st in range(PROJ_SUBTILES)], lag=nsl)


def _proj(x2, ln, w):
    tm = PROJ_ROWS
    row = lambda i: (i, 0)
    const = lambda i: (0, 0)
    return pl.pallas_call(
        _proj_kernel,
        grid=(N_TOK // tm,),
        in_specs=[pl.BlockSpec((tm, D_MODEL), row), pl.BlockSpec((N_LN_ROWS, D_MODEL), const),
                  _resident((D_MODEL, MIX_IN))],
        out_specs=[pl.BlockSpec((tm, D_MODEL), row)] + [pl.BlockSpec((tm, 512), row)] * 4,
        out_shape=((jax.ShapeDtypeStruct((N_TOK, D_MODEL), F32), jax.ShapeDtypeStruct((N_TOK, 512), F32))
                   + (jax.ShapeDtypeStruct((N_TOK, 512), BF16),) * 3),
        scratch_shapes=[pltpu.VMEM((D_MODEL, MIX_IN), BF16)],
        compiler_params=_params("arbitrary"),
        name="proj",
    )(x2, ln, w)


def _s5_kernel(u_ref, a_ref, w_ref, dg_ref, gw32_ref, y_ref, sre, sim, xre, xim, utm, ytm, gw_ref):
    @pl.when(pl.program_id(0) == 0)
    def _():
        xre[...] = jnp.zeros_like(xre)
        xim[...] = jnp.zeros_like(xim)

    _cast_once(gw32_ref, gw_ref)
    half = S5_COLS // 2
    cw = S5_SCAN_CHUNK
    nchunk = S5_COLS // cw
    rows = S5_SUB_STEPS * BATCH
    steps = S5_SUBTILES * S5_SUB_STEPS
    nslab = S5_WIDTH // LANES
    carry = [None] * nchunk

    for b in range(BATCH):
        for sl in range(nslab):
            utm[sl, pl.ds(b, steps, stride=BATCH), :] = u_ref[b, :, sl * LANES:(sl + 1) * LANES]

    def work(st):
        r0 = st * rows
        u = jnp.concatenate([utm[sl, r0:r0 + rows, :] for sl in range(nslab)], axis=-1)
        ub = u.astype(BF16)
        for hf in range(2):
            uh = ub[:, hf * MXU_DIM:(hf + 1) * MXU_DIM]
            sre[r0:r0 + rows, hf * half:(hf + 1) * half] = _dot(uh, w_ref[0, hf])
            yield
            sim[r0:r0 + rows, hf * half:(hf + 1) * half] = _dot(uh, w_ref[1, hf])
            yield
        for c in range(nchunk):
            cs = slice(c * cw, (c + 1) * cw)
            ar = a_ref[0, :, cs]
            ai = a_ref[1, :, cs]
            xr, xi = (xre[:, cs], xim[:, cs]) if st == 0 else carry[c]
            for t in range(S5_SUB_STEPS):
                r = r0 + t * BATCH
                nr = ar * xr - ai * xi + sre[r:r + BATCH, cs]
                ni = ar * xi + ai * xr + sim[r:r + BATCH, cs]
                sre[r:r + BATCH, cs] = nr
                sim[r:r + BATCH, cs] = ni
                xr, xi = nr, ni
            carry[c] = (xr, xi)
            if st == S5_SUBTILES - 1:
                xre[:, cs] = xr
                xim[:, cs] = xi
            yield
        ys = []
        for hf in range(2):
            hs = slice(hf * half, (hf + 1) * half)
            ys.append(lax.dot_general(sre[r0:r0 + rows, hs].astype(BF16), w_ref[2, hf], _NT,
                                      preferred_element_type=F32)
                      - lax.dot_general(sim[r0:r0 + rows, hs].astype(BF16), w_ref[3, hf], _NT,
                                        preferred_element_type=F32))
            yield
        y = jax.nn.gelu(jnp.concatenate(ys, axis=-1) + dg_ref[0:1, :] * u)
        yield
        z = _dot(y.astype(BF16), gw_ref[...]) + dg_ref[1:2, :]
        y = y * jax.nn.sigmoid(z)
        for sl in range(nslab):
            ytm[sl, r0:r0 + rows, :] = y[:, sl * LANES:(sl + 1) * LANES]
        t0 = st * S5_SUB_STEPS
        for b in range(BATCH):
            y_ref[b, t0:t0 + S5_SUB_STEPS, :] = jnp.concatenate(
                [ytm[sl, pl.ds(r0 + b, S5_SUB_STEPS, stride=BATCH), :] for sl in range(nslab)],
                axis=-1).astype(BF16)
        yield

    _interleave([work(st) for st in range(S5_SUBTILES)], lag=4)


def _s5(u, a8, w, dg, gw):
    steps = S5_SUBTILES * S5_SUB_STEPS
    rows = steps * BATCH
    nslab = S5_WIDTH // LANES
    blk = pl.BlockSpec((BATCH, steps, S5_WIDTH), lambda i: (0, i, 0))
    c2 = lambda i: (0, 0)
    return pl.pallas_call(
        _s5_kernel,
        grid=(SEQ // steps,),
        in_specs=[blk,
                  pl.BlockSpec((2, BATCH, S5_COLS), lambda i: (0, 0, 0)),
                  pl.BlockSpec((4, 2, MXU_DIM, S5_COLS // 2), lambda i: (0, 0, 0, 0)),
                  pl.BlockSpec((2, S5_WIDTH), c2), pl.BlockSpec((S5_WIDTH, S5_WIDTH), c2)],
        out_specs=blk,
        out_shape=jax.ShapeDtypeStruct((BATCH, SEQ, S5_WIDTH), BF16),
        scratch_shapes=[pltpu.VMEM((rows, S5_COLS), F32), pltpu.VMEM((rows, S5_COLS), F32),
                        pltpu.VMEM((BATCH, S5_COLS), F32), pltpu.VMEM((BATCH, S5_COLS), F32),
                        pltpu.VMEM((nslab, rows, LANES), F32), pltpu.VMEM((nslab, rows, LANES), F32),
                        pltpu.VMEM((S5_WIDTH, S5_WIDTH), BF16)],
        compiler_params=_params("arbitrary"),
        name="s5",
    )(u, a8, w, dg, gw)


def _diff_attn_kernel(q_ref, k_ref, v_ref, bias_ref, lv_ref, g_ref, o_ref):
    t = ATT_TILE
    half = t // 2
    sb = LANES
    lam = (jnp.exp(jnp.sum(lv_ref[0:1, :] * lv_ref[1:2, :], axis=-1, keepdims=True))
           - jnp.exp(jnp.sum(lv_ref[2:3, :] * lv_ref[3:4, :], axis=-1, keepdims=True)) + LAMBDA_INIT)
    v_t = jnp.concatenate([v_ref[...].T, jnp.ones((ATT_ONES_ROWS, SEQ), BF16)], axis=0)
    dim = lax.broadcasted_iota(jnp.int32, (DIFF_V_DIM, t), 0)
    zero = jnp.zeros((DIFF_V_DIM, t), BF16)

    def scores(qm, k0, width, lo, hi):
        kb = k_ref[k0:k0 + width, :]
        return [_dot(kb, qm[mi][:, lo:hi]) for mi in range(2)]

    def with_bias(s, r0, k0):
        width, rows = s.shape
        slabs = []
        for i in range(width // sb):
            kb = (k0 + i * sb) // sb
            slab = s[i * sb:(i + 1) * sb, :]
            qbs = [(r0 + j * sb) // sb for j in range(rows // sb)]
            if all(kb < qb - 1 for qb in qbs):
                slabs.append(slab)
                continue
            pieces = []
            for j, qb in enumerate(qbs):
                piece = slab[:, j * sb:(j + 1) * sb]
                if kb == qb:
                    piece = piece + bias_ref[0, 0]
                elif kb == qb - 1:
                    piece = piece + bias_ref[0, 1]
                elif kb > qb:
                    piece = jnp.full((sb, sb), -jnp.inf, F32)
                pieces.append(piece)
            slabs.append(jnp.concatenate(pieces, axis=-1))
        return jnp.concatenate(slabs, axis=0)

    def update(s, r0, k0, state):
        m_old, acc_old = state
        s = with_bias(s, r0, k0)
        vb = v_t[:, k0:k0 + s.shape[0]]
        blk_max = jnp.max(s, axis=0, keepdims=True)
        if m_old is None:
            return blk_max, _dot(vb, jnp.exp2(s - blk_max).astype(BF16))
        m_new = jnp.maximum(m_old, blk_max)
        alpha = jnp.exp2(m_old - m_new)
        return m_new, alpha * acc_old + _dot(vb, jnp.exp2(s - m_new).astype(BF16))

    def lanes_of(state, lo, hi):
        return tuple(None if a is None else a[:, lo:hi] for a in state)

    for c in range(SEQ // t):
        r0 = c * t
        q_t = q_ref[r0:r0 + t, :].T
        qm = (jnp.where(dim < DIFF_HEAD_DIM, q_t, zero), jnp.where(dim >= DIFF_HEAD_DIM, q_t, zero))

        def diag_scores(qm=qm, r0=r0):
            return scores(qm, r0, half, 0, half), scores(qm, r0, t, half, t)

        state = [(None, None), (None, None)]
        s_next = scores(qm, 0, t, 0, t) if c > 0 else diag_scores()
        for j in range(c):
            s_cur = s_next
            s_next = scores(qm, (j + 1) * t, t, 0, t) if j + 1 < c else diag_scores()
            state = [update(s_cur[mi], r0, j * t, state[mi]) for mi in range(2)]
        s_lo, s_hi = s_next
        outs = []
        for mi in range(2):
            lo = update(s_lo[mi], r0, r0, lanes_of(state[mi], 0, half))
            hi = update(s_hi[mi], r0 + half, r0, lanes_of(state[mi], half, t))
            acc = jnp.concatenate([lo[1], hi[1]], axis=-1)
            outs.append(acc[:DIFF_V_DIM] / acc[DIFF_V_DIM:DIFF_V_DIM + 1])
        o = outs[0] - lam * outs[1]
        ms = jnp.mean(o * o, axis=0, keepdims=True)
        y = o * lax.rsqrt(ms + LN_EPS) * g_ref[...] * (1.0 - LAMBDA_INIT)
        o_ref[r0:r0 + t, :] = y.T.astype(BF16)


def _diff_attn(q, k, v, bias, lv, g):
    vec = lambda b, h: (0, 0)
    seq = pl.BlockSpec((SEQ, DIFF_V_DIM), lambda b, h: (b, h))
    return pl.pallas_call(
        _diff_attn_kernel,
        grid=(BATCH, N_DIFF_HEADS),
        in_specs=[seq, seq, seq,
                  pl.BlockSpec((1, 2, LANES, LANES), lambda b, h: (h, 0, 0, 0)),
                  pl.BlockSpec((4, DIFF_HEAD_DIM), vec), pl.BlockSpec((DIFF_V_DIM, 1), vec)],
        out_specs=seq,
        out_shape=jax.ShapeDtypeStruct((N_TOK, DIFF_WIDTH), BF16),
        compiler_params=_params("parallel", "parallel"),
        name="diff_attn",
    )(q, k, v, bias, lv, g)


def _kv_kernel(m_ref, w32_ref, o_ref, w_ref):
    _cast_once(w32_ref, w_ref)
    o_ref[...] = _dot(m_ref[...].astype(BF16), w_ref[...]).astype(BF16)


def _kv(mem2, wkv):
    tm = KV_ROWS
    return pl.pallas_call(
        _kv_kernel,
        grid=(BATCH * MEM_LEN // tm,),
        in_specs=[pl.BlockSpec((tm, D_MODEL), lambda i: (i, 0)), _resident((D_MODEL, 2 * D_MODEL))],
        out_specs=pl.BlockSpec((tm, 2 * D_MODEL), lambda i: (i, 0)),
        out_shape=jax.ShapeDtypeStruct((BATCH * MEM_LEN, 2 * D_MODEL), BF16),
        scratch_shapes=[pltpu.VMEM((D_MODEL, 2 * D_MODEL), BF16)],
        compiler_params=_params("arbitrary"),
        name="kv",
    )(mem2, wkv)


def _mix_ca_kernel(h0_ref, ys_ref, yd_ref, ln_ref, wout32_ref, wq32_ref, kv_ref, wo32_ref, o_ref,
                   wout_ref, wq_ref, wo_ref):
    _cast_once(wout32_ref, wout_ref)
    _cast_once(wq32_ref, wq_ref)
    _cast_once(wo32_ref, wo_ref)
    sub = MIX_ROWS // MIX_SUBTILES
    nsl = MIX_SLICES
    cw = D_MODEL // nsl
    rw = sub // nsl

    def work(st):
        r0 = st * sub
        ys = ys_ref[r0:r0 + sub, :]
        yd = yd_ref[r0:r0 + sub, :]
        mix = []
        for j in range(nsl):
            cs = slice(j * cw, (j + 1) * cw)
            mix.append(_dot(ys, wout_ref[0:S5_WIDTH, cs]) + _dot(yd, wout_ref[S5_WIDTH:, cs]))
            yield
        mix = jnp.concatenate(mix, axis=-1)
        h1 = []
        for j in range(nsl):
            rs = slice(j * rw, (j + 1) * rw)
            h0 = h0_ref[r0 + j * rw:r0 + (j + 1) * rw, :]
            h1.append(_layer_norm(DEEPNORM_ALPHA * h0 + mix[rs], ln_ref[2:3, :], ln_ref[3:4, :]))
            yield
        h1 = jnp.concatenate(h1, axis=0)
        h1b = h1.astype(BF16)
        qs = []
        for hd in range(CA_HEADS):
            cs = slice(hd * CA_HEAD_DIM, (hd + 1) * CA_HEAD_DIM)
            qs.append((_dot(h1b, wq_ref[:, cs]) * (CA_HEAD_DIM ** -0.5)).astype(BF16))
            yield
        heads = []
        for hd in range(CA_HEADS):
            kh = kv_ref[:, hd * CA_HEAD_DIM:(hd + 1) * CA_HEAD_DIM]
            vh = kv_ref[:, D_MODEL + hd * CA_HEAD_DIM:D_MODEL + (hd + 1) * CA_HEAD_DIM]
            s = lax.dot_general(qs[hd], kh, _NT, preferred_element_type=F32)
            e = jnp.exp(s - jnp.max(s, axis=-1, keepdims=True))
            oh = _dot(e.astype(BF16), vh) / jnp.sum(e, axis=-1, keepdims=True)
            heads.append(oh.astype(BF16))
            yield
        o = jnp.concatenate(heads, axis=-1)
        ca = []
        for j in range(nsl):
            ca.append(_dot(o, wo_ref[:, j * cw:(j + 1) * cw]))
            yield
        ca = jnp.concatenate(ca, axis=-1)
        for j in range(nsl):
            rs = slice(j * rw, (j + 1) * rw)
            o_ref[r0 + j * rw:r0 + (j + 1) * rw, :] = _layer_norm(
                DEEPNORM_ALPHA * h1[rs] + ca[rs], ln_ref[4:5, :], ln_ref[5:6, :])
            yield

    _interleave([work(st) for st in range(MIX_SUBTILES)], lag=nsl)


def _mix_ca(h0, ys, yd, ln, wout, wq, kv, wo):
    tm = MIX_ROWS
    per_batch = SEQ // tm
    row = lambda i: (i, 0)
    const = lambda i: (0, 0)
    sq = _resident((D_MODEL, D_MODEL))
    return pl.pallas_call(
        _mix_ca_kernel,
        grid=(N_TOK // tm,),
        in_specs=[pl.BlockSpec((tm, D_MODEL), row), pl.BlockSpec((tm, 512), row), pl.BlockSpec((tm, 512), row),
                  pl.BlockSpec((N_LN_ROWS, D_MODEL), const), sq, sq,
                  pl.BlockSpec((MEM_LEN, 2 * D_MODEL), lambda i: (i // per_batch, 0)),
                  sq],
        out_specs=pl.BlockSpec((tm, D_MODEL), row),
        out_shape=jax.ShapeDtypeStruct((N_TOK, D_MODEL), F32),
        scratch_shapes=[pltpu.VMEM((D_MODEL, D_MODEL), BF16)] * 3,
        compiler_params=_params("arbitrary"),
        name="mix_ca",
    )(h0, ys, yd, ln, wout, wq, kv, wo)


def _stream_cast(src_hbm, dst_ref, stage, sem, chunks):
    def copy(i):
        return pltpu.make_async_copy(src_hbm.at[chunks[i][0]], stage.at[i % 2], sem.at[i % 2])

    copy(0).start()
    for i in range(len(chunks)):
        if i + 1 < len(chunks):
            copy(i + 1).start()
        copy(i).wait()
        dst_ref[chunks[i][1]] = stage[i % 2].astype(BF16)


def _ffn_kernel(h_ref, wgu_hbm, wd_hbm, ln_ref, o_ref, wgu_ref, wd_ref, gu_stage, d_stage, sem):
    @pl.when(pl.program_id(0) == 0)
    def _():
        cw = FFN_STAGE_COLS
        cols = [(slice(None), slice(c * cw, (c + 1) * cw)) for c in range(2 * FFN_HIDDEN // cw)]
        _stream_cast(wgu_hbm, wgu_ref, gu_stage, sem.at[0], [(ix, ix) for ix in cols])
        rw_ = FFN_STAGE_ROWS
        rows_ = [(slice(r * rw_, (r + 1) * rw_), slice(None)) for r in range(FFN_HIDDEN // rw_)]
        _stream_cast(wd_hbm, wd_ref, d_stage, sem.at[1], [(ix, ix) for ix in rows_])

    sub = FFN_ROWS // FFN_SUBTILES
    ck = FFN_CHUNK
    nsl = 4
    rw = sub // nsl

    def work(st):
        r0 = st * sub
        hb = h_ref[r0:r0 + sub, :].astype(BF16)
        acc = None
        for c in range(FFN_HIDDEN // ck):
            gate = _dot(hb, wgu_ref[:, c * ck:(c + 1) * ck])
            up = _dot(hb, wgu_ref[:, FFN_HIDDEN + c * ck:FFN_HIDDEN + (c + 1) * ck])
            act = (jax.nn.silu(gate) * up).astype(BF16)
            part = _dot(act, wd_ref[c * ck:(c + 1) * ck, :])
            acc = part if acc is None else acc + part
            yield
        for j in range(nsl):
            rs = slice(r0 + j * rw, r0 + (j + 1) * rw)
            o_ref[rs, :] = _layer_norm(DEEPNORM_ALPHA * h_ref[rs, :] + acc[j * rw:(j + 1) * rw],
                                       ln_ref[6:7, :], ln_ref[7:8, :])
            yield

    _interleave([work(st) for st in range(FFN_SUBTILES)], lag=FFN_HIDDEN // ck // 2 + 1)


def _ffn(h2, wgu, wd, ln):
    tm = FFN_ROWS
    row = lambda i: (i, 0)
    const = lambda i: (0, 0)
    hbm = pl.BlockSpec(memory_space=pl.ANY)
    return pl.pallas_call(
        _ffn_kernel,
        grid=(N_TOK // tm,),
        in_specs=[pl.BlockSpec((tm, D_MODEL), row), hbm, hbm,
                  pl.BlockSpec((N_LN_ROWS, D_MODEL), const)],
        out_specs=pl.BlockSpec((tm, D_MODEL), row),
        out_shape=jax.ShapeDtypeStruct((N_TOK, D_MODEL), F32),
        scratch_shapes=[pltpu.VMEM((D_MODEL, 2 * FFN_HIDDEN), BF16), pltpu.VMEM((FFN_HIDDEN, D_MODEL), BF16),
                        pltpu.VMEM((2, D_MODEL, FFN_STAGE_COLS), F32),
                        pltpu.VMEM((2, FFN_STAGE_ROWS, D_MODEL), F32),
                        pltpu.SemaphoreType.DMA((2, 2))],
        compiler_params=_params("arbitrary"),
        name="ffn",
    )(h2, wgu, wd, ln)


def kernel(x, mem, ln_in_g, ln_in_b, w_in, s5_lambda_re, s5_lambda_im, s5_log_dt, s5_b_re, s5_b_im,
           s5_c_re, s5_c_im, s5_d, s5_glu_w, s5_glu_b, diff_lq1, diff_lk1, diff_lq2, diff_lk2,
           diff_subln_g, rel_bias, w_out, ln1_g, ln1_b, ca_wq, ca_wkv, ca_wo, ln2_g, ln2_b,
           ffn_w_gate_up, ffn_w_down, ln3_g, ln3_b):
    x2 = x.reshape(N_TOK, D_MODEL)
    ln = jnp.stack([ln_in_g, ln_in_b, ln1_g[0], ln1_b[0], ln2_g[0], ln2_b[0], ln3_g[0], ln3_b[0]])
    lv = jnp.stack([diff_lq1[0], diff_lk1[0], diff_lq2[0], diff_lk2[0]])
    dg = jnp.stack([s5_d[0].reshape(S5_WIDTH), s5_glu_b[0]])

    h0, u, q, k, v = _proj(x2, ln, w_in[0])

    a8, w_s5 = _s5_prep(s5_lambda_re[0], s5_lambda_im[0], s5_log_dt[0], s5_b_re[0], s5_b_im[0],
                        s5_c_re[0], s5_c_im[0])
    y_s5 = _s5(u.reshape(BATCH, SEQ, S5_WIDTH), a8, w_s5, dg, s5_glu_w[0]).reshape(N_TOK, S5_WIDTH)

    y_diff = _diff_attn(q, k, v, _bias_tiles(rel_bias), lv, diff_subln_g.reshape(DIFF_V_DIM, 1))

    kv = _kv(mem.reshape(BATCH * MEM_LEN, D_MODEL), ca_wkv[0])
    h2 = _mix_ca(h0, y_s5, y_diff, ln, w_out[0], ca_wq[0], kv, ca_wo[0])
    out = _ffn(h2, ffn_w_gate_up[0], ffn_w_down[0], ln)
    return out.reshape(BATCH, SEQ, D_MODEL)
```

```python
import functools
import math

import numpy as np

import jax
import jax.numpy as jnp
from jax import lax
from jax.experimental import pallas as pl
from jax.experimental.pallas import tpu as pltpu

F32 = jnp.float32
BF16 = jnp.bfloat16

D_MODEL = 1024
BATCH = 8
SEQ = 2048
N_TOK = BATCH * SEQ
MEM_LEN = 256
S5_WIDTH = 512
S5_GROUP = 16
S5_GROUPS = 32
S5_STATE = 64
S5_COLS = S5_GROUPS * S5_STATE
DIFF_WIDTH = 512
DIFF_HEAD_DIM = 64
DIFF_V_DIM = 128
N_DIFF_HEADS = 4
MIX_IN = 2048
NUM_BUCKETS = 32
MAX_DISTANCE = 128
CA_HEADS = 4
CA_HEAD_DIM = 256
FFN_HIDDEN = 2816
DEEPNORM_ALPHA = 2.0 ** 0.25
LN_EPS = 1e-5
LAMBDA_INIT = 0.8 - 0.6 * math.exp(0.0)
LOG2E = math.log2(math.e)

VMEM_LIMIT_BYTES = 56 * 1024 * 1024
MXU_DIM = 256
LANES = 128

PROJ_ROWS = 1024
PROJ_SUBTILES = 2
S5_SUB_STEPS = 64
S5_SUBTILES = 2
S5_SCAN_CHUNK = 512
ATT_TILE = 512
ATT_ONES_ROWS = 16
FFN_CHUNK = 256
FFN_ROWS = 1024
FFN_SUBTILES = 2
FFN_STAGE_COLS = 512
FFN_STAGE_ROWS = 256
FFN_STAGE_SLOTS = 4
MIX_ROWS = 1024
MIX_SUBTILES = 2
MIX_SLICES = 4
N_LN_ROWS = 8
KV_ROWS = 1024

_NT = (((1,), (1,)), ((), ()))


def _params(*sem):
    return pltpu.CompilerParams(dimension_semantics=sem, vmem_limit_bytes=VMEM_LIMIT_BYTES)


def _layer_norm(x, g, b):
    mu = jnp.mean(x, axis=-1, keepdims=True)
    xc = x - mu
    var = jnp.mean(xc * xc, axis=-1, keepdims=True)
    return xc * lax.rsqrt(var + LN_EPS) * g + b


def _dot(a, b):
    return jnp.dot(a, b, preferred_element_type=F32)


def _cast_once(w_ref, wbf_ref):
    @pl.when(pl.program_id(0) == 0)
    def _():
        wbf_ref[...] = w_ref[...].astype(BF16)


def _resident(shape):
    return pl.BlockSpec(shape, lambda *_: (0,) * len(shape), pipeline_mode=pl.Buffered(1))


def _interleave(gens, lag):
    gens = list(gens)
    live = [True] * len(gens)
    tick = 0
    while any(live):
        for i, g in enumerate(gens):
            if live[i] and tick >= lag * i:
                try:
                    next(g)
                except StopIteration:
                    live[i] = False
        tick += 1


def _s5_prep_kernel(lam_ref, bc_ref, a_ref, w_ref):
    lr = lam_ref[0:1, :]
    li = lam_ref[1:2, :]
    dt = jnp.exp(lam_ref[2:3, :])
    mag = jnp.exp(lr * dt)
    ang = li * dt
    ar = mag * jnp.cos(ang)
    ai = mag * jnp.sin(ang)
    den = lr * lr + li * li
    nr = ar - 1.0
    fr = (nr * lr + ai * li) / den
    fi = (ai * lr - nr * li) / den
    a_ref[0] = jnp.broadcast_to(ar, (BATCH, S5_COLS))
    a_ref[1] = jnp.broadcast_to(ai, (BATCH, S5_COLS))
    b_r = bc_ref[0]
    b_i = bc_ref[1]
    mats = (fr * b_r - fi * b_i, fr * b_i + fi * b_r, bc_ref[2], bc_ref[3])
    half = S5_COLS // 2
    lane_group = lax.broadcasted_iota(jnp.int32, (S5_GROUP, half), 1) // S5_STATE
    for k, mat in enumerate(mats):
        for hf in range(2):
            src = mat[:, hf * half:(hf + 1) * half]
            for gl in range(MXU_DIM // S5_GROUP):
                w_ref[k, hf, gl * S5_GROUP:(gl + 1) * S5_GROUP, :] = jnp.where(
                    lane_group == gl, src, 0.0).astype(BF16)


def _s5_prep(lam_re, lam_im, log_dt, b_re, b_im, c_re, c_im):
    lam = jnp.stack([lam_re.reshape(S5_COLS), lam_im.reshape(S5_COLS), jnp.repeat(log_dt, S5_STATE)])
    bc = jnp.stack([jnp.transpose(b_re, (2, 0, 1)), jnp.transpose(b_im, (2, 0, 1)),
                    jnp.transpose(c_re, (1, 0, 2)), jnp.transpose(c_im, (1, 0, 2))]).reshape(4, S5_GROUP, S5_COLS)
    return pl.pallas_call(
        _s5_prep_kernel,
        out_shape=(jax.ShapeDtypeStruct((2, BATCH, S5_COLS), F32),
                   jax.ShapeDtypeStruct((4, 2, MXU_DIM, S5_COLS // 2), BF16)),
        name="s5_prep",
    )(lam, bc)


def _bias_prep_kernel(rel_ref, bucket_ref, o_ref):
    for j in range(2):
        bucket = bucket_ref[j]
        for h in range(N_DIFF_HEADS):
            far = rel_ref[NUM_BUCKETS - 1, h]
            acc = jnp.zeros(bucket.shape, F32)
            for b in range(NUM_BUCKETS):
                acc = jnp.where(bucket == b, (rel_ref[b, h] - far) * LOG2E, acc)
            o_ref[h, j] = jnp.where(bucket < 0, -jnp.inf, acc)


def _t5_bucket(dist):
    max_exact = NUM_BUCKETS // 2
    df = np.maximum(dist, 1).astype(np.float32)
    large = max_exact + (np.log(df / np.float32(max_exact)) / np.float32(math.log(MAX_DISTANCE / max_exact))
                         * np.float32(NUM_BUCKETS - max_exact)).astype(np.int32)
    large = np.minimum(large, NUM_BUCKETS - 1)
    return np.where(dist < max_exact, dist, large).astype(np.int32)


def _bias_tiles(rel_bias):
    t = LANES
    qpos = np.arange(t, dtype=np.int32)[None, :]
    kpos = np.arange(t, dtype=np.int32)[:, None]
    d_diag = qpos - kpos
    d_prev = d_diag + t
    bucket = jnp.asarray(np.stack([np.where(d_diag >= 0, _t5_bucket(np.maximum(d_diag, 0)), -1),
                                   _t5_bucket(d_prev)]).astype(np.int32))
    return pl.pallas_call(
        _bias_prep_kernel,
        in_specs=[pl.BlockSpec(memory_space=pltpu.SMEM), pl.BlockSpec(memory_space=pltpu.VMEM)],
        out_specs=pl.BlockSpec(memory_space=pltpu.VMEM),
        out_shape=jax.ShapeDtypeStruct((N_DIFF_HEADS, 2, t, t), F32),
        name="bias_prep",
    )(rel_bias, bucket)


def _proj_kernel(x_ref, ln_ref, w32_ref, h_ref, u_ref, q_ref, k_ref, v_ref, w_ref):
    _cast_once(w32_ref, w_ref)
    sub = PROJ_ROWS // PROJ_SUBTILES
    nsl = 4
    rw = sub // nsl

    def work(st):
        r0 = st * sub
        h = []
        for j in range(nsl):
            rows = slice(r0 + j * rw, r0 + (j + 1) * rw)
            hj = _layer_norm(x_ref[rows, :], ln_ref[0:1, :], ln_ref[1:2, :])
            h_ref[rows, :] = hj
            h.append(hj.astype(BF16))
            yield
        h = jnp.concatenate(h, axis=0)
        rs = slice(r0, r0 + sub)
        u_ref[rs, :] = _dot(h, w_ref[:, 0:512])
        yield
        q_ref[rs, :] = (_dot(h, w_ref[:, 512:1024]) * (DIFF_HEAD_DIM ** -0.5 * LOG2E)).astype(BF16)
        yield
        k_ref[rs, :] = _dot(h, w_ref[:, 1024:1536]).astype(BF16)
        yield
        v_ref[rs, :] = _dot(h, w_ref[:, 1536:2048]).astype(BF16)
        yield

    _interleave([work(st) for st in range(PROJ_SUBTILES)], lag=nsl)


def _proj(x2, ln, w):
    tm = PROJ_ROWS
    row = lambda i: (i, 0)
    const = lambda i: (0, 0)
    return pl.pallas_call(
        _proj_kernel,
        grid=(N_TOK // tm,),
        in_specs=[pl.BlockSpec((tm, D_MODEL), row), pl.BlockSpec((N_LN_ROWS, D_MODEL), const),
                  _resident((D_MODEL, MIX_IN))],
        out_specs=[pl.BlockSpec((tm, D_MODEL), row)] + [pl.BlockSpec((tm, 512), row)] * 4,
        out_shape=((jax.ShapeDtypeStruct((N_TOK, D_MODEL), F32), jax.ShapeDtypeStruct((N_TOK, 512), F32))
                   + (jax.ShapeDtypeStruct((N_TOK, 512), BF16),) * 3),
        scratch_shapes=[pltpu.VMEM((D_MODEL, MIX_IN), BF16)],
        compiler_params=_params("arbitrary"),
        name="proj",
    )(x2, ln, w)


def _s5_kernel(u_ref, a_ref, w_ref, dg_ref, gw32_ref, y_ref, sre, sim, xre, xim, utm, ytm, gw_ref):
    @pl.when(pl.program_id(0) == 0)
    def _():
        xre[...] = jnp.zeros_like(xre)
        xim[...] = jnp.zeros_like(xim)

    _cast_once(gw32_ref, gw_ref)
    half = S5_COLS // 2
    cw = S5_SCAN_CHUNK
    nchunk = S5_COLS // cw
    rows = S5_SUB_STEPS * BATCH
    steps = S5_SUBTILES * S5_SUB_STEPS
    nslab = S5_WIDTH // LANES
    carry = [None] * nchunk

    for b in range(BATCH):
        for sl in range(nslab):
            utm[sl, pl.ds(b, steps, stride=BATCH), :] = u_ref[b, :, sl * LANES:(sl + 1) * LANES]

    def work(st):
        r0 = st * rows
        u = jnp.concatenate([utm[sl, r0:r0 + rows, :] for sl in range(nslab)], axis=-1)
        ub = u.astype(BF16)
        for hf in range(2):
            uh = ub[:, hf * MXU_DIM:(hf + 1) * MXU_DIM]
            sre[r0:r0 + rows, hf * half:(hf + 1) * half] = _dot(uh, w_ref[0, hf])
            yield
            sim[r0:r0 + rows, hf * half:(hf + 1) * half] = _dot(uh, w_ref[1, hf])
            yield
        for c in range(nchunk):
            cs = slice(c * cw, (c + 1) * cw)
            ar = a_ref[0, :, cs]
            ai = a_ref[1, :, cs]
            xr, xi = (xre[:, cs], xim[:, cs]) if st == 0 else carry[c]
            for t in range(S5_SUB_STEPS):
                r = r0 + t * BATCH
                nr = ar * xr - ai * xi + sre[r:r + BATCH, cs]
                ni = ar * xi + ai * xr + sim[r:r + BATCH, cs]
                sre[r:r + BATCH, cs] = nr
                sim[r:r + BATCH, cs] = ni
                xr, xi = nr, ni
            carry[c] = (xr, xi)
            if st == S5_SUBTILES - 1:
                xre[:, cs] = xr
                xim[:, cs] = xi
            yield
        ys = []
        for hf in range(2):
            hs = slice(hf * half, (hf + 1) * half)
            ys.append(lax.dot_general(sre[r0:r0 + rows, hs].astype(BF16), w_ref[2, hf], _NT,
                                      preferred_element_type=F32)
                      - lax.dot_general(sim[r0:r0 + rows, hs].astype(BF16), w_ref[3, hf], _NT,
                                        preferred_element_type=F32))
            yield
        y = jax.nn.gelu(jnp.concatenate(ys, axis=-1) + dg_ref[0:1, :] * u)
        yield
        z = _dot(y.astype(BF16), gw_ref[...]) + dg_ref[1:2, :]
        y = y * jax.nn.sigmoid(z)
        for sl in range(nslab):
            ytm[sl, r0:r0 + rows, :] = y[:, sl * LANES:(sl + 1) * LANES]
        t0 = st * S5_SUB_STEPS
        for b in range(BATCH):
            y_ref[b, t0:t0 + S5_SUB_STEPS, :] = jnp.concatenate(
                [ytm[sl, pl.ds(r0 + b, S5_SUB_STEPS, stride=BATCH), :] for sl in range(nslab)],
                axis=-1).astype(BF16)
        yield

    _interleave([work(st) for st in range(S5_SUBTILES)], lag=4)


def _s5(u, a8, w, dg, gw):
    steps = S5_SUBTILES * S5_SUB_STEPS
    rows = steps * BATCH
    nslab = S5_WIDTH // LANES
    blk = pl.BlockSpec((BATCH, steps, S5_WIDTH), lambda i: (0, i, 0))
    c2 = lambda i: (0, 0)
    return pl.pallas_call(
        _s5_kernel,
        grid=(SEQ // steps,),
        in_specs=[blk,
                  pl.BlockSpec((2, BATCH, S5_COLS), lambda i: (0, 0, 0)),
                  pl.BlockSpec((4, 2, MXU_DIM, S5_COLS // 2), lambda i: (0, 0, 0, 0)),
                  pl.BlockSpec((2, S5_WIDTH), c2), pl.BlockSpec((S5_WIDTH, S5_WIDTH), c2)],
        out_specs=blk,
        out_shape=jax.ShapeDtypeStruct((BATCH, SEQ, S5_WIDTH), BF16),
        scratch_shapes=[pltpu.VMEM((rows, S5_COLS), F32), pltpu.VMEM((rows, S5_COLS), F32),
                        pltpu.VMEM((BATCH, S5_COLS), F32), pltpu.VMEM((BATCH, S5_COLS), F32),
                        pltpu.VMEM((nslab, rows, LANES), F32), pltpu.VMEM((nslab, rows, LANES), F32),
                        pltpu.VMEM((S5_WIDTH, S5_WIDTH), BF16)],
        compiler_params=_params("arbitrary"),
        name="s5",
    )(u, a8, w, dg, gw)


def _diff_attn_kernel(q_ref, k_ref, v_ref, bias_ref, lv_ref, g_ref, o_ref):
    t = ATT_TILE
    half = t // 2
    sb = LANES
    lam = (jnp.exp(jnp.sum(lv_ref[0:1, :] * lv_ref[1:2, :], axis=-1, keepdims=True))
           - jnp.exp(jnp.sum(lv_ref[2:3, :] * lv_ref[3:4, :], axis=-1, keepdims=True)) + LAMBDA_INIT)
    v_t = jnp.concatenate([v_ref[...].T, jnp.ones((ATT_ONES_ROWS, SEQ), BF16)], axis=0)
    dim = lax.broadcasted_iota(jnp.int32, (DIFF_V_DIM, t), 0)
    zero = jnp.zeros((DIFF_V_DIM, t), BF16)

    def scores(qm, k0, width, lo, hi):
        kb = k_ref[k0:k0 + width, :]
        return [_dot(kb, qm[mi][:, lo:hi]) for mi in range(2)]

    def with_bias(s, r0, k0):
        width, rows = s.shape
        slabs = []
        for i in range(width // sb):
            kb = (k0 + i * sb) // sb
            slab = s[i * sb:(i + 1) * sb, :]
            qbs = [(r0 + j * sb) // sb for j in range(rows // sb)]
            if all(kb < qb - 1 for qb in qbs):
                slabs.append(slab)
                continue
            pieces = []
            for j, qb in enumerate(qbs):
                piece = slab[:, j * sb:(j + 1) * sb]
                if kb == qb:
                    piece = piece + bias_ref[0, 0]
                elif kb == qb - 1:
                    piece = piece + bias_ref[0, 1]
                elif kb > qb:
                    piece = jnp.full((sb, sb), -jnp.inf, F32)
                pieces.append(piece)
            slabs.append(jnp.concatenate(pieces, axis=-1))
        return jnp.concatenate(slabs, axis=0)

    def update(s, r0, k0, state):
        m_old, acc_old = state
        s = with_bias(s, r0, k0)
        vb = v_t[:, k0:k0 + s.shape[0]]
        blk_max = jnp.max(s, axis=0, keepdims=True)
        if m_old is None:
            return blk_max, _dot(vb, jnp.exp2(s - blk_max).astype(BF16))
        m_new = jnp.maximum(m_old, blk_max)
        alpha = jnp.exp2(m_old - m_new)
        return m_new, alpha * acc_old + _dot(vb, jnp.exp2(s - m_new).astype(BF16))

    def lanes_of(state, lo, hi):
        return tuple(None if a is None else a[:, lo:hi] for a in state)

    for c in range(SEQ // t):
        r0 = c * t
        q_t = q_ref[r0:r0 + t, :].T
        qm = (jnp.where(dim < DIFF_HEAD_DIM, q_t, zero), jnp.where(dim >= DIFF_HEAD_DIM, q_t, zero))

        def diag_scores(qm=qm, r0=r0):
            return scores(qm, r0, half, 0, half), scores(qm, r0, t, half, t)

        state = [(None, None), (None, None)]
        s_next = scores(qm, 0, t, 0, t) if c > 0 else diag_scores()
        for j in range(c):
            s_cur = s_next
            s_next = scores(qm, (j + 1) * t, t, 0, t) if j + 1 < c else diag_scores()
            state = [update(s_cur[mi], r0, j * t, state[mi]) for mi in range(2)]
        s_lo, s_hi = s_next
        outs = []
        for mi in range(2):
            lo = update(s_lo[mi], r0, r0, lanes_of(state[mi], 0, half))
            hi = update(s_hi[mi], r0 + half, r0, lanes_of(state[mi], half, t))
            acc = jnp.concatenate([lo[1], hi[1]], axis=-1)
            outs.append(acc[:DIFF_V_DIM] / acc[DIFF_V_DIM:DIFF_V_DIM + 1])
        o = outs[0] - lam * outs[1]
        ms = jnp.mean(o * o, axis=0, keepdims=True)
        y = o * lax.rsqrt(ms + LN_EPS) * g_ref[...] * (1.0 - LAMBDA_INIT)
        o_ref[r0:r0 + t, :] = y.T.astype(BF16)


def _diff_attn(q, k, v, bias, lv, g):
    vec = lambda b, h: (0, 0)
    seq = pl.BlockSpec((SEQ, DIFF_V_DIM), lambda b, h: (b, h))
    return pl.pallas_call(
        _diff_attn_kernel,
        grid=(BATCH, N_DIFF_HEADS),
        in_specs=[seq, seq, seq,
                  pl.BlockSpec((1, 2, LANES, LANES), lambda b, h: (h, 0, 0, 0)),
                  pl.BlockSpec((4, DIFF_HEAD_DIM), vec), pl.BlockSpec((DIFF_V_DIM, 1), vec)],
        out_specs=seq,
        out_shape=jax.ShapeDtypeStruct((N_TOK, DIFF_WIDTH), BF16),
        compiler_params=_params("parallel", "parallel"),
        name="diff_attn",
    )(q, k, v, bias, lv, g)


def _kv_kernel(m_ref, w32_ref, o_ref, w_ref):
    _cast_once(w32_ref, w_ref)
    o_ref[...] = _dot(m_ref[...].astype(BF16), w_ref[...]).astype(BF16)


def _kv(mem2, wkv):
    tm = KV_ROWS
    return pl.pallas_call(
        _kv_kernel,
        grid=(BATCH * MEM_LEN // tm,),
        in_specs=[pl.BlockSpec((tm, D_MODEL), lambda i: (i, 0)), _resident((D_MODEL, 2 * D_MODEL))],
        out_specs=pl.BlockSpec((tm, 2 * D_MODEL), lambda i: (i, 0)),
        out_shape=jax.ShapeDtypeStruct((BATCH * MEM_LEN, 2 * D_MODEL), BF16),
        scratch_shapes=[pltpu.VMEM((D_MODEL, 2 * D_MODEL), BF16)],
        compiler_params=_params("arbitrary"),
        name="kv",
    )(mem2, wkv)


def _mix_ca_kernel(h0_ref, ys_ref, yd_ref, ln_ref, wout32_ref, wq32_ref, kv_ref, wo32_ref, o_ref,
                   wout_ref, wq_ref, wo_ref):
    _cast_once(wout32_ref, wout_ref)
    _cast_once(wq32_ref, wq_ref)
    _cast_once(wo32_ref, wo_ref)
    sub = MIX_ROWS // MIX_SUBTILES
    nsl = MIX_SLICES
    cw = D_MODEL // nsl
    rw = sub // nsl

    def work(st):
        r0 = st * sub
        ys = ys_ref[r0:r0 + sub, :]
        yd = yd_ref[r0:r0 + sub, :]
        mix = []
        for j in range(nsl):
            cs = slice(j * cw, (j + 1) * cw)
            mix.append(_dot(ys, wout_ref[0:S5_WIDTH, cs]) + _dot(yd, wout_ref[S5_WIDTH:, cs]))
            yield
        mix = jnp.concatenate(mix, axis=-1)
        h1 = []
        for j in range(nsl):
            rs = slice(j * rw, (j + 1) * rw)
            h0 = h0_ref[r0 + j * rw:r0 + (j + 1) * rw, :]
            h1.append(_layer_norm(DEEPNORM_ALPHA * h0 + mix[rs], ln_ref[2:3, :], ln_ref[3:4, :]))
            yield
        h1 = jnp.concatenate(h1, axis=0)
        h1b = h1.astype(BF16)
        qs = []
        for hd in range(CA_HEADS):
            cs = slice(hd * CA_HEAD_DIM, (hd + 1) * CA_HEAD_DIM)
            qs.append((_dot(h1b, wq_ref[:, cs]) * (CA_HEAD_DIM ** -0.5)).astype(BF16))
            yield
        heads = []
        for hd in range(CA_HEADS):
            kh = kv_ref[:, hd * CA_HEAD_DIM:(hd + 1) * CA_HEAD_DIM]
            vh = kv_ref[:, D_MODEL + hd * CA_HEAD_DIM:D_MODEL + (hd + 1) * CA_HEAD_DIM]
            s = lax.dot_general(qs[hd], kh, _NT, preferred_element_type=F32)
            e = jnp.exp(s - jnp.max(s, axis=-1, keepdims=True))
            oh = _dot(e.astype(BF16), vh) / jnp.sum(e, axis=-1, keepdims=True)
            heads.append(oh.astype(BF16))
            yield
        o = jnp.concatenate(heads, axis=-1)
        ca = []
        for j in range(nsl):
            ca.append(_dot(o, wo_ref[:, j * cw:(j + 1) * cw]))
            yield
        ca = jnp.concatenate(ca, axis=-1)
        for j in range(nsl):
            rs = slice(j * rw, (j + 1) * rw)
            o_ref[r0 + j * rw:r0 + (j + 1) * rw, :] = _layer_norm(
                DEEPNORM_ALPHA * h1[rs] + ca[rs], ln_ref[4:5, :], ln_ref[5:6, :])
            yield

    _interleave([work(st) for st in range(MIX_SUBTILES)], lag=nsl)


def _mix_ca(h0, ys, yd, ln, wout, wq, kv, wo):
    tm = MIX_ROWS
    per_batch = SEQ // tm
    row = lambda i: (i, 0)
    const = lambda i: (0, 0)
    sq = _resident((D_MODEL, D_MODEL))
    return pl.pallas_call(
        _mix_ca_kernel,
        grid=(N_TOK // tm,),
        in_specs=[pl.BlockSpec((tm, D_MODEL), row), pl.BlockSpec((tm, 512), row), pl.BlockSpec((tm, 512), row),
                  pl.BlockSpec((N_LN_ROWS, D_MODEL), const), sq, sq,
                  pl.BlockSpec((MEM_LEN, 2 * D_MODEL), lambda i: (i // per_batch, 0)),
                  sq],
        out_specs=pl.BlockSpec((tm, D_MODEL), row),
        out_shape=jax.ShapeDtypeStruct((N_TOK, D_MODEL), F32),
        scratch_shapes=[pltpu.VMEM((D_MODEL, D_MODEL), BF16)] * 3,
        compiler_params=_params("arbitrary"),
        name="mix_ca",
    )(h0, ys, yd, ln, wout, wq, kv, wo)


def _stream_cast(src_hbm, dst_ref, stage, sems, w, chunks):
    ns = FFN_STAGE_SLOTS

    def copy(i):
        return pltpu.make_async_copy(src_hbm.at[chunks[i][0]], stage.at[i % ns], sems.at[w, i % ns])

    for i in range(min(ns - 1, len(chunks))):
        copy(i).start()
    for i in range(len(chunks)):
        if i + ns - 1 < len(chunks):
            copy(i + ns - 1).start()
        copy(i).wait()
        dst_ref[chunks[i][1]] = stage[i % ns].astype(BF16)


def _ffn_kernel(h_ref, wgu_hbm, wd_hbm, ln_ref, o_ref, wgu_ref, wd_ref, gu_stage, d_stage, sem):
    @pl.when(pl.program_id(0) == 0)
    def _():
        cw = FFN_STAGE_COLS
        cols = [(slice(None), slice(c * cw, (c + 1) * cw)) for c in range(2 * FFN_HIDDEN // cw)]
        _stream_cast(wgu_hbm, wgu_ref, gu_stage, sem, 0, [(ix, ix) for ix in cols])
        rw_ = FFN_STAGE_ROWS
        rows_ = [(slice(r * rw_, (r + 1) * rw_), slice(None)) for r in range(FFN_HIDDEN // rw_)]
        _stream_cast(wd_hbm, wd_ref, d_stage, sem, 1, [(ix, ix) for ix in rows_])

    sub = FFN_ROWS // FFN_SUBTILES
    ck = FFN_CHUNK
    nsl = 4
    rw = sub // nsl

    def work(st):
        r0 = st * sub
        hb = h_ref[r0:r0 + sub, :].astype(BF16)
        acc = None
        for c in range(FFN_HIDDEN // ck):
            gate = _dot(hb, wgu_ref[:, c * ck:(c + 1) * ck])
            up = _dot(hb, wgu_ref[:, FFN_HIDDEN + c * ck:FFN_HIDDEN + (c + 1) * ck])
            act = (jax.nn.silu(gate) * up).astype(BF16)
            part = _dot(act, wd_ref[c * ck:(c + 1) * ck, :])
            acc = part if acc is None else acc + part
            yield
        for j in range(nsl):
            rs = slice(r0 + j * rw, r0 + (j + 1) * rw)
            o_ref[rs, :] = _layer_norm(DEEPNORM_ALPHA * h_ref[rs, :] + acc[j * rw:(j + 1) * rw],
                                       ln_ref[6:7, :], ln_ref[7:8, :])
            yield

    _interleave([work(st) for st in range(FFN_SUBTILES)], lag=FFN_HIDDEN // ck // 2 + 1)


def _ffn(h2, wgu, wd, ln):
    tm = FFN_ROWS
    row = lambda i: (i, 0)
    const = lambda i: (0, 0)
    hbm = pl.BlockSpec(memory_space=pl.ANY)
    return pl.pallas_call(
        _ffn_kernel,
        grid=(N_TOK // tm,),
        in_specs=[pl.BlockSpec((tm, D_MODEL), row), hbm, hbm,
                  pl.BlockSpec((N_LN_ROWS, D_MODEL), const)],
        out_specs=pl.BlockSpec((tm, D_MODEL), row),
        out_shape=jax.ShapeDtypeStruct((N_TOK, D_MODEL), F32),
        scratch_shapes=[pltpu.VMEM((D_MODEL, 2 * FFN_HIDDEN), BF16), pltpu.VMEM((FFN_HIDDEN, D_MODEL), BF16),
                        pltpu.VMEM((FFN_STAGE_SLOTS, D_MODEL, FFN_STAGE_COLS), F32),
                        pltpu.VMEM((FFN_STAGE_SLOTS, FFN_STAGE_ROWS, D_MODEL), F32),
                        pltpu.SemaphoreType.DMA((2, FFN_STAGE_SLOTS))],
        compiler_params=_params("arbitrary"),
        name="ffn",
    )(h2, wgu, wd, ln)


def kernel(x, mem, ln_in_g, ln_in_b, w_in, s5_lambda_re, s5_lambda_im, s5_log_dt, s5_b_re, s5_b_im,
           s5_c_re, s5_c_im, s5_d, s5_glu_w, s5_glu_b, diff_lq1, diff_lk1, diff_lq2, diff_lk2,
           diff_subln_g, rel_bias, w_out, ln1_g, ln1_b, ca_wq, ca_wkv, ca_wo, ln2_g, ln2_b,
           ffn_w_gate_up, ffn_w_down, ln3_g, ln3_b):
    x2 = x.reshape(N_TOK, D_MODEL)
    ln = jnp.stack([ln_in_g, ln_in_b, ln1_g[0], ln1_b[0], ln2_g[0], ln2_b[0], ln3_g[0], ln3_b[0]])
    lv = jnp.stack([diff_lq1[0], diff_lk1[0], diff_lq2[0], diff_lk2[0]])
    dg = jnp.stack([s5_d[0].reshape(S5_WIDTH), s5_glu_b[0]])

    h0, u, q, k, v = _proj(x2, ln, w_in[0])

    a8, w_s5 = _s5_prep(s5_lambda_re[0], s5_lambda_im[0], s5_log_dt[0], s5_b_re[0], s5_b_im[0],
                        s5_c_re[0], s5_c_im[0])
    y_s5 = _s5(u.reshape(BATCH, SEQ, S5_WIDTH), a8, w_s5, dg, s5_glu_w[0]).reshape(N_TOK, S5_WIDTH)

    y_diff = _diff_attn(q, k, v, _bias_tiles(rel_bias), lv, diff_subln_g.reshape(DIFF_V_DIM, 1))

    kv = _kv(mem.reshape(BATCH * MEM_LEN, D_MODEL), ca_wkv[0])
    h2 = _mix_ca(h0, y_s5, y_diff, ln, w_out[0], ca_wq[0], kv, ca_wo[0])
    out = _ffn(h2, ffn_w_gate_up[0], ffn_w_down[0], ln)
    return out.reshape(BATCH, SEQ, D_MODEL)
```

```python
import math

import numpy as np

import jax
import jax.numpy as jnp
from jax import lax
from jax.experimental import pallas as pl
from jax.experimental.pallas import tpu as pltpu

F32 = jnp.float32
BF16 = jnp.bfloat16

D_MODEL = 1024
BATCH = 8
SEQ = 2048
N_TOK = BATCH * SEQ
MEM_LEN = 256
S5_WIDTH = 512
S5_GROUP = 16
S5_GROUPS = 32
S5_STATE = 64
S5_COLS = S5_GROUPS * S5_STATE
DIFF_WIDTH = 512
DIFF_HEAD_DIM = 64
DIFF_V_DIM = 128
N_DIFF_HEADS = 4
DIFF_QK_WIDTH = 2 * N_DIFF_HEADS * DIFF_HEAD_DIM
MIX_IN = S5_WIDTH + 2 * DIFF_QK_WIDTH + DIFF_WIDTH
COL_Q = S5_WIDTH
COL_K = COL_Q + DIFF_QK_WIDTH
COL_V = COL_K + DIFF_QK_WIDTH
NUM_BUCKETS = 32
MAX_DISTANCE = 128
CA_HEADS = 4
CA_HEAD_DIM = 256
FFN_HIDDEN = 2816
DEEPNORM_ALPHA = 2.0 ** 0.25
LN_EPS = 1e-5
LAMBDA_INIT = 0.8 - 0.6 * math.exp(0.0)
LOG2E = math.log2(math.e)

VMEM_LIMIT_BYTES = 56 * 1024 * 1024
MXU_DIM = 256
LANES = 128

PROJ_ROWS = 1024
PROJ_SUBTILES = 2
S5_SUB_STEPS = 64
S5_SUBTILES = 2
S5_SCAN_CHUNK = 512
ATT_TILE = 512
ATT_ONES_ROWS = 16
FFN_CHUNK = 256
FFN_ROWS = 1024
FFN_SUBTILES = 2
FFN_STAGE_COLS = 512
FFN_STAGE_ROWS = 256
FFN_STAGE_SLOTS = 4
MIX_ROWS = 1024
MIX_SUBTILES = 2
MIX_SLICES = 4
N_LN_ROWS = 8
KV_ROWS = 1024

_NT = (((1,), (1,)), ((), ()))


def _params(*sem):
    return pltpu.CompilerParams(dimension_semantics=sem, vmem_limit_bytes=VMEM_LIMIT_BYTES)


def _layer_norm(x, g, b):
    mu = jnp.mean(x, axis=-1, keepdims=True)
    xc = x - mu
    var = jnp.mean(xc * xc, axis=-1, keepdims=True)
    return xc * lax.rsqrt(var + LN_EPS) * g + b


def _dot(a, b):
    return jnp.dot(a, b, preferred_element_type=F32)


def _cast_once(w_ref, wbf_ref):
    @pl.when(pl.program_id(0) == 0)
    def _():
        wbf_ref[...] = w_ref[...].astype(BF16)


def _resident(shape):
    return pl.BlockSpec(shape, lambda *_: (0,) * len(shape), pipeline_mode=pl.Buffered(1))


def _interleave(gens, lag):
    gens = list(gens)
    live = [True] * len(gens)
    tick = 0
    while any(live):
        for i, g in enumerate(gens):
            if live[i] and tick >= lag * i:
                try:
                    next(g)
                except StopIteration:
                    live[i] = False
        tick += 1


def _s5_prep_kernel(lam_ref, bc_ref, a_ref, w_ref):
    lr = lam_ref[0:1, :]
    li = lam_ref[1:2, :]
    dt = jnp.exp(lam_ref[2:3, :])
    mag = jnp.exp(lr * dt)
    ang = li * dt
    ar = mag * jnp.cos(ang)
    ai = mag * jnp.sin(ang)
    den = lr * lr + li * li
    nr = ar - 1.0
    fr = (nr * lr + ai * li) / den
    fi = (ai * lr - nr * li) / den
    a_ref[0] = jnp.broadcast_to(ar, (BATCH, S5_COLS))
    a_ref[1] = jnp.broadcast_to(ai, (BATCH, S5_COLS))
    b_r = bc_ref[0]
    b_i = bc_ref[1]
    mats = (fr * b_r - fi * b_i, fr * b_i + fi * b_r, bc_ref[2], bc_ref[3])
    half = S5_COLS // 2
    lane_group = lax.broadcasted_iota(jnp.int32, (S5_GROUP, half), 1) // S5_STATE
    for k, mat in enumerate(mats):
        for hf in range(2):
            src = mat[:, hf * half:(hf + 1) * half]
            for gl in range(MXU_DIM // S5_GROUP):
                w_ref[k, hf, gl * S5_GROUP:(gl + 1) * S5_GROUP, :] = jnp.where(
                    lane_group == gl, src, 0.0).astype(BF16)


def _s5_prep(lam_re, lam_im, log_dt, b_re, b_im, c_re, c_im):
    lam = jnp.stack([lam_re.reshape(S5_COLS), lam_im.reshape(S5_COLS), jnp.repeat(log_dt, S5_STATE)])
    bc = jnp.stack([jnp.transpose(b_re, (2, 0, 1)), jnp.transpose(b_im, (2, 0, 1)),
                    jnp.transpose(c_re, (1, 0, 2)), jnp.transpose(c_im, (1, 0, 2))]).reshape(4, S5_GROUP, S5_COLS)
    return pl.pallas_call(
        _s5_prep_kernel,
        out_shape=(jax.ShapeDtypeStruct((2, BATCH, S5_COLS), F32),
                   jax.ShapeDtypeStruct((4, 2, MXU_DIM, S5_COLS // 2), BF16)),
        name="s5_prep",
    )(lam, bc)


def _bias_prep_kernel(rel_ref, bucket_ref, o_ref):
    for j in range(2):
        bucket = bucket_ref[j]
        for h in range(N_DIFF_HEADS):
            far = rel_ref[NUM_BUCKETS - 1, h]
            acc = jnp.zeros(bucket.shape, F32)
            for b in range(NUM_BUCKETS):
                acc = jnp.where(bucket == b, (rel_ref[b, h] - far) * LOG2E, acc)
            o_ref[h, j] = jnp.where(bucket < 0, -jnp.inf, acc)


def _t5_bucket(dist):
    max_exact = NUM_BUCKETS // 2
    df = np.maximum(dist, 1).astype(np.float32)
    large = max_exact + (np.log(df / np.float32(max_exact)) / np.float32(math.log(MAX_DISTANCE / max_exact))
                         * np.float32(NUM_BUCKETS - max_exact)).astype(np.int32)
    large = np.minimum(large, NUM_BUCKETS - 1)
    return np.where(dist < max_exact, dist, large).astype(np.int32)


def _bias_tiles(rel_bias):
    t = LANES
    qpos = np.arange(t, dtype=np.int32)[None, :]
    kpos = np.arange(t, dtype=np.int32)[:, None]
    d_diag = qpos - kpos
    d_prev = d_diag + t
    bucket = jnp.asarray(np.stack([np.where(d_diag >= 0, _t5_bucket(np.maximum(d_diag, 0)), -1),
                                   _t5_bucket(d_prev)]).astype(np.int32))
    return pl.pallas_call(
        _bias_prep_kernel,
        in_specs=[pl.BlockSpec(memory_space=pltpu.SMEM), pl.BlockSpec(memory_space=pltpu.VMEM)],
        out_specs=pl.BlockSpec(memory_space=pltpu.VMEM),
        out_shape=jax.ShapeDtypeStruct((N_DIFF_HEADS, 2, t, t), F32),
        name="bias_prep",
    )(rel_bias, bucket)


def _proj_kernel(x_ref, ln_ref, w32_ref, h_ref, u_ref, q_ref, k_ref, v_ref, w_ref):
    _cast_once(w32_ref, w_ref)
    sub = PROJ_ROWS // PROJ_SUBTILES
    nsl = 4
    rw = sub // nsl

    def work(st):
        r0 = st * sub
        h = []
        for j in range(nsl):
            rows = slice(r0 + j * rw, r0 + (j + 1) * rw)
            hj = _layer_norm(x_ref[rows, :], ln_ref[0:1, :], ln_ref[1:2, :])
            h_ref[rows, :] = hj
            h.append(hj.astype(BF16))
            yield
        h = jnp.concatenate(h, axis=0)
        rs = slice(r0, r0 + sub)
        u_ref[rs, :] = _dot(h, w_ref[:, 0:COL_Q])
        yield
        q_ref[rs, :] = (_dot(h, w_ref[:, COL_Q:COL_K]) * (DIFF_HEAD_DIM ** -0.5 * LOG2E)).astype(BF16)
        yield
        k_ref[rs, :] = _dot(h, w_ref[:, COL_K:COL_V]).astype(BF16)
        yield
        v_ref[rs, :] = _dot(h, w_ref[:, COL_V:MIX_IN]).astype(BF16)
        yield

    _interleave([work(st) for st in range(PROJ_SUBTILES)], lag=nsl)


def _proj(x2, ln, w):
    tm = PROJ_ROWS
    row = lambda i: (i, 0)
    const = lambda i: (0, 0)
    return pl.pallas_call(
        _proj_kernel,
        grid=(N_TOK // tm,),
        in_specs=[pl.BlockSpec((tm, D_MODEL), row), pl.BlockSpec((N_LN_ROWS, D_MODEL), const),
                  _resident((D_MODEL, MIX_IN))],
        out_specs=([pl.BlockSpec((tm, D_MODEL), row), pl.BlockSpec((tm, S5_WIDTH), row)]
                   + [pl.BlockSpec((tm, DIFF_QK_WIDTH), row)] * 2 + [pl.BlockSpec((tm, DIFF_WIDTH), row)]),
        out_shape=(jax.ShapeDtypeStruct((N_TOK, D_MODEL), F32), jax.ShapeDtypeStruct((N_TOK, S5_WIDTH), F32),
                   jax.ShapeDtypeStruct((N_TOK, DIFF_QK_WIDTH), BF16), jax.ShapeDtypeStruct((N_TOK, DIFF_QK_WIDTH), BF16),
                   jax.ShapeDtypeStruct((N_TOK, DIFF_WIDTH), BF16)),
        scratch_shapes=[pltpu.VMEM((D_MODEL, MIX_IN), BF16)],
        compiler_params=_params("arbitrary"),
        name="proj",
    )(x2, ln, w)


def _s5_kernel(u_ref, a_ref, w_ref, dg_ref, gw32_ref, y_ref, sre, sim, xre, xim, utm, ytm, gw_ref):
    @pl.when(pl.program_id(0) == 0)
    def _():
        xre[...] = jnp.zeros_like(xre)
        xim[...] = jnp.zeros_like(xim)

    _cast_once(gw32_ref, gw_ref)
    half = S5_COLS // 2
    cw = S5_SCAN_CHUNK
    nchunk = S5_COLS // cw
    rows = S5_SUB_STEPS * BATCH
    steps = S5_SUBTILES * S5_SUB_STEPS
    nslab = S5_WIDTH // LANES
    carry = [None] * nchunk

    for b in range(BATCH):
        for sl in range(nslab):
            utm[sl, pl.ds(b, steps, stride=BATCH), :] = u_ref[b, :, sl * LANES:(sl + 1) * LANES]

    def work(st):
        r0 = st * rows
        u = jnp.concatenate([utm[sl, r0:r0 + rows, :] for sl in range(nslab)], axis=-1)
        ub = u.astype(BF16)
        for hf in range(2):
            uh = ub[:, hf * MXU_DIM:(hf + 1) * MXU_DIM]
            sre[r0:r0 + rows, hf * half:(hf + 1) * half] = _dot(uh, w_ref[0, hf])
            yield
            sim[r0:r0 + rows, hf * half:(hf + 1) * half] = _dot(uh, w_ref[1, hf])
            yield
        for c in range(nchunk):
            cs = slice(c * cw, (c + 1) * cw)
            ar = a_ref[0, :, cs]
            ai = a_ref[1, :, cs]
            xr, xi = (xre[:, cs], xim[:, cs]) if st == 0 else carry[c]
            for t in range(S5_SUB_STEPS):
                r = r0 + t * BATCH
                nr = ar * xr - ai * xi + sre[r:r + BATCH, cs]
                ni = ar * xi + ai * xr + sim[r:r + BATCH, cs]
                sre[r:r + BATCH, cs] = nr
                sim[r:r + BATCH, cs] = ni
                xr, xi = nr, ni
            carry[c] = (xr, xi)
            if st == S5_SUBTILES - 1:
                xre[:, cs] = xr
                xim[:, cs] = xi
            yield
        ys = []
        for hf in range(2):
            hs = slice(hf * half, (hf + 1) * half)
            ys.append(lax.dot_general(sre[r0:r0 + rows, hs].astype(BF16), w_ref[2, hf], _NT,
                                      preferred_element_type=F32)
                      - lax.dot_general(sim[r0:r0 + rows, hs].astype(BF16), w_ref[3, hf], _NT,
                                        preferred_element_type=F32))
            yield
        y = jax.nn.gelu(jnp.concatenate(ys, axis=-1) + dg_ref[0:1, :] * u)
        yield
        z = _dot(y.astype(BF16), gw_ref[...]) + dg_ref[1:2, :]
        y = y * jax.nn.sigmoid(z)
        for sl in range(nslab):
            ytm[sl, r0:r0 + rows, :] = y[:, sl * LANES:(sl + 1) * LANES]
        t0 = st * S5_SUB_STEPS
        for b in range(BATCH):
            y_ref[b, t0:t0 + S5_SUB_STEPS, :] = jnp.concatenate(
                [ytm[sl, pl.ds(r0 + b, S5_SUB_STEPS, stride=BATCH), :] for sl in range(nslab)],
                axis=-1).astype(BF16)
        yield

    _interleave([work(st) for st in range(S5_SUBTILES)], lag=4)


def _s5(u, a8, w, dg, gw):
    steps = S5_SUBTILES * S5_SUB_STEPS
    rows = steps * BATCH
    nslab = S5_WIDTH // LANES
    blk = pl.BlockSpec((BATCH, steps, S5_WIDTH), lambda i: (0, i, 0))
    c2 = lambda i: (0, 0)
    return pl.pallas_call(
        _s5_kernel,
        grid=(SEQ // steps,),
        in_specs=[blk,
                  pl.BlockSpec((2, BATCH, S5_COLS), lambda i: (0, 0, 0)),
                  pl.BlockSpec((4, 2, MXU_DIM, S5_COLS // 2), lambda i: (0, 0, 0, 0)),
                  pl.BlockSpec((2, S5_WIDTH), c2), pl.BlockSpec((S5_WIDTH, S5_WIDTH), c2)],
        out_specs=blk,
        out_shape=jax.ShapeDtypeStruct((BATCH, SEQ, S5_WIDTH), BF16),
        scratch_shapes=[pltpu.VMEM((rows, S5_COLS), F32), pltpu.VMEM((rows, S5_COLS), F32),
                        pltpu.VMEM((BATCH, S5_COLS), F32), pltpu.VMEM((BATCH, S5_COLS), F32),
                        pltpu.VMEM((nslab, rows, LANES), F32), pltpu.VMEM((nslab, rows, LANES), F32),
                        pltpu.VMEM((S5_WIDTH, S5_WIDTH), BF16)],
        compiler_params=_params("arbitrary"),
        name="s5",
    )(u, a8, w, dg, gw)


def _diff_attn_kernel(q_ref, k_ref, v_ref, bias_ref, lv_ref, g_ref, o_ref):
    t = ATT_TILE
    half = t // 2
    sb = LANES
    lam = (jnp.exp(jnp.sum(lv_ref[0:1, :] * lv_ref[1:2, :], axis=-1, keepdims=True))
           - jnp.exp(jnp.sum(lv_ref[2:3, :] * lv_ref[3:4, :], axis=-1, keepdims=True)) + LAMBDA_INIT)
    v_t = jnp.concatenate([v_ref[...].T, jnp.ones((ATT_ONES_ROWS, SEQ), BF16)], axis=0)
    dim = lax.broadcasted_iota(jnp.int32, (DIFF_V_DIM, t), 0)
    zero = jnp.zeros((DIFF_V_DIM, t), BF16)

    def scores(qm, k0, width, lo, hi):
        kb = k_ref[k0:k0 + width, :]
        return [_dot(kb, qm[mi][:, lo:hi]) for mi in range(2)]

    def with_bias(s, r0, k0):
        width, rows = s.shape
        slabs = []
        for i in range(width // sb):
            kb = (k0 + i * sb) // sb
            slab = s[i * sb:(i + 1) * sb, :]
            qbs = [(r0 + j * sb) // sb for j in range(rows // sb)]
            if all(kb < qb - 1 for qb in qbs):
                slabs.append(slab)
                continue
            pieces = []
            for j, qb in enumerate(qbs):
                piece = slab[:, j * sb:(j + 1) * sb]
                if kb == qb:
                    piece = piece + bias_ref[0, 0]
                elif kb == qb - 1:
                    piece = piece + bias_ref[0, 1]
                elif kb > qb:
                    piece = jnp.full((sb, sb), -jnp.inf, F32)
                pieces.append(piece)
            slabs.append(jnp.concatenate(pieces, axis=-1))
        return jnp.concatenate(slabs, axis=0)

    def update(s, r0, k0, state):
        m_old, acc_old = state
        s = with_bias(s, r0, k0)
        vb = v_t[:, k0:k0 + s.shape[0]]
        blk_max = jnp.max(s, axis=0, keepdims=True)
        if m_old is None:
            return blk_max, _dot(vb, jnp.exp2(s - blk_max).astype(BF16))
        m_new = jnp.maximum(m_old, blk_max)
        alpha = jnp.exp2(m_old - m_new)
        return m_new, alpha * acc_old + _dot(vb, jnp.exp2(s - m_new).astype(BF16))

    def lanes_of(state, lo, hi):
        return tuple(None if a is None else a[:, lo:hi] for a in state)

    for c in range(SEQ // t):
        r0 = c * t
        q_t = q_ref[r0:r0 + t, :].T
        qm = (jnp.where(dim < DIFF_HEAD_DIM, q_t, zero), jnp.where(dim >= DIFF_HEAD_DIM, q_t, zero))

        def diag_scores(qm=qm, r0=r0):
            return scores(qm, r0, half, 0, half), scores(qm, r0, t, half, t)

        state = [(None, None), (None, None)]
        s_next = scores(qm, 0, t, 0, t) if c > 0 else diag_scores()
        for j in range(c):
            s_cur = s_next
            s_next = scores(qm, (j + 1) * t, t, 0, t) if j + 1 < c else diag_scores()
            state = [update(s_cur[mi], r0, j * t, state[mi]) for mi in range(2)]
        s_lo, s_hi = s_next
        outs = []
        for mi in range(2):
            lo = update(s_lo[mi], r0, r0, lanes_of(state[mi], 0, half))
            hi = update(s_hi[mi], r0 + half, r0, lanes_of(state[mi], half, t))
            acc = jnp.concatenate([lo[1], hi[1]], axis=-1)
            outs.append(acc[:DIFF_V_DIM] / acc[DIFF_V_DIM:DIFF_V_DIM + 1])
        o = outs[0] - lam * outs[1]
        ms = jnp.mean(o * o, axis=0, keepdims=True)
        y = o * lax.rsqrt(ms + LN_EPS) * g_ref[...] * (1.0 - LAMBDA_INIT)
        o_ref[r0:r0 + t, :] = y.T.astype(BF16)


def _diff_attn(q, k, v, bias, lv, g):
    vec = lambda b, h: (0, 0)
    seq = pl.BlockSpec((SEQ, DIFF_V_DIM), lambda b, h: (b, h))
    return pl.pallas_call(
        _diff_attn_kernel,
        grid=(BATCH, N_DIFF_HEADS),
        in_specs=[seq, seq, seq,
                  pl.BlockSpec((1, 2, LANES, LANES), lambda b, h: (h, 0, 0, 0)),
                  pl.BlockSpec((4, DIFF_HEAD_DIM), vec), pl.BlockSpec((DIFF_V_DIM, 1), vec)],
        out_specs=seq,
        out_shape=jax.ShapeDtypeStruct((N_TOK, DIFF_WIDTH), BF16),
        compiler_params=_params("parallel", "parallel"),
        name="diff_attn",
    )(q, k, v, bias, lv, g)


def _kv_kernel(m_ref, w32_ref, o_ref, w_ref):
    _cast_once(w32_ref, w_ref)
    o_ref[...] = _dot(m_ref[...].astype(BF16), w_ref[...]).astype(BF16)


def _kv(mem2, wkv):
    tm = KV_ROWS
    return pl.pallas_call(
        _kv_kernel,
        grid=(BATCH * MEM_LEN // tm,),
        in_specs=[pl.BlockSpec((tm, D_MODEL), lambda i: (i, 0)), _resident((D_MODEL, 2 * D_MODEL))],
        out_specs=pl.BlockSpec((tm, 2 * D_MODEL), lambda i: (i, 0)),
        out_shape=jax.ShapeDtypeStruct((BATCH * MEM_LEN, 2 * D_MODEL), BF16),
        scratch_shapes=[pltpu.VMEM((D_MODEL, 2 * D_MODEL), BF16)],
        compiler_params=_params("arbitrary"),
        name="kv",
    )(mem2, wkv)


def _mix_ca_kernel(h0_ref, ys_ref, yd_ref, ln_ref, wout32_ref, wq32_ref, kv_ref, wo32_ref, o_ref,
                   wout_ref, wq_ref, wo_ref):
    _cast_once(wout32_ref, wout_ref)
    _cast_once(wq32_ref, wq_ref)
    _cast_once(wo32_ref, wo_ref)
    sub = MIX_ROWS // MIX_SUBTILES
    nsl = MIX_SLICES
    cw = D_MODEL // nsl
    rw = sub // nsl

    def work(st):
        r0 = st * sub
        ys = ys_ref[r0:r0 + sub, :]
        yd = yd_ref[r0:r0 + sub, :]
        mix = []
        for j in range(nsl):
            cs = slice(j * cw, (j + 1) * cw)
            mix.append(_dot(ys, wout_ref[0:S5_WIDTH, cs]) + _dot(yd, wout_ref[S5_WIDTH:, cs]))
            yield
        mix = jnp.concatenate(mix, axis=-1)
        h1 = []
        for j in range(nsl):
            rs = slice(j * rw, (j + 1) * rw)
            h0 = h0_ref[r0 + j * rw:r0 + (j + 1) * rw, :]
            h1.append(_layer_norm(DEEPNORM_ALPHA * h0 + mix[rs], ln_ref[2:3, :], ln_ref[3:4, :]))
            yield
        h1 = jnp.concatenate(h1, axis=0)
        h1b = h1.astype(BF16)
        qs = []
        for hd in range(CA_HEADS):
            cs = slice(hd * CA_HEAD_DIM, (hd + 1) * CA_HEAD_DIM)
            qs.append((_dot(h1b, wq_ref[:, cs]) * (CA_HEAD_DIM ** -0.5)).astype(BF16))
            yield
        heads = []
        for hd in range(CA_HEADS):
            kh = kv_ref[:, hd * CA_HEAD_DIM:(hd + 1) * CA_HEAD_DIM]
            vh = kv_ref[:, D_MODEL + hd * CA_HEAD_DIM:D_MODEL + (hd + 1) * CA_HEAD_DIM]
            s = lax.dot_general(qs[hd], kh, _NT, preferred_element_type=F32)
            e = jnp.exp(s - jnp.max(s, axis=-1, keepdims=True))
            oh = _dot(e.astype(BF16), vh) / jnp.sum(e, axis=-1, keepdims=True)
            heads.append(oh.astype(BF16))
            yield
        o = jnp.concatenate(heads, axis=-1)
        ca = []
        for j in range(nsl):
            ca.append(_dot(o, wo_ref[:, j * cw:(j + 1) * cw]))
            yield
        ca = jnp.concatenate(ca, axis=-1)
        for j in range(nsl):
            rs = slice(j * rw, (j + 1) * rw)
            o_ref[r0 + j * rw:r0 + (j + 1) * rw, :] = _layer_norm(
                DEEPNORM_ALPHA * h1[rs] + ca[rs], ln_ref[4:5, :], ln_ref[5:6, :])
            yield

    _interleave([work(st) for st in range(MIX_SUBTILES)], lag=nsl)


def _mix_ca(h0, ys, yd, ln, wout, wq, kv, wo):
    tm = MIX_ROWS
    per_batch = SEQ // tm
    row = lambda i: (i, 0)
    const = lambda i: (0, 0)
    sq = _resident((D_MODEL, D_MODEL))
    return pl.pallas_call(
        _mix_ca_kernel,
        grid=(N_TOK // tm,),
        in_specs=[pl.BlockSpec((tm, D_MODEL), row), pl.BlockSpec((tm, S5_WIDTH), row),
                  pl.BlockSpec((tm, DIFF_WIDTH), row),
                  pl.BlockSpec((N_LN_ROWS, D_MODEL), const), sq, sq,
                  pl.BlockSpec((MEM_LEN, 2 * D_MODEL), lambda i: (i // per_batch, 0)),
                  sq],
        out_specs=pl.BlockSpec((tm, D_MODEL), row),
        out_shape=jax.ShapeDtypeStruct((N_TOK, D_MODEL), F32),
        scratch_shapes=[pltpu.VMEM((D_MODEL, D_MODEL), BF16)] * 3,
        compiler_params=_params("arbitrary"),
        name="mix_ca",
    )(h0, ys, yd, ln, wout, wq, kv, wo)


def _stream_cast(src_hbm, dst_ref, stage, sems, w, chunks):
    ns = FFN_STAGE_SLOTS

    def copy(i):
        return pltpu.make_async_copy(src_hbm.at[chunks[i][0]], stage.at[i % ns], sems.at[w, i % ns])

    for i in range(min(ns - 1, len(chunks))):
        copy(i).start()
    for i in range(len(chunks)):
        if i + ns - 1 < len(chunks):
            copy(i + ns - 1).start()
        copy(i).wait()
        dst_ref[chunks[i][1]] = stage[i % ns].astype(BF16)


def _ffn_kernel(h_ref, wgu_hbm, wd_hbm, ln_ref, o_ref, wgu_ref, wd_ref, gu_stage, d_stage, sem):
    @pl.when(pl.program_id(0) == 0)
    def _():
        cw = FFN_STAGE_COLS
        cols = [(slice(None), slice(c * cw, (c + 1) * cw)) for c in range(2 * FFN_HIDDEN // cw)]
        _stream_cast(wgu_hbm, wgu_ref, gu_stage, sem, 0, [(ix, ix) for ix in cols])
        rw_ = FFN_STAGE_ROWS
        rows_ = [(slice(r * rw_, (r + 1) * rw_), slice(None)) for r in range(FFN_HIDDEN // rw_)]
        _stream_cast(wd_hbm, wd_ref, d_stage, sem, 1, [(ix, ix) for ix in rows_])

    sub = FFN_ROWS // FFN_SUBTILES
    ck = FFN_CHUNK
    nsl = 4
    rw = sub // nsl

    def work(st):
        r0 = st * sub
        hb = h_ref[r0:r0 + sub, :].astype(BF16)
        acc = None
        for c in range(FFN_HIDDEN // ck):
            gate = _dot(hb, wgu_ref[:, c * ck:(c + 1) * ck])
            up = _dot(hb, wgu_ref[:, FFN_HIDDEN + c * ck:FFN_HIDDEN + (c + 1) * ck])
            act = (jax.nn.silu(gate) * up).astype(BF16)
            part = _dot(act, wd_ref[c * ck:(c + 1) * ck, :])
            acc = part if acc is None else acc + part
            yield
        for j in range(nsl):
            rs = slice(r0 + j * rw, r0 + (j + 1) * rw)
            o_ref[rs, :] = _layer_norm(DEEPNORM_ALPHA * h_ref[rs, :] + acc[j * rw:(j + 1) * rw],
                                       ln_ref[6:7, :], ln_ref[7:8, :])
            yield

    _interleave([work(st) for st in range(FFN_SUBTILES)], lag=FFN_HIDDEN // ck // 2 + 1)


def _ffn(h2, wgu, wd, ln):
    tm = FFN_ROWS
    row = lambda i: (i, 0)
    const = lambda i: (0, 0)
    hbm = pl.BlockSpec(memory_space=pl.ANY)
    return pl.pallas_call(
        _ffn_kernel,
        grid=(N_TOK // tm,),
        in_specs=[pl.BlockSpec((tm, D_MODEL), row), hbm, hbm,
                  pl.BlockSpec((N_LN_ROWS, D_MODEL), const)],
        out_specs=pl.BlockSpec((tm, D_MODEL), row),
        out_shape=jax.ShapeDtypeStruct((N_TOK, D_MODEL), F32),
        scratch_shapes=[pltpu.VMEM((D_MODEL, 2 * FFN_HIDDEN), BF16), pltpu.VMEM((FFN_HIDDEN, D_MODEL), BF16),
                        pltpu.VMEM((FFN_STAGE_SLOTS, D_MODEL, FFN_STAGE_COLS), F32),
                        pltpu.VMEM((FFN_STAGE_SLOTS, FFN_STAGE_ROWS, D_MODEL), F32),
                        pltpu.SemaphoreType.DMA((2, FFN_STAGE_SLOTS))],
        compiler_params=_params("arbitrary"),
        name="ffn",
    )(h2, wgu, wd, ln)


def kernel(x, mem, ln_in_g, ln_in_b, w_in, s5_lambda_re, s5_lambda_im, s5_log_dt, s5_b_re, s5_b_im,
           s5_c_re, s5_c_im, s5_d, s5_glu_w, s5_glu_b, diff_lq1, diff_lk1, diff_lq2, diff_lk2,
           diff_subln_g, rel_bias, w_out, ln1_g, ln1_b, ca_wq, ca_wkv, ca_wo, ln2_g, ln2_b,
           ffn_w_gate_up, ffn_w_down, ln3_g, ln3_b):
    x2 = x.reshape(N_TOK, D_MODEL)
    ln = jnp.stack([ln_in_g, ln_in_b, ln1_g[0], ln1_b[0], ln2_g[0], ln2_b[0], ln3_g[0], ln3_b[0]])
    lv = jnp.stack([diff_lq1[0], diff_lk1[0], diff_lq2[0], diff_lk2[0]])
    dg = jnp.stack([s5_d[0].reshape(S5_WIDTH), s5_glu_b[0]])

    h0, u, q, k, v = _proj(x2, ln, w_in[0])

    a8, w_s5 = _s5_prep(s5_lambda_re[0], s5_lambda_im[0], s5_log_dt[0], s5_b_re[0], s5_b_im[0],
                        s5_c_re[0], s5_c_im[0])
    y_s5 = _s5(u.reshape(BATCH, SEQ, S5_WIDTH), a8, w_s5, dg, s5_glu_w[0]).reshape(N_TOK, S5_WIDTH)

    y_diff = _diff_attn(q, k, v, _bias_tiles(rel_bias), lv, diff_subln_g.reshape(DIFF_V_DIM, 1))

    kv = _kv(mem.reshape(BATCH * MEM_LEN, D_MODEL), ca_wkv[0])
    h2 = _mix_ca(h0, y_s5, y_diff, ln, w_out[0], ca_wq[0], kv, ca_wo[0])
    out = _ffn(h2, ffn_w_gate_up[0], ffn_w_down[0], ln)
    return out.reshape(BATCH, SEQ, D_MODEL)
```

```python
import math

import numpy as np

import jax
import jax.numpy as jnp
from jax import lax
from jax.experimental import pallas as pl
from jax.experimental.pallas import tpu as pltpu

F32 = jnp.float32
BF16 = jnp.bfloat16

D_MODEL = 1024
BATCH = 8
SEQ = 2048
N_TOK = BATCH * SEQ
MEM_LEN = 256
S5_WIDTH = 512
S5_GROUP = 16
S5_GROUPS = 32
S5_STATE = 64
S5_COLS = S5_GROUPS * S5_STATE
DIFF_WIDTH = 512
DIFF_HEAD_DIM = 64
DIFF_V_DIM = 128
N_DIFF_HEADS = 4
DIFF_QK_WIDTH = 2 * N_DIFF_HEADS * DIFF_HEAD_DIM
MIX_IN = S5_WIDTH + 2 * DIFF_QK_WIDTH + DIFF_WIDTH
COL_Q = S5_WIDTH
COL_K = COL_Q + DIFF_QK_WIDTH
COL_V = COL_K + DIFF_QK_WIDTH
NUM_BUCKETS = 32
MAX_DISTANCE = 128
CA_HEADS = 4
CA_HEAD_DIM = 256
FFN_HIDDEN = 2816
DEEPNORM_ALPHA = 2.0 ** 0.25
LN_EPS = 1e-5
LAMBDA_INIT = 0.8 - 0.6 * math.exp(0.0)
LOG2E = math.log2(math.e)

VMEM_LIMIT_BYTES = 58 * 1024 * 1024
MXU_DIM = 256
LANES = 128

PROJ_ROWS = 1024
PROJ_SUBTILES = 2
S5_SUB_STEPS = 64
S5_SUBTILES = 2
S5_SCAN_CHUNK = 512
ATT_TILE = 512
ATT_ONES_ROWS = 16
FFN_CHUNK = 256
FFN_ROWS = 1024
FFN_SUBTILES = 2
MIX_ROWS = 1024
MIX_SUBTILES = 2
MIX_SLICES = 4
N_LN_ROWS = 8
KV_ROWS = 1024

_NT = (((1,), (1,)), ((), ()))


def _params(*sem):
    return pltpu.CompilerParams(dimension_semantics=sem, vmem_limit_bytes=VMEM_LIMIT_BYTES)


def _layer_norm(x, g, b):
    mu = jnp.mean(x, axis=-1, keepdims=True)
    xc = x - mu
    var = jnp.mean(xc * xc, axis=-1, keepdims=True)
    return xc * lax.rsqrt(var + LN_EPS) * g + b


def _dot(a, b):
    return jnp.dot(a, b, preferred_element_type=F32)


def _cast_once(w_ref, wbf_ref):
    @pl.when(pl.program_id(0) == 0)
    def _():
        wbf_ref[...] = w_ref[...].astype(BF16)


def _resident(shape):
    return pl.BlockSpec(shape, lambda *_: (0,) * len(shape), pipeline_mode=pl.Buffered(1))


def _interleave(gens, lag):
    gens = list(gens)
    live = [True] * len(gens)
    tick = 0
    while any(live):
        for i, g in enumerate(gens):
            if live[i] and tick >= lag * i:
                try:
                    next(g)
                except StopIteration:
                    live[i] = False
        tick += 1


def _s5_prep_kernel(lam_ref, bc_ref, a_ref, w_ref):
    lr = lam_ref[0:1, :]
    li = lam_ref[1:2, :]
    dt = jnp.exp(lam_ref[2:3, :])
    mag = jnp.exp(lr * dt)
    ang = li * dt
    ar = mag * jnp.cos(ang)
    ai = mag * jnp.sin(ang)
    den = lr * lr + li * li
    nr = ar - 1.0
    fr = (nr * lr + ai * li) / den
    fi = (ai * lr - nr * li) / den
    a_ref[0] = jnp.broadcast_to(ar, (BATCH, S5_COLS))
    a_ref[1] = jnp.broadcast_to(ai, (BATCH, S5_COLS))
    b_r = bc_ref[0]
    b_i = bc_ref[1]
    mats = (fr * b_r - fi * b_i, fr * b_i + fi * b_r, bc_ref[2], bc_ref[3])
    half = S5_COLS // 2
    lane_group = lax.broadcasted_iota(jnp.int32, (S5_GROUP, half), 1) // S5_STATE
    for k, mat in enumerate(mats):
        for hf in range(2):
            src = mat[:, hf * half:(hf + 1) * half]
            for gl in range(MXU_DIM // S5_GROUP):
                w_ref[k, hf, gl * S5_GROUP:(gl + 1) * S5_GROUP, :] = jnp.where(
                    lane_group == gl, src, 0.0).astype(BF16)


def _s5_prep(lam_re, lam_im, log_dt, b_re, b_im, c_re, c_im):
    lam = jnp.stack([lam_re.reshape(S5_COLS), lam_im.reshape(S5_COLS), jnp.repeat(log_dt, S5_STATE)])
    bc = jnp.stack([jnp.transpose(b_re, (2, 0, 1)), jnp.transpose(b_im, (2, 0, 1)),
                    jnp.transpose(c_re, (1, 0, 2)), jnp.transpose(c_im, (1, 0, 2))]).reshape(4, S5_GROUP, S5_COLS)
    return pl.pallas_call(
        _s5_prep_kernel,
        out_shape=(jax.ShapeDtypeStruct((2, BATCH, S5_COLS), F32),
                   jax.ShapeDtypeStruct((4, 2, MXU_DIM, S5_COLS // 2), BF16)),
        name="s5_prep",
    )(lam, bc)


def _bias_prep_kernel(rel_ref, bucket_ref, o_ref):
    for j in range(2):
        bucket = bucket_ref[j]
        for h in range(N_DIFF_HEADS):
            far = rel_ref[NUM_BUCKETS - 1, h]
            acc = jnp.zeros(bucket.shape, F32)
            for b in range(NUM_BUCKETS):
                acc = jnp.where(bucket == b, (rel_ref[b, h] - far) * LOG2E, acc)
            o_ref[h, j] = jnp.where(bucket < 0, -jnp.inf, acc)


def _t5_bucket(dist):
    max_exact = NUM_BUCKETS // 2
    df = np.maximum(dist, 1).astype(np.float32)
    large = max_exact + (np.log(df / np.float32(max_exact)) / np.float32(math.log(MAX_DISTANCE / max_exact))
                         * np.float32(NUM_BUCKETS - max_exact)).astype(np.int32)
    large = np.minimum(large, NUM_BUCKETS - 1)
    return np.where(dist < max_exact, dist, large).astype(np.int32)


def _bias_tiles(rel_bias):
    t = LANES
    qpos = np.arange(t, dtype=np.int32)[None, :]
    kpos = np.arange(t, dtype=np.int32)[:, None]
    d_diag = qpos - kpos
    d_prev = d_diag + t
    bucket = jnp.asarray(np.stack([np.where(d_diag >= 0, _t5_bucket(np.maximum(d_diag, 0)), -1),
                                   _t5_bucket(d_prev)]).astype(np.int32))
    return pl.pallas_call(
        _bias_prep_kernel,
        in_specs=[pl.BlockSpec(memory_space=pltpu.SMEM), pl.BlockSpec(memory_space=pltpu.VMEM)],
        out_specs=pl.BlockSpec(memory_space=pltpu.VMEM),
        out_shape=jax.ShapeDtypeStruct((N_DIFF_HEADS, 2, t, t), F32),
        name="bias_prep",
    )(rel_bias, bucket)


def _proj_kernel(x_ref, ln_ref, w32_ref, h_ref, u_ref, q_ref, k_ref, v_ref, w_ref):
    _cast_once(w32_ref, w_ref)
    sub = PROJ_ROWS // PROJ_SUBTILES
    nsl = 4
    rw = sub // nsl

    def work(st):
        r0 = st * sub
        h = []
        for j in range(nsl):
            rows = slice(r0 + j * rw, r0 + (j + 1) * rw)
            hj = _layer_norm(x_ref[rows, :], ln_ref[0:1, :], ln_ref[1:2, :])
            h_ref[rows, :] = hj
            h.append(hj.astype(BF16))
            yield
        h = jnp.concatenate(h, axis=0)
        rs = slice(r0, r0 + sub)
        u_ref[rs, :] = _dot(h, w_ref[:, 0:COL_Q])
        yield
        q_ref[rs, :] = (_dot(h, w_ref[:, COL_Q:COL_K]) * (DIFF_HEAD_DIM ** -0.5 * LOG2E)).astype(BF16)
        yield
        k_ref[rs, :] = _dot(h, w_ref[:, COL_K:COL_V]).astype(BF16)
        yield
        v_ref[rs, :] = _dot(h, w_ref[:, COL_V:MIX_IN]).astype(BF16)
        yield

    _interleave([work(st) for st in range(PROJ_SUBTILES)], lag=nsl)


def _proj(x2, ln, w):
    tm = PROJ_ROWS
    row = lambda i: (i, 0)
    const = lambda i: (0, 0)
    return pl.pallas_call(
        _proj_kernel,
        grid=(N_TOK // tm,),
        in_specs=[pl.BlockSpec((tm, D_MODEL), row), pl.BlockSpec((N_LN_ROWS, D_MODEL), const),
                  _resident((D_MODEL, MIX_IN))],
        out_specs=([pl.BlockSpec((tm, D_MODEL), row), pl.BlockSpec((tm, S5_WIDTH), row)]
                   + [pl.BlockSpec((tm, DIFF_QK_WIDTH), row)] * 2 + [pl.BlockSpec((tm, DIFF_WIDTH), row)]),
        out_shape=(jax.ShapeDtypeStruct((N_TOK, D_MODEL), F32), jax.ShapeDtypeStruct((N_TOK, S5_WIDTH), F32),
                   jax.ShapeDtypeStruct((N_TOK, DIFF_QK_WIDTH), BF16), jax.ShapeDtypeStruct((N_TOK, DIFF_QK_WIDTH), BF16),
                   jax.ShapeDtypeStruct((N_TOK, DIFF_WIDTH), BF16)),
        scratch_shapes=[pltpu.VMEM((D_MODEL, MIX_IN), BF16)],
        compiler_params=_params("arbitrary"),
        name="proj",
    )(x2, ln, w)


def _s5_kernel(u_ref, a_ref, w_ref, dg_ref, gw32_ref, y_ref, sre, sim, xre, xim, utm, ytm, gw_ref):
    @pl.when(pl.program_id(0) == 0)
    def _():
        xre[...] = jnp.zeros_like(xre)
        xim[...] = jnp.zeros_like(xim)

    _cast_once(gw32_ref, gw_ref)
    half = S5_COLS // 2
    cw = S5_SCAN_CHUNK
    nchunk = S5_COLS // cw
    rows = S5_SUB_STEPS * BATCH
    steps = S5_SUBTILES * S5_SUB_STEPS
    nslab = S5_WIDTH // LANES
    carry = [None] * nchunk

    for b in range(BATCH):
        for sl in range(nslab):
            utm[sl, pl.ds(b, steps, stride=BATCH), :] = u_ref[b, :, sl * LANES:(sl + 1) * LANES]

    def work(st):
        r0 = st * rows
        u = jnp.concatenate([utm[sl, r0:r0 + rows, :] for sl in range(nslab)], axis=-1)
        ub = u.astype(BF16)
        for hf in range(2):
            uh = ub[:, hf * MXU_DIM:(hf + 1) * MXU_DIM]
            sre[r0:r0 + rows, hf * half:(hf + 1) * half] = _dot(uh, w_ref[0, hf])
            yield
            sim[r0:r0 + rows, hf * half:(hf + 1) * half] = _dot(uh, w_ref[1, hf])
            yield
        for c in range(nchunk):
            cs = slice(c * cw, (c + 1) * cw)
            ar = a_ref[0, :, cs]
            ai = a_ref[1, :, cs]
            xr, xi = (xre[:, cs], xim[:, cs]) if st == 0 else carry[c]
            for t in range(S5_SUB_STEPS):
                r = r0 + t * BATCH
                nr = ar * xr - ai * xi + sre[r:r + BATCH, cs]
                ni = ar * xi + ai * xr + sim[r:r + BATCH, cs]
                sre[r:r + BATCH, cs] = nr
                sim[r:r + BATCH, cs] = ni
                xr, xi = nr, ni
            carry[c] = (xr, xi)
            if st == S5_SUBTILES - 1:
                xre[:, cs] = xr
                xim[:, cs] = xi
            yield
        ys = []
        for hf in range(2):
            hs = slice(hf * half, (hf + 1) * half)
            ys.append(lax.dot_general(sre[r0:r0 + rows, hs].astype(BF16), w_ref[2, hf], _NT,
                                      preferred_element_type=F32)
                      - lax.dot_general(sim[r0:r0 + rows, hs].astype(BF16), w_ref[3, hf], _NT,
                                        preferred_element_type=F32))
            yield
        y = jax.nn.gelu(jnp.concatenate(ys, axis=-1) + dg_ref[0:1, :] * u)
        yield
        z = _dot(y.astype(BF16), gw_ref[...]) + dg_ref[1:2, :]
        y = y * jax.nn.sigmoid(z)
        for sl in range(nslab):
            ytm[sl, r0:r0 + rows, :] = y[:, sl * LANES:(sl + 1) * LANES]
        t0 = st * S5_SUB_STEPS
        for b in range(BATCH):
            y_ref[b, t0:t0 + S5_SUB_STEPS, :] = jnp.concatenate(
                [ytm[sl, pl.ds(r0 + b, S5_SUB_STEPS, stride=BATCH), :] for sl in range(nslab)],
                axis=-1).astype(BF16)
        yield

    _interleave([work(st) for st in range(S5_SUBTILES)], lag=4)


def _s5(u, a8, w, dg, gw):
    steps = S5_SUBTILES * S5_SUB_STEPS
    rows = steps * BATCH
    nslab = S5_WIDTH // LANES
    blk = pl.BlockSpec((BATCH, steps, S5_WIDTH), lambda i: (0, i, 0))
    c2 = lambda i: (0, 0)
    return pl.pallas_call(
        _s5_kernel,
        grid=(SEQ // steps,),
        in_specs=[blk,
                  pl.BlockSpec((2, BATCH, S5_COLS), lambda i: (0, 0, 0)),
                  pl.BlockSpec((4, 2, MXU_DIM, S5_COLS // 2), lambda i: (0, 0, 0, 0)),
                  pl.BlockSpec((2, S5_WIDTH), c2), pl.BlockSpec((S5_WIDTH, S5_WIDTH), c2)],
        out_specs=blk,
        out_shape=jax.ShapeDtypeStruct((BATCH, SEQ, S5_WIDTH), BF16),
        scratch_shapes=[pltpu.VMEM((rows, S5_COLS), F32), pltpu.VMEM((rows, S5_COLS), F32),
                        pltpu.VMEM((BATCH, S5_COLS), F32), pltpu.VMEM((BATCH, S5_COLS), F32),
                        pltpu.VMEM((nslab, rows, LANES), F32), pltpu.VMEM((nslab, rows, LANES), F32),
                        pltpu.VMEM((S5_WIDTH, S5_WIDTH), BF16)],
        compiler_params=_params("arbitrary"),
        name="s5",
    )(u, a8, w, dg, gw)


def _diff_attn_kernel(q_ref, k_ref, v_ref, bias_ref, lv_ref, g_ref, o_ref):
    t = ATT_TILE
    half = t // 2
    sb = LANES
    lam = (jnp.exp(jnp.sum(lv_ref[0:1, :] * lv_ref[1:2, :], axis=-1, keepdims=True))
           - jnp.exp(jnp.sum(lv_ref[2:3, :] * lv_ref[3:4, :], axis=-1, keepdims=True)) + LAMBDA_INIT)
    v_t = jnp.concatenate([v_ref[...].T, jnp.ones((ATT_ONES_ROWS, SEQ), BF16)], axis=0)
    dim = lax.broadcasted_iota(jnp.int32, (DIFF_V_DIM, t), 0)
    zero = jnp.zeros((DIFF_V_DIM, t), BF16)

    def scores(qm, k0, width, lo, hi):
        kb = k_ref[k0:k0 + width, :]
        return [_dot(kb, qm[mi][:, lo:hi]) for mi in range(2)]

    def with_bias(s, r0, k0):
        width, rows = s.shape
        slabs = []
        for i in range(width // sb):
            kb = (k0 + i * sb) // sb
            slab = s[i * sb:(i + 1) * sb, :]
            qbs = [(r0 + j * sb) // sb for j in range(rows // sb)]
            if all(kb < qb - 1 for qb in qbs):
                slabs.append(slab)
                continue
            pieces = []
            for j, qb in enumerate(qbs):
                piece = slab[:, j * sb:(j + 1) * sb]
                if kb == qb:
                    piece = piece + bias_ref[0, 0]
                elif kb == qb - 1:
                    piece = piece + bias_ref[0, 1]
                elif kb > qb:
                    piece = jnp.full((sb, sb), -jnp.inf, F32)
                pieces.append(piece)
            slabs.append(jnp.concatenate(pieces, axis=-1))
        return jnp.concatenate(slabs, axis=0)

    def update(s, r0, k0, state):
        m_old, acc_old = state
        s = with_bias(s, r0, k0)
        vb = v_t[:, k0:k0 + s.shape[0]]
        blk_max = jnp.max(s, axis=0, keepdims=True)
        if m_old is None:
            return blk_max, _dot(vb, jnp.exp2(s - blk_max).astype(BF16))
        m_new = jnp.maximum(m_old, blk_max)
        alpha = jnp.exp2(m_old - m_new)
        return m_new, alpha * acc_old + _dot(vb, jnp.exp2(s - m_new).astype(BF16))

    def lanes_of(state, lo, hi):
        return tuple(None if a is None else a[:, lo:hi] for a in state)

    for c in range(SEQ // t):
        r0 = c * t
        q_t = q_ref[r0:r0 + t, :].T
        qm = (jnp.where(dim < DIFF_HEAD_DIM, q_t, zero), jnp.where(dim >= DIFF_HEAD_DIM, q_t, zero))

        def diag_scores(qm=qm, r0=r0):
            return scores(qm, r0, half, 0, half), scores(qm, r0, t, half, t)

        state = [(None, None), (None, None)]
        s_next = scores(qm, 0, t, 0, t) if c > 0 else diag_scores()
        for j in range(c):
            s_cur = s_next
            s_next = scores(qm, (j + 1) * t, t, 0, t) if j + 1 < c else diag_scores()
            state = [update(s_cur[mi], r0, j * t, state[mi]) for mi in range(2)]
        s_lo, s_hi = s_next
        outs = []
        for mi in range(2):
            lo = update(s_lo[mi], r0, r0, lanes_of(state[mi], 0, half))
            hi = update(s_hi[mi], r0 + half, r0, lanes_of(state[mi], half, t))
            acc = jnp.concatenate([lo[1], hi[1]], axis=-1)
            outs.append(acc[:DIFF_V_DIM] / acc[DIFF_V_DIM:DIFF_V_DIM + 1])
        o = outs[0] - lam * outs[1]
        ms = jnp.mean(o * o, axis=0, keepdims=True)
        y = o * lax.rsqrt(ms + LN_EPS) * g_ref[...] * (1.0 - LAMBDA_INIT)
        o_ref[r0:r0 + t, :] = y.T.astype(BF16)


def _diff_attn(q, k, v, bias, lv, g):
    vec = lambda b, h: (0, 0)
    seq = pl.BlockSpec((SEQ, DIFF_V_DIM), lambda b, h: (b, h))
    return pl.pallas_call(
        _diff_attn_kernel,
        grid=(BATCH, N_DIFF_HEADS),
        in_specs=[seq, seq, seq,
                  pl.BlockSpec((1, 2, LANES, LANES), lambda b, h: (h, 0, 0, 0)),
                  pl.BlockSpec((4, DIFF_HEAD_DIM), vec), pl.BlockSpec((DIFF_V_DIM, 1), vec)],
        out_specs=seq,
        out_shape=jax.ShapeDtypeStruct((N_TOK, DIFF_WIDTH), BF16),
        compiler_params=_params("parallel", "parallel"),
        name="diff_attn",
    )(q, k, v, bias, lv, g)


def _kv_kernel(m_ref, w32_ref, o_ref, w_ref):
    _cast_once(w32_ref, w_ref)
    o_ref[...] = _dot(m_ref[...].astype(BF16), w_ref[...]).astype(BF16)


def _kv(mem2, wkv):
    tm = KV_ROWS
    return pl.pallas_call(
        _kv_kernel,
        grid=(BATCH * MEM_LEN // tm,),
        in_specs=[pl.BlockSpec((tm, D_MODEL), lambda i: (i, 0)), _resident((D_MODEL, 2 * D_MODEL))],
        out_specs=pl.BlockSpec((tm, 2 * D_MODEL), lambda i: (i, 0)),
        out_shape=jax.ShapeDtypeStruct((BATCH * MEM_LEN, 2 * D_MODEL), BF16),
        scratch_shapes=[pltpu.VMEM((D_MODEL, 2 * D_MODEL), BF16)],
        compiler_params=_params("arbitrary"),
        name="kv",
    )(mem2, wkv)


def _mix_ca_kernel(h0_ref, ys_ref, yd_ref, ln_ref, wout32_ref, wq32_ref, kv_ref, wo32_ref, wgu32_ref, wd32_ref,
                   o_ref, wgu_ref, wd_ref, wout_ref, wq_ref, wo_ref):
    wgu_ref[...] = wgu32_ref[...].astype(BF16)
    wd_ref[...] = wd32_ref[...].astype(BF16)
    _cast_once(wout32_ref, wout_ref)
    _cast_once(wq32_ref, wq_ref)
    _cast_once(wo32_ref, wo_ref)
    sub = MIX_ROWS // MIX_SUBTILES
    nsl = MIX_SLICES
    cw = D_MODEL // nsl
    rw = sub // nsl

    def work(st):
        r0 = st * sub
        ys = ys_ref[r0:r0 + sub, :]
        yd = yd_ref[r0:r0 + sub, :]
        mix = []
        for j in range(nsl):
            cs = slice(j * cw, (j + 1) * cw)
            mix.append(_dot(ys, wout_ref[0:S5_WIDTH, cs]) + _dot(yd, wout_ref[S5_WIDTH:, cs]))
            yield
        mix = jnp.concatenate(mix, axis=-1)
        h1 = []
        for j in range(nsl):
            rs = slice(j * rw, (j + 1) * rw)
            h0 = h0_ref[r0 + j * rw:r0 + (j + 1) * rw, :]
            h1.append(_layer_norm(DEEPNORM_ALPHA * h0 + mix[rs], ln_ref[2:3, :], ln_ref[3:4, :]))
            yield
        h1 = jnp.concatenate(h1, axis=0)
        h1b = h1.astype(BF16)
        qs = []
        for hd in range(CA_HEADS):
            cs = slice(hd * CA_HEAD_DIM, (hd + 1) * CA_HEAD_DIM)
            qs.append((_dot(h1b, wq_ref[:, cs]) * (CA_HEAD_DIM ** -0.5)).astype(BF16))
            yield
        heads = []
        for hd in range(CA_HEADS):
            kh = kv_ref[:, hd * CA_HEAD_DIM:(hd + 1) * CA_HEAD_DIM]
            vh = kv_ref[:, D_MODEL + hd * CA_HEAD_DIM:D_MODEL + (hd + 1) * CA_HEAD_DIM]
            s = lax.dot_general(qs[hd], kh, _NT, preferred_element_type=F32)
            e = jnp.exp(s - jnp.max(s, axis=-1, keepdims=True))
            oh = _dot(e.astype(BF16), vh) / jnp.sum(e, axis=-1, keepdims=True)
            heads.append(oh.astype(BF16))
            yield
        o = jnp.concatenate(heads, axis=-1)
        ca = []
        for j in range(nsl):
            ca.append(_dot(o, wo_ref[:, j * cw:(j + 1) * cw]))
            yield
        ca = jnp.concatenate(ca, axis=-1)
        for j in range(nsl):
            rs = slice(j * rw, (j + 1) * rw)
            o_ref[r0 + j * rw:r0 + (j + 1) * rw, :] = _layer_norm(
                DEEPNORM_ALPHA * h1[rs] + ca[rs], ln_ref[4:5, :], ln_ref[5:6, :])
            yield

    _interleave([work(st) for st in range(MIX_SUBTILES)], lag=nsl)


def _mix_ca(h0, ys, yd, ln, wout, wq, kv, wo, wgu, wd):
    tm = MIX_ROWS
    steps = N_TOK // tm
    per_batch = SEQ // tm
    row = lambda i: (i, 0)
    const = lambda i: (0, 0)
    sq = _resident((D_MODEL, D_MODEL))
    gu_slab = pl.BlockSpec((D_MODEL // steps, 2 * FFN_HIDDEN), row)
    d_slab = pl.BlockSpec((FFN_HIDDEN // steps, D_MODEL), row)
    return pl.pallas_call(
        _mix_ca_kernel,
        grid=(steps,),
        in_specs=[pl.BlockSpec((tm, D_MODEL), row), pl.BlockSpec((tm, S5_WIDTH), row),
                  pl.BlockSpec((tm, DIFF_WIDTH), row),
                  pl.BlockSpec((N_LN_ROWS, D_MODEL), const), sq, sq,
                  pl.BlockSpec((MEM_LEN, 2 * D_MODEL), lambda i: (i // per_batch, 0)),
                  sq, gu_slab, d_slab],
        out_specs=[pl.BlockSpec((tm, D_MODEL), row), gu_slab, d_slab],
        out_shape=(jax.ShapeDtypeStruct((N_TOK, D_MODEL), F32),
                   jax.ShapeDtypeStruct((D_MODEL, 2 * FFN_HIDDEN), BF16),
                   jax.ShapeDtypeStruct((FFN_HIDDEN, D_MODEL), BF16)),
        scratch_shapes=[pltpu.VMEM((D_MODEL, D_MODEL), BF16)] * 3,
        compiler_params=_params("arbitrary"),
        name="mix_ca",
    )(h0, ys, yd, ln, wout, wq, kv, wo, wgu, wd)


def _ffn_kernel(h_ref, wgu_ref, wd_ref, ln_ref, o_ref):
    sub = FFN_ROWS // FFN_SUBTILES
    ck = FFN_CHUNK
    nsl = 4
    rw = sub // nsl

    def work(st):
        r0 = st * sub
        hb = h_ref[r0:r0 + sub, :].astype(BF16)
        acc = None
        for c in range(FFN_HIDDEN // ck):
            gate = _dot(hb, wgu_ref[:, c * ck:(c + 1) * ck])
            up = _dot(hb, wgu_ref[:, FFN_HIDDEN + c * ck:FFN_HIDDEN + (c + 1) * ck])
            act = (jax.nn.silu(gate) * up).astype(BF16)
            part = _dot(act, wd_ref[c * ck:(c + 1) * ck, :])
            acc = part if acc is None else acc + part
            yield
        for j in range(nsl):
            rs = slice(r0 + j * rw, r0 + (j + 1) * rw)
            o_ref[rs, :] = _layer_norm(DEEPNORM_ALPHA * h_ref[rs, :] + acc[j * rw:(j + 1) * rw],
                                       ln_ref[6:7, :], ln_ref[7:8, :])
            yield

    _interleave([work(st) for st in range(FFN_SUBTILES)], lag=FFN_HIDDEN // ck // 2 + 1)


def _ffn(h2, wgu, wd, ln):
    tm = FFN_ROWS
    row = lambda i: (i, 0)
    const = lambda i: (0, 0)
    return pl.pallas_call(
        _ffn_kernel,
        grid=(N_TOK // tm,),
        in_specs=[pl.BlockSpec((tm, D_MODEL), row),
                  _resident((D_MODEL, 2 * FFN_HIDDEN)), _resident((FFN_HIDDEN, D_MODEL)),
                  pl.BlockSpec((N_LN_ROWS, D_MODEL), const)],
        out_specs=pl.BlockSpec((tm, D_MODEL), row),
        out_shape=jax.ShapeDtypeStruct((N_TOK, D_MODEL), F32),
        compiler_params=_params("parallel"),
        name="ffn",
    )(h2, wgu, wd, ln)


def kernel(x, mem, ln_in_g, ln_in_b, w_in, s5_lambda_re, s5_lambda_im, s5_log_dt, s5_b_re, s5_b_im,
           s5_c_re, s5_c_im, s5_d, s5_glu_w, s5_glu_b, diff_lq1, diff_lk1, diff_lq2, diff_lk2,
           diff_subln_g, rel_bias, w_out, ln1_g, ln1_b, ca_wq, ca_wkv, ca_wo, ln2_g, ln2_b,
           ffn_w_gate_up, ffn_w_down, ln3_g, ln3_b):
    x2 = x.reshape(N_TOK, D_MODEL)
    ln = jnp.stack([ln_in_g, ln_in_b, ln1_g[0], ln1_b[0], ln2_g[0], ln2_b[0], ln3_g[0], ln3_b[0]])
    lv = jnp.stack([diff_lq1[0], diff_lk1[0], diff_lq2[0], diff_lk2[0]])
    dg = jnp.stack([s5_d[0].reshape(S5_WIDTH), s5_glu_b[0]])

    h0, u, q, k, v = _proj(x2, ln, w_in[0])

    a8, w_s5 = _s5_prep(s5_lambda_re[0], s5_lambda_im[0], s5_log_dt[0], s5_b_re[0], s5_b_im[0],
                        s5_c_re[0], s5_c_im[0])
    y_s5 = _s5(u.reshape(BATCH, SEQ, S5_WIDTH), a8, w_s5, dg, s5_glu_w[0]).reshape(N_TOK, S5_WIDTH)

    y_diff = _diff_attn(q, k, v, _bias_tiles(rel_bias), lv, diff_subln_g.reshape(DIFF_V_DIM, 1))

    kv = _kv(mem.reshape(BATCH * MEM_LEN, D_MODEL), ca_wkv[0])
    h2, wgu, wd = _mix_ca(h0, y_s5, y_diff, ln, w_out[0], ca_wq[0], kv, ca_wo[0], ffn_w_gate_up[0], ffn_w_down[0])
    out = _ffn(h2, wgu, wd, ln)
    return out.reshape(BATCH, SEQ, D_MODEL)
```

```python
import math

import numpy as np

import jax
import jax.numpy as jnp
from jax import lax
from jax.experimental import pallas as pl
from jax.experimental.pallas import tpu as pltpu

F32 = jnp.float32
BF16 = jnp.bfloat16

D_MODEL = 1024
BATCH = 8
SEQ = 2048
N_TOK = BATCH * SEQ
MEM_LEN = 256
S5_WIDTH = 512
S5_GROUP = 16
S5_GROUPS = 32
S5_STATE = 64
S5_COLS = S5_GROUPS * S5_STATE
DIFF_WIDTH = 512
DIFF_HEAD_DIM = 64
DIFF_V_DIM = 128
N_DIFF_HEADS = 4
DIFF_QK_WIDTH = 2 * N_DIFF_HEADS * DIFF_HEAD_DIM
MIX_IN = S5_WIDTH + 2 * DIFF_QK_WIDTH + DIFF_WIDTH
COL_Q = S5_WIDTH
COL_K = COL_Q + DIFF_QK_WIDTH
COL_V = COL_K + DIFF_QK_WIDTH
NUM_BUCKETS = 32
MAX_DISTANCE = 128
CA_HEADS = 4
CA_HEAD_DIM = 256
FFN_HIDDEN = 2816
DEEPNORM_ALPHA = 2.0 ** 0.25
LN_EPS = 1e-5
LAMBDA_INIT = 0.8 - 0.6 * math.exp(0.0)
LOG2E = math.log2(math.e)

VMEM_LIMIT_BYTES = 56 * 1024 * 1024
MXU_DIM = 256
LANES = 128

PROJ_ROWS = 1024
PROJ_SUBTILES = 2
S5_SUB_STEPS = 64
S5_SUBTILES = 2
S5_SCAN_CHUNK = 512
ATT_TILE = 512
ATT_ONES_ROWS = 16
ATT_HEADS = 4
FFN_CHUNK = 256
FFN_ROWS = 1024
FFN_SUBTILES = 2
FFN_STAGE_COLS = 512
FFN_STAGE_ROWS = 256
FFN_STAGE_SLOTS = 4
MIX_ROWS = 1024
MIX_SUBTILES = 2
MIX_SLICES = 4
N_LN_ROWS = 8
KV_ROWS = 1024

_NT = (((1,), (1,)), ((), ()))


def _params(*sem):
    return pltpu.CompilerParams(dimension_semantics=sem, vmem_limit_bytes=VMEM_LIMIT_BYTES)


def _layer_norm(x, g, b):
    mu = jnp.mean(x, axis=-1, keepdims=True)
    xc = x - mu
    var = jnp.mean(xc * xc, axis=-1, keepdims=True)
    return xc * lax.rsqrt(var + LN_EPS) * g + b


def _dot(a, b):
    return jnp.dot(a, b, preferred_element_type=F32)


def _cast_once(w_ref, wbf_ref):
    @pl.when(pl.program_id(0) == 0)
    def _():
        wbf_ref[...] = w_ref[...].astype(BF16)


def _resident(shape):
    return pl.BlockSpec(shape, lambda *_: (0,) * len(shape), pipeline_mode=pl.Buffered(1))


def _interleave(gens, lag):
    gens = list(gens)
    live = [True] * len(gens)
    tick = 0
    while any(live):
        for i, g in enumerate(gens):
            if live[i] and tick >= lag * i:
                try:
                    next(g)
                except StopIteration:
                    live[i] = False
        tick += 1


def _s5_prep_kernel(lam_ref, bc_ref, a_ref, w_ref):
    lr = lam_ref[0:1, :]
    li = lam_ref[1:2, :]
    dt = jnp.exp(lam_ref[2:3, :])
    mag = jnp.exp(lr * dt)
    ang = li * dt
    ar = mag * jnp.cos(ang)
    ai = mag * jnp.sin(ang)
    den = lr * lr + li * li
    nr = ar - 1.0
    fr = (nr * lr + ai * li) / den
    fi = (ai * lr - nr * li) / den
    a_ref[0] = jnp.broadcast_to(ar, (BATCH, S5_COLS))
    a_ref[1] = jnp.broadcast_to(ai, (BATCH, S5_COLS))
    b_r = bc_ref[0]
    b_i = bc_ref[1]
    mats = (fr * b_r - fi * b_i, fr * b_i + fi * b_r, bc_ref[2], bc_ref[3])
    half = S5_COLS // 2
    lane_group = lax.broadcasted_iota(jnp.int32, (S5_GROUP, half), 1) // S5_STATE
    for k, mat in enumerate(mats):
        for hf in range(2):
            src = mat[:, hf * half:(hf + 1) * half]
            for gl in range(MXU_DIM // S5_GROUP):
                w_ref[k, hf, gl * S5_GROUP:(gl + 1) * S5_GROUP, :] = jnp.where(
                    lane_group == gl, src, 0.0).astype(BF16)


def _s5_prep(lam_re, lam_im, log_dt, b_re, b_im, c_re, c_im):
    lam = jnp.stack([lam_re.reshape(S5_COLS), lam_im.reshape(S5_COLS), jnp.repeat(log_dt, S5_STATE)])
    bc = jnp.stack([jnp.transpose(b_re, (2, 0, 1)), jnp.transpose(b_im, (2, 0, 1)),
                    jnp.transpose(c_re, (1, 0, 2)), jnp.transpose(c_im, (1, 0, 2))]).reshape(4, S5_GROUP, S5_COLS)
    return pl.pallas_call(
        _s5_prep_kernel,
        out_shape=(jax.ShapeDtypeStruct((2, BATCH, S5_COLS), F32),
                   jax.ShapeDtypeStruct((4, 2, MXU_DIM, S5_COLS // 2), BF16)),
        name="s5_prep",
    )(lam, bc)


def _bias_prep_kernel(rel_ref, bucket_ref, o_ref):
    for j in range(2):
        bucket = bucket_ref[j]
        for h in range(N_DIFF_HEADS):
            far = rel_ref[NUM_BUCKETS - 1, h]
            acc = jnp.zeros(bucket.shape, F32)
            for b in range(NUM_BUCKETS):
                acc = jnp.where(bucket == b, (rel_ref[b, h] - far) * LOG2E, acc)
            o_ref[h, j] = jnp.where(bucket < 0, -jnp.inf, acc)


def _t5_bucket(dist):
    max_exact = NUM_BUCKETS // 2
    df = np.maximum(dist, 1).astype(np.float32)
    large = max_exact + (np.log(df / np.float32(max_exact)) / np.float32(math.log(MAX_DISTANCE / max_exact))
                         * np.float32(NUM_BUCKETS - max_exact)).astype(np.int32)
    large = np.minimum(large, NUM_BUCKETS - 1)
    return np.where(dist < max_exact, dist, large).astype(np.int32)


def _bias_tiles(rel_bias):
    t = LANES
    qpos = np.arange(t, dtype=np.int32)[None, :]
    kpos = np.arange(t, dtype=np.int32)[:, None]
    d_diag = qpos - kpos
    d_prev = d_diag + t
    bucket = jnp.asarray(np.stack([np.where(d_diag >= 0, _t5_bucket(np.maximum(d_diag, 0)), -1),
                                   _t5_bucket(d_prev)]).astype(np.int32))
    return pl.pallas_call(
        _bias_prep_kernel,
        in_specs=[pl.BlockSpec(memory_space=pltpu.SMEM), pl.BlockSpec(memory_space=pltpu.VMEM)],
        out_specs=pl.BlockSpec(memory_space=pltpu.VMEM),
        out_shape=jax.ShapeDtypeStruct((N_DIFF_HEADS, 2, t, t), F32),
        name="bias_prep",
    )(rel_bias, bucket)


def _proj_kernel(x_ref, ln_ref, w32_ref, h_ref, u_ref, q_ref, k_ref, v_ref, w_ref):
    _cast_once(w32_ref, w_ref)
    sub = PROJ_ROWS // PROJ_SUBTILES
    nsl = 4
    rw = sub // nsl

    def work(st):
        r0 = st * sub
        h = []
        for j in range(nsl):
            rows = slice(r0 + j * rw, r0 + (j + 1) * rw)
            hj = _layer_norm(x_ref[rows, :], ln_ref[0:1, :], ln_ref[1:2, :])
            h_ref[rows, :] = hj
            h.append(hj.astype(BF16))
            yield
        h = jnp.concatenate(h, axis=0)
        rs = slice(r0, r0 + sub)
        u_ref[rs, :] = _dot(h, w_ref[:, 0:COL_Q])
        yield
        q_ref[rs, :] = (_dot(h, w_ref[:, COL_Q:COL_K]) * (DIFF_HEAD_DIM ** -0.5 * LOG2E)).astype(BF16)
        yield
        k_ref[rs, :] = _dot(h, w_ref[:, COL_K:COL_V]).astype(BF16)
        yield
        v_ref[rs, :] = _dot(h, w_ref[:, COL_V:MIX_IN]).astype(BF16)
        yield

    _interleave([work(st) for st in range(PROJ_SUBTILES)], lag=nsl)


def _proj(x2, ln, w):
    tm = PROJ_ROWS
    row = lambda i: (i, 0)
    const = lambda i: (0, 0)
    return pl.pallas_call(
        _proj_kernel,
        grid=(N_TOK // tm,),
        in_specs=[pl.BlockSpec((tm, D_MODEL), row), pl.BlockSpec((N_LN_ROWS, D_MODEL), const),
                  _resident((D_MODEL, MIX_IN))],
        out_specs=([pl.BlockSpec((tm, D_MODEL), row), pl.BlockSpec((tm, S5_WIDTH), row)]
                   + [pl.BlockSpec((tm, DIFF_QK_WIDTH), row)] * 2 + [pl.BlockSpec((tm, DIFF_WIDTH), row)]),
        out_shape=(jax.ShapeDtypeStruct((N_TOK, D_MODEL), F32), jax.ShapeDtypeStruct((N_TOK, S5_WIDTH), F32),
                   jax.ShapeDtypeStruct((N_TOK, DIFF_QK_WIDTH), BF16), jax.ShapeDtypeStruct((N_TOK, DIFF_QK_WIDTH), BF16),
                   jax.ShapeDtypeStruct((N_TOK, DIFF_WIDTH), BF16)),
        scratch_shapes=[pltpu.VMEM((D_MODEL, MIX_IN), BF16)],
        compiler_params=_params("arbitrary"),
        name="proj",
    )(x2, ln, w)


def _s5_kernel(u_ref, a_ref, w_ref, dg_ref, gw32_ref, y_ref, sre, sim, xre, xim, utm, ytm, gw_ref):
    @pl.when(pl.program_id(0) == 0)
    def _():
        xre[...] = jnp.zeros_like(xre)
        xim[...] = jnp.zeros_like(xim)

    _cast_once(gw32_ref, gw_ref)
    half = S5_COLS // 2
    cw = S5_SCAN_CHUNK
    nchunk = S5_COLS // cw
    rows = S5_SUB_STEPS * BATCH
    steps = S5_SUBTILES * S5_SUB_STEPS
    nslab = S5_WIDTH // LANES
    carry = [None] * nchunk

    for b in range(BATCH):
        for sl in range(nslab):
            utm[sl, pl.ds(b, steps, stride=BATCH), :] = u_ref[b, :, sl * LANES:(sl + 1) * LANES]

    def work(st):
        r0 = st * rows
        u = jnp.concatenate([utm[sl, r0:r0 + rows, :] for sl in range(nslab)], axis=-1)
        ub = u.astype(BF16)
        for hf in range(2):
            uh = ub[:, hf * MXU_DIM:(hf + 1) * MXU_DIM]
            sre[r0:r0 + rows, hf * half:(hf + 1) * half] = _dot(uh, w_ref[0, hf])
            yield
            sim[r0:r0 + rows, hf * half:(hf + 1) * half] = _dot(uh, w_ref[1, hf])
            yield
        for c in range(nchunk):
            cs = slice(c * cw, (c + 1) * cw)
            ar = a_ref[0, :, cs]
            ai = a_ref[1, :, cs]
            xr, xi = (xre[:, cs], xim[:, cs]) if st == 0 else carry[c]
            for t in range(S5_SUB_STEPS):
                r = r0 + t * BATCH
                nr = ar * xr - ai * xi + sre[r:r + BATCH, cs]
                ni = ar * xi + ai * xr + sim[r:r + BATCH, cs]
                sre[r:r + BATCH, cs] = nr
                sim[r:r + BATCH, cs] = ni
                xr, xi = nr, ni
            carry[c] = (xr, xi)
            if st == S5_SUBTILES - 1:
                xre[:, cs] = xr
                xim[:, cs] = xi
            yield
        ys = []
        for hf in range(2):
            hs = slice(hf * half, (hf + 1) * half)
            ys.append(lax.dot_general(sre[r0:r0 + rows, hs].astype(BF16), w_ref[2, hf], _NT,
                                      preferred_element_type=F32)
                      - lax.dot_general(sim[r0:r0 + rows, hs].astype(BF16), w_ref[3, hf], _NT,
                                        preferred_element_type=F32))
            yield
        y = jax.nn.gelu(jnp.concatenate(ys, axis=-1) + dg_ref[0:1, :] * u)
        yield
        z = _dot(y.astype(BF16), gw_ref[...]) + dg_ref[1:2, :]
        y = y * jax.nn.sigmoid(z)
        for sl in range(nslab):
            ytm[sl, r0:r0 + rows, :] = y[:, sl * LANES:(sl + 1) * LANES]
        t0 = st * S5_SUB_STEPS
        for b in range(BATCH):
            y_ref[b, t0:t0 + S5_SUB_STEPS, :] = jnp.concatenate(
                [ytm[sl, pl.ds(r0 + b, S5_SUB_STEPS, stride=BATCH), :] for sl in range(nslab)],
                axis=-1).astype(BF16)
        yield

    _interleave([work(st) for st in range(S5_SUBTILES)], lag=4)


def _s5(u, a8, w, dg, gw):
    steps = S5_SUBTILES * S5_SUB_STEPS
    rows = steps * BATCH
    nslab = S5_WIDTH // LANES
    blk = pl.BlockSpec((BATCH, steps, S5_WIDTH), lambda i: (0, i, 0))
    c2 = lambda i: (0, 0)
    return pl.pallas_call(
        _s5_kernel,
        grid=(SEQ // steps,),
        in_specs=[blk,
                  pl.BlockSpec((2, BATCH, S5_COLS), lambda i: (0, 0, 0)),
                  pl.BlockSpec((4, 2, MXU_DIM, S5_COLS // 2), lambda i: (0, 0, 0, 0)),
                  pl.BlockSpec((2, S5_WIDTH), c2), pl.BlockSpec((S5_WIDTH, S5_WIDTH), c2)],
        out_specs=blk,
        out_shape=jax.ShapeDtypeStruct((BATCH, SEQ, S5_WIDTH), BF16),
        scratch_shapes=[pltpu.VMEM((rows, S5_COLS), F32), pltpu.VMEM((rows, S5_COLS), F32),
                        pltpu.VMEM((BATCH, S5_COLS), F32), pltpu.VMEM((BATCH, S5_COLS), F32),
                        pltpu.VMEM((nslab, rows, LANES), F32), pltpu.VMEM((nslab, rows, LANES), F32),
                        pltpu.VMEM((S5_WIDTH, S5_WIDTH), BF16)],
        compiler_params=_params("arbitrary"),
        name="s5",
    )(u, a8, w, dg, gw)


def _diff_attn_head(q_ref, k_ref, v_ref, bias_ref, lam, g_ref, o_ref):
    t = ATT_TILE
    half = t // 2
    sb = LANES
    v_t = jnp.concatenate([v_ref[...].T, jnp.ones((ATT_ONES_ROWS, SEQ), BF16)], axis=0)
    dim = lax.broadcasted_iota(jnp.int32, (DIFF_V_DIM, t), 0)
    zero = jnp.zeros((DIFF_V_DIM, t), BF16)

    def scores(qm, k0, width, lo, hi):
        kb = k_ref[k0:k0 + width, :]
        return [_dot(kb, qm[mi][:, lo:hi]) for mi in range(2)]

    def with_bias(s, r0, k0):
        width, rows = s.shape
        slabs = []
        for i in range(width // sb):
            kb = (k0 + i * sb) // sb
            slab = s[i * sb:(i + 1) * sb, :]
            qbs = [(r0 + j * sb) // sb for j in range(rows // sb)]
            if all(kb < qb - 1 for qb in qbs):
                slabs.append(slab)
                continue
            pieces = []
            for j, qb in enumerate(qbs):
                piece = slab[:, j * sb:(j + 1) * sb]
                if kb == qb:
                    piece = piece + bias_ref[0, 0]
                elif kb == qb - 1:
                    piece = piece + bias_ref[0, 1]
                elif kb > qb:
                    piece = jnp.full((sb, sb), -jnp.inf, F32)
                pieces.append(piece)
            slabs.append(jnp.concatenate(pieces, axis=-1))
        return jnp.concatenate(slabs, axis=0)

    def update(s, r0, k0, state):
        m_old, acc_old = state
        s = with_bias(s, r0, k0)
        vb = v_t[:, k0:k0 + s.shape[0]]
        blk_max = jnp.max(s, axis=0, keepdims=True)
        if m_old is None:
            return blk_max, _dot(vb, jnp.exp2(s - blk_max).astype(BF16))
        m_new = jnp.maximum(m_old, blk_max)
        alpha = jnp.exp2(m_old - m_new)
        return m_new, alpha * acc_old + _dot(vb, jnp.exp2(s - m_new).astype(BF16))

    def lanes_of(state, lo, hi):
        return tuple(None if a is None else a[:, lo:hi] for a in state)

    for c in range(SEQ // t):
        r0 = c * t
        q_t = q_ref[r0:r0 + t, :].T
        qm = (jnp.where(dim < DIFF_HEAD_DIM, q_t, zero), jnp.where(dim >= DIFF_HEAD_DIM, q_t, zero))

        def diag_scores(qm=qm, r0=r0):
            return scores(qm, r0, half, 0, half), scores(qm, r0, t, half, t)

        state = [(None, None), (None, None)]
        s_next = scores(qm, 0, t, 0, t) if c > 0 else diag_scores()
        for j in range(c):
            s_cur = s_next
            s_next = scores(qm, (j + 1) * t, t, 0, t) if j + 1 < c else diag_scores()
            state = [update(s_cur[mi], r0, j * t, state[mi]) for mi in range(2)]
        s_lo, s_hi = s_next
        outs = []
        for mi in range(2):
            lo = update(s_lo[mi], r0, r0, lanes_of(state[mi], 0, half))
            hi = update(s_hi[mi], r0 + half, r0, lanes_of(state[mi], half, t))
            acc = jnp.concatenate([lo[1], hi[1]], axis=-1)
            outs.append(acc[:DIFF_V_DIM] / acc[DIFF_V_DIM:DIFF_V_DIM + 1])
        o = outs[0] - lam * outs[1]
        ms = jnp.mean(o * o, axis=0, keepdims=True)
        y = o * lax.rsqrt(ms + LN_EPS) * g_ref[...] * (1.0 - LAMBDA_INIT)
        o_ref[r0:r0 + t, :] = y.T.astype(BF16)


def _diff_attn_kernel(q_ref, k_ref, v_ref, bias_ref, lv_ref, g_ref, o_ref):
    lam = (jnp.exp(jnp.sum(lv_ref[0:1, :] * lv_ref[1:2, :], axis=-1, keepdims=True))
           - jnp.exp(jnp.sum(lv_ref[2:3, :] * lv_ref[3:4, :], axis=-1, keepdims=True)) + LAMBDA_INIT)
    for hh in range(ATT_HEADS):
        hs = pl.ds(hh * DIFF_V_DIM, DIFF_V_DIM)
        _diff_attn_head(q_ref.at[:, hs], k_ref.at[:, hs], v_ref.at[:, hs], bias_ref.at[pl.ds(hh, 1)], lam,
                        g_ref, o_ref.at[:, hs])


def _diff_attn(q, k, v, bias, lv, g):
    vec = lambda b, h: (0, 0)
    seq = pl.BlockSpec((SEQ, ATT_HEADS * DIFF_V_DIM), lambda b, h: (b, h))
    return pl.pallas_call(
        _diff_attn_kernel,
        grid=(BATCH, N_DIFF_HEADS // ATT_HEADS),
        in_specs=[seq, seq, seq,
                  pl.BlockSpec((ATT_HEADS, 2, LANES, LANES), lambda b, h: (h, 0, 0, 0)),
                  pl.BlockSpec((4, DIFF_HEAD_DIM), vec), pl.BlockSpec((DIFF_V_DIM, 1), vec)],
        out_specs=seq,
        out_shape=jax.ShapeDtypeStruct((N_TOK, DIFF_WIDTH), BF16),
        compiler_params=_params("parallel", "parallel"),
        name="diff_attn",
    )(q, k, v, bias, lv, g)


def _kv_kernel(m_ref, w32_ref, o_ref, w_ref):
    _cast_once(w32_ref, w_ref)
    o_ref[...] = _dot(m_ref[...].astype(BF16), w_ref[...]).astype(BF16)


def _kv(mem2, wkv):
    tm = KV_ROWS
    return pl.pallas_call(
        _kv_kernel,
        grid=(BATCH * MEM_LEN // tm,),
        in_specs=[pl.BlockSpec((tm, D_MODEL), lambda i: (i, 0)), _resident((D_MODEL, 2 * D_MODEL))],
        out_specs=pl.BlockSpec((tm, 2 * D_MODEL), lambda i: (i, 0)),
        out_shape=jax.ShapeDtypeStruct((BATCH * MEM_LEN, 2 * D_MODEL), BF16),
        scratch_shapes=[pltpu.VMEM((D_MODEL, 2 * D_MODEL), BF16)],
        compiler_params=_params("arbitrary"),
        name="kv",
    )(mem2, wkv)


def _mix_ca_kernel(h0_ref, ys_ref, yd_ref, ln_ref, wout32_ref, wq32_ref, kv_ref, wo32_ref, o_ref,
                   wout_ref, wq_ref, wo_ref):
    _cast_once(wout32_ref, wout_ref)
    _cast_once(wq32_ref, wq_ref)
    _cast_once(wo32_ref, wo_ref)
    sub = MIX_ROWS // MIX_SUBTILES
    nsl = MIX_SLICES
    cw = D_MODEL // nsl
    rw = sub // nsl

    def work(st):
        r0 = st * sub
        ys = ys_ref[r0:r0 + sub, :]
        yd = yd_ref[r0:r0 + sub, :]
        mix = []
        for j in range(nsl):
            cs = slice(j * cw, (j + 1) * cw)
            mix.append(_dot(ys, wout_ref[0:S5_WIDTH, cs]) + _dot(yd, wout_ref[S5_WIDTH:, cs]))
            yield
        mix = jnp.concatenate(mix, axis=-1)
        h1 = []
        for j in range(nsl):
            rs = slice(j * rw, (j + 1) * rw)
            h0 = h0_ref[r0 + j * rw:r0 + (j + 1) * rw, :]
            h1.append(_layer_norm(DEEPNORM_ALPHA * h0 + mix[rs], ln_ref[2:3, :], ln_ref[3:4, :]))
            yield
        h1 = jnp.concatenate(h1, axis=0)
        h1b = h1.astype(BF16)
        qs = []
        for hd in range(CA_HEADS):
            cs = slice(hd * CA_HEAD_DIM, (hd + 1) * CA_HEAD_DIM)
            qs.append((_dot(h1b, wq_ref[:, cs]) * (CA_HEAD_DIM ** -0.5)).astype(BF16))
            yield
        heads = []
        for hd in range(CA_HEADS):
            kh = kv_ref[:, hd * CA_HEAD_DIM:(hd + 1) * CA_HEAD_DIM]
            vh = kv_ref[:, D_MODEL + hd * CA_HEAD_DIM:D_MODEL + (hd + 1) * CA_HEAD_DIM]
            s = lax.dot_general(qs[hd], kh, _NT, preferred_element_type=F32)
            e = jnp.exp(s - jnp.max(s, axis=-1, keepdims=True))
            oh = _dot(e.astype(BF16), vh) / jnp.sum(e, axis=-1, keepdims=True)
            heads.append(oh.astype(BF16))
            yield
        o = jnp.concatenate(heads, axis=-1)
        ca = []
        for j in range(nsl):
            ca.append(_dot(o, wo_ref[:, j * cw:(j + 1) * cw]))
            yield
        ca = jnp.concatenate(ca, axis=-1)
        for j in range(nsl):
            rs = slice(j * rw, (j + 1) * rw)
            o_ref[r0 + j * rw:r0 + (j + 1) * rw, :] = _layer_norm(
                DEEPNORM_ALPHA * h1[rs] + ca[rs], ln_ref[4:5, :], ln_ref[5:6, :])
            yield

    _interleave([work(st) for st in range(MIX_SUBTILES)], lag=nsl)


def _mix_ca(h0, ys, yd, ln, wout, wq, kv, wo):
    tm = MIX_ROWS
    per_batch = SEQ // tm
    row = lambda i: (i, 0)
    const = lambda i: (0, 0)
    sq = _resident((D_MODEL, D_MODEL))
    return pl.pallas_call(
        _mix_ca_kernel,
        grid=(N_TOK // tm,),
        in_specs=[pl.BlockSpec((tm, D_MODEL), row), pl.BlockSpec((tm, S5_WIDTH), row),
                  pl.BlockSpec((tm, DIFF_WIDTH), row),
                  pl.BlockSpec((N_LN_ROWS, D_MODEL), const), sq, sq,
                  pl.BlockSpec((MEM_LEN, 2 * D_MODEL), lambda i: (i // per_batch, 0)),
                  sq],
        out_specs=pl.BlockSpec((tm, D_MODEL), row),
        out_shape=jax.ShapeDtypeStruct((N_TOK, D_MODEL), F32),
        scratch_shapes=[pltpu.VMEM((D_MODEL, D_MODEL), BF16)] * 3,
        compiler_params=_params("arbitrary"),
        name="mix_ca",
    )(h0, ys, yd, ln, wout, wq, kv, wo)


def _stream_cast(src_hbm, dst_ref, stage, sems, w, chunks):
    ns = FFN_STAGE_SLOTS

    def copy(i):
        return pltpu.make_async_copy(src_hbm.at[chunks[i][0]], stage.at[i % ns], sems.at[w, i % ns])

    for i in range(min(ns - 1, len(chunks))):
        copy(i).start()
    for i in range(len(chunks)):
        if i + ns - 1 < len(chunks):
            copy(i + ns - 1).start()
        copy(i).wait()
        dst_ref[chunks[i][1]] = stage[i % ns].astype(BF16)


def _ffn_kernel(h_ref, wgu_hbm, wd_hbm, ln_ref, o_ref, wgu_ref, wd_ref, gu_stage, d_stage, sem):
    @pl.when(pl.program_id(0) == 0)
    def _():
        cw = FFN_STAGE_COLS
        cols = [(slice(None), slice(c * cw, (c + 1) * cw)) for c in range(2 * FFN_HIDDEN // cw)]
        _stream_cast(wgu_hbm, wgu_ref, gu_stage, sem, 0, [(ix, ix) for ix in cols])
        rw_ = FFN_STAGE_ROWS
        rows_ = [(slice(r * rw_, (r + 1) * rw_), slice(None)) for r in range(FFN_HIDDEN // rw_)]
        _stream_cast(wd_hbm, wd_ref, d_stage, sem, 1, [(ix, ix) for ix in rows_])

    sub = FFN_ROWS // FFN_SUBTILES
    ck = FFN_CHUNK
    nsl = 4
    rw = sub // nsl

    def work(st):
        r0 = st * sub
        hb = h_ref[r0:r0 + sub, :].astype(BF16)
        acc = None
        for c in range(FFN_HIDDEN // ck):
            gate = _dot(hb, wgu_ref[:, c * ck:(c + 1) * ck])
            up = _dot(hb, wgu_ref[:, FFN_HIDDEN + c * ck:FFN_HIDDEN + (c + 1) * ck])
            act = (jax.nn.silu(gate) * up).astype(BF16)
            part = _dot(act, wd_ref[c * ck:(c + 1) * ck, :])
            acc = part if acc is None else acc + part
            yield
        for j in range(nsl):
            rs = slice(r0 + j * rw, r0 + (j + 1) * rw)
            o_ref[rs, :] = _layer_norm(DEEPNORM_ALPHA * h_ref[rs, :] + acc[j * rw:(j + 1) * rw],
                                       ln_ref[6:7, :], ln_ref[7:8, :])
            yield

    _interleave([work(st) for st in range(FFN_SUBTILES)], lag=FFN_HIDDEN // ck // 2 + 1)


def _ffn(h2, wgu, wd, ln):
    tm = FFN_ROWS
    row = lambda i: (i, 0)
    const = lambda i: (0, 0)
    hbm = pl.BlockSpec(memory_space=pl.ANY)
    return pl.pallas_call(
        _ffn_kernel,
        grid=(N_TOK // tm,),
        in_specs=[pl.BlockSpec((tm, D_MODEL), row), hbm, hbm,
                  pl.BlockSpec((N_LN_ROWS, D_MODEL), const)],
        out_specs=pl.BlockSpec((tm, D_MODEL), row),
        out_shape=jax.ShapeDtypeStruct((N_TOK, D_MODEL), F32),
        scratch_shapes=[pltpu.VMEM((D_MODEL, 2 * FFN_HIDDEN), BF16), pltpu.VMEM((FFN_HIDDEN, D_MODEL), BF16),
                        pltpu.VMEM((FFN_STAGE_SLOTS, D_MODEL, FFN_STAGE_COLS), F32),
                        pltpu.VMEM((FFN_STAGE_SLOTS, FFN_STAGE_ROWS, D_MODEL), F32),
                        pltpu.SemaphoreType.DMA((2, FFN_STAGE_SLOTS))],
        compiler_params=_params("arbitrary"),
        name="ffn",
    )(h2, wgu, wd, ln)


def kernel(x, mem, ln_in_g, ln_in_b, w_in, s5_lambda_re, s5_lambda_im, s5_log_dt, s5_b_re, s5_b_im,
           s5_c_re, s5_c_im, s5_d, s5_glu_w, s5_glu_b, diff_lq1, diff_lk1, diff_lq2, diff_lk2,
           diff_subln_g, rel_bias, w_out, ln1_g, ln1_b, ca_wq, ca_wkv, ca_wo, ln2_g, ln2_b,
           ffn_w_gate_up, ffn_w_down, ln3_g, ln3_b):
    x2 = x.reshape(N_TOK, D_MODEL)
    ln = jnp.stack([ln_in_g, ln_in_b, ln1_g[0], ln1_b[0], ln2_g[0], ln2_b[0], ln3_g[0], ln3_b[0]])
    lv = jnp.stack([diff_lq1[0], diff_lk1[0], diff_lq2[0], diff_lk2[0]])
    dg = jnp.stack([s5_d[0].reshape(S5_WIDTH), s5_glu_b[0]])

    h0, u, q, k, v = _proj(x2, ln, w_in[0])

    a8, w_s5 = _s5_prep(s5_lambda_re[0], s5_lambda_im[0], s5_log_dt[0], s5_b_re[0], s5_b_im[0],
                        s5_c_re[0], s5_c_im[0])
    y_s5 = _s5(u.reshape(BATCH, SEQ, S5_WIDTH), a8, w_s5, dg, s5_glu_w[0]).reshape(N_TOK, S5_WIDTH)

    y_diff = _diff_attn(q, k, v, _bias_tiles(rel_bias), lv, diff_subln_g.reshape(DIFF_V_DIM, 1))

    kv = _kv(mem.reshape(BATCH * MEM_LEN, D_MODEL), ca_wkv[0])
    h2 = _mix_ca(h0, y_s5, y_diff, ln, w_out[0], ca_wq[0], kv, ca_wo[0])
    out = _ffn(h2, ffn_w_gate_up[0], ffn_w_down[0], ln)
    return out.reshape(BATCH, SEQ, D_MODEL)
```

```python
import math

import numpy as np

import jax
import jax.numpy as jnp
from jax import lax
from jax.experimental import pallas as pl
from jax.experimental.pallas import tpu as pltpu

F32 = jnp.float32
BF16 = jnp.bfloat16

D_MODEL = 1024
BATCH = 8
SEQ = 2048
N_TOK = BATCH * SEQ
MEM_LEN = 256
S5_WIDTH = 512
S5_GROUP = 16
S5_GROUPS = 32
S5_STATE = 64
S5_COLS = S5_GROUPS * S5_STATE
DIFF_WIDTH = 512
DIFF_HEAD_DIM = 64
DIFF_V_DIM = 128
N_DIFF_HEADS = 4
DIFF_QK_WIDTH = 2 * N_DIFF_HEADS * DIFF_HEAD_DIM
MIX_IN = S5_WIDTH + 2 * DIFF_QK_WIDTH + DIFF_WIDTH
COL_Q = S5_WIDTH
COL_K = COL_Q + DIFF_QK_WIDTH
COL_V = COL_K + DIFF_QK_WIDTH
NUM_BUCKETS = 32
MAX_DISTANCE = 128
CA_HEADS = 4
CA_HEAD_DIM = 256
FFN_HIDDEN = 2816
DEEPNORM_ALPHA = 2.0 ** 0.25
LN_EPS = 1e-5
LAMBDA_INIT = 0.8 - 0.6 * math.exp(0.0)
LOG2E = math.log2(math.e)

VMEM_LIMIT_BYTES = 56 * 1024 * 1024
MXU_DIM = 256
LANES = 128

PROJ_ROWS = 1024
PROJ_SUBTILES = 2
S5_SUB_STEPS = 64
S5_SUBTILES = 2
S5_SCAN_CHUNK = 512
ATT_TILE = 512
ATT_ONES_ROWS = 16
ATT_HEADS = 4
FFN_CHUNK = 256
FFN_ROWS = 1024
FFN_SUBTILES = 2
MIX_ROWS = 1024
MIX_SUBTILES = 2
MIX_SLICES = 4
N_LN_ROWS = 8
KV_ROWS = 1024

_NT = (((1,), (1,)), ((), ()))


def _params(*sem):
    return pltpu.CompilerParams(dimension_semantics=sem, vmem_limit_bytes=VMEM_LIMIT_BYTES)


def _layer_norm(x, g, b):
    mu = jnp.mean(x, axis=-1, keepdims=True)
    xc = x - mu
    var = jnp.mean(xc * xc, axis=-1, keepdims=True)
    return xc * lax.rsqrt(var + LN_EPS) * g + b


def _dot(a, b):
    return jnp.dot(a, b, preferred_element_type=F32)


def _cast_once(w_ref, wbf_ref):
    @pl.when(pl.program_id(0) == 0)
    def _():
        wbf_ref[...] = w_ref[...].astype(BF16)


def _resident(shape):
    return pl.BlockSpec(shape, lambda *_: (0,) * len(shape), pipeline_mode=pl.Buffered(1))


def _interleave(gens, lag):
    gens = list(gens)
    live = [True] * len(gens)
    tick = 0
    while any(live):
        for i, g in enumerate(gens):
            if live[i] and tick >= lag * i:
                try:
                    next(g)
                except StopIteration:
                    live[i] = False
        tick += 1


def _s5_prep_kernel(lam_ref, bc_ref, a_ref, w_ref):
    lr = lam_ref[0:1, :]
    li = lam_ref[1:2, :]
    dt = jnp.exp(lam_ref[2:3, :])
    mag = jnp.exp(lr * dt)
    ang = li * dt
    ar = mag * jnp.cos(ang)
    ai = mag * jnp.sin(ang)
    den = lr * lr + li * li
    nr = ar - 1.0
    fr = (nr * lr + ai * li) / den
    fi = (ai * lr - nr * li) / den
    a_ref[0] = jnp.broadcast_to(ar, (BATCH, S5_COLS))
    a_ref[1] = jnp.broadcast_to(ai, (BATCH, S5_COLS))
    b_r = bc_ref[0]
    b_i = bc_ref[1]
    mats = (fr * b_r - fi * b_i, fr * b_i + fi * b_r, bc_ref[2], bc_ref[3])
    half = S5_COLS // 2
    lane_group = lax.broadcasted_iota(jnp.int32, (S5_GROUP, half), 1) // S5_STATE
    for k, mat in enumerate(mats):
        for hf in range(2):
            src = mat[:, hf * half:(hf + 1) * half]
            for gl in range(MXU_DIM // S5_GROUP):
                w_ref[k, hf, gl * S5_GROUP:(gl + 1) * S5_GROUP, :] = jnp.where(
                    lane_group == gl, src, 0.0).astype(BF16)


def _s5_prep(lam_re, lam_im, log_dt, b_re, b_im, c_re, c_im):
    lam = jnp.stack([lam_re.reshape(S5_COLS), lam_im.reshape(S5_COLS), jnp.repeat(log_dt, S5_STATE)])
    bc = jnp.stack([jnp.transpose(b_re, (2, 0, 1)), jnp.transpose(b_im, (2, 0, 1)),
                    jnp.transpose(c_re, (1, 0, 2)), jnp.transpose(c_im, (1, 0, 2))]).reshape(4, S5_GROUP, S5_COLS)
    return pl.pallas_call(
        _s5_prep_kernel,
        out_shape=(jax.ShapeDtypeStruct((2, BATCH, S5_COLS), F32),
                   jax.ShapeDtypeStruct((4, 2, MXU_DIM, S5_COLS // 2), BF16)),
        name="s5_prep",
    )(lam, bc)


def _bias_prep_kernel(rel_ref, bucket_ref, o_ref):
    for j in range(2):
        bucket = bucket_ref[j]
        for h in range(N_DIFF_HEADS):
            far = rel_ref[NUM_BUCKETS - 1, h]
            acc = jnp.zeros(bucket.shape, F32)
            for b in range(NUM_BUCKETS):
                acc = jnp.where(bucket == b, (rel_ref[b, h] - far) * LOG2E, acc)
            o_ref[h, j] = jnp.where(bucket < 0, -jnp.inf, acc)


def _t5_bucket(dist):
    max_exact = NUM_BUCKETS // 2
    df = np.maximum(dist, 1).astype(np.float32)
    large = max_exact + (np.log(df / np.float32(max_exact)) / np.float32(math.log(MAX_DISTANCE / max_exact))
                         * np.float32(NUM_BUCKETS - max_exact)).astype(np.int32)
    large = np.minimum(large, NUM_BUCKETS - 1)
    return np.where(dist < max_exact, dist, large).astype(np.int32)


def _bias_tiles(rel_bias):
    t = LANES
    qpos = np.arange(t, dtype=np.int32)[None, :]
    kpos = np.arange(t, dtype=np.int32)[:, None]
    d_diag = qpos - kpos
    d_prev = d_diag + t
    bucket = jnp.asarray(np.stack([np.where(d_diag >= 0, _t5_bucket(np.maximum(d_diag, 0)), -1),
                                   _t5_bucket(d_prev)]).astype(np.int32))
    return pl.pallas_call(
        _bias_prep_kernel,
        in_specs=[pl.BlockSpec(memory_space=pltpu.SMEM), pl.BlockSpec(memory_space=pltpu.VMEM)],
        out_specs=pl.BlockSpec(memory_space=pltpu.VMEM),
        out_shape=jax.ShapeDtypeStruct((N_DIFF_HEADS, 2, t, t), F32),
        name="bias_prep",
    )(rel_bias, bucket)


def _proj_kernel(x_ref, ln_ref, w32_ref, h_ref, u_ref, q_ref, k_ref, v_ref, w_ref):
    _cast_once(w32_ref, w_ref)
    sub = PROJ_ROWS // PROJ_SUBTILES
    nsl = 4
    rw = sub // nsl

    def work(st):
        r0 = st * sub
        h = []
        for j in range(nsl):
            rows = slice(r0 + j * rw, r0 + (j + 1) * rw)
            hj = _layer_norm(x_ref[rows, :], ln_ref[0:1, :], ln_ref[1:2, :])
            h_ref[rows, :] = hj
            h.append(hj.astype(BF16))
            yield
        h = jnp.concatenate(h, axis=0)
        rs = slice(r0, r0 + sub)
        u_ref[rs, :] = _dot(h, w_ref[:, 0:COL_Q])
        yield
        q_ref[rs, :] = (_dot(h, w_ref[:, COL_Q:COL_K]) * (DIFF_HEAD_DIM ** -0.5 * LOG2E)).astype(BF16)
        yield
        k_ref[rs, :] = _dot(h, w_ref[:, COL_K:COL_V]).astype(BF16)
        yield
        v_ref[rs, :] = _dot(h, w_ref[:, COL_V:MIX_IN]).astype(BF16)
        yield

    _interleave([work(st) for st in range(PROJ_SUBTILES)], lag=nsl)


def _proj(x2, ln, w):
    tm = PROJ_ROWS
    row = lambda i: (i, 0)
    const = lambda i: (0, 0)
    return pl.pallas_call(
        _proj_kernel,
        grid=(N_TOK // tm,),
        in_specs=[pl.BlockSpec((tm, D_MODEL), row), pl.BlockSpec((N_LN_ROWS, D_MODEL), const),
                  _resident((D_MODEL, MIX_IN))],
        out_specs=([pl.BlockSpec((tm, D_MODEL), row), pl.BlockSpec((tm, S5_WIDTH), row)]
                   + [pl.BlockSpec((tm, DIFF_QK_WIDTH), row)] * 2 + [pl.BlockSpec((tm, DIFF_WIDTH), row)]),
        out_shape=(jax.ShapeDtypeStruct((N_TOK, D_MODEL), F32), jax.ShapeDtypeStruct((N_TOK, S5_WIDTH), F32),
                   jax.ShapeDtypeStruct((N_TOK, DIFF_QK_WIDTH), BF16), jax.ShapeDtypeStruct((N_TOK, DIFF_QK_WIDTH), BF16),
                   jax.ShapeDtypeStruct((N_TOK, DIFF_WIDTH), BF16)),
        scratch_shapes=[pltpu.VMEM((D_MODEL, MIX_IN), BF16)],
        compiler_params=_params("arbitrary"),
        name="proj",
    )(x2, ln, w)


def _s5_kernel(u_ref, a_ref, w_ref, dg_ref, gw32_ref, y_ref, sre, sim, xre, xim, utm, ytm, gw_ref):
    @pl.when(pl.program_id(0) == 0)
    def _():
        xre[...] = jnp.zeros_like(xre)
        xim[...] = jnp.zeros_like(xim)

    _cast_once(gw32_ref, gw_ref)
    half = S5_COLS // 2
    cw = S5_SCAN_CHUNK
    nchunk = S5_COLS // cw
    rows = S5_SUB_STEPS * BATCH
    steps = S5_SUBTILES * S5_SUB_STEPS
    nslab = S5_WIDTH // LANES
    carry = [None] * nchunk

    for b in range(BATCH):
        for sl in range(nslab):
            utm[sl, pl.ds(b, steps, stride=BATCH), :] = u_ref[b, :, sl * LANES:(sl + 1) * LANES]

    def work(st):
        r0 = st * rows
        u = jnp.concatenate([utm[sl, r0:r0 + rows, :] for sl in range(nslab)], axis=-1)
        ub = u.astype(BF16)
        for hf in range(2):
            uh = ub[:, hf * MXU_DIM:(hf + 1) * MXU_DIM]
            sre[r0:r0 + rows, hf * half:(hf + 1) * half] = _dot(uh, w_ref[0, hf])
            yield
            sim[r0:r0 + rows, hf * half:(hf + 1) * half] = _dot(uh, w_ref[1, hf])
            yield
        for c in range(nchunk):
            cs = slice(c * cw, (c + 1) * cw)
            ar = a_ref[0, :, cs]
            ai = a_ref[1, :, cs]
            xr, xi = (xre[:, cs], xim[:, cs]) if st == 0 else carry[c]
            for t in range(S5_SUB_STEPS):
                r = r0 + t * BATCH
                nr = ar * xr - ai * xi + sre[r:r + BATCH, cs]
                ni = ar * xi + ai * xr + sim[r:r + BATCH, cs]
                sre[r:r + BATCH, cs] = nr
                sim[r:r + BATCH, cs] = ni
                xr, xi = nr, ni
            carry[c] = (xr, xi)
            if st == S5_SUBTILES - 1:
                xre[:, cs] = xr
                xim[:, cs] = xi
            yield
        ys = []
        for hf in range(2):
            hs = slice(hf * half, (hf + 1) * half)
            ys.append(lax.dot_general(sre[r0:r0 + rows, hs].astype(BF16), w_ref[2, hf], _NT,
                                      preferred_element_type=F32)
                      - lax.dot_general(sim[r0:r0 + rows, hs].astype(BF16), w_ref[3, hf], _NT,
                                        preferred_element_type=F32))
            yield
        y = jax.nn.gelu(jnp.concatenate(ys, axis=-1) + dg_ref[0:1, :] * u)
        yield
        z = _dot(y.astype(BF16), gw_ref[...]) + dg_ref[1:2, :]
        y = y * jax.nn.sigmoid(z)
        for sl in range(nslab):
            ytm[sl, r0:r0 + rows, :] = y[:, sl * LANES:(sl + 1) * LANES]
        t0 = st * S5_SUB_STEPS
        for b in range(BATCH):
            y_ref[b, t0:t0 + S5_SUB_STEPS, :] = jnp.concatenate(
                [ytm[sl, pl.ds(r0 + b, S5_SUB_STEPS, stride=BATCH), :] for sl in range(nslab)],
                axis=-1).astype(BF16)
        yield

    _interleave([work(st) for st in range(S5_SUBTILES)], lag=4)


def _s5(u, a8, w, dg, gw):
    steps = S5_SUBTILES * S5_SUB_STEPS
    rows = steps * BATCH
    nslab = S5_WIDTH // LANES
    blk = pl.BlockSpec((BATCH, steps, S5_WIDTH), lambda i: (0, i, 0))
    c2 = lambda i: (0, 0)
    return pl.pallas_call(
        _s5_kernel,
        grid=(SEQ // steps,),
        in_specs=[blk,
                  pl.BlockSpec((2, BATCH, S5_COLS), lambda i: (0, 0, 0)),
                  pl.BlockSpec((4, 2, MXU_DIM, S5_COLS // 2), lambda i: (0, 0, 0, 0)),
                  pl.BlockSpec((2, S5_WIDTH), c2), pl.BlockSpec((S5_WIDTH, S5_WIDTH), c2)],
        out_specs=blk,
        out_shape=jax.ShapeDtypeStruct((BATCH, SEQ, S5_WIDTH), BF16),
        scratch_shapes=[pltpu.VMEM((rows, S5_COLS), F32), pltpu.VMEM((rows, S5_COLS), F32),
                        pltpu.VMEM((BATCH, S5_COLS), F32), pltpu.VMEM((BATCH, S5_COLS), F32),
                        pltpu.VMEM((nslab, rows, LANES), F32), pltpu.VMEM((nslab, rows, LANES), F32),
                        pltpu.VMEM((S5_WIDTH, S5_WIDTH), BF16)],
        compiler_params=_params("arbitrary"),
        name="s5",
    )(u, a8, w, dg, gw)


def _diff_attn_head(q_ref, k_ref, v_ref, bias_ref, lam, g_ref, o_ref):
    t = ATT_TILE
    half = t // 2
    sb = LANES
    v_t = jnp.concatenate([v_ref[...].T, jnp.ones((ATT_ONES_ROWS, SEQ), BF16)], axis=0)
    dim = lax.broadcasted_iota(jnp.int32, (DIFF_V_DIM, t), 0)
    zero = jnp.zeros((DIFF_V_DIM, t), BF16)

    def scores(qm, k0, width, lo, hi):
        kb = k_ref[k0:k0 + width, :]
        return [_dot(kb, qm[mi][:, lo:hi]) for mi in range(2)]

    def with_bias(s, r0, k0):
        width, rows = s.shape
        slabs = []
        for i in range(width // sb):
            kb = (k0 + i * sb) // sb
            slab = s[i * sb:(i + 1) * sb, :]
            qbs = [(r0 + j * sb) // sb for j in range(rows // sb)]
            if all(kb < qb - 1 for qb in qbs):
                slabs.append(slab)
                continue
            pieces = []
            for j, qb in enumerate(qbs):
                piece = slab[:, j * sb:(j + 1) * sb]
                if kb == qb:
                    piece = piece + bias_ref[0, 0]
                elif kb == qb - 1:
                    piece = piece + bias_ref[0, 1]
                elif kb > qb:
                    piece = jnp.full((sb, sb), -jnp.inf, F32)
                pieces.append(piece)
            slabs.append(jnp.concatenate(pieces, axis=-1))
        return jnp.concatenate(slabs, axis=0)

    def update(s, r0, k0, state):
        m_old, acc_old = state
        s = with_bias(s, r0, k0)
        vb = v_t[:, k0:k0 + s.shape[0]]
        blk_max = jnp.max(s, axis=0, keepdims=True)
        if m_old is None:
            return blk_max, _dot(vb, jnp.exp2(s - blk_max).astype(BF16))
        m_new = jnp.maximum(m_old, blk_max)
        alpha = jnp.exp2(m_old - m_new)
        return m_new, alpha * acc_old + _dot(vb, jnp.exp2(s - m_new).astype(BF16))

    def lanes_of(state, lo, hi):
        return tuple(None if a is None else a[:, lo:hi] for a in state)

    for c in range(SEQ // t):
        r0 = c * t
        q_t = q_ref[r0:r0 + t, :].T
        qm = (jnp.where(dim < DIFF_HEAD_DIM, q_t, zero), jnp.where(dim >= DIFF_HEAD_DIM, q_t, zero))

        def diag_scores(qm=qm, r0=r0):
            return scores(qm, r0, half, 0, half), scores(qm, r0, t, half, t)

        state = [(None, None), (None, None)]
        s_next = scores(qm, 0, t, 0, t) if c > 0 else diag_scores()
        for j in range(c):
            s_cur = s_next
            s_next = scores(qm, (j + 1) * t, t, 0, t) if j + 1 < c else diag_scores()
            state = [update(s_cur[mi], r0, j * t, state[mi]) for mi in range(2)]
        s_lo, s_hi = s_next
        outs = []
        for mi in range(2):
            lo = update(s_lo[mi], r0, r0, lanes_of(state[mi], 0, half))
            hi = update(s_hi[mi], r0 + half, r0, lanes_of(state[mi], half, t))
            acc = jnp.concatenate([lo[1], hi[1]], axis=-1)
            outs.append(acc[:DIFF_V_DIM] / acc[DIFF_V_DIM:DIFF_V_DIM + 1])
        o = outs[0] - lam * outs[1]
        ms = jnp.mean(o * o, axis=0, keepdims=True)
        y = o * lax.rsqrt(ms + LN_EPS) * g_ref[...] * (1.0 - LAMBDA_INIT)
        o_ref[r0:r0 + t, :] = y.T.astype(BF16)


def _diff_attn_kernel(q_ref, k_ref, v_ref, bias_ref, lv_ref, g_ref, wgu32_ref, wd32_ref, o_ref, wgu_ref, wd_ref):
    wgu_ref[...] = wgu32_ref[...].astype(BF16)
    wd_ref[...] = wd32_ref[...].astype(BF16)
    lam = (jnp.exp(jnp.sum(lv_ref[0:1, :] * lv_ref[1:2, :], axis=-1, keepdims=True))
           - jnp.exp(jnp.sum(lv_ref[2:3, :] * lv_ref[3:4, :], axis=-1, keepdims=True)) + LAMBDA_INIT)
    for hh in range(ATT_HEADS):
        hs = pl.ds(hh * DIFF_V_DIM, DIFF_V_DIM)
        _diff_attn_head(q_ref.at[:, hs], k_ref.at[:, hs], v_ref.at[:, hs], bias_ref.at[pl.ds(hh, 1)], lam,
                        g_ref, o_ref.at[:, hs])


def _diff_attn(q, k, v, bias, lv, g, wgu, wd):
    steps = BATCH * (N_DIFF_HEADS // ATT_HEADS)
    vec = lambda b, h: (0, 0)
    seq = pl.BlockSpec((SEQ, ATT_HEADS * DIFF_V_DIM), lambda b, h: (b, h))
    slab = lambda b, h: (b * (N_DIFF_HEADS // ATT_HEADS) + h, 0)
    gu_slab = pl.BlockSpec((D_MODEL // steps, 2 * FFN_HIDDEN), slab)
    d_slab = pl.BlockSpec((FFN_HIDDEN // steps, D_MODEL), slab)
    return pl.pallas_call(
        _diff_attn_kernel,
        grid=(BATCH, N_DIFF_HEADS // ATT_HEADS),
        in_specs=[seq, seq, seq,
                  pl.BlockSpec((ATT_HEADS, 2, LANES, LANES), lambda b, h: (h, 0, 0, 0)),
                  pl.BlockSpec((4, DIFF_HEAD_DIM), vec), pl.BlockSpec((DIFF_V_DIM, 1), vec),
                  gu_slab, d_slab],
        out_specs=[seq, gu_slab, d_slab],
        out_shape=(jax.ShapeDtypeStruct((N_TOK, DIFF_WIDTH), BF16),
                   jax.ShapeDtypeStruct((D_MODEL, 2 * FFN_HIDDEN), BF16),
                   jax.ShapeDtypeStruct((FFN_HIDDEN, D_MODEL), BF16)),
        compiler_params=_params("parallel", "parallel"),
        name="diff_attn",
    )(q, k, v, bias, lv, g, wgu, wd)


def _kv_kernel(m_ref, w32_ref, o_ref, w_ref):
    _cast_once(w32_ref, w_ref)
    o_ref[...] = _dot(m_ref[...].astype(BF16), w_ref[...]).astype(BF16)


def _kv(mem2, wkv):
    tm = KV_ROWS
    return pl.pallas_call(
        _kv_kernel,
        grid=(BATCH * MEM_LEN // tm,),
        in_specs=[pl.BlockSpec((tm, D_MODEL), lambda i: (i, 0)), _resident((D_MODEL, 2 * D_MODEL))],
        out_specs=pl.BlockSpec((tm, 2 * D_MODEL), lambda i: (i, 0)),
        out_shape=jax.ShapeDtypeStruct((BATCH * MEM_LEN, 2 * D_MODEL), BF16),
        scratch_shapes=[pltpu.VMEM((D_MODEL, 2 * D_MODEL), BF16)],
        compiler_params=_params("arbitrary"),
        name="kv",
    )(mem2, wkv)


def _mix_ca_kernel(h0_ref, ys_ref, yd_ref, ln_ref, wout32_ref, wq32_ref, kv_ref, wo32_ref, o_ref,
                   wout_ref, wq_ref, wo_ref):
    _cast_once(wout32_ref, wout_ref)
    _cast_once(wq32_ref, wq_ref)
    _cast_once(wo32_ref, wo_ref)
    sub = MIX_ROWS // MIX_SUBTILES
    nsl = MIX_SLICES
    cw = D_MODEL // nsl
    rw = sub // nsl

    def work(st):
        r0 = st * sub
        ys = ys_ref[r0:r0 + sub, :]
        yd = yd_ref[r0:r0 + sub, :]
        mix = []
        for j in range(nsl):
            cs = slice(j * cw, (j + 1) * cw)
            mix.append(_dot(ys, wout_ref[0:S5_WIDTH, cs]) + _dot(yd, wout_ref[S5_WIDTH:, cs]))
            yield
        mix = jnp.concatenate(mix, axis=-1)
        h1 = []
        for j in range(nsl):
            rs = slice(j * rw, (j + 1) * rw)
            h0 = h0_ref[r0 + j * rw:r0 + (j + 1) * rw, :]
            h1.append(_layer_norm(DEEPNORM_ALPHA * h0 + mix[rs], ln_ref[2:3, :], ln_ref[3:4, :]))
            yield
        h1 = jnp.concatenate(h1, axis=0)
        h1b = h1.astype(BF16)
        qs = []
        for hd in range(CA_HEADS):
            cs = slice(hd * CA_HEAD_DIM, (hd + 1) * CA_HEAD_DIM)
            qs.append((_dot(h1b, wq_ref[:, cs]) * (CA_HEAD_DIM ** -0.5)).astype(BF16))
            yield
        heads = []
        for hd in range(CA_HEADS):
            kh = kv_ref[:, hd * CA_HEAD_DIM:(hd + 1) * CA_HEAD_DIM]
            vh = kv_ref[:, D_MODEL + hd * CA_HEAD_DIM:D_MODEL + (hd + 1) * CA_HEAD_DIM]
            s = lax.dot_general(qs[hd], kh, _NT, preferred_element_type=F32)
            e = jnp.exp(s - jnp.max(s, axis=-1, keepdims=True))
            oh = _dot(e.astype(BF16), vh) / jnp.sum(e, axis=-1, keepdims=True)
            heads.append(oh.astype(BF16))
            yield
        o = jnp.concatenate(heads, axis=-1)
        ca = []
        for j in range(nsl):
            ca.append(_dot(o, wo_ref[:, j * cw:(j + 1) * cw]))
            yield
        ca = jnp.concatenate(ca, axis=-1)
        for j in range(nsl):
            rs = slice(j * rw, (j + 1) * rw)
            o_ref[r0 + j * rw:r0 + (j + 1) * rw, :] = _layer_norm(
                DEEPNORM_ALPHA * h1[rs] + ca[rs], ln_ref[4:5, :], ln_ref[5:6, :])
            yield

    _interleave([work(st) for st in range(MIX_SUBTILES)], lag=nsl)


def _mix_ca(h0, ys, yd, ln, wout, wq, kv, wo):
    tm = MIX_ROWS
    per_batch = SEQ // tm
    row = lambda i: (i, 0)
    const = lambda i: (0, 0)
    sq = _resident((D_MODEL, D_MODEL))
    return pl.pallas_call(
        _mix_ca_kernel,
        grid=(N_TOK // tm,),
        in_specs=[pl.BlockSpec((tm, D_MODEL), row), pl.BlockSpec((tm, S5_WIDTH), row),
                  pl.BlockSpec((tm, DIFF_WIDTH), row),
                  pl.BlockSpec((N_LN_ROWS, D_MODEL), const), sq, sq,
                  pl.BlockSpec((MEM_LEN, 2 * D_MODEL), lambda i: (i // per_batch, 0)),
                  sq],
        out_specs=pl.BlockSpec((tm, D_MODEL), row),
        out_shape=jax.ShapeDtypeStruct((N_TOK, D_MODEL), F32),
        scratch_shapes=[pltpu.VMEM((D_MODEL, D_MODEL), BF16)] * 3,
        compiler_params=_params("arbitrary"),
        name="mix_ca",
    )(h0, ys, yd, ln, wout, wq, kv, wo)


def _ffn_kernel(h_ref, wgu_ref, wd_ref, ln_ref, o_ref):
    sub = FFN_ROWS // FFN_SUBTILES
    ck = FFN_CHUNK
    nsl = 4
    rw = sub // nsl

    def work(st):
        r0 = st * sub
        hb = h_ref[r0:r0 + sub, :].astype(BF16)
        acc = None
        for c in range(FFN_HIDDEN // ck):
            gate = _dot(hb, wgu_ref[:, c * ck:(c + 1) * ck])
            up = _dot(hb, wgu_ref[:, FFN_HIDDEN + c * ck:FFN_HIDDEN + (c + 1) * ck])
            act = (jax.nn.silu(gate) * up).astype(BF16)
            part = _dot(act, wd_ref[c * ck:(c + 1) * ck, :])
            acc = part if acc is None else acc + part
            yield
        for j in range(nsl):
            rs = slice(r0 + j * rw, r0 + (j + 1) * rw)
            o_ref[rs, :] = _layer_norm(DEEPNORM_ALPHA * h_ref[rs, :] + acc[j * rw:(j + 1) * rw],
                                       ln_ref[6:7, :], ln_ref[7:8, :])
            yield

    _interleave([work(st) for st in range(FFN_SUBTILES)], lag=FFN_HIDDEN // ck // 2 + 1)


def _ffn(h2, wgu, wd, ln):
    tm = FFN_ROWS
    row = lambda i: (i, 0)
    const = lambda i: (0, 0)
    return pl.pallas_call(
        _ffn_kernel,
        grid=(N_TOK // tm,),
        in_specs=[pl.BlockSpec((tm, D_MODEL), row),
                  _resident((D_MODEL, 2 * FFN_HIDDEN)), _resident((FFN_HIDDEN, D_MODEL)),
                  pl.BlockSpec((N_LN_ROWS, D_MODEL), const)],
        out_specs=pl.BlockSpec((tm, D_MODEL), row),
        out_shape=jax.ShapeDtypeStruct((N_TOK, D_MODEL), F32),
        compiler_params=_params("parallel"),
        name="ffn",
    )(h2, wgu, wd, ln)


def kernel(x, mem, ln_in_g, ln_in_b, w_in, s5_lambda_re, s5_lambda_im, s5_log_dt, s5_b_re, s5_b_im,
           s5_c_re, s5_c_im, s5_d, s5_glu_w, s5_glu_b, diff_lq1, diff_lk1, diff_lq2, diff_lk2,
           diff_subln_g, rel_bias, w_out, ln1_g, ln1_b, ca_wq, ca_wkv, ca_wo, ln2_g, ln2_b,
           ffn_w_gate_up, ffn_w_down, ln3_g, ln3_b):
    x2 = x.reshape(N_TOK, D_MODEL)
    ln = jnp.stack([ln_in_g, ln_in_b, ln1_g[0], ln1_b[0], ln2_g[0], ln2_b[0], ln3_g[0], ln3_b[0]])
    lv = jnp.stack([diff_lq1[0], diff_lk1[0], diff_lq2[0], diff_lk2[0]])
    dg = jnp.stack([s5_d[0].reshape(S5_WIDTH), s5_glu_b[0]])

    h0, u, q, k, v = _proj(x2, ln, w_in[0])

    a8, w_s5 = _s5_prep(s5_lambda_re[0], s5_lambda_im[0], s5_log_dt[0], s5_b_re[0], s5_b_im[0],
                        s5_c_re[0], s5_c_im[0])
    y_s5 = _s5(u.reshape(BATCH, SEQ, S5_WIDTH), a8, w_s5, dg, s5_glu_w[0]).reshape(N_TOK, S5_WIDTH)

    y_diff, wgu, wd = _diff_attn(q, k, v, _bias_tiles(rel_bias), lv, diff_subln_g.reshape(DIFF_V_DIM, 1),
                                 ffn_w_gate_up[0], ffn_w_down[0])

    kv = _kv(mem.reshape(BATCH * MEM_LEN, D_MODEL), ca_wkv[0])
    h2 = _mix_ca(h0, y_s5, y_diff, ln, w_out[0], ca_wq[0], kv, ca_wo[0])
    out = _ffn(h2, wgu, wd, ln)
    return out.reshape(BATCH, SEQ, D_MODEL)
```

```python
import math

import numpy as np

import jax
import jax.numpy as jnp
from jax import lax
from jax.experimental import pallas as pl
from jax.experimental.pallas import tpu as pltpu

F32 = jnp.float32
BF16 = jnp.bfloat16

D_MODEL = 1024
BATCH = 8
SEQ = 2048
N_TOK = BATCH * SEQ
MEM_LEN = 256
S5_WIDTH = 512
S5_GROUP = 16
S5_GROUPS = 32
S5_STATE = 64
S5_COLS = S5_GROUPS * S5_STATE
DIFF_WIDTH = 512
DIFF_HEAD_DIM = 64
DIFF_V_DIM = 128
N_DIFF_HEADS = 4
DIFF_QK_WIDTH = 2 * N_DIFF_HEADS * DIFF_HEAD_DIM
MIX_IN = S5_WIDTH + 2 * DIFF_QK_WIDTH + DIFF_WIDTH
COL_Q = S5_WIDTH
COL_K = COL_Q + DIFF_QK_WIDTH
COL_V = COL_K + DIFF_QK_WIDTH
NUM_BUCKETS = 32
MAX_DISTANCE = 128
CA_HEADS = 4
CA_HEAD_DIM = 256
FFN_HIDDEN = 2816
DEEPNORM_ALPHA = 2.0 ** 0.25
LN_EPS = 1e-5
LAMBDA_INIT = 0.8 - 0.6 * math.exp(0.0)
LOG2E = math.log2(math.e)

VMEM_LIMIT_BYTES = 56 * 1024 * 1024
MXU_DIM = 256
LANES = 128

PROJ_ROWS = 1024
PROJ_SUBTILES = 2
S5_SUB_STEPS = 64
S5_SUBTILES = 2
S5_SCAN_CHUNK = 512
ATT_TILE = 512
ATT_ONES_ROWS = 16
ATT_HEADS = 4
FFN_CHUNK = 256
FFN_ROWS = 1024
FFN_SUBTILES = 2
FFN_STAGE_COLS = 512
FFN_STAGE_ROWS = 256
FFN_STAGE_SLOTS = 4
MIX_ROWS = 1024
MIX_SUBTILES = 2
MIX_SLICES = 4
N_LN_ROWS = 8
KV_ROWS = 1024

_NT = (((1,), (1,)), ((), ()))


def _params(*sem):
    return pltpu.CompilerParams(dimension_semantics=sem, vmem_limit_bytes=VMEM_LIMIT_BYTES)


def _layer_norm(x, g, b):
    mu = jnp.mean(x, axis=-1, keepdims=True)
    xc = x - mu
    var = jnp.mean(xc * xc, axis=-1, keepdims=True)
    return xc * lax.rsqrt(var + LN_EPS) * g + b


def _dot(a, b):
    return jnp.dot(a, b, preferred_element_type=F32)


def _cast_once(w_ref, wbf_ref):
    @pl.when(pl.program_id(0) == 0)
    def _():
        wbf_ref[...] = w_ref[...].astype(BF16)


def _resident(shape):
    return pl.BlockSpec(shape, lambda *_: (0,) * len(shape), pipeline_mode=pl.Buffered(1))


def _interleave(gens, lag):
    gens = list(gens)
    live = [True] * len(gens)
    tick = 0
    while any(live):
        for i, g in enumerate(gens):
            if live[i] and tick >= lag * i:
                try:
                    next(g)
                except StopIteration:
                    live[i] = False
        tick += 1


def _s5_prep_kernel(lam_ref, bc_ref, a_ref, w_ref):
    lr = lam_ref[0:1, :]
    li = lam_ref[1:2, :]
    dt = jnp.exp(lam_ref[2:3, :])
    mag = jnp.exp(lr * dt)
    ang = li * dt
    ar = mag * jnp.cos(ang)
    ai = mag * jnp.sin(ang)
    den = lr * lr + li * li
    nr = ar - 1.0
    fr = (nr * lr + ai * li) / den
    fi = (ai * lr - nr * li) / den
    a_ref[0] = jnp.broadcast_to(ar, (BATCH, S5_COLS))
    a_ref[1] = jnp.broadcast_to(ai, (BATCH, S5_COLS))
    b_r = bc_ref[0]
    b_i = bc_ref[1]
    mats = (fr * b_r - fi * b_i, fr * b_i + fi * b_r, bc_ref[2], bc_ref[3])
    half = S5_COLS // 2
    lane_group = lax.broadcasted_iota(jnp.int32, (S5_GROUP, half), 1) // S5_STATE
    for k, mat in enumerate(mats):
        for hf in range(2):
            src = mat[:, hf * half:(hf + 1) * half]
            for gl in range(MXU_DIM // S5_GROUP):
                w_ref[k, hf, gl * S5_GROUP:(gl + 1) * S5_GROUP, :] = jnp.where(
                    lane_group == gl, src, 0.0).astype(BF16)


def _s5_prep(lam_re, lam_im, log_dt, b_re, b_im, c_re, c_im):
    lam = jnp.stack([lam_re.reshape(S5_COLS), lam_im.reshape(S5_COLS), jnp.repeat(log_dt, S5_STATE)])
    bc = jnp.stack([jnp.transpose(b_re, (2, 0, 1)), jnp.transpose(b_im, (2, 0, 1)),
                    jnp.transpose(c_re, (1, 0, 2)), jnp.transpose(c_im, (1, 0, 2))]).reshape(4, S5_GROUP, S5_COLS)
    return pl.pallas_call(
        _s5_prep_kernel,
        out_shape=(jax.ShapeDtypeStruct((2, BATCH, S5_COLS), F32),
                   jax.ShapeDtypeStruct((4, 2, MXU_DIM, S5_COLS // 2), BF16)),
        name="s5_prep",
    )(lam, bc)


def _bias_prep_kernel(rel_ref, bucket_ref, o_ref):
    for j in range(2):
        bucket = bucket_ref[j]
        for h in range(N_DIFF_HEADS):
            far = rel_ref[NUM_BUCKETS - 1, h]
            acc = jnp.zeros(bucket.shape, F32)
            for b in range(NUM_BUCKETS):
                acc = jnp.where(bucket == b, (rel_ref[b, h] - far) * LOG2E, acc)
            o_ref[h, j] = jnp.where(bucket < 0, -jnp.inf, acc)


def _t5_bucket(dist):
    max_exact = NUM_BUCKETS // 2
    df = np.maximum(dist, 1).astype(np.float32)
    large = max_exact + (np.log(df / np.float32(max_exact)) / np.float32(math.log(MAX_DISTANCE / max_exact))
                         * np.float32(NUM_BUCKETS - max_exact)).astype(np.int32)
    large = np.minimum(large, NUM_BUCKETS - 1)
    return np.where(dist < max_exact, dist, large).astype(np.int32)


def _bias_tiles(rel_bias):
    t = LANES
    qpos = np.arange(t, dtype=np.int32)[None, :]
    kpos = np.arange(t, dtype=np.int32)[:, None]
    d_diag = qpos - kpos
    d_prev = d_diag + t
    bucket = jnp.asarray(np.stack([np.where(d_diag >= 0, _t5_bucket(np.maximum(d_diag, 0)), -1),
                                   _t5_bucket(d_prev)]).astype(np.int32))
    return pl.pallas_call(
        _bias_prep_kernel,
        in_specs=[pl.BlockSpec(memory_space=pltpu.SMEM), pl.BlockSpec(memory_space=pltpu.VMEM)],
        out_specs=pl.BlockSpec(memory_space=pltpu.VMEM),
        out_shape=jax.ShapeDtypeStruct((N_DIFF_HEADS, 2, t, t), F32),
        name="bias_prep",
    )(rel_bias, bucket)


def _proj_kernel(x_ref, ln_ref, w32_ref, h_ref, u_ref, q_ref, k_ref, v_ref, w_ref):
    _cast_once(w32_ref, w_ref)
    sub = PROJ_ROWS // PROJ_SUBTILES
    nsl = 4
    rw = sub // nsl

    def work(st):
        r0 = st * sub
        h = []
        for j in range(nsl):
            rows = slice(r0 + j * rw, r0 + (j + 1) * rw)
            hj = _layer_norm(x_ref[rows, :], ln_ref[0:1, :], ln_ref[1:2, :])
            h_ref[rows, :] = hj
            h.append(hj.astype(BF16))
            yield
        h = jnp.concatenate(h, axis=0)
        rs = slice(r0, r0 + sub)
        u_ref[rs, :] = _dot(h, w_ref[:, 0:COL_Q])
        yield
        q_ref[rs, :] = (_dot(h, w_ref[:, COL_Q:COL_K]) * (DIFF_HEAD_DIM ** -0.5 * LOG2E)).astype(BF16)
        yield
        k_ref[rs, :] = _dot(h, w_ref[:, COL_K:COL_V]).astype(BF16)
        yield
        v_ref[rs, :] = _dot(h, w_ref[:, COL_V:MIX_IN]).astype(BF16)
        yield

    _interleave([work(st) for st in range(PROJ_SUBTILES)], lag=nsl)


def _proj(x2, ln, w):
    tm = PROJ_ROWS
    row = lambda i: (i, 0)
    const = lambda i: (0, 0)
    return pl.pallas_call(
        _proj_kernel,
        grid=(N_TOK // tm,),
        in_specs=[pl.BlockSpec((tm, D_MODEL), row), pl.BlockSpec((N_LN_ROWS, D_MODEL), const),
                  _resident((D_MODEL, MIX_IN))],
        out_specs=([pl.BlockSpec((tm, D_MODEL), row), pl.BlockSpec((tm, S5_WIDTH), row)]
                   + [pl.BlockSpec((tm, DIFF_QK_WIDTH), row)] * 2 + [pl.BlockSpec((tm, DIFF_WIDTH), row)]),
        out_shape=(jax.ShapeDtypeStruct((N_TOK, D_MODEL), F32), jax.ShapeDtypeStruct((N_TOK, S5_WIDTH), F32),
                   jax.ShapeDtypeStruct((N_TOK, DIFF_QK_WIDTH), BF16), jax.ShapeDtypeStruct((N_TOK, DIFF_QK_WIDTH), BF16),
                   jax.ShapeDtypeStruct((N_TOK, DIFF_WIDTH), BF16)),
        scratch_shapes=[pltpu.VMEM((D_MODEL, MIX_IN), BF16)],
        compiler_params=_params("arbitrary"),
        name="proj",
    )(x2, ln, w)


def _s5_kernel(u_ref, a_ref, w_ref, dg_ref, gw32_ref, y_ref, sre, sim, xbr, xbi, xre, xim, utm, ytm, gw_ref):
    @pl.when(pl.program_id(0) == 0)
    def _():
        xre[...] = jnp.zeros_like(xre)
        xim[...] = jnp.zeros_like(xim)

    _cast_once(gw32_ref, gw_ref)
    half = S5_COLS // 2
    cw = S5_SCAN_CHUNK
    nchunk = S5_COLS // cw
    rows = S5_SUB_STEPS * BATCH
    steps = S5_SUBTILES * S5_SUB_STEPS
    nslab = S5_WIDTH // LANES
    carry = [None] * nchunk

    for b in range(BATCH):
        for sl in range(nslab):
            utm[sl, pl.ds(b, steps, stride=BATCH), :] = u_ref[b, :, sl * LANES:(sl + 1) * LANES]

    def work(st):
        r0 = st * rows
        u = jnp.concatenate([utm[sl, r0:r0 + rows, :] for sl in range(nslab)], axis=-1)
        ub = u.astype(BF16)
        for hf in range(2):
            uh = ub[:, hf * MXU_DIM:(hf + 1) * MXU_DIM]
            sre[r0:r0 + rows, hf * half:(hf + 1) * half] = _dot(uh, w_ref[0, hf])
            yield
            sim[r0:r0 + rows, hf * half:(hf + 1) * half] = _dot(uh, w_ref[1, hf])
            yield
        for c in range(nchunk):
            cs = slice(c * cw, (c + 1) * cw)
            ar = a_ref[0, :, cs]
            ai = a_ref[1, :, cs]
            xr, xi = (xre[:, cs], xim[:, cs]) if st == 0 else carry[c]
            for t in range(0, S5_SUB_STEPS, 2):
                r = r0 + t * BATCH
                pair_r, pair_i = [], []
                for k in range(2):
                    rk = r + k * BATCH
                    nr = ar * xr - ai * xi + sre[rk:rk + BATCH, cs]
                    ni = ar * xi + ai * xr + sim[rk:rk + BATCH, cs]
                    pair_r.append(nr)
                    pair_i.append(ni)
                    xr, xi = nr, ni
                xbr[r:r + 2 * BATCH, cs] = jnp.concatenate(pair_r, axis=0).astype(BF16)
                xbi[r:r + 2 * BATCH, cs] = jnp.concatenate(pair_i, axis=0).astype(BF16)
            carry[c] = (xr, xi)
            if st == S5_SUBTILES - 1:
                xre[:, cs] = xr
                xim[:, cs] = xi
            yield
        ys = []
        for hf in range(2):
            hs = slice(hf * half, (hf + 1) * half)
            ys.append(lax.dot_general(xbr[r0:r0 + rows, hs], w_ref[2, hf], _NT,
                                      preferred_element_type=F32)
                      - lax.dot_general(xbi[r0:r0 + rows, hs], w_ref[3, hf], _NT,
                                        preferred_element_type=F32))
            yield
        y = jax.nn.gelu(jnp.concatenate(ys, axis=-1) + dg_ref[0:1, :] * u)
        yield
        z = _dot(y.astype(BF16), gw_ref[...]) + dg_ref[1:2, :]
        y = y * jax.nn.sigmoid(z)
        for sl in range(nslab):
            ytm[sl, r0:r0 + rows, :] = y[:, sl * LANES:(sl + 1) * LANES]
        t0 = st * S5_SUB_STEPS
        for b in range(BATCH):
            y_ref[b, t0:t0 + S5_SUB_STEPS, :] = jnp.concatenate(
                [ytm[sl, pl.ds(r0 + b, S5_SUB_STEPS, stride=BATCH), :] for sl in range(nslab)],
                axis=-1).astype(BF16)
        yield

    _interleave([work(st) for st in range(S5_SUBTILES)], lag=4)


def _s5(u, a8, w, dg, gw):
    steps = S5_SUBTILES * S5_SUB_STEPS
    rows = steps * BATCH
    nslab = S5_WIDTH // LANES
    blk = pl.BlockSpec((BATCH, steps, S5_WIDTH), lambda i: (0, i, 0))
    c2 = lambda i: (0, 0)
    return pl.pallas_call(
        _s5_kernel,
        grid=(SEQ // steps,),
        in_specs=[blk,
                  pl.BlockSpec((2, BATCH, S5_COLS), lambda i: (0, 0, 0)),
                  pl.BlockSpec((4, 2, MXU_DIM, S5_COLS // 2), lambda i: (0, 0, 0, 0)),
                  pl.BlockSpec((2, S5_WIDTH), c2), pl.BlockSpec((S5_WIDTH, S5_WIDTH), c2)],
        out_specs=blk,
        out_shape=jax.ShapeDtypeStruct((BATCH, SEQ, S5_WIDTH), BF16),
        scratch_shapes=[pltpu.VMEM((rows, S5_COLS), F32), pltpu.VMEM((rows, S5_COLS), F32),
                        pltpu.VMEM((rows, S5_COLS), BF16), pltpu.VMEM((rows, S5_COLS), BF16),
                        pltpu.VMEM((BATCH, S5_COLS), F32), pltpu.VMEM((BATCH, S5_COLS), F32),
                        pltpu.VMEM((nslab, rows, LANES), F32), pltpu.VMEM((nslab, rows, LANES), F32),
                        pltpu.VMEM((S5_WIDTH, S5_WIDTH), BF16)],
        compiler_params=_params("arbitrary"),
        name="s5",
    )(u, a8, w, dg, gw)


def _diff_attn_head(q_ref, k_ref, v_ref, bias_ref, lam, g_ref, o_ref):
    t = ATT_TILE
    half = t // 2
    sb = LANES
    v_t = jnp.concatenate([v_ref[...].T, jnp.ones((ATT_ONES_ROWS, SEQ), BF16)], axis=0)
    dim = lax.broadcasted_iota(jnp.int32, (DIFF_V_DIM, t), 0)
    zero = jnp.zeros((DIFF_V_DIM, t), BF16)

    def scores(qm, k0, width, lo, hi):
        kb = k_ref[k0:k0 + width, :]
        return [_dot(kb, qm[mi][:, lo:hi]) for mi in range(2)]

    def with_bias(s, r0, k0):
        width, rows = s.shape
        slabs = []
        for i in range(width // sb):
            kb = (k0 + i * sb) // sb
            slab = s[i * sb:(i + 1) * sb, :]
            qbs = [(r0 + j * sb) // sb for j in range(rows // sb)]
            if all(kb < qb - 1 for qb in qbs):
                slabs.append(slab)
                continue
            pieces = []
            for j, qb in enumerate(qbs):
                piece = slab[:, j * sb:(j + 1) * sb]
                if kb == qb:
                    piece = piece + bias_ref[0, 0]
                elif kb == qb - 1:
                    piece = piece + bias_ref[0, 1]
                elif kb > qb:
                    piece = jnp.full((sb, sb), -jnp.inf, F32)
                pieces.append(piece)
            slabs.append(jnp.concatenate(pieces, axis=-1))
        return jnp.concatenate(slabs, axis=0)

    def update(s, r0, k0, state):
        m_old, acc_old = state
        s = with_bias(s, r0, k0)
        vb = v_t[:, k0:k0 + s.shape[0]]
        blk_max = jnp.max(s, axis=0, keepdims=True)
        if m_old is None:
            return blk_max, _dot(vb, jnp.exp2(s - blk_max).astype(BF16))
        m_new = jnp.maximum(m_old, blk_max)
        alpha = jnp.exp2(m_old - m_new)
        return m_new, alpha * acc_old + _dot(vb, jnp.exp2(s - m_new).astype(BF16))

    def lanes_of(state, lo, hi):
        return tuple(None if a is None else a[:, lo:hi] for a in state)

    for c in range(SEQ // t):
        r0 = c * t
        q_t = q_ref[r0:r0 + t, :].T
        qm = (jnp.where(dim < DIFF_HEAD_DIM, q_t, zero), jnp.where(dim >= DIFF_HEAD_DIM, q_t, zero))

        def diag_scores(qm=qm, r0=r0):
            return scores(qm, r0, half, 0, half), scores(qm, r0, t, half, t)

        state = [(None, None), (None, None)]
        s_next = scores(qm, 0, t, 0, t) if c > 0 else diag_scores()
        for j in range(c):
            s_cur = s_next
            s_next = scores(qm, (j + 1) * t, t, 0, t) if j + 1 < c else diag_scores()
            state = [update(s_cur[mi], r0, j * t, state[mi]) for mi in range(2)]
        s_lo, s_hi = s_next
        outs = []
        for mi in range(2):
            lo = update(s_lo[mi], r0, r0, lanes_of(state[mi], 0, half))
            hi = update(s_hi[mi], r0 + half, r0, lanes_of(state[mi], half, t))
            acc = jnp.concatenate([lo[1], hi[1]], axis=-1)
            outs.append(acc[:DIFF_V_DIM] / acc[DIFF_V_DIM:DIFF_V_DIM + 1])
        o = outs[0] - lam * outs[1]
        ms = jnp.mean(o * o, axis=0, keepdims=True)
        y = o * lax.rsqrt(ms + LN_EPS) * g_ref[...] * (1.0 - LAMBDA_INIT)
        o_ref[r0:r0 + t, :] = y.T.astype(BF16)


def _diff_attn_kernel(q_ref, k_ref, v_ref, bias_ref, lv_ref, g_ref, o_ref):
    lam = (jnp.exp(jnp.sum(lv_ref[0:1, :] * lv_ref[1:2, :], axis=-1, keepdims=True))
           - jnp.exp(jnp.sum(lv_ref[2:3, :] * lv_ref[3:4, :], axis=-1, keepdims=True)) + LAMBDA_INIT)
    for hh in range(ATT_HEADS):
        hs = pl.ds(hh * DIFF_V_DIM, DIFF_V_DIM)
        _diff_attn_head(q_ref.at[:, hs], k_ref.at[:, hs], v_ref.at[:, hs], bias_ref.at[pl.ds(hh, 1)], lam,
                        g_ref, o_ref.at[:, hs])


def _diff_attn(q, k, v, bias, lv, g):
    vec = lambda b, h: (0, 0)
    seq = pl.BlockSpec((SEQ, ATT_HEADS * DIFF_V_DIM), lambda b, h: (b, h))
    return pl.pallas_call(
        _diff_attn_kernel,
        grid=(BATCH, N_DIFF_HEADS // ATT_HEADS),
        in_specs=[seq, seq, seq,
                  pl.BlockSpec((ATT_HEADS, 2, LANES, LANES), lambda b, h: (h, 0, 0, 0)),
                  pl.BlockSpec((4, DIFF_HEAD_DIM), vec), pl.BlockSpec((DIFF_V_DIM, 1), vec)],
        out_specs=seq,
        out_shape=jax.ShapeDtypeStruct((N_TOK, DIFF_WIDTH), BF16),
        compiler_params=_params("parallel", "parallel"),
        name="diff_attn",
    )(q, k, v, bias, lv, g)


def _kv_kernel(m_ref, w32_ref, o_ref, w_ref):
    _cast_once(w32_ref, w_ref)
    o_ref[...] = _dot(m_ref[...].astype(BF16), w_ref[...]).astype(BF16)


def _kv(mem2, wkv):
    tm = KV_ROWS
    return pl.pallas_call(
        _kv_kernel,
        grid=(BATCH * MEM_LEN // tm,),
        in_specs=[pl.BlockSpec((tm, D_MODEL), lambda i: (i, 0)), _resident((D_MODEL, 2 * D_MODEL))],
        out_specs=pl.BlockSpec((tm, 2 * D_MODEL), lambda i: (i, 0)),
        out_shape=jax.ShapeDtypeStruct((BATCH * MEM_LEN, 2 * D_MODEL), BF16),
        scratch_shapes=[pltpu.VMEM((D_MODEL, 2 * D_MODEL), BF16)],
        compiler_params=_params("arbitrary"),
        name="kv",
    )(mem2, wkv)


def _mix_ca_kernel(h0_ref, ys_ref, yd_ref, ln_ref, wout32_ref, wq32_ref, kv_ref, wo32_ref, o_ref,
                   wout_ref, wq_ref, wo_ref):
    _cast_once(wout32_ref, wout_ref)
    _cast_once(wq32_ref, wq_ref)
    _cast_once(wo32_ref, wo_ref)
    sub = MIX_ROWS // MIX_SUBTILES
    nsl = MIX_SLICES
    cw = D_MODEL // nsl
    rw = sub // nsl

    def work(st):
        r0 = st * sub
        ys = ys_ref[r0:r0 + sub, :]
        yd = yd_ref[r0:r0 + sub, :]
        mix = []
        for j in range(nsl):
            cs = slice(j * cw, (j + 1) * cw)
            mix.append(_dot(ys, wout_ref[0:S5_WIDTH, cs]) + _dot(yd, wout_ref[S5_WIDTH:, cs]))
            yield
        mix = jnp.concatenate(mix, axis=-1)
        h1 = []
        for j in range(nsl):
            rs = slice(j * rw, (j + 1) * rw)
            h0 = h0_ref[r0 + j * rw:r0 + (j + 1) * rw, :]
            h1.append(_layer_norm(DEEPNORM_ALPHA * h0 + mix[rs], ln_ref[2:3, :], ln_ref[3:4, :]))
            yield
        h1 = jnp.concatenate(h1, axis=0)
        h1b = h1.astype(BF16)
        qs = []
        for hd in range(CA_HEADS):
            cs = slice(hd * CA_HEAD_DIM, (hd + 1) * CA_HEAD_DIM)
            qs.append((_dot(h1b, wq_ref[:, cs]) * (CA_HEAD_DIM ** -0.5)).astype(BF16))
            yield
        heads = []
        for hd in range(CA_HEADS):
            kh = kv_ref[:, hd * CA_HEAD_DIM:(hd + 1) * CA_HEAD_DIM]
            vh = kv_ref[:, D_MODEL + hd * CA_HEAD_DIM:D_MODEL + (hd + 1) * CA_HEAD_DIM]
            s = lax.dot_general(qs[hd], kh, _NT, preferred_element_type=F32)
            e = jnp.exp(s - jnp.max(s, axis=-1, keepdims=True))
            oh = _dot(e.astype(BF16), vh) / jnp.sum(e, axis=-1, keepdims=True)
            heads.append(oh.astype(BF16))
            yield
        o = jnp.concatenate(heads, axis=-1)
        ca = []
        for j in range(nsl):
            ca.append(_dot(o, wo_ref[:, j * cw:(j + 1) * cw]))
            yield
        ca = jnp.concatenate(ca, axis=-1)
        for j in range(nsl):
            rs = slice(j * rw, (j + 1) * rw)
            o_ref[r0 + j * rw:r0 + (j + 1) * rw, :] = _layer_norm(
                DEEPNORM_ALPHA * h1[rs] + ca[rs], ln_ref[4:5, :], ln_ref[5:6, :])
            yield

    _interleave([work(st) for st in range(MIX_SUBTILES)], lag=nsl)


def _mix_ca(h0, ys, yd, ln, wout, wq, kv, wo):
    tm = MIX_ROWS
    per_batch = SEQ // tm
    row = lambda i: (i, 0)
    const = lambda i: (0, 0)
    sq = _resident((D_MODEL, D_MODEL))
    return pl.pallas_call(
        _mix_ca_kernel,
        grid=(N_TOK // tm,),
        in_specs=[pl.BlockSpec((tm, D_MODEL), row), pl.BlockSpec((tm, S5_WIDTH), row),
                  pl.BlockSpec((tm, DIFF_WIDTH), row),
                  pl.BlockSpec((N_LN_ROWS, D_MODEL), const), sq, sq,
                  pl.BlockSpec((MEM_LEN, 2 * D_MODEL), lambda i: (i // per_batch, 0)),
                  sq],
        out_specs=pl.BlockSpec((tm, D_MODEL), row),
        out_shape=jax.ShapeDtypeStruct((N_TOK, D_MODEL), F32),
        scratch_shapes=[pltpu.VMEM((D_MODEL, D_MODEL), BF16)] * 3,
        compiler_params=_params("arbitrary"),
        name="mix_ca",
    )(h0, ys, yd, ln, wout, wq, kv, wo)


def _stream_cast(src_hbm, dst_ref, stage, sems, w, chunks):
    ns = FFN_STAGE_SLOTS

    def copy(i):
        return pltpu.make_async_copy(src_hbm.at[chunks[i][0]], stage.at[i % ns], sems.at[w, i % ns])

    for i in range(min(ns - 1, len(chunks))):
        copy(i).start()
    for i in range(len(chunks)):
        if i + ns - 1 < len(chunks):
            copy(i + ns - 1).start()
        copy(i).wait()
        dst_ref[chunks[i][1]] = stage[i % ns].astype(BF16)


def _ffn_kernel(h_ref, wgu_hbm, wd_hbm, ln_ref, o_ref, wgu_ref, wd_ref, gu_stage, d_stage, sem):
    @pl.when(pl.program_id(0) == 0)
    def _():
        cw = FFN_STAGE_COLS
        cols = [(slice(None), slice(c * cw, (c + 1) * cw)) for c in range(2 * FFN_HIDDEN // cw)]
        _stream_cast(wgu_hbm, wgu_ref, gu_stage, sem, 0, [(ix, ix) for ix in cols])
        rw_ = FFN_STAGE_ROWS
        rows_ = [(slice(r * rw_, (r + 1) * rw_), slice(None)) for r in range(FFN_HIDDEN // rw_)]
        _stream_cast(wd_hbm, wd_ref, d_stage, sem, 1, [(ix, ix) for ix in rows_])

    sub = FFN_ROWS // FFN_SUBTILES
    ck = FFN_CHUNK
    nsl = 4
    rw = sub // nsl

    def work(st):
        r0 = st * sub
        hb = h_ref[r0:r0 + sub, :].astype(BF16)
        acc = None
        for c in range(FFN_HIDDEN // ck):
            gate = _dot(hb, wgu_ref[:, c * ck:(c + 1) * ck])
            up = _dot(hb, wgu_ref[:, FFN_HIDDEN + c * ck:FFN_HIDDEN + (c + 1) * ck])
            act = (jax.nn.silu(gate) * up).astype(BF16)
            part = _dot(act, wd_ref[c * ck:(c + 1) * ck, :])
            acc = part if acc is None else acc + part
            yield
        for j in range(nsl):
            rs = slice(r0 + j * rw, r0 + (j + 1) * rw)
            o_ref[rs, :] = _layer_norm(DEEPNORM_ALPHA * h_ref[rs, :] + acc[j * rw:(j + 1) * rw],
                                       ln_ref[6:7, :], ln_ref[7:8, :])
            yield

    _interleave([work(st) for st in range(FFN_SUBTILES)], lag=FFN_HIDDEN // ck // 2 + 1)


def _ffn(h2, wgu, wd, ln):
    tm = FFN_ROWS
    row = lambda i: (i, 0)
    const = lambda i: (0, 0)
    hbm = pl.BlockSpec(memory_space=pl.ANY)
    return pl.pallas_call(
        _ffn_kernel,
        grid=(N_TOK // tm,),
        in_specs=[pl.BlockSpec((tm, D_MODEL), row), hbm, hbm,
                  pl.BlockSpec((N_LN_ROWS, D_MODEL), const)],
        out_specs=pl.BlockSpec((tm, D_MODEL), row),
        out_shape=jax.ShapeDtypeStruct((N_TOK, D_MODEL), F32),
        scratch_shapes=[pltpu.VMEM((D_MODEL, 2 * FFN_HIDDEN), BF16), pltpu.VMEM((FFN_HIDDEN, D_MODEL), BF16),
                        pltpu.VMEM((FFN_STAGE_SLOTS, D_MODEL, FFN_STAGE_COLS), F32),
                        pltpu.VMEM((FFN_STAGE_SLOTS, FFN_STAGE_ROWS, D_MODEL), F32),
                        pltpu.SemaphoreType.DMA((2, FFN_STAGE_SLOTS))],
        compiler_params=_params("arbitrary"),
        name="ffn",
    )(h2, wgu, wd, ln)


def kernel(x, mem, ln_in_g, ln_in_b, w_in, s5_lambda_re, s5_lambda_im, s5_log_dt, s5_b_re, s5_b_im,
           s5_c_re, s5_c_im, s5_d, s5_glu_w, s5_glu_b, diff_lq1, diff_lk1, diff_lq2, diff_lk2,
           diff_subln_g, rel_bias, w_out, ln1_g, ln1_b, ca_wq, ca_wkv, ca_wo, ln2_g, ln2_b,
           ffn_w_gate_up, ffn_w_down, ln3_g, ln3_b):
    x2 = x.reshape(N_TOK, D_MODEL)
    ln = jnp.stack([ln_in_g, ln_in_b, ln1_g[0], ln1_b[0], ln2_g[0], ln2_b[0], ln3_g[0], ln3_b[0]])
    lv = jnp.stack([diff_lq1[0], diff_lk1[0], diff_lq2[0], diff_lk2[0]])
    dg = jnp.stack([s5_d[0].reshape(S5_WIDTH), s5_glu_b[0]])

    h0, u, q, k, v = _proj(x2, ln, w_in[0])

    a8, w_s5 = _s5_prep(s5_lambda_re[0], s5_lambda_im[0], s5_log_dt[0], s5_b_re[0], s5_b_im[0],
                        s5_c_re[0], s5_c_im[0])
    y_s5 = _s5(u.reshape(BATCH, SEQ, S5_WIDTH), a8, w_s5, dg, s5_glu_w[0]).reshape(N_TOK, S5_WIDTH)

    y_diff = _diff_attn(q, k, v, _bias_tiles(rel_bias), lv, diff_subln_g.reshape(DIFF_V_DIM, 1))

    kv = _kv(mem.reshape(BATCH * MEM_LEN, D_MODEL), ca_wkv[0])
    h2 = _mix_ca(h0, y_s5, y_diff, ln, w_out[0], ca_wq[0], kv, ca_wo[0])
    out = _ffn(h2, ffn_w_gate_up[0], ffn_w_down[0], ln)
    return out.reshape(BATCH, SEQ, D_MODEL)
```

```python
import math

import numpy as np

import jax
import jax.numpy as jnp
from jax import lax
from jax.experimental import pallas as pl
from jax.experimental.pallas import tpu as pltpu

F32 = jnp.float32
BF16 = jnp.bfloat16

D_MODEL = 1024
BATCH = 8
SEQ = 2048
N_TOK = BATCH * SEQ
MEM_LEN = 256
S5_WIDTH = 512
S5_GROUP = 16
S5_GROUPS = 32
S5_STATE = 64
S5_COLS = S5_GROUPS * S5_STATE
DIFF_WIDTH = 512
DIFF_HEAD_DIM = 64
DIFF_V_DIM = 128
N_DIFF_HEADS = 4
DIFF_QK_WIDTH = 2 * N_DIFF_HEADS * DIFF_HEAD_DIM
MIX_IN = S5_WIDTH + 2 * DIFF_QK_WIDTH + DIFF_WIDTH
COL_Q = S5_WIDTH
COL_K = COL_Q + DIFF_QK_WIDTH
COL_V = COL_K + DIFF_QK_WIDTH
NUM_BUCKETS = 32
MAX_DISTANCE = 128
CA_HEADS = 4
CA_HEAD_DIM = 256
FFN_HIDDEN = 2816
DEEPNORM_ALPHA = 2.0 ** 0.25
LN_EPS = 1e-5
LAMBDA_INIT = 0.8 - 0.6 * math.exp(0.0)
LOG2E = math.log2(math.e)

VMEM_LIMIT_BYTES = 56 * 1024 * 1024
MXU_DIM = 256
LANES = 128

PROJ_ROWS = 1024
PROJ_SUBTILES = 2
S5_SUB_STEPS = 64
S5_SUBTILES = 2
S5_SCAN_CHUNK = 512
ATT_TILE = 512
ATT_ONES_ROWS = 16
ATT_HEADS = 4
FFN_CHUNK = 256
FFN_ROWS = 1024
FFN_SUBTILES = 2
FFN_STAGE_COLS = 512
FFN_STAGE_ROWS = 256
FFN_STAGE_SLOTS = 4
MIX_ROWS = 1024
MIX_SUBTILES = 2
MIX_SLICES = 4
N_LN_ROWS = 8
KV_ROWS = 1024

_NT = (((1,), (1,)), ((), ()))


def _params(*sem):
    return pltpu.CompilerParams(dimension_semantics=sem, vmem_limit_bytes=VMEM_LIMIT_BYTES)


def _layer_norm(x, g, b):
    mu = jnp.mean(x, axis=-1, keepdims=True)
    xc = x - mu
    var = jnp.mean(xc * xc, axis=-1, keepdims=True)
    return xc * lax.rsqrt(var + LN_EPS) * g + b


def _dot(a, b):
    return jnp.dot(a, b, preferred_element_type=F32)


def _cast_once(w_ref, wbf_ref):
    @pl.when(pl.program_id(0) == 0)
    def _():
        wbf_ref[...] = w_ref[...].astype(BF16)


def _resident(shape):
    return pl.BlockSpec(shape, lambda *_: (0,) * len(shape), pipeline_mode=pl.Buffered(1))


def _interleave(gens, lag):
    gens = list(gens)
    live = [True] * len(gens)
    tick = 0
    while any(live):
        for i, g in enumerate(gens):
            if live[i] and tick >= lag * i:
                try:
                    next(g)
                except StopIteration:
                    live[i] = False
        tick += 1


def _s5_prep_kernel(lam_ref, bc_ref, a_ref, w_ref):
    lr = lam_ref[0:1, :]
    li = lam_ref[1:2, :]
    dt = jnp.exp(lam_ref[2:3, :])
    mag = jnp.exp(lr * dt)
    ang = li * dt
    ar = mag * jnp.cos(ang)
    ai = mag * jnp.sin(ang)
    den = lr * lr + li * li
    nr = ar - 1.0
    fr = (nr * lr + ai * li) / den
    fi = (ai * lr - nr * li) / den
    a_ref[0] = jnp.broadcast_to(ar, (BATCH, S5_COLS))
    a_ref[1] = jnp.broadcast_to(ai, (BATCH, S5_COLS))
    b_r = bc_ref[0]
    b_i = bc_ref[1]
    mats = (fr * b_r - fi * b_i, fr * b_i + fi * b_r, bc_ref[2], bc_ref[3])
    half = S5_COLS // 2
    lane_group = lax.broadcasted_iota(jnp.int32, (S5_GROUP, half), 1) // S5_STATE
    for k, mat in enumerate(mats):
        for hf in range(2):
            src = mat[:, hf * half:(hf + 1) * half]
            for gl in range(MXU_DIM // S5_GROUP):
                w_ref[k, hf, gl * S5_GROUP:(gl + 1) * S5_GROUP, :] = jnp.where(
                    lane_group == gl, src, 0.0).astype(BF16)


def _s5_prep(lam_re, lam_im, log_dt, b_re, b_im, c_re, c_im):
    lam = jnp.stack([lam_re.reshape(S5_COLS), lam_im.reshape(S5_COLS), jnp.repeat(log_dt, S5_STATE)])
    bc = jnp.stack([jnp.transpose(b_re, (2, 0, 1)), jnp.transpose(b_im, (2, 0, 1)),
                    jnp.transpose(c_re, (1, 0, 2)), jnp.transpose(c_im, (1, 0, 2))]).reshape(4, S5_GROUP, S5_COLS)
    return pl.pallas_call(
        _s5_prep_kernel,
        out_shape=(jax.ShapeDtypeStruct((2, BATCH, S5_COLS), F32),
                   jax.ShapeDtypeStruct((4, 2, MXU_DIM, S5_COLS // 2), BF16)),
        name="s5_prep",
    )(lam, bc)


def _bias_prep_kernel(rel_ref, bucket_ref, o_ref):
    for j in range(2):
        bucket = bucket_ref[j]
        for h in range(N_DIFF_HEADS):
            far = rel_ref[NUM_BUCKETS - 1, h]
            acc = jnp.zeros(bucket.shape, F32)
            for b in range(NUM_BUCKETS):
                acc = jnp.where(bucket == b, (rel_ref[b, h] - far) * LOG2E, acc)
            o_ref[h, j] = jnp.where(bucket < 0, -jnp.inf, acc)


def _t5_bucket(dist):
    max_exact = NUM_BUCKETS // 2
    df = np.maximum(dist, 1).astype(np.float32)
    large = max_exact + (np.log(df / np.float32(max_exact)) / np.float32(math.log(MAX_DISTANCE / max_exact))
                         * np.float32(NUM_BUCKETS - max_exact)).astype(np.int32)
    large = np.minimum(large, NUM_BUCKETS - 1)
    return np.where(dist < max_exact, dist, large).astype(np.int32)


def _bias_tiles(rel_bias):
    t = LANES
    qpos = np.arange(t, dtype=np.int32)[None, :]
    kpos = np.arange(t, dtype=np.int32)[:, None]
    d_diag = qpos - kpos
    d_prev = d_diag + t
    bucket = jnp.asarray(np.stack([np.where(d_diag >= 0, _t5_bucket(np.maximum(d_diag, 0)), -1),
                                   _t5_bucket(d_prev)]).astype(np.int32))
    return pl.pallas_call(
        _bias_prep_kernel,
        in_specs=[pl.BlockSpec(memory_space=pltpu.SMEM), pl.BlockSpec(memory_space=pltpu.VMEM)],
        out_specs=pl.BlockSpec(memory_space=pltpu.VMEM),
        out_shape=jax.ShapeDtypeStruct((N_DIFF_HEADS, 2, t, t), F32),
        name="bias_prep",
    )(rel_bias, bucket)


def _proj_kernel(x_ref, ln_ref, w32_ref, h_ref, u_ref, q_ref, k_ref, v_ref, w_ref):
    _cast_once(w32_ref, w_ref)
    sub = PROJ_ROWS // PROJ_SUBTILES
    nsl = 4
    rw = sub // nsl

    def work(st):
        r0 = st * sub
        h = []
        for j in range(nsl):
            rows = slice(r0 + j * rw, r0 + (j + 1) * rw)
            hj = _layer_norm(x_ref[rows, :], ln_ref[0:1, :], ln_ref[1:2, :])
            h_ref[rows, :] = hj
            h.append(hj.astype(BF16))
            yield
        h = jnp.concatenate(h, axis=0)
        rs = slice(r0, r0 + sub)
        u_ref[rs, :] = _dot(h, w_ref[:, 0:COL_Q])
        yield
        q_ref[rs, :] = (_dot(h, w_ref[:, COL_Q:COL_K]) * (DIFF_HEAD_DIM ** -0.5 * LOG2E)).astype(BF16)
        yield
        k_ref[rs, :] = _dot(h, w_ref[:, COL_K:COL_V]).astype(BF16)
        yield
        v_ref[rs, :] = _dot(h, w_ref[:, COL_V:MIX_IN]).astype(BF16)
        yield

    _interleave([work(st) for st in range(PROJ_SUBTILES)], lag=nsl)


def _proj(x2, ln, w):
    tm = PROJ_ROWS
    row = lambda i: (i, 0)
    const = lambda i: (0, 0)
    return pl.pallas_call(
        _proj_kernel,
        grid=(N_TOK // tm,),
        in_specs=[pl.BlockSpec((tm, D_MODEL), row), pl.BlockSpec((N_LN_ROWS, D_MODEL), const),
                  _resident((D_MODEL, MIX_IN))],
        out_specs=([pl.BlockSpec((tm, D_MODEL), row), pl.BlockSpec((tm, S5_WIDTH), row)]
                   + [pl.BlockSpec((tm, DIFF_QK_WIDTH), row)] * 2 + [pl.BlockSpec((tm, DIFF_WIDTH), row)]),
        out_shape=(jax.ShapeDtypeStruct((N_TOK, D_MODEL), F32), jax.ShapeDtypeStruct((N_TOK, S5_WIDTH), F32),
                   jax.ShapeDtypeStruct((N_TOK, DIFF_QK_WIDTH), BF16), jax.ShapeDtypeStruct((N_TOK, DIFF_QK_WIDTH), BF16),
                   jax.ShapeDtypeStruct((N_TOK, DIFF_WIDTH), BF16)),
        scratch_shapes=[pltpu.VMEM((D_MODEL, MIX_IN), BF16)],
        compiler_params=_params("arbitrary"),
        name="proj",
    )(x2, ln, w)


def _s5_kernel(u_ref, a_ref, w_ref, dg_ref, gw32_ref, y_ref, sre, sim, xbr, xbi, xre, xim, utm, ytm, gw_ref):
    @pl.when(pl.program_id(0) == 0)
    def _():
        xre[...] = jnp.zeros_like(xre)
        xim[...] = jnp.zeros_like(xim)

    _cast_once(gw32_ref, gw_ref)
    half = S5_COLS // 2
    cw = S5_SCAN_CHUNK
    nchunk = S5_COLS // cw
    rows = S5_SUB_STEPS * BATCH
    steps = S5_SUBTILES * S5_SUB_STEPS
    nslab = S5_WIDTH // LANES
    carry = [None] * nchunk

    for b in range(BATCH):
        for sl in range(nslab):
            utm[sl, pl.ds(b, steps, stride=BATCH), :] = u_ref[b, :, sl * LANES:(sl + 1) * LANES]

    def work(st):
        r0 = st * rows
        u = jnp.concatenate([utm[sl, r0:r0 + rows, :] for sl in range(nslab)], axis=-1)
        ub = u.astype(BF16)
        for hf in range(2):
            uh = ub[:, hf * MXU_DIM:(hf + 1) * MXU_DIM]
            sre[r0:r0 + rows, hf * half:(hf + 1) * half] = _dot(uh, w_ref[0, hf])
            yield
            sim[r0:r0 + rows, hf * half:(hf + 1) * half] = _dot(uh, w_ref[1, hf])
            yield
        for c in range(nchunk):
            cs = slice(c * cw, (c + 1) * cw)
            ar = a_ref[0, :, cs]
            ai = a_ref[1, :, cs]
            xr, xi = (xre[:, cs], xim[:, cs]) if st == 0 else carry[c]
            for t in range(0, S5_SUB_STEPS, 2):
                r = r0 + t * BATCH
                pair_r, pair_i = [], []
                for k in range(2):
                    rk = r + k * BATCH
                    nr = ar * xr - ai * xi + sre[rk:rk + BATCH, cs]
                    ni = ar * xi + ai * xr + sim[rk:rk + BATCH, cs]
                    pair_r.append(nr)
                    pair_i.append(ni)
                    xr, xi = nr, ni
                xbr[r:r + 2 * BATCH, cs] = jnp.concatenate(pair_r, axis=0).astype(BF16)
                xbi[r:r + 2 * BATCH, cs] = jnp.concatenate(pair_i, axis=0).astype(BF16)
            carry[c] = (xr, xi)
            if st == S5_SUBTILES - 1:
                xre[:, cs] = xr
                xim[:, cs] = xi
            yield
        ys = []
        for hf in range(2):
            hs = slice(hf * half, (hf + 1) * half)
            ys.append(lax.dot_general(xbr[r0:r0 + rows, hs], w_ref[2, hf], _NT,
                                      preferred_element_type=F32)
                      - lax.dot_general(xbi[r0:r0 + rows, hs], w_ref[3, hf], _NT,
                                        preferred_element_type=F32))
            yield
        y = jax.nn.gelu(jnp.concatenate(ys, axis=-1) + dg_ref[0:1, :] * u)
        yield
        z = _dot(y.astype(BF16), gw_ref[...]) + dg_ref[1:2, :]
        y = y * jax.nn.sigmoid(z)
        for sl in range(nslab):
            ytm[sl, r0:r0 + rows, :] = y[:, sl * LANES:(sl + 1) * LANES]
        t0 = st * S5_SUB_STEPS
        for b in range(BATCH):
            y_ref[b, t0:t0 + S5_SUB_STEPS, :] = jnp.concatenate(
                [ytm[sl, pl.ds(r0 + b, S5_SUB_STEPS, stride=BATCH), :] for sl in range(nslab)],
                axis=-1).astype(BF16)
        yield

    _interleave([work(st) for st in range(S5_SUBTILES)], lag=4)


def _s5(u, a8, w, dg, gw):
    steps = S5_SUBTILES * S5_SUB_STEPS
    rows = steps * BATCH
    nslab = S5_WIDTH // LANES
    blk = pl.BlockSpec((BATCH, steps, S5_WIDTH), lambda i: (0, i, 0))
    c2 = lambda i: (0, 0)
    return pl.pallas_call(
        _s5_kernel,
        grid=(SEQ // steps,),
        in_specs=[blk,
                  pl.BlockSpec((2, BATCH, S5_COLS), lambda i: (0, 0, 0)),
                  pl.BlockSpec((4, 2, MXU_DIM, S5_COLS // 2), lambda i: (0, 0, 0, 0)),
                  pl.BlockSpec((2, S5_WIDTH), c2), pl.BlockSpec((S5_WIDTH, S5_WIDTH), c2)],
        out_specs=blk,
        out_shape=jax.ShapeDtypeStruct((BATCH, SEQ, S5_WIDTH), BF16),
        scratch_shapes=[pltpu.VMEM((rows, S5_COLS), F32), pltpu.VMEM((rows, S5_COLS), F32),
                        pltpu.VMEM((rows, S5_COLS), BF16), pltpu.VMEM((rows, S5_COLS), BF16),
                        pltpu.VMEM((BATCH, S5_COLS), F32), pltpu.VMEM((BATCH, S5_COLS), F32),
                        pltpu.VMEM((nslab, rows, LANES), F32), pltpu.VMEM((nslab, rows, LANES), F32),
                        pltpu.VMEM((S5_WIDTH, S5_WIDTH), BF16)],
        compiler_params=_params("arbitrary"),
        name="s5",
    )(u, a8, w, dg, gw)


def _diff_attn_head(q_ref, k_ref, v_ref, bias_ref, lam, g_ref, o_ref):
    t = ATT_TILE
    half = t // 2
    sb = LANES
    v_t = jnp.concatenate([v_ref[...].T, jnp.ones((ATT_ONES_ROWS, SEQ), BF16)], axis=0)
    dim = lax.broadcasted_iota(jnp.int32, (DIFF_V_DIM, t), 0)
    zero = jnp.zeros((DIFF_V_DIM, t), BF16)

    def scores(qm, k0, width, lo, hi):
        kb = k_ref[k0:k0 + width, :]
        return [_dot(kb, qm[mi][:, lo:hi]) for mi in range(2)]

    def with_bias(s, r0, k0):
        width, rows = s.shape
        slabs = []
        for i in range(width // sb):
            kb = (k0 + i * sb) // sb
            slab = s[i * sb:(i + 1) * sb, :]
            qbs = [(r0 + j * sb) // sb for j in range(rows // sb)]
            if all(kb < qb - 1 for qb in qbs):
                slabs.append(slab)
                continue
            pieces = []
            for j, qb in enumerate(qbs):
                piece = slab[:, j * sb:(j + 1) * sb]
                if kb == qb:
                    piece = piece + bias_ref[0, 0]
                elif kb == qb - 1:
                    piece = piece + bias_ref[0, 1]
                elif kb > qb:
                    piece = jnp.full((sb, sb), -jnp.inf, F32)
                pieces.append(piece)
            slabs.append(jnp.concatenate(pieces, axis=-1))
        return jnp.concatenate(slabs, axis=0)

    def update(s, r0, k0, state):
        m_old, acc_old = state
        s = with_bias(s, r0, k0)
        vb = v_t[:, k0:k0 + s.shape[0]]
        blk_max = jnp.max(s, axis=0, keepdims=True)
        if m_old is None:
            return blk_max, _dot(vb, jnp.exp2(s - blk_max).astype(BF16))
        m_new = jnp.maximum(m_old, blk_max)
        alpha = jnp.exp2(m_old - m_new)
        return m_new, alpha * acc_old + _dot(vb, jnp.exp2(s - m_new).astype(BF16))

    def lanes_of(state, lo, hi):
        return tuple(None if a is None else a[:, lo:hi] for a in state)

    for c in range(SEQ // t):
        r0 = c * t
        q_t = q_ref[r0:r0 + t, :].T
        qm = (jnp.where(dim < DIFF_HEAD_DIM, q_t, zero), jnp.where(dim >= DIFF_HEAD_DIM, q_t, zero))

        def diag_scores(qm=qm, r0=r0):
            return scores(qm, r0, half, 0, half), scores(qm, r0, t, half, t)

        state = [(None, None), (None, None)]
        s_next = scores(qm, 0, t, 0, t) if c > 0 else diag_scores()
        for j in range(c):
            s_cur = s_next
            s_next = scores(qm, (j + 1) * t, t, 0, t) if j + 1 < c else diag_scores()
            state = [update(s_cur[mi], r0, j * t, state[mi]) for mi in range(2)]
        s_lo, s_hi = s_next
        outs = []
        for mi in range(2):
            lo = update(s_lo[mi], r0, r0, lanes_of(state[mi], 0, half))
            hi = update(s_hi[mi], r0 + half, r0, lanes_of(state[mi], half, t))
            acc = jnp.concatenate([lo[1], hi[1]], axis=-1)
            outs.append(acc[:DIFF_V_DIM] / acc[DIFF_V_DIM:DIFF_V_DIM + 1])
        o = outs[0] - lam * outs[1]
        ms = jnp.mean(o * o, axis=0, keepdims=True)
        y = o * lax.rsqrt(ms + LN_EPS) * g_ref[...] * (1.0 - LAMBDA_INIT)
        o_ref[r0:r0 + t, :] = y.T.astype(BF16)


def _diff_attn_kernel(q_ref, k_ref, v_ref, bias_ref, lv_ref, g_ref, o_ref):
    lam = (jnp.exp(jnp.sum(lv_ref[0:1, :] * lv_ref[1:2, :], axis=-1, keepdims=True))
           - jnp.exp(jnp.sum(lv_ref[2:3, :] * lv_ref[3:4, :], axis=-1, keepdims=True)) + LAMBDA_INIT)
    for hh in range(ATT_HEADS):
        hs = pl.ds(hh * DIFF_V_DIM, DIFF_V_DIM)
        _diff_attn_head(q_ref.at[:, hs], k_ref.at[:, hs], v_ref.at[:, hs], bias_ref.at[pl.ds(hh, 1)], lam,
                        g_ref, o_ref.at[:, hs])


def _diff_attn(q, k, v, bias, lv, g):
    vec = lambda b, h: (0, 0)
    seq = pl.BlockSpec((SEQ, ATT_HEADS * DIFF_V_DIM), lambda b, h: (b, h))
    return pl.pallas_call(
        _diff_attn_kernel,
        grid=(BATCH, N_DIFF_HEADS // ATT_HEADS),
        in_specs=[seq, seq, seq,
                  pl.BlockSpec((ATT_HEADS, 2, LANES, LANES), lambda b, h: (h, 0, 0, 0)),
                  pl.BlockSpec((4, DIFF_HEAD_DIM), vec), pl.BlockSpec((DIFF_V_DIM, 1), vec)],
        out_specs=seq,
        out_shape=jax.ShapeDtypeStruct((N_TOK, DIFF_WIDTH), BF16),
        compiler_params=_params("parallel", "parallel"),
        name="diff_attn",
    )(q, k, v, bias, lv, g)


def _kv_kernel(m_ref, w32_ref, o_ref, w_ref):
    _cast_once(w32_ref, w_ref)
    o_ref[...] = _dot(m_ref[...].astype(BF16), w_ref[...]).astype(BF16)


def _kv(mem2, wkv):
    tm = KV_ROWS
    return pl.pallas_call(
        _kv_kernel,
        grid=(BATCH * MEM_LEN // tm,),
        in_specs=[pl.BlockSpec((tm, D_MODEL), lambda i: (i, 0)), _resident((D_MODEL, 2 * D_MODEL))],
        out_specs=pl.BlockSpec((tm, 2 * D_MODEL), lambda i: (i, 0)),
        out_shape=jax.ShapeDtypeStruct((BATCH * MEM_LEN, 2 * D_MODEL), BF16),
        scratch_shapes=[pltpu.VMEM((D_MODEL, 2 * D_MODEL), BF16)],
        compiler_params=_params("arbitrary"),
        name="kv",
    )(mem2, wkv)


def _mix_ca_kernel(h0_ref, ys_ref, yd_ref, ln_ref, wout32_ref, wq32_ref, kv_ref, wo32_ref, o_ref,
                   wout_ref, wq_ref, wo_ref):
    _cast_once(wout32_ref, wout_ref)
    _cast_once(wq32_ref, wq_ref)
    _cast_once(wo32_ref, wo_ref)
    sub = MIX_ROWS // MIX_SUBTILES
    nsl = MIX_SLICES
    cw = D_MODEL // nsl
    rw = sub // nsl

    def work(st):
        r0 = st * sub
        ys = ys_ref[r0:r0 + sub, :]
        yd = yd_ref[r0:r0 + sub, :]
        mix = []
        for j in range(nsl):
            cs = slice(j * cw, (j + 1) * cw)
            mix.append(_dot(ys, wout_ref[0:S5_WIDTH, cs]) + _dot(yd, wout_ref[S5_WIDTH:, cs]))
            yield
        mix = jnp.concatenate(mix, axis=-1)
        h1 = []
        for j in range(nsl):
            rs = slice(j * rw, (j + 1) * rw)
            h0 = h0_ref[r0 + j * rw:r0 + (j + 1) * rw, :]
            h1.append(_layer_norm(DEEPNORM_ALPHA * h0 + mix[rs], ln_ref[2:3, :], ln_ref[3:4, :]))
            yield
        h1 = jnp.concatenate(h1, axis=0)
        h1b = h1.astype(BF16)
        qs = []
        for hd in range(CA_HEADS):
            cs = slice(hd * CA_HEAD_DIM, (hd + 1) * CA_HEAD_DIM)
            qs.append((_dot(h1b, wq_ref[:, cs]) * (CA_HEAD_DIM ** -0.5)).astype(BF16))
            yield
        heads = []
        for hd in range(CA_HEADS):
            kh = kv_ref[:, hd * CA_HEAD_DIM:(hd + 1) * CA_HEAD_DIM]
            vh = kv_ref[:, D_MODEL + hd * CA_HEAD_DIM:D_MODEL + (hd + 1) * CA_HEAD_DIM]
            s = lax.dot_general(qs[hd], kh, _NT, preferred_element_type=F32)
            e = jnp.exp(s - jnp.max(s, axis=-1, keepdims=True))
            oh = _dot(e.astype(BF16), vh) / jnp.sum(e, axis=-1, keepdims=True)
            heads.append(oh.astype(BF16))
            yield
        o = jnp.concatenate(heads, axis=-1)
        ca = []
        for j in range(nsl):
            ca.append(_dot(o, wo_ref[:, j * cw:(j + 1) * cw]))
            yield
        ca = jnp.concatenate(ca, axis=-1)
        for j in range(nsl):
            rs = slice(j * rw, (j + 1) * rw)
            o_ref[r0 + j * rw:r0 + (j + 1) * rw, :] = _layer_norm(
                DEEPNORM_ALPHA * h1[rs] + ca[rs], ln_ref[4:5, :], ln_ref[5:6, :])
            yield

    _interleave([work(st) for st in range(MIX_SUBTILES)], lag=nsl)


def _mix_ca(h0, ys, yd, ln, wout, wq, kv, wo):
    tm = MIX_ROWS
    per_batch = SEQ // tm
    row = lambda i: (i, 0)
    const = lambda i: (0, 0)
    sq = _resident((D_MODEL, D_MODEL))
    return pl.pallas_call(
        _mix_ca_kernel,
        grid=(N_TOK // tm,),
        in_specs=[pl.BlockSpec((tm, D_MODEL), row), pl.BlockSpec((tm, S5_WIDTH), row),
                  pl.BlockSpec((tm, DIFF_WIDTH), row),
                  pl.BlockSpec((N_LN_ROWS, D_MODEL), const), sq, sq,
                  pl.BlockSpec((MEM_LEN, 2 * D_MODEL), lambda i: (i // per_batch, 0)),
                  sq],
        out_specs=pl.BlockSpec((tm, D_MODEL), row),
        out_shape=jax.ShapeDtypeStruct((N_TOK, D_MODEL), F32),
        scratch_shapes=[pltpu.VMEM((D_MODEL, D_MODEL), BF16)] * 3,
        compiler_params=_params("arbitrary"),
        name="mix_ca",
    )(h0, ys, yd, ln, wout, wq, kv, wo)


def _stream_cast(src_hbm, dst_ref, stage, sems, w, chunks):
    ns = FFN_STAGE_SLOTS

    def copy(i):
        return pltpu.make_async_copy(src_hbm.at[chunks[i][0]], stage.at[i % ns], sems.at[w, i % ns])

    for i in range(min(ns - 1, len(chunks))):
        copy(i).start(priority=i % 2)
    for i in range(len(chunks)):
        if i + ns - 1 < len(chunks):
            copy(i + ns - 1).start(priority=(i + ns - 1) % 2)
        copy(i).wait()
        dst_ref[chunks[i][1]] = stage[i % ns].astype(BF16)


def _ffn_kernel(h_ref, wgu_hbm, wd_hbm, ln_ref, o_ref, wgu_ref, wd_ref, gu_stage, d_stage, sem):
    @pl.when(pl.program_id(0) == 0)
    def _():
        cw = FFN_STAGE_COLS
        cols = [(slice(None), slice(c * cw, (c + 1) * cw)) for c in range(2 * FFN_HIDDEN // cw)]
        _stream_cast(wgu_hbm, wgu_ref, gu_stage, sem, 0, [(ix, ix) for ix in cols])
        rw_ = FFN_STAGE_ROWS
        rows_ = [(slice(r * rw_, (r + 1) * rw_), slice(None)) for r in range(FFN_HIDDEN // rw_)]
        _stream_cast(wd_hbm, wd_ref, d_stage, sem, 1, [(ix, ix) for ix in rows_])

    sub = FFN_ROWS // FFN_SUBTILES
    ck = FFN_CHUNK
    nsl = 4
    rw = sub // nsl

    def work(st):
        r0 = st * sub
        hb = h_ref[r0:r0 + sub, :].astype(BF16)
        acc = None
        for c in range(FFN_HIDDEN // ck):
            gate = _dot(hb, wgu_ref[:, c * ck:(c + 1) * ck])
            up = _dot(hb, wgu_ref[:, FFN_HIDDEN + c * ck:FFN_HIDDEN + (c + 1) * ck])
            act = (jax.nn.silu(gate) * up).astype(BF16)
            part = _dot(act, wd_ref[c * ck:(c + 1) * ck, :])
            acc = part if acc is None else acc + part
            yield
        for j in range(nsl):
            rs = slice(r0 + j * rw, r0 + (j + 1) * rw)
            o_ref[rs, :] = _layer_norm(DEEPNORM_ALPHA * h_ref[rs, :] + acc[j * rw:(j + 1) * rw],
                                       ln_ref[6:7, :], ln_ref[7:8, :])
            yield

    _interleave([work(st) for st in range(FFN_SUBTILES)], lag=FFN_HIDDEN // ck // 2 + 1)


def _ffn(h2, wgu, wd, ln):
    tm = FFN_ROWS
    row = lambda i: (i, 0)
    const = lambda i: (0, 0)
    hbm = pl.BlockSpec(memory_space=pl.ANY)
    return pl.pallas_call(
        _ffn_kernel,
        grid=(N_TOK // tm,),
        in_specs=[pl.BlockSpec((tm, D_MODEL), row), hbm, hbm,
                  pl.BlockSpec((N_LN_ROWS, D_MODEL), const)],
        out_specs=pl.BlockSpec((tm, D_MODEL), row),
        out_shape=jax.ShapeDtypeStruct((N_TOK, D_MODEL), F32),
        scratch_shapes=[pltpu.VMEM((D_MODEL, 2 * FFN_HIDDEN), BF16), pltpu.VMEM((FFN_HIDDEN, D_MODEL), BF16),
                        pltpu.VMEM((FFN_STAGE_SLOTS, D_MODEL, FFN_STAGE_COLS), F32),
                        pltpu.VMEM((FFN_STAGE_SLOTS, FFN_STAGE_ROWS, D_MODEL), F32),
                        pltpu.SemaphoreType.DMA((2, FFN_STAGE_SLOTS))],
        compiler_params=_params("arbitrary"),
        name="ffn",
    )(h2, wgu, wd, ln)


def kernel(x, mem, ln_in_g, ln_in_b, w_in, s5_lambda_re, s5_lambda_im, s5_log_dt, s5_b_re, s5_b_im,
           s5_c_re, s5_c_im, s5_d, s5_glu_w, s5_glu_b, diff_lq1, diff_lk1, diff_lq2, diff_lk2,
           diff_subln_g, rel_bias, w_out, ln1_g, ln1_b, ca_wq, ca_wkv, ca_wo, ln2_g, ln2_b,
           ffn_w_gate_up, ffn_w_down, ln3_g, ln3_b):
    x2 = x.reshape(N_TOK, D_MODEL)
    ln = jnp.stack([ln_in_g, ln_in_b, ln1_g[0], ln1_b[0], ln2_g[0], ln2_b[0], ln3_g[0], ln3_b[0]])
    lv = jnp.stack([diff_lq1[0], diff_lk1[0], diff_lq2[0], diff_lk2[0]])
    dg = jnp.stack([s5_d[0].reshape(S5_WIDTH), s5_glu_b[0]])

    h0, u, q, k, v = _proj(x2, ln, w_in[0])

    a8, w_s5 = _s5_prep(s5_lambda_re[0], s5_lambda_im[0], s5_log_dt[0], s5_b_re[0], s5_b_im[0],
                        s5_c_re[0], s5_c_im[0])
    y_s5 = _s5(u.reshape(BATCH, SEQ, S5_WIDTH), a8, w_s5, dg, s5_glu_w[0]).reshape(N_TOK, S5_WIDTH)

    y_diff = _diff_attn(q, k, v, _bias_tiles(rel_bias), lv, diff_subln_g.reshape(DIFF_V_DIM, 1))

    kv = _kv(mem.reshape(BATCH * MEM_LEN, D_MODEL), ca_wkv[0])
    h2 = _mix_ca(h0, y_s5, y_diff, ln, w_out[0], ca_wq[0], kv, ca_wo[0])
    out = _ffn(h2, ffn_w_gate_up[0], ffn_w_down[0], ln)
    return out.reshape(BATCH, SEQ, D_MODEL)
```

```python
import math

import numpy as np

import jax
import jax.numpy as jnp
from jax import lax
from jax.experimental import pallas as pl
from jax.experimental.pallas import tpu as pltpu

F32 = jnp.float32
BF16 = jnp.bfloat16

D_MODEL = 1024
BATCH = 8
SEQ = 2048
N_TOK = BATCH * SEQ
MEM_LEN = 256
S5_WIDTH = 512
S5_GROUP = 16
S5_GROUPS = 32
S5_STATE = 64
S5_COLS = S5_GROUPS * S5_STATE
DIFF_WIDTH = 512
DIFF_HEAD_DIM = 64
DIFF_V_DIM = 128
N_DIFF_HEADS = 4
DIFF_QK_WIDTH = 2 * N_DIFF_HEADS * DIFF_HEAD_DIM
MIX_IN = S5_WIDTH + 2 * DIFF_QK_WIDTH + DIFF_WIDTH
COL_Q = S5_WIDTH
COL_K = COL_Q + DIFF_QK_WIDTH
COL_V = COL_K + DIFF_QK_WIDTH
NUM_BUCKETS = 32
MAX_DISTANCE = 128
CA_HEADS = 4
CA_HEAD_DIM = 256
FFN_HIDDEN = 2816
DEEPNORM_ALPHA = 2.0 ** 0.25
LN_EPS = 1e-5
LAMBDA_INIT = 0.8 - 0.6 * math.exp(0.0)
LOG2E = math.log2(math.e)

VMEM_LIMIT_BYTES = 56 * 1024 * 1024
MXU_DIM = 256
LANES = 128

PROJ_ROWS = 1024
PROJ_SUBTILES = 2
S5_SUB_STEPS = 64
S5_SUBTILES = 2
S5_SCAN_CHUNK = 512
ATT_TILE = 512
ATT_ONES_ROWS = 16
ATT_HEADS = 4
FFN_CHUNK = 256
FFN_ROWS = 1024
FFN_SUBTILES = 2
FFN_STAGE_COLS = 512
FFN_STAGE_ROWS = 256
FFN_STAGE_SLOTS = 4
MIX_ROWS = 1024
MIX_SUBTILES = 2
MIX_SLICES = 4
N_LN_ROWS = 8
KV_ROWS = 1024

_NT = (((1,), (1,)), ((), ()))


def _params(*sem):
    return pltpu.CompilerParams(dimension_semantics=sem, vmem_limit_bytes=VMEM_LIMIT_BYTES)


def _layer_norm(x, g, b):
    mu = jnp.mean(x, axis=-1, keepdims=True)
    xc = x - mu
    var = jnp.mean(xc * xc, axis=-1, keepdims=True)
    return xc * lax.rsqrt(var + LN_EPS) * g + b


def _dot(a, b):
    return jnp.dot(a, b, preferred_element_type=F32)


def _cast_once(w_ref, wbf_ref):
    @pl.when(pl.program_id(0) == 0)
    def _():
        wbf_ref[...] = w_ref[...].astype(BF16)


def _resident(shape):
    return pl.BlockSpec(shape, lambda *_: (0,) * len(shape), pipeline_mode=pl.Buffered(1))


def _interleave(gens, lag):
    gens = list(gens)
    live = [True] * len(gens)
    tick = 0
    while any(live):
        for i, g in enumerate(gens):
            if live[i] and tick >= lag * i:
                try:
                    next(g)
                except StopIteration:
                    live[i] = False
        tick += 1


def _s5_prep_kernel(lam_ref, bc_ref, a_ref, w_ref):
    lr = lam_ref[0:1, :]
    li = lam_ref[1:2, :]
    dt = jnp.exp(lam_ref[2:3, :])
    mag = jnp.exp(lr * dt)
    ang = li * dt
    ar = mag * jnp.cos(ang)
    ai = mag * jnp.sin(ang)
    den = lr * lr + li * li
    nr = ar - 1.0
    fr = (nr * lr + ai * li) / den
    fi = (ai * lr - nr * li) / den
    a_ref[0] = jnp.broadcast_to(ar, (BATCH, S5_COLS))
    a_ref[1] = jnp.broadcast_to(ai, (BATCH, S5_COLS))
    b_r = bc_ref[0]
    b_i = bc_ref[1]
    mats = (fr * b_r - fi * b_i, fr * b_i + fi * b_r, bc_ref[2], bc_ref[3])
    half = S5_COLS // 2
    lane_group = lax.broadcasted_iota(jnp.int32, (S5_GROUP, half), 1) // S5_STATE
    for k, mat in enumerate(mats):
        for hf in range(2):
            src = mat[:, hf * half:(hf + 1) * half]
            for gl in range(MXU_DIM // S5_GROUP):
                w_ref[k, hf, gl * S5_GROUP:(gl + 1) * S5_GROUP, :] = jnp.where(
                    lane_group == gl, src, 0.0).astype(BF16)


def _s5_prep(lam_re, lam_im, log_dt, b_re, b_im, c_re, c_im):
    lam = jnp.stack([lam_re.reshape(S5_COLS), lam_im.reshape(S5_COLS), jnp.repeat(log_dt, S5_STATE)])
    bc = jnp.stack([jnp.transpose(b_re, (2, 0, 1)), jnp.transpose(b_im, (2, 0, 1)),
                    jnp.transpose(c_re, (1, 0, 2)), jnp.transpose(c_im, (1, 0, 2))]).reshape(4, S5_GROUP, S5_COLS)
    return pl.pallas_call(
        _s5_prep_kernel,
        out_shape=(jax.ShapeDtypeStruct((2, BATCH, S5_COLS), F32),
                   jax.ShapeDtypeStruct((4, 2, MXU_DIM, S5_COLS // 2), BF16)),
        name="s5_prep",
    )(lam, bc)


def _bias_prep_kernel(rel_ref, bucket_ref, o_ref):
    for j in range(2):
        bucket = bucket_ref[j]
        for h in range(N_DIFF_HEADS):
            far = rel_ref[NUM_BUCKETS - 1, h]
            acc = jnp.zeros(bucket.shape, F32)
            for b in range(NUM_BUCKETS):
                acc = jnp.where(bucket == b, (rel_ref[b, h] - far) * LOG2E, acc)
            o_ref[h, j] = jnp.where(bucket < 0, -jnp.inf, acc)


def _t5_bucket(dist):
    max_exact = NUM_BUCKETS // 2
    df = np.maximum(dist, 1).astype(np.float32)
    large = max_exact + (np.log(df / np.float32(max_exact)) / np.float32(math.log(MAX_DISTANCE / max_exact))
                         * np.float32(NUM_BUCKETS - max_exact)).astype(np.int32)
    large = np.minimum(large, NUM_BUCKETS - 1)
    return np.where(dist < max_exact, dist, large).astype(np.int32)


def _bias_tiles(rel_bias):
    t = LANES
    qpos = np.arange(t, dtype=np.int32)[None, :]
    kpos = np.arange(t, dtype=np.int32)[:, None]
    d_diag = qpos - kpos
    d_prev = d_diag + t
    bucket = jnp.asarray(np.stack([np.where(d_diag >= 0, _t5_bucket(np.maximum(d_diag, 0)), -1),
                                   _t5_bucket(d_prev)]).astype(np.int32))
    return pl.pallas_call(
        _bias_prep_kernel,
        in_specs=[pl.BlockSpec(memory_space=pltpu.SMEM), pl.BlockSpec(memory_space=pltpu.VMEM)],
        out_specs=pl.BlockSpec(memory_space=pltpu.VMEM),
        out_shape=jax.ShapeDtypeStruct((N_DIFF_HEADS, 2, t, t), F32),
        name="bias_prep",
    )(rel_bias, bucket)


def _proj_kernel(x_ref, ln_ref, w32_ref, h_ref, u_ref, q_ref, k_ref, v_ref, w_ref):
    _cast_once(w32_ref, w_ref)
    sub = PROJ_ROWS // PROJ_SUBTILES
    nsl = 4
    rw = sub // nsl

    def work(st):
        r0 = st * sub
        h = []
        for j in range(nsl):
            rows = slice(r0 + j * rw, r0 + (j + 1) * rw)
            hj = _layer_norm(x_ref[rows, :], ln_ref[0:1, :], ln_ref[1:2, :])
            h_ref[rows, :] = hj
            h.append(hj.astype(BF16))
            yield
        h = jnp.concatenate(h, axis=0)
        rs = slice(r0, r0 + sub)
        u_ref[rs, :] = _dot(h, w_ref[:, 0:COL_Q])
        yield
        q_ref[rs, :] = (_dot(h, w_ref[:, COL_Q:COL_K]) * (DIFF_HEAD_DIM ** -0.5 * LOG2E)).astype(BF16)
        yield
        k_ref[rs, :] = _dot(h, w_ref[:, COL_K:COL_V]).astype(BF16)
        yield
        v_ref[rs, :] = _dot(h, w_ref[:, COL_V:MIX_IN]).astype(BF16)
        yield

    _interleave([work(st) for st in range(PROJ_SUBTILES)], lag=nsl)


def _proj(x2, ln, w):
    tm = PROJ_ROWS
    row = lambda i: (i, 0)
    const = lambda i: (0, 0)
    return pl.pallas_call(
        _proj_kernel,
        grid=(N_TOK // tm,),
        in_specs=[pl.BlockSpec((tm, D_MODEL), row), pl.BlockSpec((N_LN_ROWS, D_MODEL), const),
                  _resident((D_MODEL, MIX_IN))],
        out_specs=([pl.BlockSpec((tm, D_MODEL), row), pl.BlockSpec((tm, S5_WIDTH), row)]
                   + [pl.BlockSpec((tm, DIFF_QK_WIDTH), row)] * 2 + [pl.BlockSpec((tm, DIFF_WIDTH), row)]),
        out_shape=(jax.ShapeDtypeStruct((N_TOK, D_MODEL), F32), jax.ShapeDtypeStruct((N_TOK, S5_WIDTH), F32),
                   jax.ShapeDtypeStruct((N_TOK, DIFF_QK_WIDTH), BF16), jax.ShapeDtypeStruct((N_TOK, DIFF_QK_WIDTH), BF16),
                   jax.ShapeDtypeStruct((N_TOK, DIFF_WIDTH), BF16)),
        scratch_shapes=[pltpu.VMEM((D_MODEL, MIX_IN), BF16)],
        compiler_params=_params("arbitrary"),
        name="proj",
    )(x2, ln, w)


def _s5_kernel(u_ref, a_ref, w_ref, dg_ref, gw32_ref, y_ref, sre, sim, xbr, xbi, xre, xim, utm, ytm, gw_ref):
    @pl.when(pl.program_id(0) == 0)
    def _():
        xre[...] = jnp.zeros_like(xre)
        xim[...] = jnp.zeros_like(xim)

    _cast_once(gw32_ref, gw_ref)
    half = S5_COLS // 2
    cw = S5_SCAN_CHUNK
    nchunk = S5_COLS // cw
    rows = S5_SUB_STEPS * BATCH
    steps = S5_SUBTILES * S5_SUB_STEPS
    nslab = S5_WIDTH // LANES
    carry = [None] * nchunk

    for b in range(BATCH):
        for sl in range(nslab):
            utm[sl, pl.ds(b, steps, stride=BATCH), :] = u_ref[b, :, sl * LANES:(sl + 1) * LANES]

    def work(st):
        r0 = st * rows
        u = jnp.concatenate([utm[sl, r0:r0 + rows, :] for sl in range(nslab)], axis=-1)
        ub = u.astype(BF16)
        for hf in range(2):
            uh = ub[:, hf * MXU_DIM:(hf + 1) * MXU_DIM]
            sre[r0:r0 + rows, hf * half:(hf + 1) * half] = _dot(uh, w_ref[0, hf])
            yield
            sim[r0:r0 + rows, hf * half:(hf + 1) * half] = _dot(uh, w_ref[1, hf])
            yield
        for c in range(nchunk):
            cs = slice(c * cw, (c + 1) * cw)
            ar = a_ref[0, :, cs]
            ai = a_ref[1, :, cs]
            xr, xi = (xre[:, cs], xim[:, cs]) if st == 0 else carry[c]
            for t in range(0, S5_SUB_STEPS, 2):
                r = r0 + t * BATCH
                pair_r, pair_i = [], []
                for k in range(2):
                    rk = r + k * BATCH
                    nr = ar * xr - ai * xi + sre[rk:rk + BATCH, cs]
                    ni = ar * xi + ai * xr + sim[rk:rk + BATCH, cs]
                    pair_r.append(nr)
                    pair_i.append(ni)
                    xr, xi = nr, ni
                xbr[r:r + 2 * BATCH, cs] = jnp.concatenate(pair_r, axis=0).astype(BF16)
                xbi[r:r + 2 * BATCH, cs] = jnp.concatenate(pair_i, axis=0).astype(BF16)
            carry[c] = (xr, xi)
            if st == S5_SUBTILES - 1:
                xre[:, cs] = xr
                xim[:, cs] = xi
            yield
        ys = []
        for hf in range(2):
            hs = slice(hf * half, (hf + 1) * half)
            ys.append(lax.dot_general(xbr[r0:r0 + rows, hs], w_ref[2, hf], _NT,
                                      preferred_element_type=F32)
                      - lax.dot_general(xbi[r0:r0 + rows, hs], w_ref[3, hf], _NT,
                                        preferred_element_type=F32))
            yield
        y = jax.nn.gelu(jnp.concatenate(ys, axis=-1) + dg_ref[0:1, :] * u)
        yield
        z = _dot(y.astype(BF16), gw_ref[...]) + dg_ref[1:2, :]
        y = y * jax.nn.sigmoid(z)
        for sl in range(nslab):
            ytm[sl, r0:r0 + rows, :] = y[:, sl * LANES:(sl + 1) * LANES]
        t0 = st * S5_SUB_STEPS
        for b in range(BATCH):
            y_ref[b, t0:t0 + S5_SUB_STEPS, :] = jnp.concatenate(
                [ytm[sl, pl.ds(r0 + b, S5_SUB_STEPS, stride=BATCH), :] for sl in range(nslab)],
                axis=-1).astype(BF16)
        yield

    _interleave([work(st) for st in range(S5_SUBTILES)], lag=4)


def _s5(u, a8, w, dg, gw):
    steps = S5_SUBTILES * S5_SUB_STEPS
    rows = steps * BATCH
    nslab = S5_WIDTH // LANES
    blk = pl.BlockSpec((BATCH, steps, S5_WIDTH), lambda i: (0, i, 0))
    c2 = lambda i: (0, 0)
    return pl.pallas_call(
        _s5_kernel,
        grid=(SEQ // steps,),
        in_specs=[blk,
                  pl.BlockSpec((2, BATCH, S5_COLS), lambda i: (0, 0, 0)),
                  pl.BlockSpec((4, 2, MXU_DIM, S5_COLS // 2), lambda i: (0, 0, 0, 0)),
                  pl.BlockSpec((2, S5_WIDTH), c2), pl.BlockSpec((S5_WIDTH, S5_WIDTH), c2)],
        out_specs=blk,
        out_shape=jax.ShapeDtypeStruct((BATCH, SEQ, S5_WIDTH), BF16),
        scratch_shapes=[pltpu.VMEM((rows, S5_COLS), F32), pltpu.VMEM((rows, S5_COLS), F32),
                        pltpu.VMEM((rows, S5_COLS), BF16), pltpu.VMEM((rows, S5_COLS), BF16),
                        pltpu.VMEM((BATCH, S5_COLS), F32), pltpu.VMEM((BATCH, S5_COLS), F32),
                        pltpu.VMEM((nslab, rows, LANES), F32), pltpu.VMEM((nslab, rows, LANES), F32),
                        pltpu.VMEM((S5_WIDTH, S5_WIDTH), BF16)],
        compiler_params=_params("arbitrary"),
        name="s5",
    )(u, a8, w, dg, gw)


def _diff_attn_head(q_ref, k_ref, v_ref, bias_ref, lam, g_ref, o_ref):
    t = ATT_TILE
    half = t // 2
    sb = LANES
    v_t = jnp.concatenate([v_ref[...].T, jnp.ones((ATT_ONES_ROWS, SEQ), BF16)], axis=0)
    dim = lax.broadcasted_iota(jnp.int32, (DIFF_V_DIM, t), 0)
    zero = jnp.zeros((DIFF_V_DIM, t), BF16)

    def scores(qm, k0, width, lo, hi):
        kb = k_ref[k0:k0 + width, :]
        return [_dot(kb, qm[mi][:, lo:hi]) for mi in range(2)]

    def with_bias(s, r0, k0):
        width, rows = s.shape
        slabs = []
        for i in range(width // sb):
            kb = (k0 + i * sb) // sb
            slab = s[i * sb:(i + 1) * sb, :]
            qbs = [(r0 + j * sb) // sb for j in range(rows // sb)]
            if all(kb < qb - 1 for qb in qbs):
                slabs.append(slab)
                continue
            pieces = []
            for j, qb in enumerate(qbs):
                piece = slab[:, j * sb:(j + 1) * sb]
                if kb == qb:
                    piece = piece + bias_ref[0, 0]
                elif kb == qb - 1:
                    piece = piece + bias_ref[0, 1]
                elif kb > qb:
                    piece = jnp.full((sb, sb), -jnp.inf, F32)
                pieces.append(piece)
            slabs.append(jnp.concatenate(pieces, axis=-1))
        return jnp.concatenate(slabs, axis=0)

    def update(s, r0, k0, state):
        m_old, acc_old = state
        s = with_bias(s, r0, k0)
        vb = v_t[:, k0:k0 + s.shape[0]]
        blk_max = jnp.max(s, axis=0, keepdims=True)
        if m_old is None:
            return blk_max, _dot(vb, jnp.exp2(s - blk_max).astype(BF16))
        m_new = jnp.maximum(m_old, blk_max)
        alpha = jnp.exp2(m_old - m_new)
        return m_new, alpha * acc_old + _dot(vb, jnp.exp2(s - m_new).astype(BF16))

    def lanes_of(state, lo, hi):
        return tuple(None if a is None else a[:, lo:hi] for a in state)

    for c in range(SEQ // t):
        r0 = c * t
        q_t = q_ref[r0:r0 + t, :].T
        qm = (jnp.where(dim < DIFF_HEAD_DIM, q_t, zero), jnp.where(dim >= DIFF_HEAD_DIM, q_t, zero))

        def diag_scores(qm=qm, r0=r0):
            return scores(qm, r0, half, 0, half), scores(qm, r0, t, half, t)

        state = [(None, None), (None, None)]
        s_next = scores(qm, 0, t, 0, t) if c > 0 else diag_scores()
        for j in range(c):
            s_cur = s_next
            s_next = scores(qm, (j + 1) * t, t, 0, t) if j + 1 < c else diag_scores()
            state = [update(s_cur[mi], r0, j * t, state[mi]) for mi in range(2)]
        s_lo, s_hi = s_next
        outs = []
        for mi in range(2):
            lo = update(s_lo[mi], r0, r0, lanes_of(state[mi], 0, half))
            hi = update(s_hi[mi], r0 + half, r0, lanes_of(state[mi], half, t))
            acc = jnp.concatenate([lo[1], hi[1]], axis=-1)
            outs.append(acc[:DIFF_V_DIM] / acc[DIFF_V_DIM:DIFF_V_DIM + 1])
        o = outs[0] - lam * outs[1]
        ms = jnp.mean(o * o, axis=0, keepdims=True)
        y = o * lax.rsqrt(ms + LN_EPS) * g_ref[...] * (1.0 - LAMBDA_INIT)
        o_ref[r0:r0 + t, :] = y.T.astype(BF16)


def _diff_attn_kernel(q_ref, k_ref, v_ref, bias_ref, lv_ref, g_ref, o_ref):
    lam = (jnp.exp(jnp.sum(lv_ref[0:1, :] * lv_ref[1:2, :], axis=-1, keepdims=True))
           - jnp.exp(jnp.sum(lv_ref[2:3, :] * lv_ref[3:4, :], axis=-1, keepdims=True)) + LAMBDA_INIT)
    for hh in range(ATT_HEADS):
        hs = pl.ds(hh * DIFF_V_DIM, DIFF_V_DIM)
        _diff_attn_head(q_ref.at[:, hs], k_ref.at[:, hs], v_ref.at[:, hs], bias_ref.at[pl.ds(hh, 1)], lam,
                        g_ref, o_ref.at[:, hs])


def _diff_attn(q, k, v, bias, lv, g):
    vec = lambda b, h: (0, 0)
    seq = pl.BlockSpec((SEQ, ATT_HEADS * DIFF_V_DIM), lambda b, h: (b, h))
    return pl.pallas_call(
        _diff_attn_kernel,
        grid=(BATCH, N_DIFF_HEADS // ATT_HEADS),
        in_specs=[seq, seq, seq,
                  pl.BlockSpec((ATT_HEADS, 2, LANES, LANES), lambda b, h: (h, 0, 0, 0)),
                  pl.BlockSpec((4, DIFF_HEAD_DIM), vec), pl.BlockSpec((DIFF_V_DIM, 1), vec)],
        out_specs=seq,
        out_shape=jax.ShapeDtypeStruct((N_TOK, DIFF_WIDTH), BF16),
        compiler_params=_params("parallel", "parallel"),
        name="diff_attn",
    )(q, k, v, bias, lv, g)


def _kv_kernel(m_ref, w32_ref, o_ref, w_ref):
    _cast_once(w32_ref, w_ref)
    o_ref[...] = _dot(m_ref[...].astype(BF16), w_ref[...]).astype(BF16)


def _kv(mem2, wkv):
    tm = KV_ROWS
    return pl.pallas_call(
        _kv_kernel,
        grid=(BATCH * MEM_LEN // tm,),
        in_specs=[pl.BlockSpec((tm, D_MODEL), lambda i: (i, 0)), _resident((D_MODEL, 2 * D_MODEL))],
        out_specs=pl.BlockSpec((tm, 2 * D_MODEL), lambda i: (i, 0)),
        out_shape=jax.ShapeDtypeStruct((BATCH * MEM_LEN, 2 * D_MODEL), BF16),
        scratch_shapes=[pltpu.VMEM((D_MODEL, 2 * D_MODEL), BF16)],
        compiler_params=_params("arbitrary"),
        name="kv",
    )(mem2, wkv)


def _mix_ca_kernel(h0_ref, ys_ref, yd_ref, ln_ref, wout32_ref, wq32_ref, kv_ref, wo32_ref, o_ref,
                   wout_ref, wq_ref, wo_ref):
    _cast_once(wout32_ref, wout_ref)
    _cast_once(wq32_ref, wq_ref)
    _cast_once(wo32_ref, wo_ref)
    sub = MIX_ROWS // MIX_SUBTILES
    nsl = MIX_SLICES
    cw = D_MODEL // nsl
    rw = sub // nsl

    def work(st):
        r0 = st * sub
        ys = ys_ref[r0:r0 + sub, :]
        yd = yd_ref[r0:r0 + sub, :]
        mix = []
        for j in range(nsl):
            cs = slice(j * cw, (j + 1) * cw)
            mix.append(_dot(ys, wout_ref[0:S5_WIDTH, cs]) + _dot(yd, wout_ref[S5_WIDTH:, cs]))
            yield
        mix = jnp.concatenate(mix, axis=-1)
        h1 = []
        for j in range(nsl):
            rs = slice(j * rw, (j + 1) * rw)
            h0 = h0_ref[r0 + j * rw:r0 + (j + 1) * rw, :]
            h1.append(_layer_norm(DEEPNORM_ALPHA * h0 + mix[rs], ln_ref[2:3, :], ln_ref[3:4, :]))
            yield
        h1 = jnp.concatenate(h1, axis=0)
        h1b = h1.astype(BF16)
        qs = []
        for hd in range(CA_HEADS):
            cs = slice(hd * CA_HEAD_DIM, (hd + 1) * CA_HEAD_DIM)
            qs.append((_dot(h1b, wq_ref[:, cs]) * (CA_HEAD_DIM ** -0.5)).astype(BF16))
            yield
        heads = []
        for hd in range(CA_HEADS):
            kh = kv_ref[:, hd * CA_HEAD_DIM:(hd + 1) * CA_HEAD_DIM]
            vh = kv_ref[:, D_MODEL + hd * CA_HEAD_DIM:D_MODEL + (hd + 1) * CA_HEAD_DIM]
            s = lax.dot_general(qs[hd], kh, _NT, preferred_element_type=F32)
            e = jnp.exp(s - jnp.max(s, axis=-1, keepdims=True))
            oh = _dot(e.astype(BF16), vh) / jnp.sum(e, axis=-1, keepdims=True)
            heads.append(oh.astype(BF16))
            yield
        o = jnp.concatenate(heads, axis=-1)
        ca = []
        for j in range(nsl):
            ca.append(_dot(o, wo_ref[:, j * cw:(j + 1) * cw]))
            yield
        ca = jnp.concatenate(ca, axis=-1)
        for j in range(nsl):
            rs = slice(j * rw, (j + 1) * rw)
            o_ref[r0 + j * rw:r0 + (j + 1) * rw, :] = _layer_norm(
                DEEPNORM_ALPHA * h1[rs] + ca[rs], ln_ref[4:5, :], ln_ref[5:6, :])
            yield

    _interleave([work(st) for st in range(MIX_SUBTILES)], lag=nsl)


def _mix_ca(h0, ys, yd, ln, wout, wq, kv, wo):
    tm = MIX_ROWS
    per_batch = SEQ // tm
    row = lambda i: (i, 0)
    const = lambda i: (0, 0)
    sq = _resident((D_MODEL, D_MODEL))
    return pl.pallas_call(
        _mix_ca_kernel,
        grid=(N_TOK // tm,),
        in_specs=[pl.BlockSpec((tm, D_MODEL), row), pl.BlockSpec((tm, S5_WIDTH), row),
                  pl.BlockSpec((tm, DIFF_WIDTH), row),
                  pl.BlockSpec((N_LN_ROWS, D_MODEL), const), sq, sq,
                  pl.BlockSpec((MEM_LEN, 2 * D_MODEL), lambda i: (i // per_batch, 0)),
                  sq],
        out_specs=pl.BlockSpec((tm, D_MODEL), row),
        out_shape=jax.ShapeDtypeStruct((N_TOK, D_MODEL), F32),
        scratch_shapes=[pltpu.VMEM((D_MODEL, D_MODEL), BF16)] * 3,
        compiler_params=_params("arbitrary"),
        name="mix_ca",
    )(h0, ys, yd, ln, wout, wq, kv, wo)


def _stream_cast(src_hbm, dst_ref, stage, sems, w, chunks):
    ns = FFN_STAGE_SLOTS

    def copy(i):
        return pltpu.make_async_copy(src_hbm.at[chunks[i][0]], stage.at[i % ns], sems.at[w, i % ns])

    for i in range(min(ns - 1, len(chunks))):
        copy(i).start()
    for i in range(len(chunks)):
        if i + ns - 1 < len(chunks):
            copy(i + ns - 1).start()
        copy(i).wait()
        dst_ref[chunks[i][1]] = stage[i % ns].astype(BF16)


def _ffn_kernel(h_ref, wgu_hbm, wd_hbm, ln_ref, o_ref, wgu_ref, wd_ref, gu_stage, d_stage, sem):
    @pl.when(pl.program_id(0) == 0)
    def _():
        cw = FFN_STAGE_COLS
        cols = [(slice(None), slice(c * cw, (c + 1) * cw)) for c in range(2 * FFN_HIDDEN // cw)]
        _stream_cast(wgu_hbm, wgu_ref, gu_stage, sem, 0, [(ix, ix) for ix in cols])
        rw_ = FFN_STAGE_ROWS
        rows_ = [(slice(r * rw_, (r + 1) * rw_), slice(None)) for r in range(FFN_HIDDEN // rw_)]
        _stream_cast(wd_hbm, wd_ref, d_stage, sem, 1, [(ix, ix) for ix in rows_])

    sub = FFN_ROWS // FFN_SUBTILES
    ck = FFN_CHUNK
    nsl = 4
    rw = sub // nsl

    def work(st):
        r0 = st * sub
        acc = None
        for c in range(FFN_HIDDEN // ck):
            hb = h_ref[r0:r0 + sub, :].astype(BF16)
            gate = _dot(hb, wgu_ref[:, c * ck:(c + 1) * ck])
            up = _dot(hb, wgu_ref[:, FFN_HIDDEN + c * ck:FFN_HIDDEN + (c + 1) * ck])
            act = (jax.nn.silu(gate) * up).astype(BF16)
            part = _dot(act, wd_ref[c * ck:(c + 1) * ck, :])
            acc = part if acc is None else acc + part
            yield
        for j in range(nsl):
            rs = slice(r0 + j * rw, r0 + (j + 1) * rw)
            o_ref[rs, :] = _layer_norm(DEEPNORM_ALPHA * h_ref[rs, :] + acc[j * rw:(j + 1) * rw],
                                       ln_ref[6:7, :], ln_ref[7:8, :])
            yield

    _interleave([work(st) for st in range(FFN_SUBTILES)], lag=FFN_HIDDEN // ck // 2 + 1)


def _ffn(h2, wgu, wd, ln):
    tm = FFN_ROWS
    row = lambda i: (i, 0)
    const = lambda i: (0, 0)
    hbm = pl.BlockSpec(memory_space=pl.ANY)
    return pl.pallas_call(
        _ffn_kernel,
        grid=(N_TOK // tm,),
        in_specs=[pl.BlockSpec((tm, D_MODEL), row), hbm, hbm,
                  pl.BlockSpec((N_LN_ROWS, D_MODEL), const)],
        out_specs=pl.BlockSpec((tm, D_MODEL), row),
        out_shape=jax.ShapeDtypeStruct((N_TOK, D_MODEL), F32),
        scratch_shapes=[pltpu.VMEM((D_MODEL, 2 * FFN_HIDDEN), BF16), pltpu.VMEM((FFN_HIDDEN, D_MODEL), BF16),
                        pltpu.VMEM((FFN_STAGE_SLOTS, D_MODEL, FFN_STAGE_COLS), F32),
                        pltpu.VMEM((FFN_STAGE_SLOTS, FFN_STAGE_ROWS, D_MODEL), F32),
                        pltpu.SemaphoreType.DMA((2, FFN_STAGE_SLOTS))],
        compiler_params=_params("arbitrary"),
        name="ffn",
    )(h2, wgu, wd, ln)


def kernel(x, mem, ln_in_g, ln_in_b, w_in, s5_lambda_re, s5_lambda_im, s5_log_dt, s5_b_re, s5_b_im,
           s5_c_re, s5_c_im, s5_d, s5_glu_w, s5_glu_b, diff_lq1, diff_lk1, diff_lq2, diff_lk2,
           diff_subln_g, rel_bias, w_out, ln1_g, ln1_b, ca_wq, ca_wkv, ca_wo, ln2_g, ln2_b,
           ffn_w_gate_up, ffn_w_down, ln3_g, ln3_b):
    x2 = x.reshape(N_TOK, D_MODEL)
    ln = jnp.stack([ln_in_g, ln_in_b, ln1_g[0], ln1_b[0], ln2_g[0], ln2_b[0], ln3_g[0], ln3_b[0]])
    lv = jnp.stack([diff_lq1[0], diff_lk1[0], diff_lq2[0], diff_lk2[0]])
    dg = jnp.stack([s5_d[0].reshape(S5_WIDTH), s5_glu_b[0]])

    h0, u, q, k, v = _proj(x2, ln, w_in[0])

    a8, w_s5 = _s5_prep(s5_lambda_re[0], s5_lambda_im[0], s5_log_dt[0], s5_b_re[0], s5_b_im[0],
                        s5_c_re[0], s5_c_im[0])
    y_s5 = _s5(u.reshape(BATCH, SEQ, S5_WIDTH), a8, w_s5, dg, s5_glu_w[0]).reshape(N_TOK, S5_WIDTH)

    y_diff = _diff_attn(q, k, v, _bias_tiles(rel_bias), lv, diff_subln_g.reshape(DIFF_V_DIM, 1))

    kv = _kv(mem.reshape(BATCH * MEM_LEN, D_MODEL), ca_wkv[0])
    h2 = _mix_ca(h0, y_s5, y_diff, ln, w_out[0], ca_wq[0], kv, ca_wo[0])
    out = _ffn(h2, ffn_w_gate_up[0], ffn_w_down[0], ln)
    return out.reshape(BATCH, SEQ, D_MODEL)
```
